```python
import math
import jax, jax.numpy as jnp
from jax import lax
import numpy as np

D_MODEL = 2048
BATCH = 4
SEQ = 2048
DEPTH = 4
DEC_BATCH = 8
DEC_SEQ = 1
PAST_LEN = 16384
PAGE_SIZE = 128

N_A_LAYERS = DEPTH // 2
N_B_LAYERS = DEPTH - N_A_LAYERS
PLE_DIM = 256
D_FF = ((8 * D_MODEL // 3 + 127) // 128) * 128
N_NORMS = 7
NORM_EPS = 1e-6
SSM_GROUP = 16
SSM_GROUPS = D_MODEL // SSM_GROUP
SSM_STATE = 64
SSM_MAX_CHUNK = 128
DT_MIN = 1e-3
DT_MAX = 1e-1
HEAD_DIM = 128
N_HEADS = D_MODEL // HEAD_DIM
N_KV = 2
HEADS_PER_KV = N_HEADS // N_KV
CMP_LEN = 32
CMP_STRIDE = 16
SEL_LEN = 64
SEL_TOP = 16
WINDOW = 512
Q_BLOCK = 128
ROPE_THETA = 10000.0
N_BRANCH = 3
N_KV_SLOTS = 6
FORCE_BONUS = 1e3
NEG = -1e30

kernel_name = 'yoco_s5_nsa_decoder_step'


def rms_norm(x, g):
    xf = x.astype(jnp.float32)
    y = xf * lax.rsqrt(jnp.mean(xf * xf, axis=-1, keepdims=True) + NORM_EPS)
    return (y * g.astype(jnp.float32)).astype(x.dtype)


def swiglu(x, w_gate, w_up, w_down):
    return (jax.nn.silu(x @ w_gate) * (x @ w_up)) @ w_down


def rope(x, pos):
    half = HEAD_DIM // 2
    inv = jnp.exp(-math.log(ROPE_THETA) * jnp.arange(half, dtype=jnp.float32) / half)
    ang = pos.astype(jnp.float32)[:, None] * inv[None, :]
    cos = jnp.cos(ang)[None, :, None, :]
    sin = jnp.sin(ang)[None, :, None, :]
    xf = x.astype(jnp.float32)
    x1, x2 = xf[..., :half], xf[..., half:]
    return jnp.concatenate([x1 * cos - x2 * sin, x2 * cos + x1 * sin], axis=-1).astype(x.dtype)


def masked_softmax(s, mask):
    s = jnp.where(mask, s.astype(jnp.float32), NEG)
    m = jnp.max(s, axis=-1, keepdims=True)
    e = jnp.where(mask, jnp.exp(s - m), 0.0)
    return e / jnp.maximum(jnp.sum(e, axis=-1, keepdims=True), 1e-30)


def s5_discretise(lam_re, lam_im, log_dt, b_re, b_im):
    dt = jnp.exp(log_dt)[:, None]
    mag = jnp.exp(lam_re * dt)
    a_re, a_im = mag * jnp.cos(lam_im * dt), mag * jnp.sin(lam_im * dt)
    den = lam_re * lam_re + lam_im * lam_im
    nr, ni = a_re - 1.0, a_im
    r_re = (nr * lam_re + ni * lam_im) / den
    r_im = (ni * lam_re - nr * lam_im) / den
    bb_re = r_re[..., None] * b_re - r_im[..., None] * b_im
    bb_im = r_re[..., None] * b_im + r_im[..., None] * b_re
    return a_re, a_im, bb_re, bb_im


def _complex_affine_combine(e1, e2):
    ar1, ai1, br1, bi1 = e1
    ar2, ai2, br2, bi2 = e2
    return (ar2 * ar1 - ai2 * ai1, ar2 * ai1 + ai2 * ar1,
            ar2 * br1 - ai2 * bi1 + br2, ar2 * bi1 + ai2 * br1 + bi2)


def s5_mixer(u, s_re, s_im, lam_re, lam_im, log_dt, b_re, b_im, c_re, c_im, d_skip, glu_w, glu_b):
    f32 = jnp.float32
    bsz, t, _ = u.shape
    a_re, a_im, bb_re, bb_im = s5_discretise(lam_re.astype(f32), lam_im.astype(f32), log_dt.astype(f32),
                                             b_re.astype(f32), b_im.astype(f32))
    c_re, c_im = c_re.astype(f32), c_im.astype(f32)
    chunk = math.gcd(t, SSM_MAX_CHUNK)
    uf = u.astype(f32)
    uc = uf.reshape(bsz, t // chunk, chunk, SSM_GROUPS, SSM_GROUP).transpose(1, 0, 2, 3, 4)

    def step(carry, ub):
        hr, hi = carry
        xr = jnp.einsum('bcgm,gpm->bcgp', ub, bb_re)
        xi = jnp.einsum('bcgm,gpm->bcgp', ub, bb_im)
        xr = xr.at[:, 0].add(a_re * hr - a_im * hi)
        xi = xi.at[:, 0].add(a_re * hi + a_im * hr)
        ar = jnp.broadcast_to(a_re, xr.shape)
        ai = jnp.broadcast_to(a_im, xi.shape)
        _, _, sr, si = lax.associative_scan(_complex_affine_combine, (ar, ai, xr, xi), axis=1)
        y = jnp.einsum('bcgp,gmp->bcgm', sr, c_re) - jnp.einsum('bcgp,gmp->bcgm', si, c_im)
        return (sr[:, -1], si[:, -1]), y

    (hr, hi), ys = lax.scan(step, (s_re.astype(f32), s_im.astype(f32)), uc)
    y = ys.transpose(1, 0, 2, 3, 4).reshape(bsz, t, D_MODEL) + d_skip.astype(f32) * uf
    y = jax.nn.gelu(y)
    y = y * jax.nn.sigmoid(y @ glu_w + glu_b)
    return y.astype(u.dtype), hr, hi


def shared_kv(h, pos, g, w_kv):
    bsz, t, _ = h.shape
    kv = (rms_norm(h, g) @ w_kv).reshape(bsz, t, N_KV_SLOTS, N_KV, HEAD_DIM)
    k_sel = rope(kv[:, :, 2], pos)
    k_win = rope(kv[:, :, 4], pos)
    rows = jnp.stack([kv[:, :, 0], kv[:, :, 1], k_sel, kv[:, :, 3]], axis=2)
    win = jnp.stack([k_win, kv[:, :, 5]], axis=2)
    return rows, win


def compress_rows(rows, pe, w1, w2):
    bsz, length = rows.shape[:2]
    n_cmp = (length - CMP_LEN) // CMP_STRIDE + 1
    idx = jnp.arange(n_cmp)[:, None] * CMP_STRIDE + jnp.arange(CMP_LEN)[None, :]
    blocks = rows[:, idx] + pe[None, None, :, None, :]
    flat = blocks.transpose(0, 1, 3, 2, 4).reshape(bsz, n_cmp, N_KV, CMP_LEN * HEAD_DIM)
    return jax.nn.silu(flat @ w1) @ w2


def sel_blocks(rows):
    bsz, length = rows.shape[:2]
    n_sel = -(-length // SEL_LEN)
    rows = jnp.pad(rows, ((0, 0), (0, n_sel * SEL_LEN - length), (0, 0), (0, 0)))
    return rows.reshape(bsz, n_sel, SEL_LEN, N_KV, HEAD_DIM).transpose(0, 3, 1, 2, 4)


def cmp_sel_overlap(n_cmp, n_sel):
    c0 = np.arange(n_cmp)[:, None] * CMP_STRIDE
    s0 = np.arange(n_sel)[None, :] * SEL_LEN
    return ((c0 < s0 + SEL_LEN) & (c0 + CMP_LEN > s0)).astype(np.float32)


def prepare_keys(rows, win_rows, win_start, prm):
    k_cmp = compress_rows(rows[:, :, 0], prm['cmp_pe_k'], prm['cmp_w1_k'], prm['cmp_w2_k'])
    v_cmp = compress_rows(rows[:, :, 1], prm['cmp_pe_v'], prm['cmp_w1_v'], prm['cmp_w2_v'])
    n_cmp = k_cmp.shape[1]
    cmp_end = jnp.arange(n_cmp) * CMP_STRIDE + (CMP_LEN - 1)
    k_sel = sel_blocks(rows[:, :, 2])
    v_sel = sel_blocks(rows[:, :, 3])
    overlap = jnp.asarray(cmp_sel_overlap(n_cmp, k_sel.shape[2]))
    win_pos = win_start + jnp.arange(win_rows.shape[1])
    return (k_cmp, v_cmp, cmp_end, overlap, k_sel, v_sel, win_rows[:, :, 0], win_rows[:, :, 1], win_pos)


def nsa_attend(q_c, q_r, gates, qpos, k_cmp, v_cmp, cmp_end, overlap, k_sel, v_sel, k_win, v_win, win_pos):
    bsz, nq = q_c.shape[:2]
    scale = HEAD_DIM ** -0.5
    s = jnp.einsum('bqghd,bngd->bghqn', q_c, k_cmp).astype(jnp.float32) * scale
    p_cmp = masked_softmax(s, cmp_end[None, :] <= qpos[:, None])
    o_cmp = jnp.einsum('bghqn,bngd->bqghd', p_cmp.astype(v_cmp.dtype), v_cmp)
    n_sel = k_sel.shape[2]
    blk = jnp.arange(n_sel)[None, :]
    cur = (qpos // SEL_LEN)[:, None]
    imp = jnp.einsum('bghqn,ns->bgqs', p_cmp, overlap)
    forced = (blk == 0) | (blk == cur) | (blk == cur - 1)
    imp = jnp.where(blk * SEL_LEN <= qpos[:, None], imp + jnp.where(forced, FORCE_BONUS, 0.0), NEG)
    n_top = min(SEL_TOP, n_sel)
    _, top = lax.top_k(imp, n_top)
    b_ix = jnp.arange(bsz)[:, None, None, None]
    g_ix = jnp.arange(N_KV)[None, :, None, None]
    ks = k_sel[b_ix, g_ix, top].reshape(bsz, N_KV, nq, n_top * SEL_LEN, HEAD_DIM)
    vs = v_sel[b_ix, g_ix, top].reshape(bsz, N_KV, nq, n_top * SEL_LEN, HEAD_DIM)
    kpos = (top[..., None] * SEL_LEN + jnp.arange(SEL_LEN)).reshape(bsz, N_KV, nq, n_top * SEL_LEN)
    s = jnp.einsum('bqghd,bgqsd->bghqs', q_r, ks).astype(jnp.float32) * scale
    p_sel = masked_softmax(s, (kpos <= qpos[None, None, :, None])[:, :, None])
    o_sel = jnp.einsum('bghqs,bgqsd->bqghd', p_sel.astype(vs.dtype), vs)
    s = jnp.einsum('bqghd,bkgd->bghqk', q_r, k_win).astype(jnp.float32) * scale
    kp, qp = win_pos[None, :], qpos[:, None]
    p_win = masked_softmax(s, (kp <= qp) & (kp > qp - WINDOW) & (kp >= 0))
    o_win = jnp.einsum('bghqk,bkgd->bqghd', p_win.astype(v_win.dtype), v_win)
    o = o_cmp * gates[..., 0:1] + o_sel * gates[..., 1:2] + o_win * gates[..., 2:3]
    return o.reshape(bsz, nq, N_HEADS * HEAD_DIM)


def nsa_mixer(xn, pos, w_qg, w_o, keys, blocked):
    bsz, t, _ = xn.shape
    qg = xn @ w_qg
    q = qg[..., :N_HEADS * HEAD_DIM].reshape(bsz, t, N_HEADS, HEAD_DIM)
    gates = jax.nn.sigmoid(qg[..., N_HEADS * HEAD_DIM:].astype(jnp.float32))
    gates = gates.reshape(bsz, t, N_KV, HEADS_PER_KV, N_BRANCH)
    q_r = rope(q, pos).reshape(bsz, t, N_KV, HEADS_PER_KV, HEAD_DIM)
    q_c = q.reshape(bsz, t, N_KV, HEADS_PER_KV, HEAD_DIM)
    k_cmp, v_cmp, cmp_end, overlap, k_sel, v_sel, k_win, v_win, win_pos = keys
    if not blocked:
        o = nsa_attend(q_c, q_r, gates, pos, k_cmp, v_cmp, cmp_end, overlap,
                       k_sel, v_sel, k_win, v_win, win_pos)
    else:
        qb = math.gcd(t, Q_BLOCK)
        pad = ((0, 0), (WINDOW, 0), (0, 0), (0, 0))
        kw_pad, vw_pad = jnp.pad(k_win, pad), jnp.pad(v_win, pad)
        wpos_pad = jnp.concatenate([win_pos[0] - WINDOW + jnp.arange(WINDOW), win_pos])

        def block(n):
            q0 = n * qb
            sl = lambda a: lax.dynamic_slice_in_dim(a, q0, qb, axis=1)
            kw = lax.dynamic_slice_in_dim(kw_pad, q0, WINDOW + qb, axis=1)
            vw = lax.dynamic_slice_in_dim(vw_pad, q0, WINDOW + qb, axis=1)
            wpos = lax.dynamic_slice_in_dim(wpos_pad, q0, WINDOW + qb)
            qpos = lax.dynamic_slice_in_dim(pos, q0, qb)
            return nsa_attend(sl(q_c), sl(q_r), sl(gates), qpos, k_cmp, v_cmp, cmp_end, overlap,
                              k_sel, v_sel, kw, vw, wpos)

        o = lax.map(block, jnp.arange(t // qb))
        o = jnp.moveaxis(o, 0, 1).reshape(bsz, t, N_HEADS * HEAD_DIM)
    return o @ w_o


def run_group(x, p, start, s_re, s_im, past_rows, past_win, prm):
    bsz, t, _ = x.shape
    pos = start + jnp.arange(t)
    h = x
    ssm_re, ssm_im = [], []
    keys, new_rows, new_win = None, None, None
    for i in range(DEPTH):
        g = prm['norm_g'][i]
        f = swiglu(rms_norm(h, g[0]), prm['ffn_w_gate'][i, 0], prm['ffn_w_up'][i, 0], prm['ffn_w_down'][i, 0])
        h = h + 0.5 * rms_norm(f, g[1])
        xn = rms_norm(h, g[2])
        if i < N_A_LAYERS:
            m, hr, hi = s5_mixer(xn, s_re[i], s_im[i], prm['ssm_lam_re'][i], prm['ssm_lam_im'][i],
                                 prm['ssm_log_dt'][i], prm['ssm_b_re'][i], prm['ssm_b_im'][i],
                                 prm['ssm_c_re'][i], prm['ssm_c_im'][i], prm['ssm_d'][i],
                                 prm['glu_w'][i], prm['glu_b'][i])
            ssm_re.append(hr)
            ssm_im.append(hi)
        else:
            j = i - N_A_LAYERS
            m = nsa_mixer(xn, pos, prm['attn_w_qg'][j], prm['attn_w_o'][j], keys, past_rows is None)
        h = h + rms_norm(m, g[3])
        f = swiglu(rms_norm(h, g[4]), prm['ffn_w_gate'][i, 1], prm['ffn_w_up'][i, 1], prm['ffn_w_down'][i, 1])
        h = h + 0.5 * rms_norm(f, g[5])
        ple = jax.nn.sigmoid(h @ prm['ple_w_gate'][i]) * (p[i] @ prm['ple_w_proj'][i])
        h = h + rms_norm(ple, g[6])
        if i == N_A_LAYERS - 1:
            new_rows, win_rows = shared_kv(h, pos, prm['kv_norm_g'], prm['w_kv'])
            if past_rows is None:
                keys = prepare_keys(new_rows, win_rows, 0, prm)
                new_win = win_rows[:, t - min(WINDOW, t):]
            else:
                all_rows = jnp.concatenate([past_rows, new_rows], axis=1)
                all_win = jnp.concatenate([past_win, win_rows], axis=1)
                keys = prepare_keys(all_rows, all_win, start - past_win.shape[1], prm)
                new_win = all_win[:, t:]
    return h, jnp.stack(ssm_re), jnp.stack(ssm_im), new_rows, new_win


def setup_inputs(seed: int = 0) -> dict:
    key = jax.random.key(seed)
    keys = iter(jax.random.split(key, 64))

    def nrm(shape, scale):
        return jax.random.normal(next(keys), shape, jnp.float32) * scale

    n_pages = PAST_LEN // PAGE_SIZE
    n_pool = (DEC_BATCH * n_pages * 5 + 3) // 4
    win_buf = min(WINDOW, PAST_LEN)
    na, nb = N_A_LAYERS, N_B_LAYERS
    perm = jax.random.permutation(next(keys), n_pool)
    page_table = perm[:DEC_BATCH * n_pages].reshape(DEC_BATCH, n_pages).astype(jnp.int32)
    return {
        'x_prompt': nrm((BATCH, SEQ, D_MODEL), 1.0),
        'x_sample': nrm((DEC_BATCH, DEC_SEQ, D_MODEL), 1.0),
        'state_ssm_re': nrm((na, DEC_BATCH, SSM_GROUPS, SSM_STATE), 0.1),
        'state_ssm_im': nrm((na, DEC_BATCH, SSM_GROUPS, SSM_STATE), 0.1),
        'cache_kv': nrm((n_pool, PAGE_SIZE, 4, N_KV, HEAD_DIM), 1.0),
        'cache_win': nrm((DEC_BATCH, win_buf, 2, N_KV, HEAD_DIM), 1.0),
        'page_table': page_table,
        'p_prompt': nrm((DEPTH, BATCH, SEQ, PLE_DIM), 1.0),
        'p_sample': nrm((DEPTH, DEC_BATCH, DEC_SEQ, PLE_DIM), 1.0),
        'norm_g': 1.0 + nrm((DEPTH, N_NORMS, D_MODEL), 0.05),
        'ffn_w_gate': nrm((DEPTH, 2, D_MODEL, D_FF), D_MODEL ** -0.5),
        'ffn_w_up': nrm((DEPTH, 2, D_MODEL, D_FF), D_MODEL ** -0.5),
        'ffn_w_down': nrm((DEPTH, 2, D_FF, D_MODEL), D_FF ** -0.5),
        'ple_w_proj': nrm((DEPTH, PLE_DIM, D_MODEL), PLE_DIM ** -0.5),
        'ple_w_gate': nrm((DEPTH, D_MODEL, D_MODEL), D_MODEL ** -0.5),
        'ssm_lam_re': -0.5 + nrm((na, SSM_GROUPS, SSM_STATE), 0.01),
        'ssm_lam_im': jnp.pi * jnp.arange(SSM_STATE, dtype=jnp.float32) + nrm((na, SSM_GROUPS, SSM_STATE), 0.01),
        'ssm_log_dt': jax.random.uniform(next(keys), (na, SSM_GROUPS), jnp.float32,
                                         math.log(DT_MIN), math.log(DT_MAX)),
        'ssm_b_re': nrm((na, SSM_GROUPS, SSM_STATE, SSM_GROUP), (2 * SSM_GROUP) ** -0.5),
        'ssm_b_im': nrm((na, SSM_GROUPS, SSM_STATE, SSM_GROUP), (2 * SSM_GROUP) ** -0.5),
        'ssm_c_re': nrm((na, SSM_GROUPS, SSM_GROUP, SSM_STATE), SSM_STATE ** -0.5),
        'ssm_c_im': nrm((na, SSM_GROUPS, SSM_GROUP, SSM_STATE), SSM_STATE ** -0.5),
        'ssm_d': nrm((na, D_MODEL), 0.5),
        'glu_w': nrm((na, D_MODEL, D_MODEL), D_MODEL ** -0.5),
        'glu_b': nrm((na, D_MODEL), 0.02),
        'kv_norm_g': 1.0 + nrm((D_MODEL,), 0.05),
        'w_kv': nrm((D_MODEL, N_KV_SLOTS * N_KV * HEAD_DIM), D_MODEL ** -0.5),
        'cmp_pe_k': nrm((CMP_LEN, HEAD_DIM), 0.1),
        'cmp_pe_v': nrm((CMP_LEN, HEAD_DIM), 0.1),
        'cmp_w1_k': nrm((CMP_LEN * HEAD_DIM, HEAD_DIM), (CMP_LEN * HEAD_DIM) ** -0.5),
        'cmp_w2_k': nrm((HEAD_DIM, HEAD_DIM), HEAD_DIM ** -0.5),
        'cmp_w1_v': nrm((CMP_LEN * HEAD_DIM, HEAD_DIM), (CMP_LEN * HEAD_DIM) ** -0.5),
        'cmp_w2_v': nrm((HEAD_DIM, HEAD_DIM), HEAD_DIM ** -0.5),
        'attn_w_qg': nrm((nb, D_MODEL, N_HEADS * HEAD_DIM + N_BRANCH * N_HEADS), D_MODEL ** -0.5),
        'attn_w_o': nrm((nb, N_HEADS * HEAD_DIM, D_MODEL), (N_HEADS * HEAD_DIM) ** -0.5),
    }


def reference(x_prompt, x_sample, state_ssm_re, state_ssm_im, cache_kv, cache_win, page_table,
              p_prompt, p_sample, norm_g, ffn_w_gate, ffn_w_up, ffn_w_down, ple_w_proj, ple_w_gate,
              ssm_lam_re, ssm_lam_im, ssm_log_dt, ssm_b_re, ssm_b_im, ssm_c_re, ssm_c_im, ssm_d,
              glu_w, glu_b, kv_norm_g, w_kv, cmp_pe_k, cmp_pe_v, cmp_w1_k, cmp_w2_k, cmp_w1_v,
              cmp_w2_v, attn_w_qg, attn_w_o):
    prm = dict(norm_g=norm_g, ffn_w_gate=ffn_w_gate, ffn_w_up=ffn_w_up, ffn_w_down=ffn_w_down,
               ple_w_proj=ple_w_proj, ple_w_gate=ple_w_gate, ssm_lam_re=ssm_lam_re,
               ssm_lam_im=ssm_lam_im, ssm_log_dt=ssm_log_dt, ssm_b_re=ssm_b_re, ssm_b_im=ssm_b_im,
               ssm_c_re=ssm_c_re, ssm_c_im=ssm_c_im, ssm_d=ssm_d, glu_w=glu_w, glu_b=glu_b,
               kv_norm_g=kv_norm_g, w_kv=w_kv, cmp_pe_k=cmp_pe_k, cmp_pe_v=cmp_pe_v,
               cmp_w1_k=cmp_w1_k, cmp_w2_k=cmp_w2_k, cmp_w1_v=cmp_w1_v, cmp_w2_v=cmp_w2_v,
               attn_w_qg=attn_w_qg, attn_w_o=attn_w_o)
    zeros = jnp.zeros((N_A_LAYERS, x_prompt.shape[0], SSM_GROUPS, SSM_STATE), jnp.float32)
    y_prompt, re_p, im_p, kv_p, win_p = run_group(x_prompt, p_prompt, 0, zeros, zeros, None, None, prm)
    past_len = page_table.shape[1] * cache_kv.shape[1]
    past_rows = cache_kv[page_table].reshape(page_table.shape[0], past_len, *cache_kv.shape[2:])
    y_sample, re_s, im_s, kv_s, win_s = run_group(x_sample, p_sample, past_len, state_ssm_re, state_ssm_im,
                                                  past_rows, cache_win, prm)
    return (y_prompt, y_sample, re_p, im_p, re_s, im_s, kv_p, kv_s, win_p, win_s)
```

```python
import functools
import math

import jax
import jax.numpy as jnp
from jax import lax
from jax.experimental import pallas as pl
from jax.experimental.pallas import tpu as pltpu

F32 = jnp.float32
BF16 = jnp.bfloat16
HIGHEST = lax.Precision.HIGHEST

N_A_LAYERS = 2
NORM_EPS = 1e-6
SSM_GROUP = 16
SSM_STATE = 64
HEAD_DIM = 128
N_KV = 2
HEADS_PER_KV = 8
N_HEADS = N_KV * HEADS_PER_KV
N_BRANCH = 3
N_KV_SLOTS = 6
CMP_LEN = 32
CMP_STRIDE = 16
SEL_LEN = 64
SEL_TOP = 16
WINDOW = 512
ROPE_THETA = 10000.0
FORCE_BONUS = 1e3
NEG = -1e30
LOWEST = -3.0e38
ATT_SCALE = HEAD_DIM ** -0.5

LANES = 128
VMEM_LIMIT_BYTES = 56 * 2 ** 20
FFN_TILE_F = 512
S5_CHUNK = 16
S5_GROUPS_PER_STEP = LANES // SSM_GROUP
S5_STATE_LANES = S5_GROUPS_PER_STEP * SSM_STATE


def _cparams(*sem):
    return pltpu.CompilerParams(dimension_semantics=sem, vmem_limit_bytes=VMEM_LIMIT_BYTES)


def _rms(x, g):
    y = x * lax.rsqrt(jnp.mean(x * x, axis=-1, keepdims=True) + NORM_EPS)
    return y * g


def _dot(a, b, **kw):
    return jnp.dot(a, b, preferred_element_type=F32, **kw)


def _dot_nt(a, b):
    return lax.dot_general(a, b, (((1,), (1,)), ((), ())), preferred_element_type=F32)


def _rope(x, cos, sin):
    return x * cos + pltpu.roll(x, HEAD_DIM // 2, 1) * sin


def _iota(shape, axis):
    return lax.broadcasted_iota(jnp.int32, shape, axis)


def _ffn_body(h_ref, gpre_ref, gpost_ref, gnext_ref, wg_ref, wu_ref, wd_ref, o_ref, xn_out_ref,
              xn_ref, acc_ref, *, nk):
    k = pl.program_id(1)

    @pl.when(k == 0)
    def _():
        xn_ref[...] = _rms(h_ref[...], gpre_ref[...]).astype(BF16)
        acc_ref[...] = jnp.zeros_like(acc_ref)

    xn = xn_ref[...]
    gate = _dot(xn, wg_ref[...])
    up = _dot(xn, wu_ref[...])
    mid = (jax.nn.silu(gate) * up).astype(BF16)
    acc_ref[...] += _dot(mid, wd_ref[...])

    @pl.when(k == nk - 1)
    def _():
        h_new = h_ref[...] + 0.5 * _rms(acc_ref[...], gpost_ref[...])
        o_ref[...] = h_new
        xn_out_ref[...] = _rms(h_new, gnext_ref[...])


def _ffn(h, g_pre, g_post, g_next, wg, wu, wd, layer, which, tm):
    n, d = h.shape
    nk = wg.shape[-1] // FFN_TILE_F
    row = lambda i, k: (i, 0)
    vec = lambda i, k: (0, 0)
    return pl.pallas_call(
        functools.partial(_ffn_body, nk=nk),
        grid=(n // tm, nk),
        in_specs=[pl.BlockSpec((tm, d), row), pl.BlockSpec((1, d), vec), pl.BlockSpec((1, d), vec),
                  pl.BlockSpec((1, d), vec),
                  pl.BlockSpec((None, None, d, FFN_TILE_F), lambda i, k: (layer, which, 0, k)),
                  pl.BlockSpec((None, None, d, FFN_TILE_F), lambda i, k: (layer, which, 0, k)),
                  pl.BlockSpec((None, None, FFN_TILE_F, d), lambda i, k: (layer, which, k, 0))],
        out_specs=[pl.BlockSpec((tm, d), row), pl.BlockSpec((tm, d), row)],
        out_shape=[jax.ShapeDtypeStruct((n, d), F32)] * 2,
        scratch_shapes=[pltpu.VMEM((tm, d), BF16), pltpu.VMEM((tm, d), F32)],
        compiler_params=_cparams("parallel", "arbitrary"),
        name="ffn",
    )(h, g_pre, g_post, g_next, wg, wu, wd)


def _ple_body(h_ref, p_ref, wgate_ref, wproj_ref, g_ref, o_ref):
    h = h_ref[...]
    gate = jax.nn.sigmoid(_dot(h.astype(BF16), wgate_ref[...]))
    proj = _dot(p_ref[...].astype(BF16), wproj_ref[...])
    o_ref[...] = h + _rms(gate * proj, g_ref[...])


def _ple(h, p, wgate, wproj, g, layer, tm):
    n, d = h.shape
    pd = p.shape[-1]
    row = lambda i: (i, 0)
    return pl.pallas_call(
        _ple_body,
        grid=(n // tm,),
        in_specs=[pl.BlockSpec((tm, d), row), pl.BlockSpec((tm, pd), row),
                  pl.BlockSpec((None, d, d), lambda i: (layer, 0, 0)),
                  pl.BlockSpec((None, pd, d), lambda i: (layer, 0, 0)),
                  pl.BlockSpec((1, d), lambda i: (0, 0))],
        out_specs=pl.BlockSpec((tm, d), row),
        out_shape=jax.ShapeDtypeStruct((n, d), F32),
        compiler_params=_cparams("parallel"),
        name="ple",
    )(h, p, wgate, wproj, g)


def _kv_body(h_ref, g_ref, w_ref, cos_ref, sin_ref, kv_ref):
    xn = _rms(h_ref[...], g_ref[...]).astype(BF16)
    kv = _dot(xn, w_ref[...])
    cos, sin = cos_ref[...], sin_ref[...]
    width = N_KV * HEAD_DIM
    for slot in range(N_KV_SLOTS):
        for g in range(N_KV):
            lo = slot * width + g * HEAD_DIM
            x = kv[:, lo:lo + HEAD_DIM]
            if slot in (2, 4):
                x = _rope(x, cos, sin)
            kv_ref[:, lo:lo + HEAD_DIM] = x


def _kv_proj(h, g, w, cos, sin, tm):
    n, d = h.shape
    nw = w.shape[-1]
    nt = cos.shape[0] // tm
    row = lambda i: (i, 0)
    tab = lambda i: (i % nt, 0)
    return pl.pallas_call(
        _kv_body,
        grid=(n // tm,),
        in_specs=[pl.BlockSpec((tm, d), row), pl.BlockSpec((1, d), lambda i: (0, 0)),
                  pl.BlockSpec((d, nw), lambda i: (0, 0)),
                  pl.BlockSpec((tm, HEAD_DIM), tab), pl.BlockSpec((tm, HEAD_DIM), tab)],
        out_specs=pl.BlockSpec((tm, nw), row),
        out_shape=jax.ShapeDtypeStruct((n, nw), F32),
        compiler_params=_cparams("parallel"),
        name="kv_proj",
    )(h, g, w, cos, sin)


def _qg_body(xn_ref, wq_ref, wgt_ref, cos_ref, sin_ref, qc_ref, qr_ref, gate_ref):
    xn = xn_ref[...].astype(BF16)
    q = _dot(xn, wq_ref[...])
    qc_ref[...] = q.astype(BF16)
    cos, sin = cos_ref[...], sin_ref[...]
    for hd in range(N_HEADS):
        sl = slice(hd * HEAD_DIM, (hd + 1) * HEAD_DIM)
        qr_ref[:, sl] = _rope(q[:, sl], cos, sin).astype(BF16)
    gate_ref[...] = jax.nn.sigmoid(_dot(xn, wgt_ref[...]))


def _qg_proj(xn, wq, wgt, cos, sin, layer, tm):
    n, d = xn.shape
    nq, ng = wq.shape[-1], wgt.shape[-1]
    nt = cos.shape[0] // tm
    row = lambda i: (i, 0)
    tab = lambda i: (i % nt, 0)
    return pl.pallas_call(
        _qg_body,
        grid=(n // tm,),
        in_specs=[pl.BlockSpec((tm, d), row),
                  pl.BlockSpec((None, d, nq), lambda i: (layer, 0, 0)),
                  pl.BlockSpec((None, d, ng), lambda i: (layer, 0, 0)),
                  pl.BlockSpec((tm, HEAD_DIM), tab), pl.BlockSpec((tm, HEAD_DIM), tab)],
        out_specs=[pl.BlockSpec((tm, nq), row), pl.BlockSpec((tm, nq), row), pl.BlockSpec((tm, ng), row)],
        out_shape=[jax.ShapeDtypeStruct((n, nq), BF16), jax.ShapeDtypeStruct((n, nq), BF16),
                   jax.ShapeDtypeStruct((n, ng), F32)],
        compiler_params=_cparams("parallel"),
        name="qg_proj",
    )(xn, wq, wgt, cos, sin)


def _oproj_body(h_ref, o_ref, w_ref, g_ref, out_ref):
    m = _dot(o_ref[...].astype(BF16), w_ref[...])
    out_ref[...] = h_ref[...] + _rms(m, g_ref[...])


def _o_proj(h, o, w, g, layer, tm):
    n, d = h.shape
    row = lambda i: (i, 0)
    return pl.pallas_call(
        _oproj_body,
        grid=(n // tm,),
        in_specs=[pl.BlockSpec((tm, d), row), pl.BlockSpec((tm, d), row),
                  pl.BlockSpec((None, d, d), lambda i: (layer, 0, 0)),
                  pl.BlockSpec((1, d), lambda i: (0, 0))],
        out_specs=pl.BlockSpec((tm, d), row),
        out_shape=jax.ShapeDtypeStruct((n, d), F32),
        compiler_params=_cparams("parallel"),
        name="o_proj",
    )(h, o, w, g)


def _s5gate_body(h_ref, xn_ref, y_ref, d_ref, w_ref, b_ref, g_ref, out_ref):
    y = jax.nn.gelu(y_ref[...] + d_ref[...] * xn_ref[...])
    z = jax.nn.sigmoid(_dot(y.astype(BF16), w_ref[...]) + b_ref[...])
    out_ref[...] = h_ref[...] + _rms(y * z, g_ref[...])


def _s5_gate(h, xn, y, dskip, w, b, g, layer, tm):
    n, d = h.shape
    row = lambda i: (i, 0)
    vec = lambda i: (0, 0)
    return pl.pallas_call(
        _s5gate_body,
        grid=(n // tm,),
        in_specs=[pl.BlockSpec((tm, d), row), pl.BlockSpec((tm, d), row), pl.BlockSpec((tm, d), row),
                  pl.BlockSpec((1, d), vec),
                  pl.BlockSpec((None, d, d), lambda i: (layer, 0, 0)),
                  pl.BlockSpec((1, d), vec), pl.BlockSpec((1, d), vec)],
        out_specs=pl.BlockSpec((tm, d), row),
        out_shape=jax.ShapeDtypeStruct((n, d), F32),
        compiler_params=_cparams("parallel"),
        name="s5_gate",
    )(h, xn, y, dskip, w, b, g)


def _s5_disc(lam_re, lam_im, log_dt):
    dt = jnp.exp(log_dt)
    mag = jnp.exp(lam_re * dt)
    a_re, a_im = mag * jnp.cos(lam_im * dt), mag * jnp.sin(lam_im * dt)
    den = lam_re * lam_re + lam_im * lam_im
    nr, ni = a_re - 1.0, a_im
    r_re = (nr * lam_re + ni * lam_im) / den
    r_im = (ni * lam_re - nr * lam_im) / den
    return a_re, a_im, r_re, r_im


def _cmul(ar, ai, br, bi):
    return ar * br - ai * bi, ar * bi + ai * br


def _bd_rows(m):
    shape = (LANES, S5_STATE_LANES)
    same = _iota(shape, 0) // SSM_GROUP == _iota(shape, 1) // SSM_STATE
    return jnp.where(same, jnp.concatenate([m] * S5_GROUPS_PER_STEP, axis=1), 0.0)


def _bd_cols(m):
    shape = (S5_STATE_LANES, LANES)
    same = _iota(shape, 0) // SSM_STATE == _iota(shape, 1) // SSM_GROUP
    return jnp.where(same, jnp.concatenate([m] * S5_GROUPS_PER_STEP, axis=0), 0.0)


def _s5_input_weights(lam_re, lam_im, log_dt, b_re, b_im):
    a_re, a_im, r_re, r_im = _s5_disc(lam_re, lam_im, log_dt)
    bb_re, bb_im = _cmul(r_re, r_im, b_re, b_im)
    return a_re, a_im, bb_re, bb_im


def _s5_prompt_body(u_ref, lr_re_ref, lr_im_ref, lr_dt_ref, b_re_ref, b_im_ref,
                    lc_re_ref, lc_im_ref, lc_dt_ref, c_re_ref, c_im_ref,
                    lv_re_ref, lv_im_ref, lv_dt_ref,
                    y_ref, st_ref,
                    sin_ref, toep_ref, fout_ref, v_ref, xp_ref, *, n_batch, n_chunk):
    L = S5_CHUNK
    rows = n_batch * n_chunk
    half = S5_STATE_LANES

    a_re, a_im, w_re, w_im = _s5_input_weights(lr_re_ref[...], lr_im_ref[...], lr_dt_ref[...],
                                               b_re_ref[...], b_im_ref[...])
    ws_re, ws_im = [w_re], [w_im]
    for _ in range(L - 1):
        w_re, w_im = _cmul(a_re, a_im, w_re, w_im)
        ws_re.append(w_re)
        ws_im.append(w_im)
    for s in range(L):
        k = L - 1 - s
        sin_ref[s * LANES:(s + 1) * LANES, :] = jnp.concatenate(
            [_bd_rows(ws_re[k]), _bd_rows(ws_im[k])], axis=1).astype(BF16)
    stack_re = jnp.concatenate([ws_re[L - 1 - s] for s in range(L)], axis=0)
    stack_im = jnp.concatenate([ws_im[L - 1 - s] for s in range(L)], axis=0)
    taps = (_dot(stack_re, c_re_ref[...], precision=HIGHEST)
            - _dot(stack_im, c_im_ref[...], precision=HIGHEST))
    tshape = (L * LANES, LANES)
    same = (_iota(tshape, 0) % LANES) // SSM_GROUP == _iota(tshape, 1) // SSM_GROUP
    toep_ref[...] = jnp.where(same, taps, 0.0).astype(BF16)
    ac_re, ac_im, _, _ = _s5_disc(lc_re_ref[...], lc_im_ref[...], lc_dt_ref[...])
    ca_re, ca_im = c_re_ref[...], c_im_ref[...]
    for t in range(L):
        ca_re, ca_im = _cmul(ac_re, ac_im, ca_re, ca_im)
        fout_ref[:, t * LANES:(t + 1) * LANES] = jnp.concatenate(
            [_bd_cols(ca_re), -_bd_cols(ca_im)], axis=0).astype(BF16)
    al_re, al_im, _, _ = _s5_disc(lv_re_ref[...], lv_im_ref[...], lv_dt_ref[...])
    for _ in range(int(math.log2(L))):
        al_re, al_im = _cmul(al_re, al_im, al_re, al_im)

    xb = jnp.concatenate([u_ref[pl.ds(s, rows, stride=L), :] for s in range(L)], axis=1).astype(BF16)
    v_ref[...] = _dot(xb, sin_ref[...])

    def step(c, carry):
        new = []
        for b in range(n_batch):
            r = b * n_chunk + c
            x_re, x_im = carry[b]
            xp_ref[pl.ds(r, 1), :] = jnp.concatenate([x_re, x_im], axis=1)
            inc = v_ref[pl.ds(r, 1), :]
            new.append((al_re * x_re - al_im * x_im + inc[:, :half],
                        al_re * x_im + al_im * x_re + inc[:, half:]))
        return tuple(new)

    zero = jnp.zeros((1, half), F32)
    final = lax.fori_loop(0, n_chunk, step, tuple((zero, zero) for _ in range(n_batch)))
    for b in range(n_batch):
        st_ref[b:b + 1, :] = jnp.concatenate(final[b], axis=1)

    xpb = xp_ref[...].astype(BF16)
    for t in range(L):
        yt = (_dot(xb[:, :(t + 1) * LANES], toep_ref[(L - 1 - t) * LANES:, :])
              + _dot(xpb, fout_ref[:, t * LANES:(t + 1) * LANES]))
        y_ref[pl.ds(t, rows, stride=L), :] = yt


def _s5_layouts(lam_re, lam_im, log_dt, b_re, b_im, c_re, c_im):
    n_g = lam_re.shape[0]
    ldt = jnp.broadcast_to(log_dt[:, None], lam_re.shape)
    rep = lambda a: jnp.repeat(a, SSM_GROUP, axis=0)
    rows = (rep(lam_re), rep(lam_im), rep(ldt),
            b_re.transpose(0, 2, 1).reshape(n_g * SSM_GROUP, SSM_STATE),
            b_im.transpose(0, 2, 1).reshape(n_g * SSM_GROUP, SSM_STATE))
    cols = (rep(lam_re).T, rep(lam_im).T, rep(ldt).T,
            c_re.reshape(n_g * SSM_GROUP, SSM_STATE).T, c_im.reshape(n_g * SSM_GROUP, SSM_STATE).T)
    n_j = n_g // S5_GROUPS_PER_STEP
    lanes = tuple(a.reshape(n_j, 1, S5_STATE_LANES) for a in (lam_re, lam_im, ldt))
    return rows, cols, lanes


def _s5_specs():
    rspec = pl.BlockSpec((LANES, SSM_STATE), lambda j: (j, 0))
    cspec = pl.BlockSpec((SSM_STATE, LANES), lambda j: (0, j))
    vspec = pl.BlockSpec((None, 1, S5_STATE_LANES), lambda j: (j, 0, 0))
    return [rspec] * 5 + [cspec] * 5 + [vspec] * 3


def _s5_prompt(u, n_batch, layouts):
    n, d = u.shape
    n_j = d // LANES
    n_chunk = n // n_batch // S5_CHUNK
    rows = n // S5_CHUNK
    rows_p, cols_p, lanes_p = layouts
    col = lambda j: (0, j)
    return pl.pallas_call(
        functools.partial(_s5_prompt_body, n_batch=n_batch, n_chunk=n_chunk),
        grid=(n_j,),
        in_specs=[pl.BlockSpec((n, LANES), col)] + _s5_specs(),
        out_specs=[pl.BlockSpec((n, LANES), col),
                   pl.BlockSpec((None, n_batch, 2 * S5_STATE_LANES), lambda j: (j, 0, 0))],
        out_shape=[jax.ShapeDtypeStruct((n, d), F32),
                   jax.ShapeDtypeStruct((n_j, n_batch, 2 * S5_STATE_LANES), F32)],
        scratch_shapes=[pltpu.VMEM((S5_CHUNK * LANES, 2 * S5_STATE_LANES), BF16),
                        pltpu.VMEM((S5_CHUNK * LANES, LANES), BF16),
                        pltpu.VMEM((2 * S5_STATE_LANES, S5_CHUNK * LANES), BF16),
                        pltpu.VMEM((rows, 2 * S5_STATE_LANES), F32),
                        pltpu.VMEM((rows, 2 * S5_STATE_LANES), F32)],
        compiler_params=_cparams("parallel"),
        name="s5_prompt",
    )(u, *rows_p, *cols_p, *lanes_p)


def _s5_step_body(u_ref, hre_ref, him_ref, lr_re_ref, lr_im_ref, lr_dt_ref, b_re_ref, b_im_ref,
                  lc_re_ref, lc_im_ref, lc_dt_ref, c_re_ref, c_im_ref,
                  lv_re_ref, lv_im_ref, lv_dt_ref, y_ref, ore_ref, oim_ref):
    _, _, bb_re, bb_im = _s5_input_weights(lr_re_ref[...], lr_im_ref[...], lr_dt_ref[...],
                                           b_re_ref[...], b_im_ref[...])
    a_re, a_im, _, _ = _s5_disc(lv_re_ref[...], lv_im_ref[...], lv_dt_ref[...])
    u = u_ref[...]
    h_re, h_im = hre_ref[...], him_ref[...]
    x_re = _dot(u, _bd_rows(bb_re), precision=HIGHEST) + (a_re * h_re - a_im * h_im)
    x_im = _dot(u, _bd_rows(bb_im), precision=HIGHEST) + (a_re * h_im + a_im * h_re)
    ore_ref[...] = x_re
    oim_ref[...] = x_im
    y_ref[...] = (_dot(x_re, _bd_cols(c_re_ref[...]), precision=HIGHEST)
                  - _dot(x_im, _bd_cols(c_im_ref[...]), precision=HIGHEST))


def _s5_step(u, h_re, h_im, layouts):
    nb, d = u.shape
    rows_p, cols_p, lanes_p = layouts
    col = lambda j: (0, j)
    sspec = pl.BlockSpec((nb, S5_STATE_LANES), col)
    return pl.pallas_call(
        _s5_step_body,
        grid=(d // LANES,),
        in_specs=[pl.BlockSpec((nb, LANES), col), sspec, sspec] + _s5_specs(),
        out_specs=[pl.BlockSpec((nb, LANES), col), sspec, sspec],
        out_shape=[jax.ShapeDtypeStruct((nb, d), F32),
                   jax.ShapeDtypeStruct(h_re.shape, F32), jax.ShapeDtypeStruct(h_im.shape, F32)],
        compiler_params=_cparams("parallel"),
        name="s5_step",
    )(u, h_re, h_im, *rows_p, *cols_p, *lanes_p)


PAGE_ROWS = 128
CHUNKS_PER_PAGE = PAGE_ROWS // CMP_STRIDE
CMP_SLOT_LANES = 2 * N_KV * HEAD_DIM


def _compress_body(pt_ref, src_ref, pek_ref, pev_ref, w1k_ref, w1v_ref, w2k_ref, w2v_ref,
                   kc_ref, vc_ref, buf_ref, shift_ref, *, n_pages):
    del pt_ref
    p = pl.program_id(1)
    r0 = pl.multiple_of(p * PAGE_ROWS, PAGE_ROWS)
    for sg in range(2 * N_KV):
        buf_ref[sg, pl.ds(r0, PAGE_ROWS), :] = src_ref[:, sg * HEAD_DIM:(sg + 1) * HEAD_DIM]

    @pl.when(p == n_pages - 1)
    def _():
        n_chunk = n_pages * CHUNKS_PER_PAGE
        half = CMP_STRIDE
        for slot, (pe_ref, w1_ref, w2_ref, out_ref) in enumerate(
                ((pek_ref, w1k_ref, w2k_ref, kc_ref), (pev_ref, w1v_ref, w2v_ref, vc_ref))):
            for g in range(N_KV):
                sg = slot * N_KV + g
                first = jnp.zeros((n_chunk, HEAD_DIM), F32)
                second = jnp.zeros((n_chunk, HEAD_DIM), F32)
                for l in range(half):
                    x = buf_ref[sg, pl.ds(l, n_chunk, stride=CMP_STRIDE), :]
                    xa = (x + pe_ref[l:l + 1, :]).astype(BF16)
                    xb = (x + pe_ref[half + l:half + l + 1, :]).astype(BF16)
                    first += _dot(xa, w1_ref[l * HEAD_DIM:(l + 1) * HEAD_DIM, :])
                    second += _dot(xb, w1_ref[(half + l) * HEAD_DIM:(half + l + 1) * HEAD_DIM, :])
                shift_ref[0:n_chunk, :] = second
                shift_ref[n_chunk:n_chunk + 8, :] = jnp.zeros((8, HEAD_DIM), F32)
                hid = first + shift_ref[1:n_chunk + 1, :]
                out = _dot(jax.nn.silu(hid).astype(BF16), w2_ref[...])
                live = _iota(out.shape, 0) < n_chunk - 1
                out_ref[g] = jnp.where(live, out, 0.0).astype(BF16)


def _compress(src, page_ids, pe_k, pe_v, w1k, w1v, w2k, w2v):
    nb, n_pages = page_ids.shape
    n_chunk = n_pages * CHUNKS_PER_PAGE
    const = lambda b, p, pt: (0, 0)
    out_spec = pl.BlockSpec((None, N_KV, n_chunk, HEAD_DIM), lambda b, p, pt: (b, 0, 0, 0))
    grid_spec = pltpu.PrefetchScalarGridSpec(
        num_scalar_prefetch=1,
        grid=(nb, n_pages),
        in_specs=[pl.BlockSpec((None, PAGE_ROWS, CMP_SLOT_LANES),
                               lambda b, p, pt: (pt[b * n_pages + p], 0, 0)),
                  pl.BlockSpec(pe_k.shape, const), pl.BlockSpec(pe_v.shape, const),
                  pl.BlockSpec(w1k.shape, const), pl.BlockSpec(w1v.shape, const),
                  pl.BlockSpec(w2k.shape, const), pl.BlockSpec(w2v.shape, const)],
        out_specs=[out_spec, out_spec],
        scratch_shapes=[pltpu.VMEM((2 * N_KV, n_pages * PAGE_ROWS, HEAD_DIM), F32),
                        pltpu.VMEM((n_chunk + 8, HEAD_DIM), F32)])
    return pl.pallas_call(
        functools.partial(_compress_body, n_pages=n_pages),
        grid_spec=grid_spec,
        out_shape=[jax.ShapeDtypeStruct((nb, N_KV, n_chunk, HEAD_DIM), BF16)] * 2,
        compiler_params=_cparams("parallel", "arbitrary"),
        name="compress",
    )(page_ids.reshape(-1), src, pe_k, pe_v, w1k, w1v, w2k, w2v)


def _masked_softmax(s, mask):
    s = jnp.where(mask, s, NEG)
    m = jnp.max(s, axis=-1, keepdims=True)
    e = jnp.where(mask, jnp.exp(s - m), 0.0)
    return e / jnp.maximum(jnp.sum(e, axis=-1, keepdims=True), 1e-30)


def _overlap(n_rows, n_lanes, n_cmp, n_sel):
    ci, sj = _iota((n_rows, n_lanes), 0), _iota((n_rows, n_lanes), 1)
    hit = ((ci * CMP_STRIDE < sj * SEL_LEN + SEL_LEN) & (ci * CMP_STRIDE + CMP_LEN > sj * SEL_LEN)
           & (ci < n_cmp) & (sj < n_sel))
    return jnp.where(hit, 1.0, 0.0).astype(BF16)


def _importance(imp, lane, qpos, n_sel):
    cur = qpos // SEL_LEN
    forced = (lane == 0) | (lane == cur) | (lane == cur - 1)
    imp = jnp.where(lane * SEL_LEN <= qpos, imp + jnp.where(forced, FORCE_BONUS, 0.0), NEG)
    return jnp.where(lane < n_sel, imp, LOWEST)


def _online_attend(q, k_ref, v_ref, lo, hi, ck, mask_fn):
    rows = q.shape[0]

    def body(c, carry):
        m, l, acc = carry
        k0 = pl.multiple_of(c * ck, ck)
        k = k_ref[pl.ds(k0, ck), :].astype(BF16)
        v = v_ref[pl.ds(k0, ck), :].astype(BF16)
        mask = mask_fn(k0)
        s = jnp.where(mask, _dot_nt(q, k) * ATT_SCALE, NEG)
        m_new = jnp.maximum(m, jnp.max(s, axis=-1, keepdims=True))
        alpha = jnp.exp(m - m_new)
        e = jnp.where(mask, jnp.exp(s - m_new), 0.0)
        l = alpha * l + jnp.sum(e, axis=-1, keepdims=True)
        acc = alpha * acc + _dot(e.astype(BF16), v)
        return m_new, l, acc

    init = (jnp.full((rows, 1), NEG, F32), jnp.zeros((rows, 1), F32), jnp.zeros((rows, HEAD_DIM), F32))
    _, l, acc = lax.fori_loop(lo, hi, body, init)
    return acc / jnp.maximum(l, 1e-30)


Q_BLOCK = 128
SEL_KEY_CHUNK = 256
WIN_KEY_CHUNK = 128


def _nsa_prompt_body(qc_ref, qr_ref, gate_ref, ksel_ref, vsel_ref, kwin_ref, vwin_ref,
                     kcmp_ref, vcmp_ref, o_ref, *, n_cmp, n_sel):
    qi = pl.program_id(2)
    q0 = qi * Q_BLOCK
    nh = HEADS_PER_KV
    stack = lambda ref: jnp.concatenate(
        [ref[:, hd * HEAD_DIM:(hd + 1) * HEAD_DIM] for hd in range(nh)], axis=0)
    per_head = lambda x: jnp.concatenate([x] * nh, axis=0)
    qc, qr = stack(qc_ref), stack(qr_ref)
    sq = (Q_BLOCK, LANES)
    qpos = q0 + _iota(sq, 0)
    lane = _iota(sq, 1)

    cmask = per_head(jnp.where((lane < n_cmp) & (lane * CMP_STRIDE + CMP_LEN - 1 <= qpos), 1.0, 0.0)) > 0.5
    p_cmp = _masked_softmax(_dot_nt(qc, kcmp_ref[...]) * ATT_SCALE, cmask).astype(BF16)
    o_cmp = _dot(p_cmp, vcmp_ref[...])

    imp_h = _dot(p_cmp, _overlap(LANES, LANES, n_cmp, n_sel))
    imp = imp_h[0:Q_BLOCK]
    for hd in range(1, nh):
        imp = imp + imp_h[hd * Q_BLOCK:(hd + 1) * Q_BLOCK]
    imp = _importance(imp, lane, qpos, n_sel)
    rank = jnp.zeros(sq, F32)
    for k in range(n_sel):
        col = imp[:, k:k + 1]
        rank += jnp.where((col > imp) | ((col == imp) & (lane > k)), 1.0, 0.0)
    sel = jnp.where((rank < min(SEL_TOP, n_sel)) & (lane < n_sel), 1.0, 0.0).astype(BF16)

    def sel_mask(k0):
        shape = (Q_BLOCK, SEL_KEY_CHUNK)
        kpos = k0 + _iota(shape, 1)
        expand = jnp.where((k0 + _iota((LANES, SEL_KEY_CHUNK), 1)) // SEL_LEN == _iota((LANES, SEL_KEY_CHUNK), 0),
                           1.0, 0.0).astype(BF16)
        chosen = _dot(sel, expand)
        ok = (chosen > 0.5) & (kpos <= q0 + _iota(shape, 0))
        return per_head(jnp.where(ok, 1.0, 0.0)) > 0.5

    def win_mask(k0):
        shape = (Q_BLOCK, WIN_KEY_CHUNK)
        kpos = k0 + _iota(shape, 1)
        qp = q0 + _iota(shape, 0)
        ok = (kpos <= qp) & (kpos > qp - WINDOW)
        return per_head(jnp.where(ok, 1.0, 0.0)) > 0.5

    sel_chunks = (q0 + Q_BLOCK + SEL_KEY_CHUNK - 1) // SEL_KEY_CHUNK
    o_sel = _online_attend(qr, ksel_ref, vsel_ref, 0, sel_chunks, SEL_KEY_CHUNK, sel_mask)
    win_lo = jnp.maximum(qi - WINDOW // WIN_KEY_CHUNK, 0)
    o_win = _online_attend(qr, kwin_ref, vwin_ref, win_lo, qi + 1, WIN_KEY_CHUNK, win_mask)

    gates = gate_ref[...]
    for hd in range(nh):
        rows = slice(hd * Q_BLOCK, (hd + 1) * Q_BLOCK)
        c0 = hd * N_BRANCH
        o = (o_cmp[rows] * gates[:, c0:c0 + 1] + o_sel[rows] * gates[:, c0 + 1:c0 + 2]
             + o_win[rows] * gates[:, c0 + 2:c0 + 3])
        o_ref[:, hd * HEAD_DIM:(hd + 1) * HEAD_DIM] = o.astype(BF16)


def _nsa_prompt(qc, qr, gates, kv, kcmp, vcmp):
    bsz, t, dq = qc.shape
    n_cmp = (t - CMP_LEN) // CMP_STRIDE + 1
    n_sel = -(-t // SEL_LEN)
    gw = HEADS_PER_KV * HEAD_DIM
    qspec = pl.BlockSpec((None, Q_BLOCK, gw), lambda b, g, i: (b, i, g))
    kvspec = lambda slot: pl.BlockSpec((None, t, HEAD_DIM), lambda b, g, i: (b, 0, slot * N_KV + g))
    cspec = pl.BlockSpec((None, None, kcmp.shape[2], HEAD_DIM), lambda b, g, i: (b, g, 0, 0))
    return pl.pallas_call(
        functools.partial(_nsa_prompt_body, n_cmp=n_cmp, n_sel=n_sel),
        grid=(bsz, N_KV, t // Q_BLOCK),
        in_specs=[qspec, qspec, pl.BlockSpec((None, Q_BLOCK, LANES), lambda b, g, i: (b, i, g)),
                  kvspec(2), kvspec(3), kvspec(4), kvspec(5), cspec, cspec],
        out_specs=qspec,
        out_shape=jax.ShapeDtypeStruct((bsz, t, dq), BF16),
        compiler_params=_cparams("parallel", "parallel", "arbitrary"),
        name="nsa_prompt",
    )(qc, qr, gates, kv, kv, kv, kv, kcmp, vcmp)


SEL_LANES = 384


def _select_body(qc_ref, kcmp_ref, vcmp_ref, ocmp_ref, idx_ref, *, qpos, n_cmp, n_sel):
    q = qc_ref[...]
    nc = kcmp_ref.shape[0]
    ci = _iota((HEADS_PER_KV, nc), 1)
    cmask = (ci < n_cmp) & (ci * CMP_STRIDE + CMP_LEN - 1 <= qpos)
    p_cmp = _masked_softmax(_dot_nt(q, kcmp_ref[...]) * ATT_SCALE, cmask).astype(BF16)
    ocmp_ref[...] = _dot(p_cmp, vcmp_ref[...])

    imp = jnp.sum(_dot(p_cmp, _overlap(nc, SEL_LANES, n_cmp, n_sel)), axis=0, keepdims=True)
    lane = _iota((1, SEL_LANES), 1)
    imp = _importance(imp, lane, qpos, n_sel)
    sq = (SEL_LANES, SEL_LANES)
    ri, cj = _iota(sq, 0), _iota(sq, 1)
    impc = jnp.sum(jnp.where(ri == cj, imp, 0.0), axis=1, keepdims=True)
    n_top = min(SEL_TOP, n_sel)
    before_col = jnp.where((impc > imp) | ((impc == imp) & (ri < cj)), 1.0, 0.0)
    before_row = jnp.where((imp > impc) | ((imp == impc) & (cj < ri)), 1.0, 0.0)
    sel_lane = (jnp.sum(before_col, axis=0, keepdims=True) < n_top) & (lane < n_sel)
    sel_row = (jnp.sum(before_row, axis=1, keepdims=True) < n_top) & (_iota((SEL_LANES, 1), 0) < n_sel)
    slot_of = jnp.sum(jnp.where((ri < cj) & sel_row, 1.0, 0.0), axis=0, keepdims=True)
    lshape = (SEL_TOP, SEL_LANES)
    onehot = jnp.where(sel_lane & (slot_of == _iota(lshape, 0).astype(F32)), 1.0, 0.0)
    idx = jnp.sum(onehot * _iota(lshape, 1).astype(F32), axis=1, keepdims=True)
    idx_ref[...] = jnp.broadcast_to(idx, (SEL_TOP, LANES)).astype(jnp.int32)


def _nsa_select(qc, kcmp, vcmp, qpos, n_cmp, n_sel):
    nb = qc.shape[0]
    nc = kcmp.shape[2]
    hspec = pl.BlockSpec((None, None, HEADS_PER_KV, HEAD_DIM), lambda b, g: (b, g, 0, 0))
    cspec = pl.BlockSpec((None, None, nc, HEAD_DIM), lambda b, g: (b, g, 0, 0))
    return pl.pallas_call(
        functools.partial(_select_body, qpos=qpos, n_cmp=n_cmp, n_sel=n_sel),
        grid=(nb, N_KV),
        in_specs=[hspec, cspec, cspec],
        out_specs=[hspec, pl.BlockSpec((None, None, SEL_TOP, LANES), lambda b, g: (b, g, 0, 0))],
        out_shape=[jax.ShapeDtypeStruct((nb, N_KV, HEADS_PER_KV, HEAD_DIM), F32),
                   jax.ShapeDtypeStruct((nb, N_KV, SEL_TOP, LANES), jnp.int32)],
        compiler_params=_cparams("parallel", "parallel"),
        name="nsa_select",
    )(qc, kcmp, vcmp)


def _attend_plus_new(q, k, v, valid, k_new, v_new, new_ok):
    qf = q.astype(F32)
    s = _dot_nt(q, k.astype(BF16)) * ATT_SCALE
    s_new = jnp.sum(qf * k_new.astype(BF16).astype(F32), axis=1, keepdims=True) * ATT_SCALE
    s = jnp.where(valid, s, NEG)
    s_new = jnp.where(new_ok, s_new, NEG)
    m = jnp.maximum(jnp.max(s, axis=-1, keepdims=True), s_new)
    e = jnp.where(valid, jnp.exp(s - m), 0.0)
    e_new = jnp.where(new_ok, jnp.exp(s_new - m), 0.0)
    den = jnp.maximum(jnp.sum(e, axis=-1, keepdims=True) + e_new, 1e-30)
    p = (e / den).astype(BF16)
    p_new = (e_new / den).astype(BF16).astype(F32)
    return _dot(p, v.astype(BF16)) + p_new * v_new.astype(BF16).astype(F32)


def _attend_body(idx_ref, pt_ref, qr_ref, gate_ref, ocmp_ref, ksel_ref, vsel_ref, kwin_ref, vwin_ref,
                 new_ref, o_ref, kbuf_ref, vbuf_ref, *, qpos, past_len):
    del pt_ref
    b, g, s = pl.program_id(0), pl.program_id(1), pl.program_id(2)
    r0 = pl.multiple_of(s * SEL_LEN, SEL_LEN)
    kbuf_ref[pl.ds(r0, SEL_LEN), :] = ksel_ref[...]
    vbuf_ref[pl.ds(r0, SEL_LEN), :] = vsel_ref[...]

    @pl.when(s == SEL_TOP - 1)
    def _():
        q = qr_ref[...]
        n_keys = SEL_TOP * SEL_LEN
        base = (b * N_KV + g) * SEL_TOP
        lane = _iota((1, n_keys), 1)
        kpos = jnp.zeros((1, n_keys), jnp.int32)
        n_new = jnp.int32(0)
        for t in range(SEL_TOP):
            j = idx_ref[base + t]
            kpos = jnp.where(lane // SEL_LEN == t, j * SEL_LEN + lane % SEL_LEN, kpos)
            n_new = n_new + jnp.where(j == qpos // SEL_LEN, 1, 0)
        valid = (kpos <= qpos) & (kpos < past_len)
        new_ok = (jnp.zeros((HEADS_PER_KV, 1), jnp.int32) + n_new) > 0
        pick = lambda slot: new_ref[pl.ds(slot * N_KV + g, 1), :]
        o_sel = _attend_plus_new(q, kbuf_ref[...], vbuf_ref[...], valid, pick(2), pick(3), new_ok)

        n_win = kwin_ref.shape[0]
        wpos = (past_len - n_win) + _iota((1, n_win), 1)
        wvalid = (wpos <= qpos) & (wpos > qpos - WINDOW) & (wpos >= 0)
        always = jnp.zeros((HEADS_PER_KV, 1), jnp.int32) == 0
        o_win = _attend_plus_new(q, kwin_ref[...], vwin_ref[...], wvalid, pick(4), pick(5), always)

        gates = gate_ref[...]
        o_ref[...] = ocmp_ref[...] * gates[:, 0:1] + o_sel * gates[:, 1:2] + o_win * gates[:, 2:3]


def _nsa_attend(idx, page_table, qr, gates, ocmp, cache, cache_win, kv_new, past_len):
    nb, n_pages = page_table.shape
    n_win = cache_win.shape[1]
    halves = PAGE_ROWS // SEL_LEN
    n_past_blocks = n_pages * halves

    def sel_block(slot):
        def index(b, g, s, idx_ref, pt_ref):
            j = jnp.minimum(idx_ref[(b * N_KV + g) * SEL_TOP + s], n_past_blocks - 1)
            return (pt_ref[b * n_pages + j // halves] * halves + j % halves, 0, slot * N_KV + g)
        return pl.BlockSpec((None, SEL_LEN, HEAD_DIM), index)

    hspec = pl.BlockSpec((None, None, HEADS_PER_KV, HEAD_DIM), lambda b, g, s, i, p: (b, g, 0, 0))
    wspec = lambda slot: pl.BlockSpec((None, n_win, HEAD_DIM), lambda b, g, s, i, p: (b, 0, slot * N_KV + g))
    grid_spec = pltpu.PrefetchScalarGridSpec(
        num_scalar_prefetch=2,
        grid=(nb, N_KV, SEL_TOP),
        in_specs=[hspec, hspec, hspec, sel_block(2), sel_block(3), wspec(0), wspec(1),
                  pl.BlockSpec((None, N_KV_SLOTS * N_KV, HEAD_DIM), lambda b, g, s, i, p: (b, 0, 0))],
        out_specs=hspec,
        scratch_shapes=[pltpu.VMEM((SEL_TOP * SEL_LEN, HEAD_DIM), F32)] * 2)
    return pl.pallas_call(
        functools.partial(_attend_body, qpos=past_len, past_len=past_len),
        grid_spec=grid_spec,
        out_shape=jax.ShapeDtypeStruct((nb, N_KV, HEADS_PER_KV, HEAD_DIM), F32),
        compiler_params=_cparams("parallel", "parallel", "arbitrary"),
        name="nsa_attend",
    )(idx, page_table.reshape(-1), qr, gates, ocmp, cache, cache, cache_win, cache_win, kv_new)


PROMPT_FFN_ROWS = 512
PROMPT_ROWS = 256


def _rope_tables(pos):
    half = HEAD_DIM // 2
    inv = jnp.exp(-math.log(ROPE_THETA) * jnp.arange(half, dtype=F32) / half)
    ang = pos.astype(F32)[:, None] * inv[None, :]
    cos, sin = jnp.cos(ang), jnp.sin(ang)
    return jnp.concatenate([cos, cos], axis=1), jnp.concatenate([-sin, sin], axis=1)


def _prepare_weights(prm):
    d_ff = prm["ffn_w_gate"].shape[-1]
    pad = -d_ff % FFN_TILE_F
    nq = N_HEADS * HEAD_DIM
    w_qg = prm["attn_w_qg"]
    nb, d = w_qg.shape[:2]
    per_kv = HEADS_PER_KV * N_BRANCH
    w_gate = w_qg[:, :, nq:].reshape(nb, d, N_KV, per_kv)
    w_gate = jnp.pad(w_gate, ((0, 0), (0, 0), (0, 0), (0, LANES - per_kv))).reshape(nb, d, N_KV * LANES)
    return dict(
        wg=jnp.pad(prm["ffn_w_gate"].astype(BF16), ((0, 0), (0, 0), (0, 0), (0, pad))),
        wu=jnp.pad(prm["ffn_w_up"].astype(BF16), ((0, 0), (0, 0), (0, 0), (0, pad))),
        wd=jnp.pad(prm["ffn_w_down"].astype(BF16), ((0, 0), (0, 0), (0, pad), (0, 0))),
        ple_gate=prm["ple_w_gate"].astype(BF16), ple_proj=prm["ple_w_proj"].astype(BF16),
        glu=prm["glu_w"].astype(BF16), kv=prm["w_kv"].astype(BF16),
        q=w_qg[:, :, :nq].astype(BF16), gate=w_gate.astype(BF16), o=prm["attn_w_o"].astype(BF16),
        w1k=prm["cmp_w1_k"].astype(BF16), w1v=prm["cmp_w1_v"].astype(BF16),
        w2k=prm["cmp_w2_k"].astype(BF16), w2v=prm["cmp_w2_v"].astype(BF16),
        s5=[_s5_layouts(prm["ssm_lam_re"][i], prm["ssm_lam_im"][i], prm["ssm_log_dt"][i],
                        prm["ssm_b_re"][i], prm["ssm_b_im"][i], prm["ssm_c_re"][i], prm["ssm_c_im"][i])
            for i in range(N_A_LAYERS)])


def _run_group(x, p, start, state, past, prm, w):
    bsz, t, d = x.shape
    n = bsz * t
    prompt = past is None
    tm_ffn = PROMPT_FFN_ROWS if prompt else n
    tm = PROMPT_ROWS if prompt else n
    depth = prm["norm_g"].shape[0]
    cos, sin = _rope_tables(start + jnp.arange(t))
    if not prompt:
        cos, sin = jnp.broadcast_to(cos, (n, HEAD_DIM)), jnp.broadcast_to(sin, (n, HEAD_DIM))
    h = x.reshape(n, d)
    ssm_re, ssm_im = [], []
    rows = win = kv = kcmp = vcmp = None
    for i in range(depth):
        g = prm["norm_g"][i]
        h, xn = _ffn(h, g[0:1], g[1:2], g[2:3], w["wg"], w["wu"], w["wd"], i, 0, tm_ffn)
        if i < N_A_LAYERS:
            if prompt:
                y, st = _s5_prompt(xn, bsz, w["s5"][i])
                st = st.reshape(d // LANES, bsz, 2, S5_GROUPS_PER_STEP, SSM_STATE).transpose(2, 1, 0, 3, 4)
                st = st.reshape(2, bsz, d // SSM_GROUP, SSM_STATE)
                ssm_re.append(st[0])
                ssm_im.append(st[1])
            else:
                y, s_re, s_im = _s5_step(xn, state[0][i].reshape(bsz, -1), state[1][i].reshape(bsz, -1),
                                         w["s5"][i])
                ssm_re.append(s_re.reshape(state[0][i].shape))
                ssm_im.append(s_im.reshape(state[1][i].shape))
            h = _s5_gate(h, xn, y, prm["ssm_d"][i][None], w["glu"], prm["glu_b"][i][None], g[3:4], i, tm)
        else:
            j = i - N_A_LAYERS
            qc, qr, gates = _qg_proj(xn, w["q"], w["gate"], cos, sin, j, tm)
            if prompt:
                o = _nsa_prompt(qc.reshape(bsz, t, -1), qr.reshape(bsz, t, -1), gates.reshape(bsz, t, -1),
                                kv.reshape(bsz, t, -1), kcmp, vcmp).reshape(n, -1)
            else:
                cache_kv, cache_win, page_table = past
                past_len = page_table.shape[1] * cache_kv.shape[1]
                heads = lambda a: a.reshape(bsz, N_KV, HEADS_PER_KV, HEAD_DIM)
                n_cmp = (past_len + t - CMP_LEN) // CMP_STRIDE + 1
                n_sel = -(-(past_len + t) // SEL_LEN)
                ocmp, idx = _nsa_select(heads(qc), kcmp, vcmp, past_len, n_cmp, n_sel)
                gsm = gates.reshape(bsz, N_KV, LANES)[:, :, :HEADS_PER_KV * N_BRANCH]
                gsm = gsm.reshape(bsz, N_KV, HEADS_PER_KV, N_BRANCH)
                gsm = jnp.pad(gsm, ((0, 0), (0, 0), (0, 0), (0, LANES - N_BRANCH)))
                o = _nsa_attend(idx[..., 0].reshape(-1), page_table, heads(qr), gsm, ocmp,
                                cache_kv.reshape(cache_kv.shape[0] * (PAGE_ROWS // SEL_LEN), SEL_LEN, -1),
                                cache_win.reshape(bsz, cache_win.shape[1], -1),
                                kv.reshape(bsz, N_KV_SLOTS * N_KV, HEAD_DIM), past_len).reshape(n, -1)
            h = _o_proj(h, o, w["o"], g[3:4], j, tm)
        h, _ = _ffn(h, g[4:5], g[5:6], g[6:7], w["wg"], w["wu"], w["wd"], i, 1, tm_ffn)
        h = _ple(h, p[i].reshape(n, -1), w["ple_gate"], w["ple_proj"], g[6:7], i, tm)
        if i == N_A_LAYERS - 1:
            kv = _kv_proj(h, prm["kv_norm_g"][None], w["kv"], cos, sin, tm)
            n_row = 4 * N_KV * HEAD_DIM
            rows = kv[:, :n_row].reshape(bsz, t, 4, N_KV, HEAD_DIM)
            win = kv[:, n_row:].reshape(bsz, t, 2, N_KV, HEAD_DIM)
            cmp_w = (prm["cmp_pe_k"], prm["cmp_pe_v"], w["w1k"], w["w1v"], w["w2k"], w["w2v"])
            if prompt:
                pages = jnp.arange(n // PAGE_ROWS, dtype=jnp.int32).reshape(bsz, t // PAGE_ROWS)
                kcmp, vcmp = _compress(kv.reshape(n // PAGE_ROWS, PAGE_ROWS, -1), pages, *cmp_w)
                win = win[:, t - min(WINDOW, t):]
            else:
                cache_kv, cache_win, page_table = past
                kcmp, vcmp = _compress(cache_kv.reshape(cache_kv.shape[0], PAGE_ROWS, -1), page_table, *cmp_w)
                win = jnp.concatenate([cache_win, win], axis=1)[:, t:]
    return h.reshape(bsz, t, d), jnp.stack(ssm_re), jnp.stack(ssm_im), rows, win


def kernel(x_prompt, x_sample, state_ssm_re, state_ssm_im, cache_kv, cache_win, page_table,
           p_prompt, p_sample, norm_g, ffn_w_gate, ffn_w_up, ffn_w_down, ple_w_proj, ple_w_gate,
           ssm_lam_re, ssm_lam_im, ssm_log_dt, ssm_b_re, ssm_b_im, ssm_c_re, ssm_c_im, ssm_d,
           glu_w, glu_b, kv_norm_g, w_kv, cmp_pe_k, cmp_pe_v, cmp_w1_k, cmp_w2_k, cmp_w1_v,
           cmp_w2_v, attn_w_qg, attn_w_o):
    prm = dict(norm_g=norm_g, ffn_w_gate=ffn_w_gate, ffn_w_up=ffn_w_up, ffn_w_down=ffn_w_down,
               ple_w_proj=ple_w_proj, ple_w_gate=ple_w_gate, ssm_lam_re=ssm_lam_re,
               ssm_lam_im=ssm_lam_im, ssm_log_dt=ssm_log_dt, ssm_b_re=ssm_b_re, ssm_b_im=ssm_b_im,
               ssm_c_re=ssm_c_re, ssm_c_im=ssm_c_im, ssm_d=ssm_d, glu_w=glu_w, glu_b=glu_b,
               kv_norm_g=kv_norm_g, w_kv=w_kv, cmp_pe_k=cmp_pe_k, cmp_pe_v=cmp_pe_v,
               cmp_w1_k=cmp_w1_k, cmp_w2_k=cmp_w2_k, cmp_w1_v=cmp_w1_v, cmp_w2_v=cmp_w2_v,
               attn_w_qg=attn_w_qg, attn_w_o=attn_w_o)
    assert x_sample.shape[1] == 1, "the decode group handles one new token per sequence"
    assert cache_kv.shape[1] == PAGE_ROWS
    w = _prepare_weights(prm)
    y_p, re_p, im_p, kv_p, win_p = _run_group(x_prompt, p_prompt, 0, None, None, prm, w)
    past_len = page_table.shape[1] * cache_kv.shape[1]
    y_s, re_s, im_s, kv_s, win_s = _run_group(x_sample, p_sample, past_len, (state_ssm_re, state_ssm_im),
                                              (cache_kv, cache_win, page_table), prm, w)
    return (y_p, y_s, re_p, im_p, re_s, im_s, kv_p, kv_s, win_p, win_s)
```

```python
import functools
import math

import jax
import jax.numpy as jnp
from jax import lax
from jax.experimental import pallas as pl
from jax.experimental.pallas import tpu as pltpu

F32 = jnp.float32
BF16 = jnp.bfloat16
HIGHEST = lax.Precision.HIGHEST

N_A_LAYERS = 2
NORM_EPS = 1e-6
SSM_GROUP = 16
SSM_STATE = 64
HEAD_DIM = 128
N_KV = 2
HEADS_PER_KV = 8
N_HEADS = N_KV * HEADS_PER_KV
N_BRANCH = 3
N_KV_SLOTS = 6
CMP_LEN = 32
CMP_STRIDE = 16
SEL_LEN = 64
SEL_TOP = 16
WINDOW = 512
ROPE_THETA = 10000.0
FORCE_BONUS = 1e3
NEG = -1e30
LOWEST = -3.0e38
ATT_SCALE = HEAD_DIM ** -0.5

LANES = 128
VMEM_LIMIT_BYTES = 56 * 2 ** 20
FFN_TILE_F = 512
S5_CHUNK = 16
S5_GROUPS_PER_STEP = LANES // SSM_GROUP
S5_STATE_LANES = S5_GROUPS_PER_STEP * SSM_STATE


def _cparams(*sem):
    return pltpu.CompilerParams(dimension_semantics=sem, vmem_limit_bytes=VMEM_LIMIT_BYTES)


def _rms(x, g):
    y = x * lax.rsqrt(jnp.mean(x * x, axis=-1, keepdims=True) + NORM_EPS)
    return y * g


def _dot(a, b, **kw):
    return jnp.dot(a, b, preferred_element_type=F32, **kw)


def _dot_nt(a, b):
    return lax.dot_general(a, b, (((1,), (1,)), ((), ())), preferred_element_type=F32)


def _rope(x, cos, sin):
    return x * cos + pltpu.roll(x, HEAD_DIM // 2, 1) * sin


def _iota(shape, axis):
    return lax.broadcasted_iota(jnp.int32, shape, axis)


def _ffn_body(h_ref, gpre_ref, gpost_ref, gnext_ref, wg_ref, wu_ref, wd_ref, o_ref, xn_out_ref,
              xn_ref, acc_ref, *, nk):
    k = pl.program_id(1)

    @pl.when(k == 0)
    def _():
        xn_ref[...] = _rms(h_ref[...], gpre_ref[...]).astype(BF16)
        acc_ref[...] = jnp.zeros_like(acc_ref)

    xn = xn_ref[...]
    gate = _dot(xn, wg_ref[...])
    up = _dot(xn, wu_ref[...])
    mid = (jax.nn.silu(gate) * up).astype(BF16)
    acc_ref[...] += _dot(mid, wd_ref[...])

    @pl.when(k == nk - 1)
    def _():
        h_new = h_ref[...] + 0.5 * _rms(acc_ref[...], gpost_ref[...])
        o_ref[...] = h_new
        xn_out_ref[...] = _rms(h_new, gnext_ref[...])


def _ffn(h, g_pre, g_post, g_next, wg, wu, wd, layer, which, tm):
    n, d = h.shape
    nk = wg.shape[-1] // FFN_TILE_F
    row = lambda i, k: (i, 0)
    vec = lambda i, k: (0, 0)
    return pl.pallas_call(
        functools.partial(_ffn_body, nk=nk),
        grid=(n // tm, nk),
        in_specs=[pl.BlockSpec((tm, d), row), pl.BlockSpec((1, d), vec), pl.BlockSpec((1, d), vec),
                  pl.BlockSpec((1, d), vec),
                  pl.BlockSpec((None, None, d, FFN_TILE_F), lambda i, k: (layer, which, 0, k)),
                  pl.BlockSpec((None, None, d, FFN_TILE_F), lambda i, k: (layer, which, 0, k)),
                  pl.BlockSpec((None, None, FFN_TILE_F, d), lambda i, k: (layer, which, k, 0))],
        out_specs=[pl.BlockSpec((tm, d), row), pl.BlockSpec((tm, d), row)],
        out_shape=[jax.ShapeDtypeStruct((n, d), F32)] * 2,
        scratch_shapes=[pltpu.VMEM((tm, d), BF16), pltpu.VMEM((tm, d), F32)],
        compiler_params=_cparams("parallel", "arbitrary"),
        name="ffn",
    )(h, g_pre, g_post, g_next, wg, wu, wd)


def _ple_body(h_ref, p_ref, wgate_ref, wproj_ref, g_ref, o_ref):
    h = h_ref[...]
    gate = jax.nn.sigmoid(_dot(h.astype(BF16), wgate_ref[...]))
    proj = _dot(p_ref[...].astype(BF16), wproj_ref[...])
    o_ref[...] = h + _rms(gate * proj, g_ref[...])


def _ple(h, p, wgate, wproj, g, layer, tm):
    n, d = h.shape
    pd = p.shape[-1]
    row = lambda i: (i, 0)
    return pl.pallas_call(
        _ple_body,
        grid=(n // tm,),
        in_specs=[pl.BlockSpec((tm, d), row), pl.BlockSpec((tm, pd), row),
                  pl.BlockSpec((None, d, d), lambda i: (layer, 0, 0)),
                  pl.BlockSpec((None, pd, d), lambda i: (layer, 0, 0)),
                  pl.BlockSpec((1, d), lambda i: (0, 0))],
        out_specs=pl.BlockSpec((tm, d), row),
        out_shape=jax.ShapeDtypeStruct((n, d), F32),
        compiler_params=_cparams("parallel"),
        name="ple",
    )(h, p, wgate, wproj, g)


ROW_SLOTS = 4
WIN_SLOTS = N_KV_SLOTS - ROW_SLOTS
ROW_LINES = ROW_SLOTS * N_KV
WIN_LINES = WIN_SLOTS * N_KV


def _kv_body(h_ref, g_ref, w_ref, cos_ref, sin_ref, rows_ref, win_ref, kv16_ref, *, tm):
    xn = _rms(h_ref[...], g_ref[...]).astype(BF16)
    kv = _dot(xn, w_ref[...])
    cos, sin = cos_ref[...], sin_ref[...]
    for slot in range(N_KV_SLOTS):
        for g in range(N_KV):
            line = slot * N_KV + g
            lo = line * HEAD_DIM
            x = kv[:, lo:lo + HEAD_DIM]
            if slot in (2, 4):
                x = _rope(x, cos, sin)
            kv16_ref[:, lo:lo + HEAD_DIM] = x.astype(BF16)
            if slot < ROW_SLOTS:
                rows_ref[pl.ds(line, tm, stride=ROW_LINES), :] = x
            else:
                win_ref[pl.ds(line - ROW_LINES, tm, stride=WIN_LINES), :] = x


def _kv_proj(h, g, w, cos, sin, tm):
    n, d = h.shape
    nw = w.shape[-1]
    nt = cos.shape[0] // tm
    row = lambda i: (i, 0)
    tab = lambda i: (i % nt, 0)
    return pl.pallas_call(
        functools.partial(_kv_body, tm=tm),
        grid=(n // tm,),
        in_specs=[pl.BlockSpec((tm, d), row), pl.BlockSpec((1, d), lambda i: (0, 0)),
                  pl.BlockSpec((d, nw), lambda i: (0, 0)),
                  pl.BlockSpec((tm, HEAD_DIM), tab), pl.BlockSpec((tm, HEAD_DIM), tab)],
        out_specs=[pl.BlockSpec((tm * ROW_LINES, HEAD_DIM), row),
                   pl.BlockSpec((tm * WIN_LINES, HEAD_DIM), row),
                   pl.BlockSpec((tm, nw), row)],
        out_shape=[jax.ShapeDtypeStruct((n * ROW_LINES, HEAD_DIM), F32),
                   jax.ShapeDtypeStruct((n * WIN_LINES, HEAD_DIM), F32),
                   jax.ShapeDtypeStruct((n, nw), BF16)],
        compiler_params=_cparams("parallel"),
        name="kv_proj",
    )(h, g, w, cos, sin)


def _qg_body(xn_ref, wq_ref, wgt_ref, cos_ref, sin_ref, qc_ref, qr_ref, gate_ref):
    xn = xn_ref[...].astype(BF16)
    q = _dot(xn, wq_ref[...])
    qc_ref[...] = q.astype(BF16)
    cos, sin = cos_ref[...], sin_ref[...]
    for hd in range(N_HEADS):
        sl = slice(hd * HEAD_DIM, (hd + 1) * HEAD_DIM)
        qr_ref[:, sl] = _rope(q[:, sl], cos, sin).astype(BF16)
    gate_ref[...] = jax.nn.sigmoid(_dot(xn, wgt_ref[...]))


def _qg_proj(xn, wq, wgt, cos, sin, layer, tm):
    n, d = xn.shape
    nq, ng = wq.shape[-1], wgt.shape[-1]
    nt = cos.shape[0] // tm
    row = lambda i: (i, 0)
    tab = lambda i: (i % nt, 0)
    return pl.pallas_call(
        _qg_body,
        grid=(n // tm,),
        in_specs=[pl.BlockSpec((tm, d), row),
                  pl.BlockSpec((None, d, nq), lambda i: (layer, 0, 0)),
                  pl.BlockSpec((None, d, ng), lambda i: (layer, 0, 0)),
                  pl.BlockSpec((tm, HEAD_DIM), tab), pl.BlockSpec((tm, HEAD_DIM), tab)],
        out_specs=[pl.BlockSpec((tm, nq), row), pl.BlockSpec((tm, nq), row), pl.BlockSpec((tm, ng), row)],
        out_shape=[jax.ShapeDtypeStruct((n, nq), BF16), jax.ShapeDtypeStruct((n, nq), BF16),
                   jax.ShapeDtypeStruct((n, ng), F32)],
        compiler_params=_cparams("parallel"),
        name="qg_proj",
    )(xn, wq, wgt, cos, sin)


def _oproj_body(h_ref, o_ref, w_ref, g_ref, out_ref):
    m = _dot(o_ref[...].astype(BF16), w_ref[...])
    out_ref[...] = h_ref[...] + _rms(m, g_ref[...])


def _o_proj(h, o, w, g, layer, tm):
    n, d = h.shape
    row = lambda i: (i, 0)
    return pl.pallas_call(
        _oproj_body,
        grid=(n // tm,),
        in_specs=[pl.BlockSpec((tm, d), row), pl.BlockSpec((tm, d), row),
                  pl.BlockSpec((None, d, d), lambda i: (layer, 0, 0)),
                  pl.BlockSpec((1, d), lambda i: (0, 0))],
        out_specs=pl.BlockSpec((tm, d), row),
        out_shape=jax.ShapeDtypeStruct((n, d), F32),
        compiler_params=_cparams("parallel"),
        name="o_proj",
    )(h, o, w, g)


def _s5gate_body(h_ref, xn_ref, y_ref, d_ref, w_ref, b_ref, g_ref, out_ref):
    y = jax.nn.gelu(y_ref[...] + d_ref[...] * xn_ref[...])
    z = jax.nn.sigmoid(_dot(y.astype(BF16), w_ref[...]) + b_ref[...])
    out_ref[...] = h_ref[...] + _rms(y * z, g_ref[...])


def _s5_gate(h, xn, y, dskip, w, b, g, layer, tm):
    n, d = h.shape
    row = lambda i: (i, 0)
    vec = lambda i: (0, 0)
    return pl.pallas_call(
        _s5gate_body,
        grid=(n // tm,),
        in_specs=[pl.BlockSpec((tm, d), row), pl.BlockSpec((tm, d), row), pl.BlockSpec((tm, d), row),
                  pl.BlockSpec((1, d), vec),
                  pl.BlockSpec((None, d, d), lambda i: (layer, 0, 0)),
                  pl.BlockSpec((1, d), vec), pl.BlockSpec((1, d), vec)],
        out_specs=pl.BlockSpec((tm, d), row),
        out_shape=jax.ShapeDtypeStruct((n, d), F32),
        compiler_params=_cparams("parallel"),
        name="s5_gate",
    )(h, xn, y, dskip, w, b, g)


def _s5_disc(lam_re, lam_im, log_dt):
    dt = jnp.exp(log_dt)
    mag = jnp.exp(lam_re * dt)
    a_re, a_im = mag * jnp.cos(lam_im * dt), mag * jnp.sin(lam_im * dt)
    den = lam_re * lam_re + lam_im * lam_im
    nr, ni = a_re - 1.0, a_im
    r_re = (nr * lam_re + ni * lam_im) / den
    r_im = (ni * lam_re - nr * lam_im) / den
    return a_re, a_im, r_re, r_im


def _cmul(ar, ai, br, bi):
    return ar * br - ai * bi, ar * bi + ai * br


def _bd_rows(m):
    shape = (LANES, S5_STATE_LANES)
    same = _iota(shape, 0) // SSM_GROUP == _iota(shape, 1) // SSM_STATE
    return jnp.where(same, jnp.concatenate([m] * S5_GROUPS_PER_STEP, axis=1), 0.0)


def _bd_cols(m):
    shape = (S5_STATE_LANES, LANES)
    same = _iota(shape, 0) // SSM_STATE == _iota(shape, 1) // SSM_GROUP
    return jnp.where(same, jnp.concatenate([m] * S5_GROUPS_PER_STEP, axis=0), 0.0)


def _s5_input_weights(lam_re, lam_im, log_dt, b_re, b_im):
    a_re, a_im, r_re, r_im = _s5_disc(lam_re, lam_im, log_dt)
    bb_re, bb_im = _cmul(r_re, r_im, b_re, b_im)
    return a_re, a_im, bb_re, bb_im


def _s5_prompt_body(u_ref, lr_re_ref, lr_im_ref, lr_dt_ref, b_re_ref, b_im_ref,
                    lc_re_ref, lc_im_ref, lc_dt_ref, c_re_ref, c_im_ref,
                    lv_re_ref, lv_im_ref, lv_dt_ref,
                    y_ref, st_ref,
                    sin_ref, toep_ref, fout_ref, v_ref, xp_ref, *, n_batch, n_chunk):
    L = S5_CHUNK
    rows = n_batch * n_chunk
    half = S5_STATE_LANES

    a_re, a_im, w_re, w_im = _s5_input_weights(lr_re_ref[...], lr_im_ref[...], lr_dt_ref[...],
                                               b_re_ref[...], b_im_ref[...])
    ws_re, ws_im = [w_re], [w_im]
    for _ in range(L - 1):
        w_re, w_im = _cmul(a_re, a_im, w_re, w_im)
        ws_re.append(w_re)
        ws_im.append(w_im)
    for s in range(L):
        k = L - 1 - s
        sin_ref[s * LANES:(s + 1) * LANES, :] = jnp.concatenate(
            [_bd_rows(ws_re[k]), _bd_rows(ws_im[k])], axis=1).astype(BF16)
    stack_re = jnp.concatenate([ws_re[L - 1 - s] for s in range(L)], axis=0)
    stack_im = jnp.concatenate([ws_im[L - 1 - s] for s in range(L)], axis=0)
    taps = (_dot(stack_re, c_re_ref[...], precision=HIGHEST)
            - _dot(stack_im, c_im_ref[...], precision=HIGHEST))
    tshape = (L * LANES, LANES)
    same = (_iota(tshape, 0) % LANES) // SSM_GROUP == _iota(tshape, 1) // SSM_GROUP
    toep_ref[...] = jnp.where(same, taps, 0.0).astype(BF16)
    ac_re, ac_im, _, _ = _s5_disc(lc_re_ref[...], lc_im_ref[...], lc_dt_ref[...])
    ca_re, ca_im = c_re_ref[...], c_im_ref[...]
    for t in range(L):
        ca_re, ca_im = _cmul(ac_re, ac_im, ca_re, ca_im)
        fout_ref[:, t * LANES:(t + 1) * LANES] = jnp.concatenate(
            [_bd_cols(ca_re), -_bd_cols(ca_im)], axis=0).astype(BF16)
    al_re, al_im, _, _ = _s5_disc(lv_re_ref[...], lv_im_ref[...], lv_dt_ref[...])
    for _ in range(int(math.log2(L))):
        al_re, al_im = _cmul(al_re, al_im, al_re, al_im)

    xb = jnp.concatenate([u_ref[pl.ds(s, rows, stride=L), :] for s in range(L)], axis=1).astype(BF16)
    v_ref[...] = _dot(xb, sin_ref[...])

    def step(c, carry):
        new = []
        for b in range(n_batch):
            r = b * n_chunk + c
            x_re, x_im = carry[b]
            xp_ref[pl.ds(r, 1), :] = jnp.concatenate([x_re, x_im], axis=1)
            inc = v_ref[pl.ds(r, 1), :]
            new.append((al_re * x_re - al_im * x_im + inc[:, :half],
                        al_re * x_im + al_im * x_re + inc[:, half:]))
        return tuple(new)

    zero = jnp.zeros((1, half), F32)
    final = lax.fori_loop(0, n_chunk, step, tuple((zero, zero) for _ in range(n_batch)))
    for b in range(n_batch):
        st_ref[b:b + 1, :] = jnp.concatenate(final[b], axis=1)

    xpb = xp_ref[...].astype(BF16)
    for t in range(L):
        yt = (_dot(xb[:, :(t + 1) * LANES], toep_ref[(L - 1 - t) * LANES:, :])
              + _dot(xpb, fout_ref[:, t * LANES:(t + 1) * LANES]))
        y_ref[pl.ds(t, rows, stride=L), :] = yt


def _s5_layouts(lam_re, lam_im, log_dt, b_re, b_im, c_re, c_im):
    n_g = lam_re.shape[0]
    ldt = jnp.broadcast_to(log_dt[:, None], lam_re.shape)
    rep = lambda a: jnp.repeat(a, SSM_GROUP, axis=0)
    rows = (rep(lam_re), rep(lam_im), rep(ldt),
            b_re.transpose(0, 2, 1).reshape(n_g * SSM_GROUP, SSM_STATE),
            b_im.transpose(0, 2, 1).reshape(n_g * SSM_GROUP, SSM_STATE))
    cols = (rep(lam_re).T, rep(lam_im).T, rep(ldt).T,
            c_re.reshape(n_g * SSM_GROUP, SSM_STATE).T, c_im.reshape(n_g * SSM_GROUP, SSM_STATE).T)
    n_j = n_g // S5_GROUPS_PER_STEP
    lanes = tuple(a.reshape(n_j, 1, S5_STATE_LANES) for a in (lam_re, lam_im, ldt))
    return rows, cols, lanes


def _s5_specs():
    rspec = pl.BlockSpec((LANES, SSM_STATE), lambda j: (j, 0))
    cspec = pl.BlockSpec((SSM_STATE, LANES), lambda j: (0, j))
    vspec = pl.BlockSpec((None, 1, S5_STATE_LANES), lambda j: (j, 0, 0))
    return [rspec] * 5 + [cspec] * 5 + [vspec] * 3


def _s5_prompt(u, n_batch, layouts):
    n, d = u.shape
    n_j = d // LANES
    n_chunk = n // n_batch // S5_CHUNK
    rows = n // S5_CHUNK
    rows_p, cols_p, lanes_p = layouts
    col = lambda j: (0, j)
    return pl.pallas_call(
        functools.partial(_s5_prompt_body, n_batch=n_batch, n_chunk=n_chunk),
        grid=(n_j,),
        in_specs=[pl.BlockSpec((n, LANES), col)] + _s5_specs(),
        out_specs=[pl.BlockSpec((n, LANES), col),
                   pl.BlockSpec((None, n_batch, 2 * S5_STATE_LANES), lambda j: (j, 0, 0))],
        out_shape=[jax.ShapeDtypeStruct((n, d), F32),
                   jax.ShapeDtypeStruct((n_j, n_batch, 2 * S5_STATE_LANES), F32)],
        scratch_shapes=[pltpu.VMEM((S5_CHUNK * LANES, 2 * S5_STATE_LANES), BF16),
                        pltpu.VMEM((S5_CHUNK * LANES, LANES), BF16),
                        pltpu.VMEM((2 * S5_STATE_LANES, S5_CHUNK * LANES), BF16),
                        pltpu.VMEM((rows, 2 * S5_STATE_LANES), F32),
                        pltpu.VMEM((rows, 2 * S5_STATE_LANES), F32)],
        compiler_params=_cparams("parallel"),
        name="s5_prompt",
    )(u, *rows_p, *cols_p, *lanes_p)


def _s5_step_body(u_ref, hre_ref, him_ref, lr_re_ref, lr_im_ref, lr_dt_ref, b_re_ref, b_im_ref,
                  lc_re_ref, lc_im_ref, lc_dt_ref, c_re_ref, c_im_ref,
                  lv_re_ref, lv_im_ref, lv_dt_ref, y_ref, ore_ref, oim_ref):
    _, _, bb_re, bb_im = _s5_input_weights(lr_re_ref[...], lr_im_ref[...], lr_dt_ref[...],
                                           b_re_ref[...], b_im_ref[...])
    a_re, a_im, _, _ = _s5_disc(lv_re_ref[...], lv_im_ref[...], lv_dt_ref[...])
    u = u_ref[...]
    h_re, h_im = hre_ref[...], him_ref[...]
    x_re = _dot(u, _bd_rows(bb_re), precision=HIGHEST) + (a_re * h_re - a_im * h_im)
    x_im = _dot(u, _bd_rows(bb_im), precision=HIGHEST) + (a_re * h_im + a_im * h_re)
    ore_ref[...] = x_re
    oim_ref[...] = x_im
    y_ref[...] = (_dot(x_re, _bd_cols(c_re_ref[...]), precision=HIGHEST)
                  - _dot(x_im, _bd_cols(c_im_ref[...]), precision=HIGHEST))


def _s5_step(u, h_re, h_im, layouts):
    nb, d = u.shape
    rows_p, cols_p, lanes_p = layouts
    col = lambda j: (0, j)
    sspec = pl.BlockSpec((nb, S5_STATE_LANES), col)
    return pl.pallas_call(
        _s5_step_body,
        grid=(d // LANES,),
        in_specs=[pl.BlockSpec((nb, LANES), col), sspec, sspec] + _s5_specs(),
        out_specs=[pl.BlockSpec((nb, LANES), col), sspec, sspec],
        out_shape=[jax.ShapeDtypeStruct((nb, d), F32),
                   jax.ShapeDtypeStruct(h_re.shape, F32), jax.ShapeDtypeStruct(h_im.shape, F32)],
        compiler_params=_cparams("parallel"),
        name="s5_step",
    )(u, h_re, h_im, *rows_p, *cols_p, *lanes_p)


PAGE_ROWS = 128
PAGE_LINES = PAGE_ROWS * ROW_LINES
CHUNKS_PER_PAGE = PAGE_ROWS // CMP_STRIDE
CMP_PAGES_PER_STEP = 8
CMP_LINES = 2 * N_KV


def _compress_body(pt_ref, *refs, n_pages):
    del pt_ref
    k_pages = CMP_PAGES_PER_STEP
    srcs = refs[:k_pages]
    (pek_ref, pev_ref, w1k_ref, w1v_ref, w2k_ref, w2v_ref, kc_ref, vc_ref, buf_ref, shift_ref) = refs[k_pages:]
    p = pl.program_id(1)
    for k in range(k_pages):
        r0 = pl.multiple_of((p * k_pages + k) * PAGE_ROWS, PAGE_ROWS)
        for sg in range(CMP_LINES):
            buf_ref[sg, pl.ds(r0, PAGE_ROWS), :] = srcs[k][pl.ds(sg, PAGE_ROWS, stride=ROW_LINES), :]

    @pl.when(p == n_pages // k_pages - 1)
    def _():
        n_chunk = n_pages * CHUNKS_PER_PAGE
        half = CMP_STRIDE
        for slot, (pe_ref, w1_ref, w2_ref, out_ref) in enumerate(
                ((pek_ref, w1k_ref, w2k_ref, kc_ref), (pev_ref, w1v_ref, w2v_ref, vc_ref))):
            for g in range(N_KV):
                sg = slot * N_KV + g
                first = jnp.zeros((n_chunk, HEAD_DIM), F32)
                second = jnp.zeros((n_chunk, HEAD_DIM), F32)
                for l in range(half):
                    x = buf_ref[sg, pl.ds(l, n_chunk, stride=CMP_STRIDE), :]
                    xa = (x + pe_ref[l:l + 1, :]).astype(BF16)
                    xb = (x + pe_ref[half + l:half + l + 1, :]).astype(BF16)
                    first += _dot(xa, w1_ref[l * HEAD_DIM:(l + 1) * HEAD_DIM, :])
                    second += _dot(xb, w1_ref[(half + l) * HEAD_DIM:(half + l + 1) * HEAD_DIM, :])
                shift_ref[0:n_chunk, :] = second
                shift_ref[n_chunk:n_chunk + 8, :] = jnp.zeros((8, HEAD_DIM), F32)
                hid = first + shift_ref[1:n_chunk + 1, :]
                out = _dot(jax.nn.silu(hid).astype(BF16), w2_ref[...])
                live = _iota(out.shape, 0) < n_chunk - 1
                out_ref[g] = jnp.where(live, out, 0.0).astype(BF16)


def _compress(src, page_ids, pe_k, pe_v, w1k, w1v, w2k, w2v):
    nb, n_pages = page_ids.shape
    k_pages = CMP_PAGES_PER_STEP
    n_chunk = n_pages * CHUNKS_PER_PAGE
    const = lambda b, p, pt: (0, 0)
    page = lambda k: pl.BlockSpec((PAGE_LINES, HEAD_DIM),
                                  lambda b, p, pt: (pt[b * n_pages + p * k_pages + k], 0))
    out_spec = pl.BlockSpec((None, N_KV, n_chunk, HEAD_DIM), lambda b, p, pt: (b, 0, 0, 0))
    grid_spec = pltpu.PrefetchScalarGridSpec(
        num_scalar_prefetch=1,
        grid=(nb, n_pages // k_pages),
        in_specs=[page(k) for k in range(k_pages)]
        + [pl.BlockSpec(pe_k.shape, const), pl.BlockSpec(pe_v.shape, const),
           pl.BlockSpec(w1k.shape, const), pl.BlockSpec(w1v.shape, const),
           pl.BlockSpec(w2k.shape, const), pl.BlockSpec(w2v.shape, const)],
        out_specs=[out_spec, out_spec],
        scratch_shapes=[pltpu.VMEM((CMP_LINES, n_pages * PAGE_ROWS, HEAD_DIM), F32),
                        pltpu.VMEM((n_chunk + 8, HEAD_DIM), F32)])
    return pl.pallas_call(
        functools.partial(_compress_body, n_pages=n_pages),
        grid_spec=grid_spec,
        out_shape=[jax.ShapeDtypeStruct((nb, N_KV, n_chunk, HEAD_DIM), BF16)] * 2,
        compiler_params=_cparams("parallel", "arbitrary"),
        name="compress",
    )(page_ids.reshape(-1), *([src] * k_pages), pe_k, pe_v, w1k, w1v, w2k, w2v)


def _masked_softmax(s, mask):
    s = jnp.where(mask, s, NEG)
    m = jnp.max(s, axis=-1, keepdims=True)
    e = jnp.where(mask, jnp.exp(s - m), 0.0)
    return e / jnp.maximum(jnp.sum(e, axis=-1, keepdims=True), 1e-30)


def _overlap(n_rows, n_lanes, n_cmp, n_sel):
    ci, sj = _iota((n_rows, n_lanes), 0), _iota((n_rows, n_lanes), 1)
    hit = ((ci * CMP_STRIDE < sj * SEL_LEN + SEL_LEN) & (ci * CMP_STRIDE + CMP_LEN > sj * SEL_LEN)
           & (ci < n_cmp) & (sj < n_sel))
    return jnp.where(hit, 1.0, 0.0).astype(BF16)


def _importance(imp, lane, qpos, n_sel):
    cur = qpos // SEL_LEN
    forced = (lane == 0) | (lane == cur) | (lane == cur - 1)
    imp = jnp.where(lane * SEL_LEN <= qpos, imp + jnp.where(forced, FORCE_BONUS, 0.0), NEG)
    return jnp.where(lane < n_sel, imp, LOWEST)


def _dot_tn(a, b):
    return lax.dot_general(a, b, (((0,), (0,)), ((), ())), preferred_element_type=F32)


LOG2E = math.log2(math.e)
ATTEND_COLUMN_GROUPS = 1


def _online_attend_t(q, k_ref, v_ref, lo, hi, ck, bias_fn):
    cols = q.shape[0]
    width = cols // ATTEND_COLUMN_GROUPS
    q_groups = [q[i * width:(i + 1) * width] for i in range(ATTEND_COLUMN_GROUPS)]

    def body(c, carry):
        k0 = pl.multiple_of(c * ck, ck)
        k = k_ref[pl.ds(k0, ck), :]
        v = v_ref[pl.ds(k0, ck), :]
        bias = jnp.concatenate([bias_fn(k0)] * (width // Q_BLOCK), axis=1)
        out = []
        for (m, l, acc), qg in zip(carry, q_groups):
            s = _dot_nt(k, qg) * (ATT_SCALE * LOG2E) + bias
            m_new = jnp.maximum(m, jnp.max(s, axis=0, keepdims=True))
            alpha = jnp.exp2(m - m_new)
            e = jnp.exp2(s - m_new)
            l = alpha * l + jnp.sum(e, axis=0, keepdims=True)
            acc = alpha * acc + _dot_tn(v, e.astype(BF16))
            out.append((m_new, l, acc))
        return tuple(out)

    init = (jnp.full((1, width), NEG, F32), jnp.zeros((1, width), F32), jnp.zeros((HEAD_DIM, width), F32))
    final = lax.fori_loop(lo, hi, body, (init,) * ATTEND_COLUMN_GROUPS)
    return jnp.concatenate([jnp.where(m > NEG, acc / jnp.maximum(l, 1e-30), 0.0) for m, l, acc in final],
                           axis=1)


Q_BLOCK = 128
SEL_KEY_CHUNK = 256
WIN_KEY_CHUNK = 128


def _nsa_prompt_body(qc_ref, qr_ref, gate_ref, ksel_ref, vsel_ref, kwin_ref, vwin_ref,
                     kcmp_ref, vcmp_ref, o_ref, sel_ref, *, n_cmp, n_sel):
    qi = pl.program_id(2)
    q0 = qi * Q_BLOCK
    nh = HEADS_PER_KV
    stack = lambda ref: jnp.concatenate(
        [ref[:, hd * HEAD_DIM:(hd + 1) * HEAD_DIM] for hd in range(nh)], axis=0)
    per_head = lambda x: jnp.concatenate([x] * nh, axis=1)
    qc, qr = stack(qc_ref), stack(qr_ref)
    n_cmp_rows = kcmp_ref.shape[0]
    n_sel_rows = sel_ref.shape[0]

    cshape = (n_cmp_rows, Q_BLOCK)
    cblk, cq = _iota(cshape, 0), q0 + _iota(cshape, 1)
    cmask = per_head(jnp.where((cblk < n_cmp) & (cblk * CMP_STRIDE + CMP_LEN - 1 <= cq), 1.0, 0.0)) > 0.5
    s = jnp.where(cmask, _dot_nt(kcmp_ref[...], qc) * ATT_SCALE, NEG)
    e = jnp.where(cmask, jnp.exp(s - jnp.max(s, axis=0, keepdims=True)), 0.0)
    p_cmp = (e / jnp.maximum(jnp.sum(e, axis=0, keepdims=True), 1e-30)).astype(BF16)
    o_cmp = _dot_tn(vcmp_ref[...], p_cmp)

    oshape = (n_sel_rows, n_cmp_rows)
    sj, ci = _iota(oshape, 0), _iota(oshape, 1)
    hit = ((ci * CMP_STRIDE < sj * SEL_LEN + SEL_LEN) & (ci * CMP_STRIDE + CMP_LEN > sj * SEL_LEN)
           & (ci < n_cmp) & (sj < n_sel))
    imp_h = _dot(jnp.where(hit, 1.0, 0.0).astype(BF16), p_cmp)
    imp = imp_h[:, 0:Q_BLOCK]
    for hd in range(1, nh):
        imp = imp + imp_h[:, hd * Q_BLOCK:(hd + 1) * Q_BLOCK]
    ishape = (n_sel_rows, Q_BLOCK)
    blk = _iota(ishape, 0)
    imp = _importance(imp, blk, q0 + _iota(ishape, 1), n_sel)
    rank = jnp.zeros(ishape, F32)
    for k in range(n_sel):
        row = imp[k:k + 1, :]
        rank += jnp.where((row > imp) | ((row == imp) & (blk > k)), 1.0, 0.0)
    sel_ref[...] = jnp.where((rank < min(SEL_TOP, n_sel)) & (blk < n_sel), 1.0, 0.0)

    def sel_mask(k0):
        shape = (SEL_KEY_CHUNK, Q_BLOCK)
        j0 = k0 // SEL_LEN
        chosen = jnp.concatenate(
            [jnp.broadcast_to(sel_ref[pl.ds(j0 + i, 1), :], (SEL_LEN, Q_BLOCK))
             for i in range(SEL_KEY_CHUNK // SEL_LEN)], axis=0)
        live = (k0 + _iota(shape, 0) <= q0 + _iota(shape, 1)) & (chosen > 0.5)
        return jnp.where(live, 0.0, NEG)

    def win_mask(k0):
        shape = (WIN_KEY_CHUNK, Q_BLOCK)
        kpos = k0 + _iota(shape, 0)
        qp = q0 + _iota(shape, 1)
        return jnp.where((kpos <= qp) & (kpos > qp - WINDOW), 0.0, NEG)

    sel_chunks = (q0 + Q_BLOCK + SEL_KEY_CHUNK - 1) // SEL_KEY_CHUNK
    o_sel = _online_attend_t(qr, ksel_ref, vsel_ref, 0, sel_chunks, SEL_KEY_CHUNK, sel_mask)
    win_lo = jnp.maximum(qi - WINDOW // WIN_KEY_CHUNK, 0)
    o_win = _online_attend_t(qr, kwin_ref, vwin_ref, win_lo, qi + 1, WIN_KEY_CHUNK, win_mask)

    gates = jnp.transpose(gate_ref[...])
    for hd in range(nh):
        cols = slice(hd * Q_BLOCK, (hd + 1) * Q_BLOCK)
        c0 = hd * N_BRANCH
        o = (o_cmp[:, cols] * gates[c0:c0 + 1, :] + o_sel[:, cols] * gates[c0 + 1:c0 + 2, :]
             + o_win[:, cols] * gates[c0 + 2:c0 + 3, :])
        o_ref[:, hd * HEAD_DIM:(hd + 1) * HEAD_DIM] = jnp.transpose(o).astype(BF16)


def _nsa_prompt(qc, qr, gates, kv, kcmp, vcmp):
    bsz, t, dq = qc.shape
    n_cmp = (t - CMP_LEN) // CMP_STRIDE + 1
    n_sel = -(-t // SEL_LEN)
    gw = HEADS_PER_KV * HEAD_DIM
    qspec = pl.BlockSpec((None, Q_BLOCK, gw), lambda b, g, i: (b, i, g))
    kvspec = lambda slot: pl.BlockSpec((None, t, HEAD_DIM), lambda b, g, i: (b, 0, slot * N_KV + g))
    cspec = pl.BlockSpec((None, None, kcmp.shape[2], HEAD_DIM), lambda b, g, i: (b, g, 0, 0))
    return pl.pallas_call(
        functools.partial(_nsa_prompt_body, n_cmp=n_cmp, n_sel=n_sel),
        grid=(bsz, N_KV, t // Q_BLOCK),
        in_specs=[qspec, qspec, pl.BlockSpec((None, Q_BLOCK, LANES), lambda b, g, i: (b, i, g)),
                  kvspec(2), kvspec(3), kvspec(4), kvspec(5), cspec, cspec],
        out_specs=qspec,
        out_shape=jax.ShapeDtypeStruct((bsz, t, dq), BF16),
        scratch_shapes=[pltpu.VMEM((-(-n_sel // 8) * 8, Q_BLOCK), F32)],
        compiler_params=_cparams("parallel", "parallel", "arbitrary"),
        name="nsa_prompt",
    )(qc, qr, gates, kv, kv, kv, kv, kcmp, vcmp)


SEL_LANES = 384


def _select_body(qc_ref, kcmp_ref, vcmp_ref, ocmp_ref, idx_ref, *, qpos, n_cmp, n_sel):
    q = qc_ref[...]
    nc = kcmp_ref.shape[0]
    ci = _iota((HEADS_PER_KV, nc), 1)
    cmask = (ci < n_cmp) & (ci * CMP_STRIDE + CMP_LEN - 1 <= qpos)
    p_cmp = _masked_softmax(_dot_nt(q, kcmp_ref[...]) * ATT_SCALE, cmask).astype(BF16)
    ocmp_ref[...] = _dot(p_cmp, vcmp_ref[...])

    imp = jnp.sum(_dot(p_cmp, _overlap(nc, SEL_LANES, n_cmp, n_sel)), axis=0, keepdims=True)
    lane = _iota((1, SEL_LANES), 1)
    imp = _importance(imp, lane, qpos, n_sel)
    sq = (SEL_LANES, SEL_LANES)
    ri, cj = _iota(sq, 0), _iota(sq, 1)
    impc = jnp.sum(jnp.where(ri == cj, imp, 0.0), axis=1, keepdims=True)
    n_top = min(SEL_TOP, n_sel)
    before_col = jnp.where((impc > imp) | ((impc == imp) & (ri < cj)), 1.0, 0.0)
    before_row = jnp.where((imp > impc) | ((imp == impc) & (cj < ri)), 1.0, 0.0)
    sel_lane = (jnp.sum(before_col, axis=0, keepdims=True) < n_top) & (lane < n_sel)
    sel_row = (jnp.sum(before_row, axis=1, keepdims=True) < n_top) & (_iota((SEL_LANES, 1), 0) < n_sel)
    slot_of = jnp.sum(jnp.where((ri < cj) & sel_row, 1.0, 0.0), axis=0, keepdims=True)
    lshape = (SEL_TOP, SEL_LANES)
    onehot = jnp.where(sel_lane & (slot_of == _iota(lshape, 0).astype(F32)), 1.0, 0.0)
    idx = jnp.sum(onehot * _iota(lshape, 1).astype(F32), axis=1, keepdims=True)
    idx_ref[...] = jnp.broadcast_to(idx, (SEL_TOP, LANES)).astype(jnp.int32)


def _nsa_select(qc, kcmp, vcmp, qpos, n_cmp, n_sel):
    nb = qc.shape[0]
    nc = kcmp.shape[2]
    hspec = pl.BlockSpec((None, None, HEADS_PER_KV, HEAD_DIM), lambda b, g: (b, g, 0, 0))
    cspec = pl.BlockSpec((None, None, nc, HEAD_DIM), lambda b, g: (b, g, 0, 0))
    return pl.pallas_call(
        functools.partial(_select_body, qpos=qpos, n_cmp=n_cmp, n_sel=n_sel),
        grid=(nb, N_KV),
        in_specs=[hspec, cspec, cspec],
        out_specs=[hspec, pl.BlockSpec((None, None, SEL_TOP, LANES), lambda b, g: (b, g, 0, 0))],
        out_shape=[jax.ShapeDtypeStruct((nb, N_KV, HEADS_PER_KV, HEAD_DIM), F32),
                   jax.ShapeDtypeStruct((nb, N_KV, SEL_TOP, LANES), jnp.int32)],
        compiler_params=_cparams("parallel", "parallel"),
        name="nsa_select",
    )(qc, kcmp, vcmp)


def _attend_plus_new(q, k, v, valid, k_new, v_new, new_ok):
    qf = q.astype(F32)
    s = _dot_nt(q, k.astype(BF16)) * ATT_SCALE
    s_new = jnp.sum(qf * k_new.astype(BF16).astype(F32), axis=1, keepdims=True) * ATT_SCALE
    s = jnp.where(valid, s, NEG)
    s_new = jnp.where(new_ok, s_new, NEG)
    m = jnp.maximum(jnp.max(s, axis=-1, keepdims=True), s_new)
    e = jnp.where(valid, jnp.exp(s - m), 0.0)
    e_new = jnp.where(new_ok, jnp.exp(s_new - m), 0.0)
    den = jnp.maximum(jnp.sum(e, axis=-1, keepdims=True) + e_new, 1e-30)
    p = (e / den).astype(BF16)
    p_new = (e_new / den).astype(BF16).astype(F32)
    return _dot(p, v.astype(BF16)) + p_new * v_new.astype(BF16).astype(F32)


SEL_BLOCK_LINES = SEL_LEN * ROW_LINES


def _attend_body(idx_ref, pt_ref, qr_ref, gate_ref, ocmp_ref, *refs, qpos, past_len, n_win):
    del pt_ref
    blocks = refs[:SEL_TOP]
    win_ref, rows_new_ref, win_new_ref, o_ref = refs[SEL_TOP:]
    b, g = pl.program_id(0), pl.program_id(1)
    q = qr_ref[...]
    n_keys = SEL_TOP * SEL_LEN
    base = (b * N_KV + g) * SEL_TOP
    lane = _iota((1, n_keys), 1)
    kpos = jnp.zeros((1, n_keys), jnp.int32)
    n_new = jnp.int32(0)
    for t in range(SEL_TOP):
        j = idx_ref[base + t]
        kpos = jnp.where(lane // SEL_LEN == t, j * SEL_LEN + lane % SEL_LEN, kpos)
        n_new = n_new + jnp.where(j == qpos // SEL_LEN, 1, 0)
    valid = (kpos <= qpos) & (kpos < past_len)
    new_ok = (jnp.zeros((HEADS_PER_KV, 1), jnp.int32) + n_new) > 0
    line = lambda ref, slot, n, per_token: ref[pl.ds(slot * N_KV + g, n, stride=per_token), :]
    k_sel = jnp.concatenate([line(blk, 2, SEL_LEN, ROW_LINES) for blk in blocks], axis=0)
    v_sel = jnp.concatenate([line(blk, 3, SEL_LEN, ROW_LINES) for blk in blocks], axis=0)
    o_sel = _attend_plus_new(q, k_sel, v_sel, valid, line(rows_new_ref, 2, 1, ROW_LINES),
                             line(rows_new_ref, 3, 1, ROW_LINES), new_ok)

    wpos = (past_len - n_win) + _iota((1, n_win), 1)
    wvalid = (wpos <= qpos) & (wpos > qpos - WINDOW) & (wpos >= 0)
    always = jnp.zeros((HEADS_PER_KV, 1), jnp.int32) == 0
    o_win = _attend_plus_new(q, line(win_ref, 0, n_win, WIN_LINES), line(win_ref, 1, n_win, WIN_LINES),
                             wvalid, line(win_new_ref, 0, 1, WIN_LINES), line(win_new_ref, 1, 1, WIN_LINES),
                             always)

    gates = gate_ref[...]
    o_ref[...] = ocmp_ref[...] * gates[:, 0:1] + o_sel * gates[:, 1:2] + o_win * gates[:, 2:3]


def _nsa_attend(idx, page_table, qr, gates, ocmp, cache, cache_win, rows_new, win_new, past_len):
    nb, n_pages = page_table.shape
    n_win = cache_win.shape[0] // (nb * WIN_LINES)
    halves = PAGE_ROWS // SEL_LEN
    n_past_blocks = n_pages * halves

    def sel_block(t):
        def index(b, g, idx_ref, pt_ref):
            j = jnp.minimum(idx_ref[(b * N_KV + g) * SEL_TOP + t], n_past_blocks - 1)
            return (pt_ref[b * n_pages + j // halves] * halves + j % halves, 0)
        return pl.BlockSpec((SEL_BLOCK_LINES, HEAD_DIM), index)

    hspec = pl.BlockSpec((None, None, HEADS_PER_KV, HEAD_DIM), lambda b, g, i, p: (b, g, 0, 0))
    grid_spec = pltpu.PrefetchScalarGridSpec(
        num_scalar_prefetch=2,
        grid=(nb, N_KV),
        in_specs=[hspec, hspec, hspec] + [sel_block(t) for t in range(SEL_TOP)]
        + [pl.BlockSpec((n_win * WIN_LINES, HEAD_DIM), lambda b, g, i, p: (b, 0)),
           pl.BlockSpec((None, ROW_LINES, HEAD_DIM), lambda b, g, i, p: (b, 0, 0)),
           pl.BlockSpec((None, WIN_LINES, HEAD_DIM), lambda b, g, i, p: (b, 0, 0))],
        out_specs=hspec)
    return pl.pallas_call(
        functools.partial(_attend_body, qpos=past_len, past_len=past_len, n_win=n_win),
        grid_spec=grid_spec,
        out_shape=jax.ShapeDtypeStruct((nb, N_KV, HEADS_PER_KV, HEAD_DIM), F32),
        compiler_params=_cparams("parallel", "arbitrary"),
        name="nsa_attend",
    )(idx, page_table.reshape(-1), qr, gates, ocmp, *([cache] * SEL_TOP), cache_win, rows_new, win_new)


PROMPT_FFN_ROWS = 512
PROMPT_ROWS = 256


def _rope_tables(pos):
    half = HEAD_DIM // 2
    inv = jnp.exp(-math.log(ROPE_THETA) * jnp.arange(half, dtype=F32) / half)
    ang = pos.astype(F32)[:, None] * inv[None, :]
    cos, sin = jnp.cos(ang), jnp.sin(ang)
    return jnp.concatenate([cos, cos], axis=1), jnp.concatenate([-sin, sin], axis=1)


def _prepare_weights(prm):
    d_ff = prm["ffn_w_gate"].shape[-1]
    pad = -d_ff % FFN_TILE_F
    nq = N_HEADS * HEAD_DIM
    w_qg = prm["attn_w_qg"]
    nb, d = w_qg.shape[:2]
    per_kv = HEADS_PER_KV * N_BRANCH
    w_gate = w_qg[:, :, nq:].reshape(nb, d, N_KV, per_kv)
    w_gate = jnp.pad(w_gate, ((0, 0), (0, 0), (0, 0), (0, LANES - per_kv))).reshape(nb, d, N_KV * LANES)
    return dict(
        wg=jnp.pad(prm["ffn_w_gate"], ((0, 0), (0, 0), (0, 0), (0, pad))).astype(BF16),
        wu=jnp.pad(prm["ffn_w_up"], ((0, 0), (0, 0), (0, 0), (0, pad))).astype(BF16),
        wd=jnp.pad(prm["ffn_w_down"], ((0, 0), (0, 0), (0, pad), (0, 0))).astype(BF16),
        ple_gate=prm["ple_w_gate"].astype(BF16), ple_proj=prm["ple_w_proj"].astype(BF16),
        glu=prm["glu_w"].astype(BF16), kv=prm["w_kv"].astype(BF16),
        q=w_qg[:, :, :nq].astype(BF16), gate=w_gate.astype(BF16), o=prm["attn_w_o"].astype(BF16),
        w1k=prm["cmp_w1_k"].astype(BF16), w1v=prm["cmp_w1_v"].astype(BF16),
        w2k=prm["cmp_w2_k"].astype(BF16), w2v=prm["cmp_w2_v"].astype(BF16),
        s5=[_s5_layouts(prm["ssm_lam_re"][i], prm["ssm_lam_im"][i], prm["ssm_log_dt"][i],
                        prm["ssm_b_re"][i], prm["ssm_b_im"][i], prm["ssm_c_re"][i], prm["ssm_c_im"][i])
            for i in range(N_A_LAYERS)])


def _run_group(x, p, start, state, past, prm, w):
    bsz, t, d = x.shape
    n = bsz * t
    prompt = past is None
    tm_ffn = PROMPT_FFN_ROWS if prompt else n
    tm = PROMPT_ROWS if prompt else n
    depth = prm["norm_g"].shape[0]
    cos, sin = _rope_tables(start + jnp.arange(t))
    if not prompt:
        cos, sin = jnp.broadcast_to(cos, (n, HEAD_DIM)), jnp.broadcast_to(sin, (n, HEAD_DIM))
    h = x.reshape(n, d)
    ssm_re, ssm_im = [], []
    rows = win = kv16 = kcmp = vcmp = None
    if not prompt:
        cache_kv, cache_win, page_table = past
        past_len = page_table.shape[1] * cache_kv.shape[1]
        cache_lines = cache_kv.reshape(-1, HEAD_DIM)
        win_lines = cache_win.reshape(-1, HEAD_DIM)
    for i in range(depth):
        g = prm["norm_g"][i]
        h, xn = _ffn(h, g[0:1], g[1:2], g[2:3], w["wg"], w["wu"], w["wd"], i, 0, tm_ffn)
        if i < N_A_LAYERS:
            if prompt:
                y, st = _s5_prompt(xn, bsz, w["s5"][i])
                st = st.reshape(d // LANES, bsz, 2, S5_GROUPS_PER_STEP, SSM_STATE).transpose(2, 1, 0, 3, 4)
                st = st.reshape(2, bsz, d // SSM_GROUP, SSM_STATE)
                ssm_re.append(st[0])
                ssm_im.append(st[1])
            else:
                y, s_re, s_im = _s5_step(xn, state[0][i].reshape(bsz, -1), state[1][i].reshape(bsz, -1),
                                         w["s5"][i])
                ssm_re.append(s_re.reshape(state[0][i].shape))
                ssm_im.append(s_im.reshape(state[1][i].shape))
            h = _s5_gate(h, xn, y, prm["ssm_d"][i][None], w["glu"], prm["glu_b"][i][None], g[3:4], i, tm)
        else:
            j = i - N_A_LAYERS
            qc, qr, gates = _qg_proj(xn, w["q"], w["gate"], cos, sin, j, tm)
            if prompt:
                o = _nsa_prompt(qc.reshape(bsz, t, -1), qr.reshape(bsz, t, -1), gates.reshape(bsz, t, -1),
                                kv16.reshape(bsz, t, -1), kcmp, vcmp).reshape(n, -1)
            else:
                heads = lambda a: a.reshape(bsz, N_KV, HEADS_PER_KV, HEAD_DIM)
                n_cmp = (past_len + t - CMP_LEN) // CMP_STRIDE + 1
                n_sel = -(-(past_len + t) // SEL_LEN)
                ocmp, idx = _nsa_select(heads(qc), kcmp, vcmp, past_len, n_cmp, n_sel)
                gsm = gates.reshape(bsz, N_KV, LANES)[:, :, :HEADS_PER_KV * N_BRANCH]
                gsm = gsm.reshape(bsz, N_KV, HEADS_PER_KV, N_BRANCH)
                gsm = jnp.pad(gsm, ((0, 0), (0, 0), (0, 0), (0, LANES - N_BRANCH)))
                o = _nsa_attend(idx[..., 0].reshape(-1), page_table, heads(qr), gsm, ocmp,
                                cache_lines, win_lines, rows_l.reshape(bsz, ROW_LINES, HEAD_DIM),
                                win_l.reshape(bsz, WIN_LINES, HEAD_DIM), past_len).reshape(n, -1)
            h = _o_proj(h, o, w["o"], g[3:4], j, tm)
        h, _ = _ffn(h, g[4:5], g[5:6], g[6:7], w["wg"], w["wu"], w["wd"], i, 1, tm_ffn)
        h = _ple(h, p[i].reshape(n, -1), w["ple_gate"], w["ple_proj"], g[6:7], i, tm)
        if i == N_A_LAYERS - 1:
            rows_l, win_l, kv16 = _kv_proj(h, prm["kv_norm_g"][None], w["kv"], cos, sin, tm)
            rows = rows_l.reshape(bsz, t, ROW_SLOTS, N_KV, HEAD_DIM)
            win = win_l.reshape(bsz, t, WIN_SLOTS, N_KV, HEAD_DIM)
            cmp_w = (prm["cmp_pe_k"], prm["cmp_pe_v"], w["w1k"], w["w1v"], w["w2k"], w["w2v"])
            if prompt:
                pages = jnp.arange(n // PAGE_ROWS, dtype=jnp.int32).reshape(bsz, t // PAGE_ROWS)
                kcmp, vcmp = _compress(rows_l, pages, *cmp_w)
                win = win[:, t - min(WINDOW, t):]
            else:
                kcmp, vcmp = _compress(cache_lines, page_table, *cmp_w)
                win = jnp.concatenate([cache_win, win], axis=1)[:, t:]
    return h.reshape(bsz, t, d), jnp.stack(ssm_re), jnp.stack(ssm_im), rows, win


def kernel(x_prompt, x_sample, state_ssm_re, state_ssm_im, cache_kv, cache_win, page_table,
           p_prompt, p_sample, norm_g, ffn_w_gate, ffn_w_up, ffn_w_down, ple_w_proj, ple_w_gate,
           ssm_lam_re, ssm_lam_im, ssm_log_dt, ssm_b_re, ssm_b_im, ssm_c_re, ssm_c_im, ssm_d,
           glu_w, glu_b, kv_norm_g, w_kv, cmp_pe_k, cmp_pe_v, cmp_w1_k, cmp_w2_k, cmp_w1_v,
           cmp_w2_v, attn_w_qg, attn_w_o):
    prm = dict(norm_g=norm_g, ffn_w_gate=ffn_w_gate, ffn_w_up=ffn_w_up, ffn_w_down=ffn_w_down,
               ple_w_proj=ple_w_proj, ple_w_gate=ple_w_gate, ssm_lam_re=ssm_lam_re,
               ssm_lam_im=ssm_lam_im, ssm_log_dt=ssm_log_dt, ssm_b_re=ssm_b_re, ssm_b_im=ssm_b_im,
               ssm_c_re=ssm_c_re, ssm_c_im=ssm_c_im, ssm_d=ssm_d, glu_w=glu_w, glu_b=glu_b,
               kv_norm_g=kv_norm_g, w_kv=w_kv, cmp_pe_k=cmp_pe_k, cmp_pe_v=cmp_pe_v,
               cmp_w1_k=cmp_w1_k, cmp_w2_k=cmp_w2_k, cmp_w1_v=cmp_w1_v, cmp_w2_v=cmp_w2_v,
               attn_w_qg=attn_w_qg, attn_w_o=attn_w_o)
    assert x_sample.shape[1] == 1, "the decode group handles one new token per sequence"
    assert cache_kv.shape[1] == PAGE_ROWS
    w = _prepare_weights(prm)
    y_p, re_p, im_p, kv_p, win_p = _run_group(x_prompt, p_prompt, 0, None, None, prm, w)
    past_len = page_table.shape[1] * cache_kv.shape[1]
    y_s, re_s, im_s, kv_s, win_s = _run_group(x_sample, p_sample, past_len, (state_ssm_re, state_ssm_im),
                                              (cache_kv, cache_win, page_table), prm, w)
    return (y_p, y_s, re_p, im_p, re_s, im_s, kv_p, kv_s, win_p, win_s)
```

```python
import functools
import math

import jax
import jax.numpy as jnp
from jax import lax
from jax.experimental import pallas as pl
from jax.experimental.pallas import tpu as pltpu

F32 = jnp.float32
BF16 = jnp.bfloat16
HIGHEST = lax.Precision.HIGHEST

N_A_LAYERS = 2
NORM_EPS = 1e-6
SSM_GROUP = 16
SSM_STATE = 64
HEAD_DIM = 128
N_KV = 2
HEADS_PER_KV = 8
N_HEADS = N_KV * HEADS_PER_KV
N_BRANCH = 3
N_KV_SLOTS = 6
CMP_LEN = 32
CMP_STRIDE = 16
SEL_LEN = 64
SEL_TOP = 16
WINDOW = 512
ROPE_THETA = 10000.0
FORCE_BONUS = 1e3
NEG = -1e30
LOWEST = -3.0e38
ATT_SCALE = HEAD_DIM ** -0.5

LANES = 128
VMEM_LIMIT_BYTES = 56 * 2 ** 20
FFN_TILE_F = 512
S5_CHUNK = 16
S5_GROUPS_PER_STEP = LANES // SSM_GROUP
S5_STATE_LANES = S5_GROUPS_PER_STEP * SSM_STATE


def _cparams(*sem):
    return pltpu.CompilerParams(dimension_semantics=sem, vmem_limit_bytes=VMEM_LIMIT_BYTES)


def _rms(x, g):
    y = x * lax.rsqrt(jnp.mean(x * x, axis=-1, keepdims=True) + NORM_EPS)
    return y * g


def _dot(a, b, **kw):
    return jnp.dot(a, b, preferred_element_type=F32, **kw)


def _dot_nt(a, b):
    return lax.dot_general(a, b, (((1,), (1,)), ((), ())), preferred_element_type=F32)


def _rope(x, cos, sin):
    return x * cos + pltpu.roll(x, HEAD_DIM // 2, 1) * sin


def _iota(shape, axis):
    return lax.broadcasted_iota(jnp.int32, shape, axis)


def _ffn_step(k, nk, h_ref, gpre_ref, gpost_ref, gnext_ref, wg, wu, wd, o_ref, xn_out_ref, xn_ref, acc_ref):
    @pl.when(k == 0)
    def _():
        xn_ref[...] = _rms(h_ref[...], gpre_ref[...]).astype(BF16)
        acc_ref[...] = jnp.zeros_like(acc_ref)

    xn = xn_ref[...]
    mid = (jax.nn.silu(_dot(xn, wg)) * _dot(xn, wu)).astype(BF16)
    acc_ref[...] += _dot(mid, wd)

    @pl.when(k == nk - 1)
    def _():
        h_new = h_ref[...] + 0.5 * _rms(acc_ref[...], gpost_ref[...])
        o_ref[...] = h_new
        xn_out_ref[...] = _rms(h_new, gnext_ref[...])


def _ffn_body(h_ref, gpre_ref, gpost_ref, gnext_ref, wg_ref, wu_ref, wd_ref, o_ref, xn_out_ref,
              xn_ref, acc_ref, *, nk):
    _ffn_step(pl.program_id(1), nk, h_ref, gpre_ref, gpost_ref, gnext_ref,
              wg_ref[...], wu_ref[...], wd_ref[...], o_ref, xn_out_ref, xn_ref, acc_ref)


def _ffn(h, g_pre, g_post, g_next, wg, wu, wd, tm):
    n, d = h.shape
    nk = wg.shape[-1] // FFN_TILE_F
    row = lambda i, k: (i, 0)
    vec = lambda i, k: (0, 0)
    return pl.pallas_call(
        functools.partial(_ffn_body, nk=nk),
        grid=(n // tm, nk),
        in_specs=[pl.BlockSpec((tm, d), row), pl.BlockSpec((1, d), vec), pl.BlockSpec((1, d), vec),
                  pl.BlockSpec((1, d), vec),
                  pl.BlockSpec((d, FFN_TILE_F), lambda i, k: (0, k)),
                  pl.BlockSpec((d, FFN_TILE_F), lambda i, k: (0, k)),
                  pl.BlockSpec((FFN_TILE_F, d), lambda i, k: (k, 0))],
        out_specs=[pl.BlockSpec((tm, d), row), pl.BlockSpec((tm, d), row)],
        out_shape=[jax.ShapeDtypeStruct((n, d), F32)] * 2,
        scratch_shapes=[pltpu.VMEM((tm, d), BF16), pltpu.VMEM((tm, d), F32)],
        compiler_params=_cparams("parallel", "arbitrary"),
        name="ffn",
    )(h, g_pre, g_post, g_next, wg, wu, wd)


def _ffn_cast_body(h_ref, gpre_ref, gpost_ref, gnext_ref, wg_ref, wu_ref, wd_ref,
                   o_ref, xn_out_ref, wg16_ref, wu16_ref, wd16_ref, xn_ref, acc_ref, *, nk, d_ff):
    k = pl.program_id(1)
    live = d_ff - k * FFN_TILE_F
    wshape, dshape = wg_ref.shape, wd_ref.shape
    wg = jnp.where(_iota(wshape, 1) < live, wg_ref[...], 0.0).astype(BF16)
    wu = jnp.where(_iota(wshape, 1) < live, wu_ref[...], 0.0).astype(BF16)
    wd = jnp.where(_iota(dshape, 0) < live, wd_ref[...], 0.0).astype(BF16)
    wg16_ref[...] = wg
    wu16_ref[...] = wu
    wd16_ref[...] = wd
    _ffn_step(k, nk, h_ref, gpre_ref, gpost_ref, gnext_ref, wg, wu, wd, o_ref, xn_out_ref, xn_ref, acc_ref)


def _ffn_cast(h, g_pre, g_post, g_next, wg, wu, wd, layer, which):
    n, d = h.shape
    d_ff = wg.shape[-1]
    nk = -(-d_ff // FFN_TILE_F)
    fp = nk * FFN_TILE_F
    row = lambda i, k: (0, 0)
    return pl.pallas_call(
        functools.partial(_ffn_cast_body, nk=nk, d_ff=d_ff),
        grid=(1, nk),
        in_specs=[pl.BlockSpec((n, d), row), pl.BlockSpec((1, d), row), pl.BlockSpec((1, d), row),
                  pl.BlockSpec((1, d), row),
                  pl.BlockSpec((None, None, d, FFN_TILE_F), lambda i, k: (layer, which, 0, k)),
                  pl.BlockSpec((None, None, d, FFN_TILE_F), lambda i, k: (layer, which, 0, k)),
                  pl.BlockSpec((None, None, FFN_TILE_F, d), lambda i, k: (layer, which, k, 0))],
        out_specs=[pl.BlockSpec((n, d), row), pl.BlockSpec((n, d), row),
                   pl.BlockSpec((d, FFN_TILE_F), lambda i, k: (0, k)),
                   pl.BlockSpec((d, FFN_TILE_F), lambda i, k: (0, k)),
                   pl.BlockSpec((FFN_TILE_F, d), lambda i, k: (k, 0))],
        out_shape=[jax.ShapeDtypeStruct((n, d), F32)] * 2
        + [jax.ShapeDtypeStruct((d, fp), BF16)] * 2 + [jax.ShapeDtypeStruct((fp, d), BF16)],
        scratch_shapes=[pltpu.VMEM((n, d), BF16), pltpu.VMEM((n, d), F32)],
        compiler_params=_cparams("arbitrary", "arbitrary"),
        name="ffn_cast",
    )(h, g_pre, g_post, g_next, wg, wu, wd)


def _ple_body(h_ref, p_ref, wgate_ref, wproj_ref, g_ref, o_ref):
    h = h_ref[...]
    gate = jax.nn.sigmoid(_dot(h.astype(BF16), wgate_ref[...]))
    proj = _dot(p_ref[...].astype(BF16), wproj_ref[...])
    o_ref[...] = h + _rms(gate * proj, g_ref[...])


def _ple(h, p, wgate, wproj, g, layer, tm):
    n, d = h.shape
    pd = p.shape[-1]
    row = lambda i: (i, 0)
    return pl.pallas_call(
        _ple_body,
        grid=(n // tm,),
        in_specs=[pl.BlockSpec((tm, d), row), pl.BlockSpec((tm, pd), row),
                  pl.BlockSpec((None, d, d), lambda i: (layer, 0, 0)),
                  pl.BlockSpec((None, pd, d), lambda i: (layer, 0, 0)),
                  pl.BlockSpec((1, d), lambda i: (0, 0))],
        out_specs=pl.BlockSpec((tm, d), row),
        out_shape=jax.ShapeDtypeStruct((n, d), F32),
        compiler_params=_cparams("parallel"),
        name="ple",
    )(h, p, wgate, wproj, g)


ROW_SLOTS = 4
WIN_SLOTS = N_KV_SLOTS - ROW_SLOTS
ROW_LINES = ROW_SLOTS * N_KV
WIN_LINES = WIN_SLOTS * N_KV


def _kv_body(h_ref, g_ref, w_ref, cos_ref, sin_ref, rows_ref, win_ref, kv16_ref, *, tm):
    xn = _rms(h_ref[...], g_ref[...]).astype(BF16)
    kv = _dot(xn, w_ref[...])
    cos, sin = cos_ref[...], sin_ref[...]
    for slot in range(N_KV_SLOTS):
        for g in range(N_KV):
            line = slot * N_KV + g
            lo = line * HEAD_DIM
            x = kv[:, lo:lo + HEAD_DIM]
            if slot in (2, 4):
                x = _rope(x, cos, sin)
            kv16_ref[:, lo:lo + HEAD_DIM] = x.astype(BF16)
            if slot < ROW_SLOTS:
                rows_ref[pl.ds(line, tm, stride=ROW_LINES), :] = x
            else:
                win_ref[pl.ds(line - ROW_LINES, tm, stride=WIN_LINES), :] = x


def _kv_proj(h, g, w, cos, sin, tm):
    n, d = h.shape
    nw = w.shape[-1]
    nt = cos.shape[0] // tm
    row = lambda i: (i, 0)
    tab = lambda i: (i % nt, 0)
    return pl.pallas_call(
        functools.partial(_kv_body, tm=tm),
        grid=(n // tm,),
        in_specs=[pl.BlockSpec((tm, d), row), pl.BlockSpec((1, d), lambda i: (0, 0)),
                  pl.BlockSpec((d, nw), lambda i: (0, 0)),
                  pl.BlockSpec((tm, HEAD_DIM), tab), pl.BlockSpec((tm, HEAD_DIM), tab)],
        out_specs=[pl.BlockSpec((tm * ROW_LINES, HEAD_DIM), row),
                   pl.BlockSpec((tm * WIN_LINES, HEAD_DIM), row),
                   pl.BlockSpec((tm, nw), row)],
        out_shape=[jax.ShapeDtypeStruct((n * ROW_LINES, HEAD_DIM), F32),
                   jax.ShapeDtypeStruct((n * WIN_LINES, HEAD_DIM), F32),
                   jax.ShapeDtypeStruct((n, nw), BF16)],
        compiler_params=_cparams("parallel"),
        name="kv_proj",
    )(h, g, w, cos, sin)


def _qg_body(xn_ref, wq_ref, wgt_ref, cos_ref, sin_ref, qc_ref, qr_ref, gate_ref):
    xn = xn_ref[...].astype(BF16)
    q = _dot(xn, wq_ref[...])
    qc_ref[...] = q.astype(BF16)
    cos, sin = cos_ref[...], sin_ref[...]
    for hd in range(N_HEADS):
        sl = slice(hd * HEAD_DIM, (hd + 1) * HEAD_DIM)
        qr_ref[:, sl] = _rope(q[:, sl], cos, sin).astype(BF16)
    gate_ref[...] = jax.nn.sigmoid(_dot(xn, wgt_ref[...]))


def _qg_proj(xn, wq, wgt, cos, sin, layer, tm):
    n, d = xn.shape
    nq, ng = wq.shape[-1], wgt.shape[-1]
    nt = cos.shape[0] // tm
    row = lambda i: (i, 0)
    tab = lambda i: (i % nt, 0)
    return pl.pallas_call(
        _qg_body,
        grid=(n // tm,),
        in_specs=[pl.BlockSpec((tm, d), row),
                  pl.BlockSpec((None, d, nq), lambda i: (layer, 0, 0)),
                  pl.BlockSpec((None, d, ng), lambda i: (layer, 0, 0)),
                  pl.BlockSpec((tm, HEAD_DIM), tab), pl.BlockSpec((tm, HEAD_DIM), tab)],
        out_specs=[pl.BlockSpec((tm, nq), row), pl.BlockSpec((tm, nq), row), pl.BlockSpec((tm, ng), row)],
        out_shape=[jax.ShapeDtypeStruct((n, nq), BF16), jax.ShapeDtypeStruct((n, nq), BF16),
                   jax.ShapeDtypeStruct((n, ng), F32)],
        compiler_params=_cparams("parallel"),
        name="qg_proj",
    )(xn, wq, wgt, cos, sin)


def _oproj_body(h_ref, o_ref, w_ref, g_ref, out_ref):
    m = _dot(o_ref[...].astype(BF16), w_ref[...])
    out_ref[...] = h_ref[...] + _rms(m, g_ref[...])


def _o_proj(h, o, w, g, layer, tm):
    n, d = h.shape
    row = lambda i: (i, 0)
    return pl.pallas_call(
        _oproj_body,
        grid=(n // tm,),
        in_specs=[pl.BlockSpec((tm, d), row), pl.BlockSpec((tm, d), row),
                  pl.BlockSpec((None, d, d), lambda i: (layer, 0, 0)),
                  pl.BlockSpec((1, d), lambda i: (0, 0))],
        out_specs=pl.BlockSpec((tm, d), row),
        out_shape=jax.ShapeDtypeStruct((n, d), F32),
        compiler_params=_cparams("parallel"),
        name="o_proj",
    )(h, o, w, g)


def _s5gate_body(h_ref, xn_ref, y_ref, d_ref, w_ref, b_ref, g_ref, out_ref):
    y = jax.nn.gelu(y_ref[...] + d_ref[...] * xn_ref[...])
    z = jax.nn.sigmoid(_dot(y.astype(BF16), w_ref[...]) + b_ref[...])
    out_ref[...] = h_ref[...] + _rms(y * z, g_ref[...])


def _s5_gate(h, xn, y, dskip, w, b, g, layer, tm):
    n, d = h.shape
    row = lambda i: (i, 0)
    vec = lambda i: (0, 0)
    return pl.pallas_call(
        _s5gate_body,
        grid=(n // tm,),
        in_specs=[pl.BlockSpec((tm, d), row), pl.BlockSpec((tm, d), row), pl.BlockSpec((tm, d), row),
                  pl.BlockSpec((1, d), vec),
                  pl.BlockSpec((None, d, d), lambda i: (layer, 0, 0)),
                  pl.BlockSpec((1, d), vec), pl.BlockSpec((1, d), vec)],
        out_specs=pl.BlockSpec((tm, d), row),
        out_shape=jax.ShapeDtypeStruct((n, d), F32),
        compiler_params=_cparams("parallel"),
        name="s5_gate",
    )(h, xn, y, dskip, w, b, g)


def _s5_disc(lam_re, lam_im, log_dt):
    dt = jnp.exp(log_dt)
    mag = jnp.exp(lam_re * dt)
    a_re, a_im = mag * jnp.cos(lam_im * dt), mag * jnp.sin(lam_im * dt)
    den = lam_re * lam_re + lam_im * lam_im
    nr, ni = a_re - 1.0, a_im
    r_re = (nr * lam_re + ni * lam_im) / den
    r_im = (ni * lam_re - nr * lam_im) / den
    return a_re, a_im, r_re, r_im


def _cmul(ar, ai, br, bi):
    return ar * br - ai * bi, ar * bi + ai * br


def _bd_rows(m):
    shape = (LANES, S5_STATE_LANES)
    same = _iota(shape, 0) // SSM_GROUP == _iota(shape, 1) // SSM_STATE
    return jnp.where(same, jnp.concatenate([m] * S5_GROUPS_PER_STEP, axis=1), 0.0)


def _bd_cols(m):
    shape = (S5_STATE_LANES, LANES)
    same = _iota(shape, 0) // SSM_STATE == _iota(shape, 1) // SSM_GROUP
    return jnp.where(same, jnp.concatenate([m] * S5_GROUPS_PER_STEP, axis=0), 0.0)


def _s5_input_weights(lam_re, lam_im, log_dt, b_re, b_im):
    a_re, a_im, r_re, r_im = _s5_disc(lam_re, lam_im, log_dt)
    bb_re, bb_im = _cmul(r_re, r_im, b_re, b_im)
    return a_re, a_im, bb_re, bb_im


def _s5_prompt_body(u_ref, lr_re_ref, lr_im_ref, lr_dt_ref, b_re_ref, b_im_ref,
                    lc_re_ref, lc_im_ref, lc_dt_ref, c_re_ref, c_im_ref,
                    lv_re_ref, lv_im_ref, lv_dt_ref,
                    y_ref, st_ref,
                    sin_ref, toep_ref, fout_ref, v_ref, xp_ref, *, n_batch, n_chunk):
    L = S5_CHUNK
    rows = n_batch * n_chunk
    half = S5_STATE_LANES

    a_re, a_im, w_re, w_im = _s5_input_weights(lr_re_ref[...], lr_im_ref[...], lr_dt_ref[...],
                                               b_re_ref[...], b_im_ref[...])
    ws_re, ws_im = [w_re], [w_im]
    for _ in range(L - 1):
        w_re, w_im = _cmul(a_re, a_im, w_re, w_im)
        ws_re.append(w_re)
        ws_im.append(w_im)
    for s in range(L):
        k = L - 1 - s
        sin_ref[s * LANES:(s + 1) * LANES, :] = jnp.concatenate(
            [_bd_rows(ws_re[k]), _bd_rows(ws_im[k])], axis=1).astype(BF16)
    stack_re = jnp.concatenate([ws_re[L - 1 - s] for s in range(L)], axis=0)
    stack_im = jnp.concatenate([ws_im[L - 1 - s] for s in range(L)], axis=0)
    taps = (_dot(stack_re, c_re_ref[...], precision=HIGHEST)
            - _dot(stack_im, c_im_ref[...], precision=HIGHEST))
    tshape = (L * LANES, LANES)
    same = (_iota(tshape, 0) % LANES) // SSM_GROUP == _iota(tshape, 1) // SSM_GROUP
    toep_ref[...] = jnp.where(same, taps, 0.0).astype(BF16)
    ac_re, ac_im, _, _ = _s5_disc(lc_re_ref[...], lc_im_ref[...], lc_dt_ref[...])
    ca_re, ca_im = c_re_ref[...], c_im_ref[...]
    for t in range(L):
        ca_re, ca_im = _cmul(ac_re, ac_im, ca_re, ca_im)
        fout_ref[:, t * LANES:(t + 1) * LANES] = jnp.concatenate(
            [_bd_cols(ca_re), -_bd_cols(ca_im)], axis=0).astype(BF16)
    al_re, al_im, _, _ = _s5_disc(lv_re_ref[...], lv_im_ref[...], lv_dt_ref[...])
    for _ in range(int(math.log2(L))):
        al_re, al_im = _cmul(al_re, al_im, al_re, al_im)

    xb = jnp.concatenate([u_ref[pl.ds(s, rows, stride=L), :] for s in range(L)], axis=1).astype(BF16)
    v_ref[...] = _dot(xb, sin_ref[...])

    def step(c, carry):
        new = []
        for b in range(n_batch):
            r = b * n_chunk + c
            x_re, x_im = carry[b]
            xp_ref[pl.ds(r, 1), :] = jnp.concatenate([x_re, x_im], axis=1)
            inc = v_ref[pl.ds(r, 1), :]
            new.append((al_re * x_re - al_im * x_im + inc[:, :half],
                        al_re * x_im + al_im * x_re + inc[:, half:]))
        return tuple(new)

    zero = jnp.zeros((1, half), F32)
    final = lax.fori_loop(0, n_chunk, step, tuple((zero, zero) for _ in range(n_batch)))
    for b in range(n_batch):
        st_ref[b:b + 1, :] = jnp.concatenate(final[b], axis=1)

    xpb = xp_ref[...].astype(BF16)
    for t in range(L):
        yt = (_dot(xb[:, :(t + 1) * LANES], toep_ref[(L - 1 - t) * LANES:, :])
              + _dot(xpb, fout_ref[:, t * LANES:(t + 1) * LANES]))
        y_ref[pl.ds(t, rows, stride=L), :] = yt


def _s5_layouts(lam_re, lam_im, log_dt, b_re, b_im, c_re, c_im):
    n_g = lam_re.shape[0]
    ldt = jnp.broadcast_to(log_dt[:, None], lam_re.shape)
    rep = lambda a: jnp.repeat(a, SSM_GROUP, axis=0)
    rows = (rep(lam_re), rep(lam_im), rep(ldt),
            b_re.transpose(0, 2, 1).reshape(n_g * SSM_GROUP, SSM_STATE),
            b_im.transpose(0, 2, 1).reshape(n_g * SSM_GROUP, SSM_STATE))
    cols = (rep(lam_re).T, rep(lam_im).T, rep(ldt).T,
            c_re.reshape(n_g * SSM_GROUP, SSM_STATE).T, c_im.reshape(n_g * SSM_GROUP, SSM_STATE).T)
    n_j = n_g // S5_GROUPS_PER_STEP
    lanes = tuple(a.reshape(n_j, 1, S5_STATE_LANES) for a in (lam_re, lam_im, ldt))
    return rows, cols, lanes


def _s5_specs():
    rspec = pl.BlockSpec((LANES, SSM_STATE), lambda j: (j, 0))
    cspec = pl.BlockSpec((SSM_STATE, LANES), lambda j: (0, j))
    vspec = pl.BlockSpec((None, 1, S5_STATE_LANES), lambda j: (j, 0, 0))
    return [rspec] * 5 + [cspec] * 5 + [vspec] * 3


def _s5_prompt(u, n_batch, layouts):
    n, d = u.shape
    n_j = d // LANES
    n_chunk = n // n_batch // S5_CHUNK
    rows = n // S5_CHUNK
    rows_p, cols_p, lanes_p = layouts
    col = lambda j: (0, j)
    return pl.pallas_call(
        functools.partial(_s5_prompt_body, n_batch=n_batch, n_chunk=n_chunk),
        grid=(n_j,),
        in_specs=[pl.BlockSpec((n, LANES), col)] + _s5_specs(),
        out_specs=[pl.BlockSpec((n, LANES), col),
                   pl.BlockSpec((None, n_batch, 2 * S5_STATE_LANES), lambda j: (j, 0, 0))],
        out_shape=[jax.ShapeDtypeStruct((n, d), F32),
                   jax.ShapeDtypeStruct((n_j, n_batch, 2 * S5_STATE_LANES), F32)],
        scratch_shapes=[pltpu.VMEM((S5_CHUNK * LANES, 2 * S5_STATE_LANES), BF16),
                        pltpu.VMEM((S5_CHUNK * LANES, LANES), BF16),
                        pltpu.VMEM((2 * S5_STATE_LANES, S5_CHUNK * LANES), BF16),
                        pltpu.VMEM((rows, 2 * S5_STATE_LANES), F32),
                        pltpu.VMEM((rows, 2 * S5_STATE_LANES), F32)],
        compiler_params=_cparams("parallel"),
        name="s5_prompt",
    )(u, *rows_p, *cols_p, *lanes_p)


def _s5_step_body(u_ref, hre_ref, him_ref, lr_re_ref, lr_im_ref, lr_dt_ref, b_re_ref, b_im_ref,
                  lc_re_ref, lc_im_ref, lc_dt_ref, c_re_ref, c_im_ref,
                  lv_re_ref, lv_im_ref, lv_dt_ref, y_ref, ore_ref, oim_ref):
    _, _, bb_re, bb_im = _s5_input_weights(lr_re_ref[...], lr_im_ref[...], lr_dt_ref[...],
                                           b_re_ref[...], b_im_ref[...])
    a_re, a_im, _, _ = _s5_disc(lv_re_ref[...], lv_im_ref[...], lv_dt_ref[...])
    u = u_ref[...]
    h_re, h_im = hre_ref[...], him_ref[...]
    x_re = _dot(u, _bd_rows(bb_re), precision=HIGHEST) + (a_re * h_re - a_im * h_im)
    x_im = _dot(u, _bd_rows(bb_im), precision=HIGHEST) + (a_re * h_im + a_im * h_re)
    ore_ref[...] = x_re
    oim_ref[...] = x_im
    y_ref[...] = (_dot(x_re, _bd_cols(c_re_ref[...]), precision=HIGHEST)
                  - _dot(x_im, _bd_cols(c_im_ref[...]), precision=HIGHEST))


def _s5_step(u, h_re, h_im, layouts):
    nb, d = u.shape
    rows_p, cols_p, lanes_p = layouts
    col = lambda j: (0, j)
    sspec = pl.BlockSpec((nb, S5_STATE_LANES), col)
    return pl.pallas_call(
        _s5_step_body,
        grid=(d // LANES,),
        in_specs=[pl.BlockSpec((nb, LANES), col), sspec, sspec] + _s5_specs(),
        out_specs=[pl.BlockSpec((nb, LANES), col), sspec, sspec],
        out_shape=[jax.ShapeDtypeStruct((nb, d), F32),
                   jax.ShapeDtypeStruct(h_re.shape, F32), jax.ShapeDtypeStruct(h_im.shape, F32)],
        compiler_params=_cparams("parallel"),
        name="s5_step",
    )(u, h_re, h_im, *rows_p, *cols_p, *lanes_p)


PAGE_ROWS = 128
PAGE_LINES = PAGE_ROWS * ROW_LINES
CHUNKS_PER_PAGE = PAGE_ROWS // CMP_STRIDE
CMP_PAGES_PER_STEP = 8
CMP_LINES = 2 * N_KV


def _compress_body(pt_ref, *refs, n_pages):
    del pt_ref
    k_pages = CMP_PAGES_PER_STEP
    srcs = refs[:k_pages]
    (pek_ref, pev_ref, w1k_ref, w1v_ref, w2k_ref, w2v_ref, kc_ref, vc_ref, buf_ref, shift_ref) = refs[k_pages:]
    p = pl.program_id(1)
    for k in range(k_pages):
        r0 = pl.multiple_of((p * k_pages + k) * PAGE_ROWS, PAGE_ROWS)
        for sg in range(CMP_LINES):
            buf_ref[sg, pl.ds(r0, PAGE_ROWS), :] = srcs[k][pl.ds(sg, PAGE_ROWS, stride=ROW_LINES), :]

    @pl.when(p == n_pages // k_pages - 1)
    def _():
        n_chunk = n_pages * CHUNKS_PER_PAGE
        half = CMP_STRIDE
        for slot, (pe_ref, w1_ref, w2_ref, out_ref) in enumerate(
                ((pek_ref, w1k_ref, w2k_ref, kc_ref), (pev_ref, w1v_ref, w2v_ref, vc_ref))):
            for g in range(N_KV):
                sg = slot * N_KV + g
                first = jnp.zeros((n_chunk, HEAD_DIM), F32)
                second = jnp.zeros((n_chunk, HEAD_DIM), F32)
                for l in range(half):
                    x = buf_ref[sg, pl.ds(l, n_chunk, stride=CMP_STRIDE), :]
                    xa = (x + pe_ref[l:l + 1, :]).astype(BF16)
                    xb = (x + pe_ref[half + l:half + l + 1, :]).astype(BF16)
                    first += _dot(xa, w1_ref[l * HEAD_DIM:(l + 1) * HEAD_DIM, :])
                    second += _dot(xb, w1_ref[(half + l) * HEAD_DIM:(half + l + 1) * HEAD_DIM, :])
                shift_ref[0:n_chunk, :] = second
                shift_ref[n_chunk:n_chunk + 8, :] = jnp.zeros((8, HEAD_DIM), F32)
                hid = first + shift_ref[1:n_chunk + 1, :]
                out = _dot(jax.nn.silu(hid).astype(BF16), w2_ref[...])
                live = _iota(out.shape, 0) < n_chunk - 1
                out_ref[g] = jnp.where(live, out, 0.0).astype(BF16)


def _compress(src, page_ids, pe_k, pe_v, w1k, w1v, w2k, w2v):
    nb, n_pages = page_ids.shape
    k_pages = CMP_PAGES_PER_STEP
    n_chunk = n_pages * CHUNKS_PER_PAGE
    const = lambda b, p, pt: (0, 0)
    page = lambda k: pl.BlockSpec((PAGE_LINES, HEAD_DIM),
                                  lambda b, p, pt: (pt[b * n_pages + p * k_pages + k], 0))
    out_spec = pl.BlockSpec((None, N_KV, n_chunk, HEAD_DIM), lambda b, p, pt: (b, 0, 0, 0))
    grid_spec = pltpu.PrefetchScalarGridSpec(
        num_scalar_prefetch=1,
        grid=(nb, n_pages // k_pages),
        in_specs=[page(k) for k in range(k_pages)]
        + [pl.BlockSpec(pe_k.shape, const), pl.BlockSpec(pe_v.shape, const),
           pl.BlockSpec(w1k.shape, const), pl.BlockSpec(w1v.shape, const),
           pl.BlockSpec(w2k.shape, const), pl.BlockSpec(w2v.shape, const)],
        out_specs=[out_spec, out_spec],
        scratch_shapes=[pltpu.VMEM((CMP_LINES, n_pages * PAGE_ROWS, HEAD_DIM), F32),
                        pltpu.VMEM((n_chunk + 8, HEAD_DIM), F32)])
    return pl.pallas_call(
        functools.partial(_compress_body, n_pages=n_pages),
        grid_spec=grid_spec,
        out_shape=[jax.ShapeDtypeStruct((nb, N_KV, n_chunk, HEAD_DIM), BF16)] * 2,
        compiler_params=_cparams("parallel", "arbitrary"),
        name="compress",
    )(page_ids.reshape(-1), *([src] * k_pages), pe_k, pe_v, w1k, w1v, w2k, w2v)


def _masked_softmax(s, mask):
    s = jnp.where(mask, s, NEG)
    m = jnp.max(s, axis=-1, keepdims=True)
    e = jnp.where(mask, jnp.exp(s - m), 0.0)
    return e / jnp.maximum(jnp.sum(e, axis=-1, keepdims=True), 1e-30)


def _overlap(n_rows, n_lanes, n_cmp, n_sel):
    ci, sj = _iota((n_rows, n_lanes), 0), _iota((n_rows, n_lanes), 1)
    hit = ((ci * CMP_STRIDE < sj * SEL_LEN + SEL_LEN) & (ci * CMP_STRIDE + CMP_LEN > sj * SEL_LEN)
           & (ci < n_cmp) & (sj < n_sel))
    return jnp.where(hit, 1.0, 0.0).astype(BF16)


def _importance(imp, lane, qpos, n_sel):
    cur = qpos // SEL_LEN
    forced = (lane == 0) | (lane == cur) | (lane == cur - 1)
    imp = jnp.where(lane * SEL_LEN <= qpos, imp + jnp.where(forced, FORCE_BONUS, 0.0), NEG)
    return jnp.where(lane < n_sel, imp, LOWEST)


def _dot_tn(a, b):
    return lax.dot_general(a, b, (((0,), (0,)), ((), ())), preferred_element_type=F32)


LOG2E = math.log2(math.e)


def _online_attend_t(q, k_ref, v_ref, lo, hi, ck, bias_fn):
    cols = q.shape[0]

    def body(c, carry):
        m, l, acc = carry
        k0 = pl.multiple_of(c * ck, ck)
        bias = jnp.concatenate([bias_fn(k0)] * (cols // Q_BLOCK), axis=1)
        s = _dot_nt(k_ref[pl.ds(k0, ck), :], q) * (ATT_SCALE * LOG2E) + bias
        m_new = jnp.maximum(m, jnp.max(s, axis=0, keepdims=True))
        alpha = jnp.exp2(m - m_new)
        e = jnp.exp2(s - m_new)
        l = alpha * l + jnp.sum(e, axis=0, keepdims=True)
        acc = alpha * acc + _dot_tn(v_ref[pl.ds(k0, ck), :], e.astype(BF16))
        return m_new, l, acc

    init = (jnp.full((1, cols), NEG, F32), jnp.zeros((1, cols), F32), jnp.zeros((HEAD_DIM, cols), F32))
    m, l, acc = lax.fori_loop(lo, hi, body, init)
    return jnp.where(m > NEG, acc / jnp.maximum(l, 1e-30), 0.0)


Q_BLOCK = 128
SEL_KEY_CHUNK = 256
WIN_KEY_CHUNK = 128


def _nsa_prompt_body(qc_ref, qr_ref, gate_ref, ksel_ref, vsel_ref, kwin_ref, vwin_ref,
                     kcmp_ref, vcmp_ref, o_ref, sel_ref, *, n_cmp, n_sel):
    qi = pl.program_id(2)
    q0 = qi * Q_BLOCK
    nh = HEADS_PER_KV
    stack = lambda ref: jnp.concatenate(
        [ref[:, hd * HEAD_DIM:(hd + 1) * HEAD_DIM] for hd in range(nh)], axis=0)
    per_head = lambda x: jnp.concatenate([x] * nh, axis=1)
    qc, qr = stack(qc_ref), stack(qr_ref)
    n_cmp_rows = kcmp_ref.shape[0]
    n_sel_rows = sel_ref.shape[0]

    cshape = (n_cmp_rows, Q_BLOCK)
    cblk, cq = _iota(cshape, 0), q0 + _iota(cshape, 1)
    cmask = per_head(jnp.where((cblk < n_cmp) & (cblk * CMP_STRIDE + CMP_LEN - 1 <= cq), 1.0, 0.0)) > 0.5
    s = jnp.where(cmask, _dot_nt(kcmp_ref[...], qc) * ATT_SCALE, NEG)
    e = jnp.where(cmask, jnp.exp(s - jnp.max(s, axis=0, keepdims=True)), 0.0)
    p_cmp = (e / jnp.maximum(jnp.sum(e, axis=0, keepdims=True), 1e-30)).astype(BF16)
    o_cmp = _dot_tn(vcmp_ref[...], p_cmp)

    oshape = (n_sel_rows, n_cmp_rows)
    sj, ci = _iota(oshape, 0), _iota(oshape, 1)
    hit = ((ci * CMP_STRIDE < sj * SEL_LEN + SEL_LEN) & (ci * CMP_STRIDE + CMP_LEN > sj * SEL_LEN)
           & (ci < n_cmp) & (sj < n_sel))
    imp_h = _dot(jnp.where(hit, 1.0, 0.0).astype(BF16), p_cmp)
    imp = imp_h[:, 0:Q_BLOCK]
    for hd in range(1, nh):
        imp = imp + imp_h[:, hd * Q_BLOCK:(hd + 1) * Q_BLOCK]
    ishape = (n_sel_rows, Q_BLOCK)
    blk = _iota(ishape, 0)
    imp = _importance(imp, blk, q0 + _iota(ishape, 1), n_sel)
    rank = jnp.zeros(ishape, F32)
    for k in range(n_sel):
        row = imp[k:k + 1, :]
        rank += jnp.where((row > imp) | ((row == imp) & (blk > k)), 1.0, 0.0)
    sel_ref[...] = jnp.where((rank < min(SEL_TOP, n_sel)) & (blk < n_sel), 1.0, 0.0)

    def sel_mask(k0):
        shape = (SEL_KEY_CHUNK, Q_BLOCK)
        j0 = k0 // SEL_LEN
        chosen = jnp.concatenate(
            [jnp.broadcast_to(sel_ref[pl.ds(j0 + i, 1), :], (SEL_LEN, Q_BLOCK))
             for i in range(SEL_KEY_CHUNK // SEL_LEN)], axis=0)
        live = (k0 + _iota(shape, 0) <= q0 + _iota(shape, 1)) & (chosen > 0.5)
        return jnp.where(live, 0.0, NEG)

    def win_mask(k0):
        shape = (WIN_KEY_CHUNK, Q_BLOCK)
        kpos = k0 + _iota(shape, 0)
        qp = q0 + _iota(shape, 1)
        return jnp.where((kpos <= qp) & (kpos > qp - WINDOW), 0.0, NEG)

    sel_chunks = (q0 + Q_BLOCK + SEL_KEY_CHUNK - 1) // SEL_KEY_CHUNK
    o_sel = _online_attend_t(qr, ksel_ref, vsel_ref, 0, sel_chunks, SEL_KEY_CHUNK, sel_mask)
    win_lo = jnp.maximum(qi - WINDOW // WIN_KEY_CHUNK, 0)
    o_win = _online_attend_t(qr, kwin_ref, vwin_ref, win_lo, qi + 1, WIN_KEY_CHUNK, win_mask)

    gates = jnp.transpose(gate_ref[...])
    for hd in range(nh):
        cols = slice(hd * Q_BLOCK, (hd + 1) * Q_BLOCK)
        c0 = hd * N_BRANCH
        o = (o_cmp[:, cols] * gates[c0:c0 + 1, :] + o_sel[:, cols] * gates[c0 + 1:c0 + 2, :]
             + o_win[:, cols] * gates[c0 + 2:c0 + 3, :])
        o_ref[:, hd * HEAD_DIM:(hd + 1) * HEAD_DIM] = jnp.transpose(o).astype(BF16)


def _nsa_prompt(qc, qr, gates, kv, kcmp, vcmp):
    bsz, t, dq = qc.shape
    n_cmp = (t - CMP_LEN) // CMP_STRIDE + 1
    n_sel = -(-t // SEL_LEN)
    gw = HEADS_PER_KV * HEAD_DIM
    qspec = pl.BlockSpec((None, Q_BLOCK, gw), lambda b, g, i: (b, i, g))
    kvspec = lambda slot: pl.BlockSpec((None, t, HEAD_DIM), lambda b, g, i: (b, 0, slot * N_KV + g))
    cspec = pl.BlockSpec((None, None, kcmp.shape[2], HEAD_DIM), lambda b, g, i: (b, g, 0, 0))
    return pl.pallas_call(
        functools.partial(_nsa_prompt_body, n_cmp=n_cmp, n_sel=n_sel),
        grid=(bsz, N_KV, t // Q_BLOCK),
        in_specs=[qspec, qspec, pl.BlockSpec((None, Q_BLOCK, LANES), lambda b, g, i: (b, i, g)),
                  kvspec(2), kvspec(3), kvspec(4), kvspec(5), cspec, cspec],
        out_specs=qspec,
        out_shape=jax.ShapeDtypeStruct((bsz, t, dq), BF16),
        scratch_shapes=[pltpu.VMEM((-(-n_sel // 8) * 8, Q_BLOCK), F32)],
        compiler_params=_cparams("parallel", "parallel", "arbitrary"),
        name="nsa_prompt",
    )(qc, qr, gates, kv, kv, kv, kv, kcmp, vcmp)


SEL_LANES = 384


def _select_body(qc_ref, kcmp_ref, vcmp_ref, ocmp_ref, idx_ref, *, qpos, n_cmp, n_sel):
    q = qc_ref[...]
    nc = kcmp_ref.shape[0]
    ci = _iota((HEADS_PER_KV, nc), 1)
    cmask = (ci < n_cmp) & (ci * CMP_STRIDE + CMP_LEN - 1 <= qpos)
    p_cmp = _masked_softmax(_dot_nt(q, kcmp_ref[...]) * ATT_SCALE, cmask).astype(BF16)
    ocmp_ref[...] = _dot(p_cmp, vcmp_ref[...])

    imp = jnp.sum(_dot(p_cmp, _overlap(nc, SEL_LANES, n_cmp, n_sel)), axis=0, keepdims=True)
    lane = _iota((1, SEL_LANES), 1)
    imp = _importance(imp, lane, qpos, n_sel)
    sq = (SEL_LANES, SEL_LANES)
    ri, cj = _iota(sq, 0), _iota(sq, 1)
    impc = jnp.sum(jnp.where(ri == cj, imp, 0.0), axis=1, keepdims=True)
    n_top = min(SEL_TOP, n_sel)
    before_col = jnp.where((impc > imp) | ((impc == imp) & (ri < cj)), 1.0, 0.0)
    before_row = jnp.where((imp > impc) | ((imp == impc) & (cj < ri)), 1.0, 0.0)
    sel_lane = (jnp.sum(before_col, axis=0, keepdims=True) < n_top) & (lane < n_sel)
    sel_row = (jnp.sum(before_row, axis=1, keepdims=True) < n_top) & (_iota((SEL_LANES, 1), 0) < n_sel)
    slot_of = jnp.sum(jnp.where((ri < cj) & sel_row, 1.0, 0.0), axis=0, keepdims=True)
    lshape = (SEL_TOP, SEL_LANES)
    onehot = jnp.where(sel_lane & (slot_of == _iota(lshape, 0).astype(F32)), 1.0, 0.0)
    idx = jnp.sum(onehot * _iota(lshape, 1).astype(F32), axis=1, keepdims=True)
    idx_ref[...] = jnp.broadcast_to(idx, (SEL_TOP, LANES)).astype(jnp.int32)


def _nsa_select(qc, kcmp, vcmp, qpos, n_cmp, n_sel):
    nb = qc.shape[0]
    nc = kcmp.shape[2]
    hspec = pl.BlockSpec((None, None, HEADS_PER_KV, HEAD_DIM), lambda b, g: (b, g, 0, 0))
    cspec = pl.BlockSpec((None, None, nc, HEAD_DIM), lambda b, g: (b, g, 0, 0))
    return pl.pallas_call(
        functools.partial(_select_body, qpos=qpos, n_cmp=n_cmp, n_sel=n_sel),
        grid=(nb, N_KV),
        in_specs=[hspec, cspec, cspec],
        out_specs=[hspec, pl.BlockSpec((None, None, SEL_TOP, LANES), lambda b, g: (b, g, 0, 0))],
        out_shape=[jax.ShapeDtypeStruct((nb, N_KV, HEADS_PER_KV, HEAD_DIM), F32),
                   jax.ShapeDtypeStruct((nb, N_KV, SEL_TOP, LANES), jnp.int32)],
        compiler_params=_cparams("parallel", "parallel"),
        name="nsa_select",
    )(qc, kcmp, vcmp)


def _attend_plus_new(q, k, v, valid, k_new, v_new, new_ok):
    qf = q.astype(F32)
    s = _dot_nt(q, k.astype(BF16)) * ATT_SCALE
    s_new = jnp.sum(qf * k_new.astype(BF16).astype(F32), axis=1, keepdims=True) * ATT_SCALE
    s = jnp.where(valid, s, NEG)
    s_new = jnp.where(new_ok, s_new, NEG)
    m = jnp.maximum(jnp.max(s, axis=-1, keepdims=True), s_new)
    e = jnp.where(valid, jnp.exp(s - m), 0.0)
    e_new = jnp.where(new_ok, jnp.exp(s_new - m), 0.0)
    den = jnp.maximum(jnp.sum(e, axis=-1, keepdims=True) + e_new, 1e-30)
    p = (e / den).astype(BF16)
    p_new = (e_new / den).astype(BF16).astype(F32)
    return _dot(p, v.astype(BF16)) + p_new * v_new.astype(BF16).astype(F32)


SEL_BLOCK_LINES = SEL_LEN * ROW_LINES


def _attend_body(idx_ref, pt_ref, qr_ref, gate_ref, ocmp_ref, *refs, qpos, past_len, n_win):
    del pt_ref
    blocks = refs[:SEL_TOP]
    win_ref, rows_new_ref, win_new_ref, o_ref = refs[SEL_TOP:]
    b, g = pl.program_id(0), pl.program_id(1)
    q = qr_ref[...]
    n_keys = SEL_TOP * SEL_LEN
    base = (b * N_KV + g) * SEL_TOP
    lane = _iota((1, n_keys), 1)
    kpos = jnp.zeros((1, n_keys), jnp.int32)
    n_new = jnp.int32(0)
    for t in range(SEL_TOP):
        j = idx_ref[base + t]
        kpos = jnp.where(lane // SEL_LEN == t, j * SEL_LEN + lane % SEL_LEN, kpos)
        n_new = n_new + jnp.where(j == qpos // SEL_LEN, 1, 0)
    valid = (kpos <= qpos) & (kpos < past_len)
    new_ok = (jnp.zeros((HEADS_PER_KV, 1), jnp.int32) + n_new) > 0
    line = lambda ref, slot, n, per_token: ref[pl.ds(slot * N_KV + g, n, stride=per_token), :]
    k_sel = jnp.concatenate([line(blk, 2, SEL_LEN, ROW_LINES) for blk in blocks], axis=0)
    v_sel = jnp.concatenate([line(blk, 3, SEL_LEN, ROW_LINES) for blk in blocks], axis=0)
    o_sel = _attend_plus_new(q, k_sel, v_sel, valid, line(rows_new_ref, 2, 1, ROW_LINES),
                             line(rows_new_ref, 3, 1, ROW_LINES), new_ok)

    wpos = (past_len - n_win) + _iota((1, n_win), 1)
    wvalid = (wpos <= qpos) & (wpos > qpos - WINDOW) & (wpos >= 0)
    always = jnp.zeros((HEADS_PER_KV, 1), jnp.int32) == 0
    o_win = _attend_plus_new(q, line(win_ref, 0, n_win, WIN_LINES), line(win_ref, 1, n_win, WIN_LINES),
                             wvalid, line(win_new_ref, 0, 1, WIN_LINES), line(win_new_ref, 1, 1, WIN_LINES),
                             always)

    gates = gate_ref[...]
    o_ref[...] = ocmp_ref[...] * gates[:, 0:1] + o_sel * gates[:, 1:2] + o_win * gates[:, 2:3]


def _nsa_attend(idx, page_table, qr, gates, ocmp, cache, cache_win, rows_new, win_new, past_len):
    nb, n_pages = page_table.shape
    n_win = cache_win.shape[0] // (nb * WIN_LINES)
    halves = PAGE_ROWS // SEL_LEN
    n_past_blocks = n_pages * halves

    def sel_block(t):
        def index(b, g, idx_ref, pt_ref):
            j = jnp.minimum(idx_ref[(b * N_KV + g) * SEL_TOP + t], n_past_blocks - 1)
            return (pt_ref[b * n_pages + j // halves] * halves + j % halves, 0)
        return pl.BlockSpec((SEL_BLOCK_LINES, HEAD_DIM), index)

    hspec = pl.BlockSpec((None, None, HEADS_PER_KV, HEAD_DIM), lambda b, g, i, p: (b, g, 0, 0))
    grid_spec = pltpu.PrefetchScalarGridSpec(
        num_scalar_prefetch=2,
        grid=(nb, N_KV),
        in_specs=[hspec, hspec, hspec] + [sel_block(t) for t in range(SEL_TOP)]
        + [pl.BlockSpec((n_win * WIN_LINES, HEAD_DIM), lambda b, g, i, p: (b, 0)),
           pl.BlockSpec((None, ROW_LINES, HEAD_DIM), lambda b, g, i, p: (b, 0, 0)),
           pl.BlockSpec((None, WIN_LINES, HEAD_DIM), lambda b, g, i, p: (b, 0, 0))],
        out_specs=hspec)
    return pl.pallas_call(
        functools.partial(_attend_body, qpos=past_len, past_len=past_len, n_win=n_win),
        grid_spec=grid_spec,
        out_shape=jax.ShapeDtypeStruct((nb, N_KV, HEADS_PER_KV, HEAD_DIM), F32),
        compiler_params=_cparams("parallel", "arbitrary"),
        name="nsa_attend",
    )(idx, page_table.reshape(-1), qr, gates, ocmp, *([cache] * SEL_TOP), cache_win, rows_new, win_new)


PROMPT_FFN_ROWS = 512
PROMPT_ROWS = 256


def _rope_tables(pos):
    half = HEAD_DIM // 2
    inv = jnp.exp(-math.log(ROPE_THETA) * jnp.arange(half, dtype=F32) / half)
    ang = pos.astype(F32)[:, None] * inv[None, :]
    cos, sin = jnp.cos(ang), jnp.sin(ang)
    return jnp.concatenate([cos, cos], axis=1), jnp.concatenate([-sin, sin], axis=1)


def _prepare_weights(prm):
    nq = N_HEADS * HEAD_DIM
    w_qg = prm["attn_w_qg"]
    nb, d = w_qg.shape[:2]
    per_kv = HEADS_PER_KV * N_BRANCH
    w_gate = w_qg[:, :, nq:].reshape(nb, d, N_KV, per_kv)
    w_gate = jnp.pad(w_gate, ((0, 0), (0, 0), (0, 0), (0, LANES - per_kv))).reshape(nb, d, N_KV * LANES)
    return dict(
        ffn16={},
        ple_gate=prm["ple_w_gate"].astype(BF16), ple_proj=prm["ple_w_proj"].astype(BF16),
        glu=prm["glu_w"].astype(BF16), kv=prm["w_kv"].astype(BF16),
        q=w_qg[:, :, :nq].astype(BF16), gate=w_gate.astype(BF16), o=prm["attn_w_o"].astype(BF16),
        w1k=prm["cmp_w1_k"].astype(BF16), w1v=prm["cmp_w1_v"].astype(BF16),
        w2k=prm["cmp_w2_k"].astype(BF16), w2v=prm["cmp_w2_v"].astype(BF16),
        s5=[_s5_layouts(prm["ssm_lam_re"][i], prm["ssm_lam_im"][i], prm["ssm_log_dt"][i],
                        prm["ssm_b_re"][i], prm["ssm_b_im"][i], prm["ssm_c_re"][i], prm["ssm_c_im"][i])
            for i in range(N_A_LAYERS)])


def _run_group(x, p, start, state, past, prm, w):
    bsz, t, d = x.shape
    n = bsz * t
    prompt = past is None
    tm_ffn = PROMPT_FFN_ROWS if prompt else n
    tm = PROMPT_ROWS if prompt else n
    depth = prm["norm_g"].shape[0]
    cos, sin = _rope_tables(start + jnp.arange(t))
    if not prompt:
        cos, sin = jnp.broadcast_to(cos, (n, HEAD_DIM)), jnp.broadcast_to(sin, (n, HEAD_DIM))
    h = x.reshape(n, d)
    ssm_re, ssm_im = [], []
    rows = win = kv16 = kcmp = vcmp = None

    def ffn(h, g_pre, g_post, g_next, layer, which):
        if prompt:
            return _ffn(h, g_pre, g_post, g_next, *w["ffn16"][layer, which], tm_ffn)
        h_new, xn, *w16 = _ffn_cast(h, g_pre, g_post, g_next, prm["ffn_w_gate"], prm["ffn_w_up"],
                                    prm["ffn_w_down"], layer, which)
        w["ffn16"][layer, which] = w16
        return h_new, xn

    if not prompt:
        cache_kv, cache_win, page_table = past
        past_len = page_table.shape[1] * cache_kv.shape[1]
        cache_lines = cache_kv.reshape(-1, HEAD_DIM)
        win_lines = cache_win.reshape(-1, HEAD_DIM)
    for i in range(depth):
        g = prm["norm_g"][i]
        h, xn = ffn(h, g[0:1], g[1:2], g[2:3], i, 0)
        if i < N_A_LAYERS:
            if prompt:
                y, st = _s5_prompt(xn, bsz, w["s5"][i])
                st = st.reshape(d // LANES, bsz, 2, S5_GROUPS_PER_STEP, SSM_STATE).transpose(2, 1, 0, 3, 4)
                st = st.reshape(2, bsz, d // SSM_GROUP, SSM_STATE)
                ssm_re.append(st[0])
                ssm_im.append(st[1])
            else:
                y, s_re, s_im = _s5_step(xn, state[0][i].reshape(bsz, -1), state[1][i].reshape(bsz, -1),
                                         w["s5"][i])
                ssm_re.append(s_re.reshape(state[0][i].shape))
                ssm_im.append(s_im.reshape(state[1][i].shape))
            h = _s5_gate(h, xn, y, prm["ssm_d"][i][None], w["glu"], prm["glu_b"][i][None], g[3:4], i, tm)
        else:
            j = i - N_A_LAYERS
            qc, qr, gates = _qg_proj(xn, w["q"], w["gate"], cos, sin, j, tm)
            if prompt:
                o = _nsa_prompt(qc.reshape(bsz, t, -1), qr.reshape(bsz, t, -1), gates.reshape(bsz, t, -1),
                                kv16.reshape(bsz, t, -1), kcmp, vcmp).reshape(n, -1)
            else:
                heads = lambda a: a.reshape(bsz, N_KV, HEADS_PER_KV, HEAD_DIM)
                n_cmp = (past_len + t - CMP_LEN) // CMP_STRIDE + 1
                n_sel = -(-(past_len + t) // SEL_LEN)
                ocmp, idx = _nsa_select(heads(qc), kcmp, vcmp, past_len, n_cmp, n_sel)
                gsm = gates.reshape(bsz, N_KV, LANES)[:, :, :HEADS_PER_KV * N_BRANCH]
                gsm = gsm.reshape(bsz, N_KV, HEADS_PER_KV, N_BRANCH)
                gsm = jnp.pad(gsm, ((0, 0), (0, 0), (0, 0), (0, LANES - N_BRANCH)))
                o = _nsa_attend(idx[..., 0].reshape(-1), page_table, heads(qr), gsm, ocmp,
                                cache_lines, win_lines, rows_l.reshape(bsz, ROW_LINES, HEAD_DIM),
                                win_l.reshape(bsz, WIN_LINES, HEAD_DIM), past_len).reshape(n, -1)
            h = _o_proj(h, o, w["o"], g[3:4], j, tm)
        h, _ = ffn(h, g[4:5], g[5:6], g[6:7], i, 1)
        h = _ple(h, p[i].reshape(n, -1), w["ple_gate"], w["ple_proj"], g[6:7], i, tm)
        if i == N_A_LAYERS - 1:
            rows_l, win_l, kv16 = _kv_proj(h, prm["kv_norm_g"][None], w["kv"], cos, sin, tm)
            rows = rows_l.reshape(bsz, t, ROW_SLOTS, N_KV, HEAD_DIM)
            win = win_l.reshape(bsz, t, WIN_SLOTS, N_KV, HEAD_DIM)
            cmp_w = (prm["cmp_pe_k"], prm["cmp_pe_v"], w["w1k"], w["w1v"], w["w2k"], w["w2v"])
            if prompt:
                pages = jnp.arange(n // PAGE_ROWS, dtype=jnp.int32).reshape(bsz, t // PAGE_ROWS)
                kcmp, vcmp = _compress(rows_l, pages, *cmp_w)
                win = win[:, t - min(WINDOW, t):]
            else:
                kcmp, vcmp = _compress(cache_lines, page_table, *cmp_w)
                win = jnp.concatenate([cache_win, win], axis=1)[:, t:]
    return h.reshape(bsz, t, d), jnp.stack(ssm_re), jnp.stack(ssm_im), rows, win


def kernel(x_prompt, x_sample, state_ssm_re, state_ssm_im, cache_kv, cache_win, page_table,
           p_prompt, p_sample, norm_g, ffn_w_gate, ffn_w_up, ffn_w_down, ple_w_proj, ple_w_gate,
           ssm_lam_re, ssm_lam_im, ssm_log_dt, ssm_b_re, ssm_b_im, ssm_c_re, ssm_c_im, ssm_d,
           glu_w, glu_b, kv_norm_g, w_kv, cmp_pe_k, cmp_pe_v, cmp_w1_k, cmp_w2_k, cmp_w1_v,
           cmp_w2_v, attn_w_qg, attn_w_o):
    prm = dict(norm_g=norm_g, ffn_w_gate=ffn_w_gate, ffn_w_up=ffn_w_up, ffn_w_down=ffn_w_down,
               ple_w_proj=ple_w_proj, ple_w_gate=ple_w_gate, ssm_lam_re=ssm_lam_re,
               ssm_lam_im=ssm_lam_im, ssm_log_dt=ssm_log_dt, ssm_b_re=ssm_b_re, ssm_b_im=ssm_b_im,
               ssm_c_re=ssm_c_re, ssm_c_im=ssm_c_im, ssm_d=ssm_d, glu_w=glu_w, glu_b=glu_b,
               kv_norm_g=kv_norm_g, w_kv=w_kv, cmp_pe_k=cmp_pe_k, cmp_pe_v=cmp_pe_v,
               cmp_w1_k=cmp_w1_k, cmp_w2_k=cmp_w2_k, cmp_w1_v=cmp_w1_v, cmp_w2_v=cmp_w2_v,
               attn_w_qg=attn_w_qg, attn_w_o=attn_w_o)
    assert x_sample.shape[1] == 1, "the decode group handles one new token per sequence"
    assert cache_kv.shape[1] == PAGE_ROWS
    w = _prepare_weights(prm)
    past_len = page_table.shape[1] * cache_kv.shape[1]
    y_s, re_s, im_s, kv_s, win_s = _run_group(x_sample, p_sample, past_len, (state_ssm_re, state_ssm_im),
                                              (cache_kv, cache_win, page_table), prm, w)
    y_p, re_p, im_p, kv_p, win_p = _run_group(x_prompt, p_prompt, 0, None, None, prm, w)
    return (y_p, y_s, re_p, im_p, re_s, im_s, kv_p, kv_s, win_p, win_s)
```

```python
import functools
import math

import jax
import jax.numpy as jnp
from jax import lax
from jax.experimental import pallas as pl
from jax.experimental.pallas import tpu as pltpu

F32 = jnp.float32
BF16 = jnp.bfloat16
HIGHEST = lax.Precision.HIGHEST

N_A_LAYERS = 2
NORM_EPS = 1e-6
SSM_GROUP = 16
SSM_STATE = 64
HEAD_DIM = 128
N_KV = 2
HEADS_PER_KV = 8
N_HEADS = N_KV * HEADS_PER_KV
N_BRANCH = 3
N_KV_SLOTS = 6
CMP_LEN = 32
CMP_STRIDE = 16
SEL_LEN = 64
SEL_TOP = 16
WINDOW = 512
ROPE_THETA = 10000.0
FORCE_BONUS = 1e3
NEG = -1e30
LOWEST = -3.0e38
ATT_SCALE = HEAD_DIM ** -0.5

LANES = 128
VMEM_LIMIT_BYTES = 56 * 2 ** 20
FFN_TILE_F = 512
S5_CHUNK = 16
S5_GROUPS_PER_STEP = LANES // SSM_GROUP
S5_STATE_LANES = S5_GROUPS_PER_STEP * SSM_STATE


def _cparams(*sem):
    return pltpu.CompilerParams(dimension_semantics=sem, vmem_limit_bytes=VMEM_LIMIT_BYTES)


def _rms(x, g):
    y = x * lax.rsqrt(jnp.mean(x * x, axis=-1, keepdims=True) + NORM_EPS)
    return y * g


def _dot(a, b, **kw):
    return jnp.dot(a, b, preferred_element_type=F32, **kw)


def _dot_nt(a, b):
    return lax.dot_general(a, b, (((1,), (1,)), ((), ())), preferred_element_type=F32)


def _rope(x, cos, sin):
    return x * cos + pltpu.roll(x, HEAD_DIM // 2, 1) * sin


def _iota(shape, axis):
    return lax.broadcasted_iota(jnp.int32, shape, axis)


def _ffn_step(k, nk, h_ref, gpre_ref, gpost_ref, gnext_ref, wg_ref, wu_ref, wd_ref, o_ref, xn_out_ref,
              xn_ref, acc_ref):
    @pl.when(k == 0)
    def _():
        xn_ref[...] = _rms(h_ref[...], gpre_ref[...]).astype(BF16)
        acc_ref[...] = jnp.zeros_like(acc_ref)

    xn = xn_ref[...]
    gate = _dot(xn, wg_ref[...])
    up = _dot(xn, wu_ref[...])
    mid = (jax.nn.silu(gate) * up).astype(BF16)
    acc_ref[...] += _dot(mid, wd_ref[...])

    @pl.when(k == nk - 1)
    def _():
        h_new = h_ref[...] + 0.5 * _rms(acc_ref[...], gpost_ref[...])
        o_ref[...] = h_new
        xn_out_ref[...] = _rms(h_new, gnext_ref[...])


def _ffn_body(h_ref, gpre_ref, gpost_ref, gnext_ref, wg_ref, wu_ref, wd_ref, o_ref, xn_out_ref,
              xn_ref, acc_ref, *, nk):
    _ffn_step(pl.program_id(1), nk, h_ref, gpre_ref, gpost_ref, gnext_ref,
              wg_ref, wu_ref, wd_ref, o_ref, xn_out_ref, xn_ref, acc_ref)


def _ffn(h, g_pre, g_post, g_next, wg, wu, wd, tm):
    n, d = h.shape
    nk = wg.shape[-1] // FFN_TILE_F
    row = lambda i, k: (i, 0)
    vec = lambda i, k: (0, 0)
    return pl.pallas_call(
        functools.partial(_ffn_body, nk=nk),
        grid=(n // tm, nk),
        in_specs=[pl.BlockSpec((tm, d), row), pl.BlockSpec((1, d), vec), pl.BlockSpec((1, d), vec),
                  pl.BlockSpec((1, d), vec),
                  pl.BlockSpec((d, FFN_TILE_F), lambda i, k: (0, k)),
                  pl.BlockSpec((d, FFN_TILE_F), lambda i, k: (0, k)),
                  pl.BlockSpec((FFN_TILE_F, d), lambda i, k: (k, 0))],
        out_specs=[pl.BlockSpec((tm, d), row), pl.BlockSpec((tm, d), row)],
        out_shape=[jax.ShapeDtypeStruct((n, d), F32)] * 2,
        scratch_shapes=[pltpu.VMEM((tm, d), BF16), pltpu.VMEM((tm, d), F32)],
        compiler_params=_cparams("parallel", "arbitrary"),
        name="ffn",
    )(h, g_pre, g_post, g_next, wg, wu, wd)


def _ffn_cast_body(h_ref, gpre_ref, gpost_ref, gnext_ref, wg_ref, wu_ref, wd_ref,
                   o_ref, xn_out_ref, wg16_ref, wu16_ref, wd16_ref, xn_ref, acc_ref, *, nk, d_ff):
    k = pl.program_id(1)
    live = d_ff - k * FFN_TILE_F
    wshape, dshape = wg_ref.shape, wd_ref.shape
    wg16_ref[...] = jnp.where(_iota(wshape, 1) < live, wg_ref[...], 0.0).astype(BF16)
    wu16_ref[...] = jnp.where(_iota(wshape, 1) < live, wu_ref[...], 0.0).astype(BF16)
    wd16_ref[...] = jnp.where(_iota(dshape, 0) < live, wd_ref[...], 0.0).astype(BF16)
    _ffn_step(k, nk, h_ref, gpre_ref, gpost_ref, gnext_ref, wg16_ref, wu16_ref, wd16_ref,
              o_ref, xn_out_ref, xn_ref, acc_ref)


def _ffn_cast(h, g_pre, g_post, g_next, wg, wu, wd, layer, which):
    n, d = h.shape
    d_ff = wg.shape[-1]
    nk = -(-d_ff // FFN_TILE_F)
    fp = nk * FFN_TILE_F
    row = lambda i, k: (0, 0)
    return pl.pallas_call(
        functools.partial(_ffn_cast_body, nk=nk, d_ff=d_ff),
        grid=(1, nk),
        in_specs=[pl.BlockSpec((n, d), row), pl.BlockSpec((1, d), row), pl.BlockSpec((1, d), row),
                  pl.BlockSpec((1, d), row),
                  pl.BlockSpec((None, None, d, FFN_TILE_F), lambda i, k: (layer, which, 0, k)),
                  pl.BlockSpec((None, None, d, FFN_TILE_F), lambda i, k: (layer, which, 0, k)),
                  pl.BlockSpec((None, None, FFN_TILE_F, d), lambda i, k: (layer, which, k, 0))],
        out_specs=[pl.BlockSpec((n, d), row), pl.BlockSpec((n, d), row),
                   pl.BlockSpec((d, FFN_TILE_F), lambda i, k: (0, k)),
                   pl.BlockSpec((d, FFN_TILE_F), lambda i, k: (0, k)),
                   pl.BlockSpec((FFN_TILE_F, d), lambda i, k: (k, 0))],
        out_shape=[jax.ShapeDtypeStruct((n, d), F32)] * 2
        + [jax.ShapeDtypeStruct((d, fp), BF16)] * 2 + [jax.ShapeDtypeStruct((fp, d), BF16)],
        scratch_shapes=[pltpu.VMEM((n, d), BF16), pltpu.VMEM((n, d), F32)],
        compiler_params=_cparams("arbitrary", "arbitrary"),
        name="ffn_cast",
    )(h, g_pre, g_post, g_next, wg, wu, wd)


def _ple_body(h_ref, p_ref, wgate_ref, wproj_ref, g_ref, o_ref):
    h = h_ref[...]
    gate = jax.nn.sigmoid(_dot(h.astype(BF16), wgate_ref[...]))
    proj = _dot(p_ref[...].astype(BF16), wproj_ref[...])
    o_ref[...] = h + _rms(gate * proj, g_ref[...])


def _ple(h, p, wgate, wproj, g, layer, tm):
    n, d = h.shape
    pd = p.shape[-1]
    row = lambda i: (i, 0)
    return pl.pallas_call(
        _ple_body,
        grid=(n // tm,),
        in_specs=[pl.BlockSpec((tm, d), row), pl.BlockSpec((tm, pd), row),
                  pl.BlockSpec((None, d, d), lambda i: (layer, 0, 0)),
                  pl.BlockSpec((None, pd, d), lambda i: (layer, 0, 0)),
                  pl.BlockSpec((1, d), lambda i: (0, 0))],
        out_specs=pl.BlockSpec((tm, d), row),
        out_shape=jax.ShapeDtypeStruct((n, d), F32),
        compiler_params=_cparams("parallel"),
        name="ple",
    )(h, p, wgate, wproj, g)


ROW_SLOTS = 4
WIN_SLOTS = N_KV_SLOTS - ROW_SLOTS
ROW_LINES = ROW_SLOTS * N_KV
WIN_LINES = WIN_SLOTS * N_KV


def _kv_body(h_ref, g_ref, w_ref, cos_ref, sin_ref, rows_ref, win_ref, kv16_ref, *, tm):
    xn = _rms(h_ref[...], g_ref[...]).astype(BF16)
    kv = _dot(xn, w_ref[...])
    cos, sin = cos_ref[...], sin_ref[...]
    for slot in range(N_KV_SLOTS):
        for g in range(N_KV):
            line = slot * N_KV + g
            lo = line * HEAD_DIM
            x = kv[:, lo:lo + HEAD_DIM]
            if slot in (2, 4):
                x = _rope(x, cos, sin)
            kv16_ref[:, lo:lo + HEAD_DIM] = x.astype(BF16)
            if slot < ROW_SLOTS:
                rows_ref[pl.ds(line, tm, stride=ROW_LINES), :] = x
            else:
                win_ref[pl.ds(line - ROW_LINES, tm, stride=WIN_LINES), :] = x


def _kv_proj(h, g, w, cos, sin, tm):
    n, d = h.shape
    nw = w.shape[-1]
    nt = cos.shape[0] // tm
    row = lambda i: (i, 0)
    tab = lambda i: (i % nt, 0)
    return pl.pallas_call(
        functools.partial(_kv_body, tm=tm),
        grid=(n // tm,),
        in_specs=[pl.BlockSpec((tm, d), row), pl.BlockSpec((1, d), lambda i: (0, 0)),
                  pl.BlockSpec((d, nw), lambda i: (0, 0)),
                  pl.BlockSpec((tm, HEAD_DIM), tab), pl.BlockSpec((tm, HEAD_DIM), tab)],
        out_specs=[pl.BlockSpec((tm * ROW_LINES, HEAD_DIM), row),
                   pl.BlockSpec((tm * WIN_LINES, HEAD_DIM), row),
                   pl.BlockSpec((tm, nw), row)],
        out_shape=[jax.ShapeDtypeStruct((n * ROW_LINES, HEAD_DIM), F32),
                   jax.ShapeDtypeStruct((n * WIN_LINES, HEAD_DIM), F32),
                   jax.ShapeDtypeStruct((n, nw), BF16)],
        compiler_params=_cparams("parallel"),
        name="kv_proj",
    )(h, g, w, cos, sin)


def _qg_body(xn_ref, wq_ref, wgt_ref, cos_ref, sin_ref, qc_ref, qr_ref, gate_ref):
    xn = xn_ref[...].astype(BF16)
    q = _dot(xn, wq_ref[...])
    qc_ref[...] = q.astype(BF16)
    cos, sin = cos_ref[...], sin_ref[...]
    for hd in range(N_HEADS):
        sl = slice(hd * HEAD_DIM, (hd + 1) * HEAD_DIM)
        qr_ref[:, sl] = _rope(q[:, sl], cos, sin).astype(BF16)
    gate_ref[...] = jax.nn.sigmoid(_dot(xn, wgt_ref[...]))


def _qg_proj(xn, wq, wgt, cos, sin, layer, tm):
    n, d = xn.shape
    nq, ng = wq.shape[-1], wgt.shape[-1]
    nt = cos.shape[0] // tm
    row = lambda i: (i, 0)
    tab = lambda i: (i % nt, 0)
    return pl.pallas_call(
        _qg_body,
        grid=(n // tm,),
        in_specs=[pl.BlockSpec((tm, d), row),
                  pl.BlockSpec((None, d, nq), lambda i: (layer, 0, 0)),
                  pl.BlockSpec((None, d, ng), lambda i: (layer, 0, 0)),
                  pl.BlockSpec((tm, HEAD_DIM), tab), pl.BlockSpec((tm, HEAD_DIM), tab)],
        out_specs=[pl.BlockSpec((tm, nq), row), pl.BlockSpec((tm, nq), row), pl.BlockSpec((tm, ng), row)],
        out_shape=[jax.ShapeDtypeStruct((n, nq), BF16), jax.ShapeDtypeStruct((n, nq), BF16),
                   jax.ShapeDtypeStruct((n, ng), F32)],
        compiler_params=_cparams("parallel"),
        name="qg_proj",
    )(xn, wq, wgt, cos, sin)


def _oproj_body(h_ref, o_ref, w_ref, g_ref, out_ref):
    m = _dot(o_ref[...].astype(BF16), w_ref[...])
    out_ref[...] = h_ref[...] + _rms(m, g_ref[...])


def _o_proj(h, o, w, g, layer, tm):
    n, d = h.shape
    row = lambda i: (i, 0)
    return pl.pallas_call(
        _oproj_body,
        grid=(n // tm,),
        in_specs=[pl.BlockSpec((tm, d), row), pl.BlockSpec((tm, d), row),
                  pl.BlockSpec((None, d, d), lambda i: (layer, 0, 0)),
                  pl.BlockSpec((1, d), lambda i: (0, 0))],
        out_specs=pl.BlockSpec((tm, d), row),
        out_shape=jax.ShapeDtypeStruct((n, d), F32),
        compiler_params=_cparams("parallel"),
        name="o_proj",
    )(h, o, w, g)


def _s5gate_body(h_ref, xn_ref, y_ref, d_ref, w_ref, b_ref, g_ref, out_ref):
    y = jax.nn.gelu(y_ref[...] + d_ref[...] * xn_ref[...])
    z = jax.nn.sigmoid(_dot(y.astype(BF16), w_ref[...]) + b_ref[...])
    out_ref[...] = h_ref[...] + _rms(y * z, g_ref[...])


def _s5_gate(h, xn, y, dskip, w, b, g, layer, tm):
    n, d = h.shape
    row = lambda i: (i, 0)
    vec = lambda i: (0, 0)
    return pl.pallas_call(
        _s5gate_body,
        grid=(n // tm,),
        in_specs=[pl.BlockSpec((tm, d), row), pl.BlockSpec((tm, d), row), pl.BlockSpec((tm, d), row),
                  pl.BlockSpec((1, d), vec),
                  pl.BlockSpec((None, d, d), lambda i: (layer, 0, 0)),
                  pl.BlockSpec((1, d), vec), pl.BlockSpec((1, d), vec)],
        out_specs=pl.BlockSpec((tm, d), row),
        out_shape=jax.ShapeDtypeStruct((n, d), F32),
        compiler_params=_cparams("parallel"),
        name="s5_gate",
    )(h, xn, y, dskip, w, b, g)


def _s5_disc(lam_re, lam_im, log_dt):
    dt = jnp.exp(log_dt)
    mag = jnp.exp(lam_re * dt)
    a_re, a_im = mag * jnp.cos(lam_im * dt), mag * jnp.sin(lam_im * dt)
    den = lam_re * lam_re + lam_im * lam_im
    nr, ni = a_re - 1.0, a_im
    r_re = (nr * lam_re + ni * lam_im) / den
    r_im = (ni * lam_re - nr * lam_im) / den
    return a_re, a_im, r_re, r_im


def _cmul(ar, ai, br, bi):
    return ar * br - ai * bi, ar * bi + ai * br


def _bd_rows(m):
    shape = (LANES, S5_STATE_LANES)
    same = _iota(shape, 0) // SSM_GROUP == _iota(shape, 1) // SSM_STATE
    return jnp.where(same, jnp.concatenate([m] * S5_GROUPS_PER_STEP, axis=1), 0.0)


def _bd_cols(m):
    shape = (S5_STATE_LANES, LANES)
    same = _iota(shape, 0) // SSM_STATE == _iota(shape, 1) // SSM_GROUP
    return jnp.where(same, jnp.concatenate([m] * S5_GROUPS_PER_STEP, axis=0), 0.0)


def _s5_input_weights(lam_re, lam_im, log_dt, b_re, b_im):
    a_re, a_im, r_re, r_im = _s5_disc(lam_re, lam_im, log_dt)
    bb_re, bb_im = _cmul(r_re, r_im, b_re, b_im)
    return a_re, a_im, bb_re, bb_im


def _s5_prompt_body(u_ref, lr_re_ref, lr_im_ref, lr_dt_ref, b_re_ref, b_im_ref,
                    lc_re_ref, lc_im_ref, lc_dt_ref, c_re_ref, c_im_ref,
                    lv_re_ref, lv_im_ref, lv_dt_ref,
                    y_ref, st_ref,
                    sin_ref, toep_ref, fout_ref, v_ref, xp_ref, *, n_batch, n_chunk):
    L = S5_CHUNK
    rows = n_batch * n_chunk
    half = S5_STATE_LANES

    a_re, a_im, w_re, w_im = _s5_input_weights(lr_re_ref[...], lr_im_ref[...], lr_dt_ref[...],
                                               b_re_ref[...], b_im_ref[...])
    ws_re, ws_im = [w_re], [w_im]
    for _ in range(L - 1):
        w_re, w_im = _cmul(a_re, a_im, w_re, w_im)
        ws_re.append(w_re)
        ws_im.append(w_im)
    for s in range(L):
        k = L - 1 - s
        sin_ref[s * LANES:(s + 1) * LANES, :] = jnp.concatenate(
            [_bd_rows(ws_re[k]), _bd_rows(ws_im[k])], axis=1).astype(BF16)
    stack_re = jnp.concatenate([ws_re[L - 1 - s] for s in range(L)], axis=0)
    stack_im = jnp.concatenate([ws_im[L - 1 - s] for s in range(L)], axis=0)
    taps = (_dot(stack_re, c_re_ref[...], precision=HIGHEST)
            - _dot(stack_im, c_im_ref[...], precision=HIGHEST))
    tshape = (L * LANES, LANES)
    same = (_iota(tshape, 0) % LANES) // SSM_GROUP == _iota(tshape, 1) // SSM_GROUP
    taps = jnp.where(same, taps, 0.0).astype(BF16)
    toep_ref[:, LANES:] = taps
    toep_ref[:(L - 1) * LANES, :LANES] = taps[LANES:]
    toep_ref[(L - 1) * LANES:, :LANES] = jnp.zeros((LANES, LANES), BF16)
    ac_re, ac_im, _, _ = _s5_disc(lc_re_ref[...], lc_im_ref[...], lc_dt_ref[...])
    ca_re, ca_im = c_re_ref[...], c_im_ref[...]
    for t in range(L):
        ca_re, ca_im = _cmul(ac_re, ac_im, ca_re, ca_im)
        fout_ref[:, t * LANES:(t + 1) * LANES] = jnp.concatenate(
            [_bd_cols(ca_re), -_bd_cols(ca_im)], axis=0).astype(BF16)
    al_re, al_im, _, _ = _s5_disc(lv_re_ref[...], lv_im_ref[...], lv_dt_ref[...])
    for _ in range(int(math.log2(L))):
        al_re, al_im = _cmul(al_re, al_im, al_re, al_im)

    xb = jnp.concatenate([u_ref[pl.ds(s, rows, stride=L), :] for s in range(L)], axis=1).astype(BF16)
    v_ref[...] = _dot(xb, sin_ref[...])

    def step(c, carry):
        new = []
        for b in range(n_batch):
            r = b * n_chunk + c
            x_re, x_im = carry[b]
            xp_ref[pl.ds(r, 1), :] = jnp.concatenate([x_re, x_im], axis=1)
            inc = v_ref[pl.ds(r, 1), :]
            new.append((al_re * x_re - al_im * x_im + inc[:, :half],
                        al_re * x_im + al_im * x_re + inc[:, half:]))
        return tuple(new)

    zero = jnp.zeros((1, half), F32)
    final = lax.fori_loop(0, n_chunk, step, tuple((zero, zero) for _ in range(n_batch)))
    for b in range(n_batch):
        st_ref[b:b + 1, :] = jnp.concatenate(final[b], axis=1)

    xpb = xp_ref[...].astype(BF16)
    for t in range(0, L, 2):
        pair = (_dot(xb[:, :(t + 2) * LANES], toep_ref[(L - 2 - t) * LANES:, :])
                + _dot(xpb, fout_ref[:, t * LANES:(t + 2) * LANES]))
        y_ref[pl.ds(t, rows, stride=L), :] = pair[:, :LANES]
        y_ref[pl.ds(t + 1, rows, stride=L), :] = pair[:, LANES:]


def _s5_layouts(lam_re, lam_im, log_dt, b_re, b_im, c_re, c_im):
    n_g = lam_re.shape[0]
    ldt = jnp.broadcast_to(log_dt[:, None], lam_re.shape)
    rep = lambda a: jnp.repeat(a, SSM_GROUP, axis=0)
    rows = (rep(lam_re), rep(lam_im), rep(ldt),
            b_re.transpose(0, 2, 1).reshape(n_g * SSM_GROUP, SSM_STATE),
            b_im.transpose(0, 2, 1).reshape(n_g * SSM_GROUP, SSM_STATE))
    cols = (rep(lam_re).T, rep(lam_im).T, rep(ldt).T,
            c_re.reshape(n_g * SSM_GROUP, SSM_STATE).T, c_im.reshape(n_g * SSM_GROUP, SSM_STATE).T)
    n_j = n_g // S5_GROUPS_PER_STEP
    lanes = tuple(a.reshape(n_j, 1, S5_STATE_LANES) for a in (lam_re, lam_im, ldt))
    return rows, cols, lanes


def _s5_specs():
    rspec = pl.BlockSpec((LANES, SSM_STATE), lambda j: (j, 0))
    cspec = pl.BlockSpec((SSM_STATE, LANES), lambda j: (0, j))
    vspec = pl.BlockSpec((None, 1, S5_STATE_LANES), lambda j: (j, 0, 0))
    return [rspec] * 5 + [cspec] * 5 + [vspec] * 3


def _s5_prompt(u, n_batch, layouts):
    n, d = u.shape
    n_j = d // LANES
    n_chunk = n // n_batch // S5_CHUNK
    rows = n // S5_CHUNK
    rows_p, cols_p, lanes_p = layouts
    col = lambda j: (0, j)
    return pl.pallas_call(
        functools.partial(_s5_prompt_body, n_batch=n_batch, n_chunk=n_chunk),
        grid=(n_j,),
        in_specs=[pl.BlockSpec((n, LANES), col)] + _s5_specs(),
        out_specs=[pl.BlockSpec((n, LANES), col),
                   pl.BlockSpec((None, n_batch, 2 * S5_STATE_LANES), lambda j: (j, 0, 0))],
        out_shape=[jax.ShapeDtypeStruct((n, d), F32),
                   jax.ShapeDtypeStruct((n_j, n_batch, 2 * S5_STATE_LANES), F32)],
        scratch_shapes=[pltpu.VMEM((S5_CHUNK * LANES, 2 * S5_STATE_LANES), BF16),
                        pltpu.VMEM((S5_CHUNK * LANES, 2 * LANES), BF16),
                        pltpu.VMEM((2 * S5_STATE_LANES, S5_CHUNK * LANES), BF16),
                        pltpu.VMEM((rows, 2 * S5_STATE_LANES), F32),
                        pltpu.VMEM((rows, 2 * S5_STATE_LANES), F32)],
        compiler_params=_cparams("parallel"),
        name="s5_prompt",
    )(u, *rows_p, *cols_p, *lanes_p)


def _s5_step_body(u_ref, hre_ref, him_ref, lr_re_ref, lr_im_ref, lr_dt_ref, b_re_ref, b_im_ref,
                  lc_re_ref, lc_im_ref, lc_dt_ref, c_re_ref, c_im_ref,
                  lv_re_ref, lv_im_ref, lv_dt_ref, y_ref, ore_ref, oim_ref):
    _, _, bb_re, bb_im = _s5_input_weights(lr_re_ref[...], lr_im_ref[...], lr_dt_ref[...],
                                           b_re_ref[...], b_im_ref[...])
    a_re, a_im, _, _ = _s5_disc(lv_re_ref[...], lv_im_ref[...], lv_dt_ref[...])
    u = u_ref[...]
    h_re, h_im = hre_ref[...], him_ref[...]
    x_re = _dot(u, _bd_rows(bb_re), precision=HIGHEST) + (a_re * h_re - a_im * h_im)
    x_im = _dot(u, _bd_rows(bb_im), precision=HIGHEST) + (a_re * h_im + a_im * h_re)
    ore_ref[...] = x_re
    oim_ref[...] = x_im
    y_ref[...] = (_dot(x_re, _bd_cols(c_re_ref[...]), precision=HIGHEST)
                  - _dot(x_im, _bd_cols(c_im_ref[...]), precision=HIGHEST))


def _s5_step(u, h_re, h_im, layouts):
    nb, d = u.shape
    rows_p, cols_p, lanes_p = layouts
    col = lambda j: (0, j)
    sspec = pl.BlockSpec((nb, S5_STATE_LANES), col)
    return pl.pallas_call(
        _s5_step_body,
        grid=(d // LANES,),
        in_specs=[pl.BlockSpec((nb, LANES), col), sspec, sspec] + _s5_specs(),
        out_specs=[pl.BlockSpec((nb, LANES), col), sspec, sspec],
        out_shape=[jax.ShapeDtypeStruct((nb, d), F32),
                   jax.ShapeDtypeStruct(h_re.shape, F32), jax.ShapeDtypeStruct(h_im.shape, F32)],
        compiler_params=_cparams("parallel"),
        name="s5_step",
    )(u, h_re, h_im, *rows_p, *cols_p, *lanes_p)


PAGE_ROWS = 128
PAGE_LINES = PAGE_ROWS * ROW_LINES
CHUNKS_PER_PAGE = PAGE_ROWS // CMP_STRIDE
CMP_PAGES_PER_STEP = 8
CMP_LINES = 2 * N_KV


def _compress_body(pt_ref, *refs, n_pages):
    del pt_ref
    k_pages = CMP_PAGES_PER_STEP
    srcs = refs[:k_pages]
    (pek_ref, pev_ref, w1k_ref, w1v_ref, w2k_ref, w2v_ref, kc_ref, vc_ref, buf_ref, shift_ref) = refs[k_pages:]
    p = pl.program_id(1)
    for k in range(k_pages):
        r0 = pl.multiple_of((p * k_pages + k) * PAGE_ROWS, PAGE_ROWS)
        for sg in range(CMP_LINES):
            buf_ref[sg, pl.ds(r0, PAGE_ROWS), :] = srcs[k][pl.ds(sg, PAGE_ROWS, stride=ROW_LINES), :]

    @pl.when(p == n_pages // k_pages - 1)
    def _():
        n_chunk = n_pages * CHUNKS_PER_PAGE
        half = CMP_STRIDE
        for slot, (pe_ref, w1_ref, w2_ref, out_ref) in enumerate(
                ((pek_ref, w1k_ref, w2k_ref, kc_ref), (pev_ref, w1v_ref, w2v_ref, vc_ref))):
            for g in range(N_KV):
                sg = slot * N_KV + g
                xs = [buf_ref[sg, pl.ds(l, n_chunk, stride=CMP_STRIDE), :] for l in range(half)]
                as_first = jnp.concatenate(
                    [(x + pe_ref[l:l + 1, :]).astype(BF16) for l, x in enumerate(xs)], axis=1)
                as_second = jnp.concatenate(
                    [(x + pe_ref[half + l:half + l + 1, :]).astype(BF16) for l, x in enumerate(xs)], axis=1)
                first = _dot(as_first, w1_ref[:half * HEAD_DIM, :])
                shift_ref[0:n_chunk, :] = _dot(as_second, w1_ref[half * HEAD_DIM:, :])
                shift_ref[n_chunk:n_chunk + 8, :] = jnp.zeros((8, HEAD_DIM), F32)
                hid = first + shift_ref[1:n_chunk + 1, :]
                out = _dot(jax.nn.silu(hid).astype(BF16), w2_ref[...])
                live = _iota(out.shape, 0) < n_chunk - 1
                out_ref[g] = jnp.where(live, out, 0.0).astype(BF16)


def _compress(src, page_ids, pe_k, pe_v, w1k, w1v, w2k, w2v):
    nb, n_pages = page_ids.shape
    k_pages = CMP_PAGES_PER_STEP
    n_chunk = n_pages * CHUNKS_PER_PAGE
    const = lambda b, p, pt: (0, 0)
    page = lambda k: pl.BlockSpec((PAGE_LINES, HEAD_DIM),
                                  lambda b, p, pt: (pt[b * n_pages + p * k_pages + k], 0))
    out_spec = pl.BlockSpec((None, N_KV, n_chunk, HEAD_DIM), lambda b, p, pt: (b, 0, 0, 0))
    grid_spec = pltpu.PrefetchScalarGridSpec(
        num_scalar_prefetch=1,
        grid=(nb, n_pages // k_pages),
        in_specs=[page(k) for k in range(k_pages)]
        + [pl.BlockSpec(pe_k.shape, const), pl.BlockSpec(pe_v.shape, const),
           pl.BlockSpec(w1k.shape, const), pl.BlockSpec(w1v.shape, const),
           pl.BlockSpec(w2k.shape, const), pl.BlockSpec(w2v.shape, const)],
        out_specs=[out_spec, out_spec],
        scratch_shapes=[pltpu.VMEM((CMP_LINES, n_pages * PAGE_ROWS, HEAD_DIM), F32),
                        pltpu.VMEM((n_chunk + 8, HEAD_DIM), F32)])
    return pl.pallas_call(
        functools.partial(_compress_body, n_pages=n_pages),
        grid_spec=grid_spec,
        out_shape=[jax.ShapeDtypeStruct((nb, N_KV, n_chunk, HEAD_DIM), BF16)] * 2,
        compiler_params=_cparams("parallel", "arbitrary"),
        name="compress",
    )(page_ids.reshape(-1), *([src] * k_pages), pe_k, pe_v, w1k, w1v, w2k, w2v)


def _masked_softmax(s, mask):
    s = jnp.where(mask, s, NEG)
    m = jnp.max(s, axis=-1, keepdims=True)
    e = jnp.where(mask, jnp.exp(s - m), 0.0)
    return e / jnp.maximum(jnp.sum(e, axis=-1, keepdims=True), 1e-30)


def _overlap(n_rows, n_lanes, n_cmp, n_sel):
    ci, sj = _iota((n_rows, n_lanes), 0), _iota((n_rows, n_lanes), 1)
    hit = ((ci * CMP_STRIDE < sj * SEL_LEN + SEL_LEN) & (ci * CMP_STRIDE + CMP_LEN > sj * SEL_LEN)
           & (ci < n_cmp) & (sj < n_sel))
    return jnp.where(hit, 1.0, 0.0).astype(BF16)


def _importance(imp, lane, qpos, n_sel):
    cur = qpos // SEL_LEN
    forced = (lane == 0) | (lane == cur) | (lane == cur - 1)
    imp = jnp.where(lane * SEL_LEN <= qpos, imp + jnp.where(forced, FORCE_BONUS, 0.0), NEG)
    return jnp.where(lane < n_sel, imp, LOWEST)


def _dot_tn(a, b):
    return lax.dot_general(a, b, (((0,), (0,)), ((), ())), preferred_element_type=F32)


LOG2E = math.log2(math.e)


def _online_attend_t(q, k_ref, v_ref, lo, hi, ck, bias_fn, s_even_ref, s_odd_ref):
    cols = q.shape[0]

    def chunk_start(c):
        return pl.multiple_of(jnp.minimum(c, hi - 1) * ck, ck)

    def scores(c, s_ref):
        k0 = chunk_start(c)
        bias = jnp.concatenate([bias_fn(k0)] * (cols // Q_BLOCK), axis=1) + jnp.where(c < hi, 0.0, NEG)
        s_ref[:ck, :] = _dot_nt(k_ref[pl.ds(k0, ck), :], q) * (ATT_SCALE * LOG2E) + bias

    def update(c, s_ref, carry):
        m, l, acc = carry
        s = s_ref[:ck, :]
        m_new = jnp.maximum(m, jnp.max(s, axis=0, keepdims=True))
        alpha = jnp.exp2(m - m_new)
        e = jnp.exp2(s - m_new)
        l = alpha * l + jnp.sum(e, axis=0, keepdims=True)
        acc = alpha * acc + _dot_tn(v_ref[pl.ds(chunk_start(c), ck), :], e.astype(BF16))
        return m_new, l, acc

    def body(i, carry):
        c = lo + 2 * i
        scores(c + 1, s_odd_ref)
        carry = update(c, s_even_ref, carry)
        scores(c + 2, s_even_ref)
        return update(c + 1, s_odd_ref, carry)

    scores(lo, s_even_ref)
    init = (jnp.full((1, cols), NEG, F32), jnp.zeros((1, cols), F32), jnp.zeros((HEAD_DIM, cols), F32))
    m, l, acc = lax.fori_loop(0, (hi - lo + 1) // 2, body, init)
    return jnp.where(m > NEG, acc / jnp.maximum(l, 1e-30), 0.0)


Q_BLOCK = 128
SEL_KEY_CHUNK = 256
WIN_KEY_CHUNK = 128


def _nsa_prompt_body(qc_ref, qr_ref, gate_ref, ksel_ref, vsel_ref, kwin_ref, vwin_ref,
                     kcmp_ref, vcmp_ref, o_ref, sel_ref, s_even_ref, s_odd_ref, *, n_cmp, n_sel):
    qi = pl.program_id(2)
    q0 = qi * Q_BLOCK
    nh = HEADS_PER_KV
    stack = lambda ref: jnp.concatenate(
        [ref[:, hd * HEAD_DIM:(hd + 1) * HEAD_DIM] for hd in range(nh)], axis=0)
    per_head = lambda x: jnp.concatenate([x] * nh, axis=1)
    qc, qr = stack(qc_ref), stack(qr_ref)
    n_cmp_rows = kcmp_ref.shape[0]
    n_sel_rows = sel_ref.shape[0]

    cshape = (n_cmp_rows, Q_BLOCK)
    cblk, cq = _iota(cshape, 0), q0 + _iota(cshape, 1)
    cmask = per_head(jnp.where((cblk < n_cmp) & (cblk * CMP_STRIDE + CMP_LEN - 1 <= cq), 1.0, 0.0)) > 0.5
    s = jnp.where(cmask, _dot_nt(kcmp_ref[...], qc) * ATT_SCALE, NEG)
    e = jnp.where(cmask, jnp.exp(s - jnp.max(s, axis=0, keepdims=True)), 0.0)
    p_cmp = (e / jnp.maximum(jnp.sum(e, axis=0, keepdims=True), 1e-30)).astype(BF16)
    o_cmp = _dot_tn(vcmp_ref[...], p_cmp)

    oshape = (n_sel_rows, n_cmp_rows)
    sj, ci = _iota(oshape, 0), _iota(oshape, 1)
    hit = ((ci * CMP_STRIDE < sj * SEL_LEN + SEL_LEN) & (ci * CMP_STRIDE + CMP_LEN > sj * SEL_LEN)
           & (ci < n_cmp) & (sj < n_sel))
    imp_h = _dot(jnp.where(hit, 1.0, 0.0).astype(BF16), p_cmp)
    imp = imp_h[:, 0:Q_BLOCK]
    for hd in range(1, nh):
        imp = imp + imp_h[:, hd * Q_BLOCK:(hd + 1) * Q_BLOCK]
    ishape = (n_sel_rows, Q_BLOCK)
    blk = _iota(ishape, 0)
    imp = _importance(imp, blk, q0 + _iota(ishape, 1), n_sel)
    rank = jnp.zeros(ishape, F32)
    for k in range(n_sel):
        row = imp[k:k + 1, :]
        rank += jnp.where((row > imp) | ((row == imp) & (blk > k)), 1.0, 0.0)
    sel_ref[...] = jnp.where((rank < min(SEL_TOP, n_sel)) & (blk < n_sel), 1.0, 0.0)

    def sel_mask(k0):
        shape = (SEL_KEY_CHUNK, Q_BLOCK)
        j0 = k0 // SEL_LEN
        chosen = jnp.concatenate(
            [jnp.broadcast_to(sel_ref[pl.ds(j0 + i, 1), :], (SEL_LEN, Q_BLOCK))
             for i in range(SEL_KEY_CHUNK // SEL_LEN)], axis=0)
        live = (k0 + _iota(shape, 0) <= q0 + _iota(shape, 1)) & (chosen > 0.5)
        return jnp.where(live, 0.0, NEG)

    def win_mask(k0):
        shape = (WIN_KEY_CHUNK, Q_BLOCK)
        kpos = k0 + _iota(shape, 0)
        qp = q0 + _iota(shape, 1)
        return jnp.where((kpos <= qp) & (kpos > qp - WINDOW), 0.0, NEG)

    sel_chunks = (q0 + Q_BLOCK + SEL_KEY_CHUNK - 1) // SEL_KEY_CHUNK
    o_sel = _online_attend_t(qr, ksel_ref, vsel_ref, 0, sel_chunks, SEL_KEY_CHUNK, sel_mask,
                             s_even_ref, s_odd_ref)
    win_lo = jnp.maximum(qi - WINDOW // WIN_KEY_CHUNK, 0)
    o_win = _online_attend_t(qr, kwin_ref, vwin_ref, win_lo, qi + 1, WIN_KEY_CHUNK, win_mask,
                             s_even_ref, s_odd_ref)

    gates = jnp.transpose(gate_ref[...])
    for hd in range(nh):
        cols = slice(hd * Q_BLOCK, (hd + 1) * Q_BLOCK)
        c0 = hd * N_BRANCH
        o = (o_cmp[:, cols] * gates[c0:c0 + 1, :] + o_sel[:, cols] * gates[c0 + 1:c0 + 2, :]
             + o_win[:, cols] * gates[c0 + 2:c0 + 3, :])
        o_ref[:, hd * HEAD_DIM:(hd + 1) * HEAD_DIM] = jnp.transpose(o).astype(BF16)


def _nsa_prompt(qc, qr, gates, kv, kcmp, vcmp):
    bsz, t, dq = qc.shape
    n_cmp = (t - CMP_LEN) // CMP_STRIDE + 1
    n_sel = -(-t // SEL_LEN)
    gw = HEADS_PER_KV * HEAD_DIM
    qspec = pl.BlockSpec((None, Q_BLOCK, gw), lambda b, g, i: (b, i, g))
    kvspec = lambda slot: pl.BlockSpec((None, t, HEAD_DIM), lambda b, g, i: (b, 0, slot * N_KV + g))
    cspec = pl.BlockSpec((None, None, kcmp.shape[2], HEAD_DIM), lambda b, g, i: (b, g, 0, 0))
    return pl.pallas_call(
        functools.partial(_nsa_prompt_body, n_cmp=n_cmp, n_sel=n_sel),
        grid=(bsz, N_KV, t // Q_BLOCK),
        in_specs=[qspec, qspec, pl.BlockSpec((None, Q_BLOCK, LANES), lambda b, g, i: (b, i, g)),
                  kvspec(2), kvspec(3), kvspec(4), kvspec(5), cspec, cspec],
        out_specs=qspec,
        out_shape=jax.ShapeDtypeStruct((bsz, t, dq), BF16),
        scratch_shapes=[pltpu.VMEM((-(-n_sel // 8) * 8, Q_BLOCK), F32)]
        + [pltpu.VMEM((max(SEL_KEY_CHUNK, WIN_KEY_CHUNK), gw), F32)] * 2,
        compiler_params=_cparams("parallel", "parallel", "arbitrary"),
        name="nsa_prompt",
    )(qc, qr, gates, kv, kv, kv, kv, kcmp, vcmp)


SEL_LANES = 384


def _select_body(qc_ref, kcmp_ref, vcmp_ref, ocmp_ref, idx_ref, *, qpos, n_cmp, n_sel):
    q = qc_ref[...]
    nc = kcmp_ref.shape[0]
    ci = _iota((HEADS_PER_KV, nc), 1)
    cmask = (ci < n_cmp) & (ci * CMP_STRIDE + CMP_LEN - 1 <= qpos)
    p_cmp = _masked_softmax(_dot_nt(q, kcmp_ref[...]) * ATT_SCALE, cmask).astype(BF16)
    ocmp_ref[...] = _dot(p_cmp, vcmp_ref[...])

    imp = jnp.sum(_dot(p_cmp, _overlap(nc, SEL_LANES, n_cmp, n_sel)), axis=0, keepdims=True)
    lane = _iota((1, SEL_LANES), 1)
    imp = _importance(imp, lane, qpos, n_sel)
    sq = (SEL_LANES, SEL_LANES)
    ri, cj = _iota(sq, 0), _iota(sq, 1)
    impc = jnp.sum(jnp.where(ri == cj, imp, 0.0), axis=1, keepdims=True)
    n_top = min(SEL_TOP, n_sel)
    before_col = jnp.where((impc > imp) | ((impc == imp) & (ri < cj)), 1.0, 0.0)
    before_row = jnp.where((imp > impc) | ((imp == impc) & (cj < ri)), 1.0, 0.0)
    sel_lane = (jnp.sum(before_col, axis=0, keepdims=True) < n_top) & (lane < n_sel)
    sel_row = (jnp.sum(before_row, axis=1, keepdims=True) < n_top) & (_iota((SEL_LANES, 1), 0) < n_sel)
    slot_of = jnp.sum(jnp.where((ri < cj) & sel_row, 1.0, 0.0), axis=0, keepdims=True)
    lshape = (SEL_TOP, SEL_LANES)
    onehot = jnp.where(sel_lane & (slot_of == _iota(lshape, 0).astype(F32)), 1.0, 0.0)
    idx = jnp.sum(onehot * _iota(lshape, 1).astype(F32), axis=1, keepdims=True)
    idx_ref[...] = jnp.broadcast_to(idx, (SEL_TOP, LANES)).astype(jnp.int32)


def _nsa_select(qc, kcmp, vcmp, qpos, n_cmp, n_sel):
    nb = qc.shape[0]
    nc = kcmp.shape[2]
    hspec = pl.BlockSpec((None, None, HEADS_PER_KV, HEAD_DIM), lambda b, g: (b, g, 0, 0))
    cspec = pl.BlockSpec((None, None, nc, HEAD_DIM), lambda b, g: (b, g, 0, 0))
    return pl.pallas_call(
        functools.partial(_select_body, qpos=qpos, n_cmp=n_cmp, n_sel=n_sel),
        grid=(nb, N_KV),
        in_specs=[hspec, cspec, cspec],
        out_specs=[hspec, pl.BlockSpec((None, None, SEL_TOP, LANES), lambda b, g: (b, g, 0, 0))],
        out_shape=[jax.ShapeDtypeStruct((nb, N_KV, HEADS_PER_KV, HEAD_DIM), F32),
                   jax.ShapeDtypeStruct((nb, N_KV, SEL_TOP, LANES), jnp.int32)],
        compiler_params=_cparams("parallel", "parallel"),
        name="nsa_select",
    )(qc, kcmp, vcmp)


def _attend_plus_new(q, k, v, valid, k_new, v_new, new_ok):
    qf = q.astype(F32)
    s = _dot_nt(q, k.astype(BF16)) * ATT_SCALE
    s_new = jnp.sum(qf * k_new.astype(BF16).astype(F32), axis=1, keepdims=True) * ATT_SCALE
    s = jnp.where(valid, s, NEG)
    s_new = jnp.where(new_ok, s_new, NEG)
    m = jnp.maximum(jnp.max(s, axis=-1, keepdims=True), s_new)
    e = jnp.where(valid, jnp.exp(s - m), 0.0)
    e_new = jnp.where(new_ok, jnp.exp(s_new - m), 0.0)
    den = jnp.maximum(jnp.sum(e, axis=-1, keepdims=True) + e_new, 1e-30)
    p = (e / den).astype(BF16)
    p_new = (e_new / den).astype(BF16).astype(F32)
    return _dot(p, v.astype(BF16)) + p_new * v_new.astype(BF16).astype(F32)


SEL_BLOCK_LINES = SEL_LEN * ROW_LINES


def _attend_body(idx_ref, pt_ref, qr_ref, gate_ref, ocmp_ref, *refs, qpos, past_len, n_win):
    del pt_ref
    blocks = refs[:SEL_TOP]
    win_ref, rows_new_ref, win_new_ref, o_ref = refs[SEL_TOP:]
    b, g = pl.program_id(0), pl.program_id(1)
    q = qr_ref[...]
    n_keys = SEL_TOP * SEL_LEN
    base = (b * N_KV + g) * SEL_TOP
    lane = _iota((1, n_keys), 1)
    kpos = jnp.zeros((1, n_keys), jnp.int32)
    n_new = jnp.int32(0)
    for t in range(SEL_TOP):
        j = idx_ref[base + t]
        kpos = jnp.where(lane // SEL_LEN == t, j * SEL_LEN + lane % SEL_LEN, kpos)
        n_new = n_new + jnp.where(j == qpos // SEL_LEN, 1, 0)
    valid = (kpos <= qpos) & (kpos < past_len)
    new_ok = (jnp.zeros((HEADS_PER_KV, 1), jnp.int32) + n_new) > 0
    line = lambda ref, slot, n, per_token: ref[pl.ds(slot * N_KV + g, n, stride=per_token), :]
    k_sel = jnp.concatenate([line(blk, 2, SEL_LEN, ROW_LINES) for blk in blocks], axis=0)
    v_sel = jnp.concatenate([line(blk, 3, SEL_LEN, ROW_LINES) for blk in blocks], axis=0)
    o_sel = _attend_plus_new(q, k_sel, v_sel, valid, line(rows_new_ref, 2, 1, ROW_LINES),
                             line(rows_new_ref, 3, 1, ROW_LINES), new_ok)

    wpos = (past_len - n_win) + _iota((1, n_win), 1)
    wvalid = (wpos <= qpos) & (wpos > qpos - WINDOW) & (wpos >= 0)
    always = jnp.zeros((HEADS_PER_KV, 1), jnp.int32) == 0
    o_win = _attend_plus_new(q, line(win_ref, 0, n_win, WIN_LINES), line(win_ref, 1, n_win, WIN_LINES),
                             wvalid, line(win_new_ref, 0, 1, WIN_LINES), line(win_new_ref, 1, 1, WIN_LINES),
                             always)

    gates = gate_ref[...]
    o_ref[...] = ocmp_ref[...] * gates[:, 0:1] + o_sel * gates[:, 1:2] + o_win * gates[:, 2:3]


def _nsa_attend(idx, page_table, qr, gates, ocmp, cache, cache_win, rows_new, win_new, past_len):
    nb, n_pages = page_table.shape
    n_win = cache_win.shape[0] // (nb * WIN_LINES)
    halves = PAGE_ROWS // SEL_LEN
    n_past_blocks = n_pages * halves

    def sel_block(t):
        def index(b, g, idx_ref, pt_ref):
            j = jnp.minimum(idx_ref[(b * N_KV + g) * SEL_TOP + t], n_past_blocks - 1)
            return (pt_ref[b * n_pages + j // halves] * halves + j % halves, 0)
        return pl.BlockSpec((SEL_BLOCK_LINES, HEAD_DIM), index)

    hspec = pl.BlockSpec((None, None, HEADS_PER_KV, HEAD_DIM), lambda b, g, i, p: (b, g, 0, 0))
    grid_spec = pltpu.PrefetchScalarGridSpec(
        num_scalar_prefetch=2,
        grid=(nb, N_KV),
        in_specs=[hspec, hspec, hspec] + [sel_block(t) for t in range(SEL_TOP)]
        + [pl.BlockSpec((n_win * WIN_LINES, HEAD_DIM), lambda b, g, i, p: (b, 0)),
           pl.BlockSpec((None, ROW_LINES, HEAD_DIM), lambda b, g, i, p: (b, 0, 0)),
           pl.BlockSpec((None, WIN_LINES, HEAD_DIM), lambda b, g, i, p: (b, 0, 0))],
        out_specs=hspec)
    return pl.pallas_call(
        functools.partial(_attend_body, qpos=past_len, past_len=past_len, n_win=n_win),
        grid_spec=grid_spec,
        out_shape=jax.ShapeDtypeStruct((nb, N_KV, HEADS_PER_KV, HEAD_DIM), F32),
        compiler_params=_cparams("parallel", "arbitrary"),
        name="nsa_attend",
    )(idx, page_table.reshape(-1), qr, gates, ocmp, *([cache] * SEL_TOP), cache_win, rows_new, win_new)


PROMPT_FFN_ROWS = 512
PROMPT_ROWS = 256


def _rope_tables(pos):
    half = HEAD_DIM // 2
    inv = jnp.exp(-math.log(ROPE_THETA) * jnp.arange(half, dtype=F32) / half)
    ang = pos.astype(F32)[:, None] * inv[None, :]
    cos, sin = jnp.cos(ang), jnp.sin(ang)
    return jnp.concatenate([cos, cos], axis=1), jnp.concatenate([-sin, sin], axis=1)


def _prepare_weights(prm):
    nq = N_HEADS * HEAD_DIM
    w_qg = prm["attn_w_qg"]
    nb, d = w_qg.shape[:2]
    per_kv = HEADS_PER_KV * N_BRANCH
    w_gate = w_qg[:, :, nq:].reshape(nb, d, N_KV, per_kv)
    w_gate = jnp.pad(w_gate, ((0, 0), (0, 0), (0, 0), (0, LANES - per_kv))).reshape(nb, d, N_KV * LANES)
    return dict(
        ffn16={},
        ple_gate=prm["ple_w_gate"].astype(BF16), ple_proj=prm["ple_w_proj"].astype(BF16),
        glu=prm["glu_w"].astype(BF16), kv=prm["w_kv"].astype(BF16),
        q=w_qg[:, :, :nq].astype(BF16), gate=w_gate.astype(BF16), o=prm["attn_w_o"].astype(BF16),
        w1k=prm["cmp_w1_k"].astype(BF16), w1v=prm["cmp_w1_v"].astype(BF16),
        w2k=prm["cmp_w2_k"].astype(BF16), w2v=prm["cmp_w2_v"].astype(BF16),
        s5=[_s5_layouts(prm["ssm_lam_re"][i], prm["ssm_lam_im"][i], prm["ssm_log_dt"][i],
                        prm["ssm_b_re"][i], prm["ssm_b_im"][i], prm["ssm_c_re"][i], prm["ssm_c_im"][i])
            for i in range(N_A_LAYERS)])


def _run_group(x, p, start, state, past, prm, w):
    bsz, t, d = x.shape
    n = bsz * t
    prompt = past is None
    tm_ffn = PROMPT_FFN_ROWS if prompt else n
    tm = PROMPT_ROWS if prompt else n
    depth = prm["norm_g"].shape[0]
    cos, sin = _rope_tables(start + jnp.arange(t))
    if not prompt:
        cos, sin = jnp.broadcast_to(cos, (n, HEAD_DIM)), jnp.broadcast_to(sin, (n, HEAD_DIM))
    h = x.reshape(n, d)
    ssm_re, ssm_im = [], []
    rows = win = kv16 = kcmp = vcmp = None

    def ffn(h, g_pre, g_post, g_next, layer, which):
        if prompt:
            return _ffn(h, g_pre, g_post, g_next, *w["ffn16"][layer, which], tm_ffn)
        h_new, xn, *w16 = _ffn_cast(h, g_pre, g_post, g_next, prm["ffn_w_gate"], prm["ffn_w_up"],
                                    prm["ffn_w_down"], layer, which)
        w["ffn16"][layer, which] = w16
        return h_new, xn

    if not prompt:
        cache_kv, cache_win, page_table = past
        past_len = page_table.shape[1] * cache_kv.shape[1]
        cache_lines = cache_kv.reshape(-1, HEAD_DIM)
        win_lines = cache_win.reshape(-1, HEAD_DIM)
    for i in range(depth):
        g = prm["norm_g"][i]
        h, xn = ffn(h, g[0:1], g[1:2], g[2:3], i, 0)
        if i < N_A_LAYERS:
            if prompt:
                y, st = _s5_prompt(xn, bsz, w["s5"][i])
                st = st.reshape(d // LANES, bsz, 2, S5_GROUPS_PER_STEP, SSM_STATE).transpose(2, 1, 0, 3, 4)
                st = st.reshape(2, bsz, d // SSM_GROUP, SSM_STATE)
                ssm_re.append(st[0])
                ssm_im.append(st[1])
            else:
                y, s_re, s_im = _s5_step(xn, state[0][i].reshape(bsz, -1), state[1][i].reshape(bsz, -1),
                                         w["s5"][i])
                ssm_re.append(s_re.reshape(state[0][i].shape))
                ssm_im.append(s_im.reshape(state[1][i].shape))
            h = _s5_gate(h, xn, y, prm["ssm_d"][i][None], w["glu"], prm["glu_b"][i][None], g[3:4], i, tm)
        else:
            j = i - N_A_LAYERS
            qc, qr, gates = _qg_proj(xn, w["q"], w["gate"], cos, sin, j, tm)
            if prompt:
                o = _nsa_prompt(qc.reshape(bsz, t, -1), qr.reshape(bsz, t, -1), gates.reshape(bsz, t, -1),
                                kv16.reshape(bsz, t, -1), kcmp, vcmp).reshape(n, -1)
            else:
                heads = lambda a: a.reshape(bsz, N_KV, HEADS_PER_KV, HEAD_DIM)
                n_cmp = (past_len + t - CMP_LEN) // CMP_STRIDE + 1
                n_sel = -(-(past_len + t) // SEL_LEN)
                ocmp, idx = _nsa_select(heads(qc), kcmp, vcmp, past_len, n_cmp, n_sel)
                gsm = gates.reshape(bsz, N_KV, LANES)[:, :, :HEADS_PER_KV * N_BRANCH]
                gsm = gsm.reshape(bsz, N_KV, HEADS_PER_KV, N_BRANCH)
                gsm = jnp.pad(gsm, ((0, 0), (0, 0), (0, 0), (0, LANES - N_BRANCH)))
                o = _nsa_attend(idx[..., 0].reshape(-1), page_table, heads(qr), gsm, ocmp,
                                cache_lines, win_lines, rows_l.reshape(bsz, ROW_LINES, HEAD_DIM),
                                win_l.reshape(bsz, WIN_LINES, HEAD_DIM), past_len).reshape(n, -1)
            h = _o_proj(h, o, w["o"], g[3:4], j, tm)
        h, _ = ffn(h, g[4:5], g[5:6], g[6:7], i, 1)
        h = _ple(h, p[i].reshape(n, -1), w["ple_gate"], w["ple_proj"], g[6:7], i, tm)
        if i == N_A_LAYERS - 1:
            rows_l, win_l, kv16 = _kv_proj(h, prm["kv_norm_g"][None], w["kv"], cos, sin, tm)
            rows = rows_l.reshape(bsz, t, ROW_SLOTS, N_KV, HEAD_DIM)
            win = win_l.reshape(bsz, t, WIN_SLOTS, N_KV, HEAD_DIM)
            cmp_w = (prm["cmp_pe_k"], prm["cmp_pe_v"], w["w1k"], w["w1v"], w["w2k"], w["w2v"])
            if prompt:
                pages = jnp.arange(n // PAGE_ROWS, dtype=jnp.int32).reshape(bsz, t // PAGE_ROWS)
                kcmp, vcmp = _compress(rows_l, pages, *cmp_w)
                win = win[:, t - min(WINDOW, t):]
            else:
                kcmp, vcmp = _compress(cache_lines, page_table, *cmp_w)
                win = jnp.concatenate([cache_win, win], axis=1)[:, t:]
    return h.reshape(bsz, t, d), jnp.stack(ssm_re), jnp.stack(ssm_im), rows, win


def kernel(x_prompt, x_sample, state_ssm_re, state_ssm_im, cache_kv, cache_win, page_table,
           p_prompt, p_sample, norm_g, ffn_w_gate, ffn_w_up, ffn_w_down, ple_w_proj, ple_w_gate,
           ssm_lam_re, ssm_lam_im, ssm_log_dt, ssm_b_re, ssm_b_im, ssm_c_re, ssm_c_im, ssm_d,
           glu_w, glu_b, kv_norm_g, w_kv, cmp_pe_k, cmp_pe_v, cmp_w1_k, cmp_w2_k, cmp_w1_v,
           cmp_w2_v, attn_w_qg, attn_w_o):
    prm = dict(norm_g=norm_g, ffn_w_gate=ffn_w_gate, ffn_w_up=ffn_w_up, ffn_w_down=ffn_w_down,
               ple_w_proj=ple_w_proj, ple_w_gate=ple_w_gate, ssm_lam_re=ssm_lam_re,
               ssm_lam_im=ssm_lam_im, ssm_log_dt=ssm_log_dt, ssm_b_re=ssm_b_re, ssm_b_im=ssm_b_im,
               ssm_c_re=ssm_c_re, ssm_c_im=ssm_c_im, ssm_d=ssm_d, glu_w=glu_w, glu_b=glu_b,
               kv_norm_g=kv_norm_g, w_kv=w_kv, cmp_pe_k=cmp_pe_k, cmp_pe_v=cmp_pe_v,
               cmp_w1_k=cmp_w1_k, cmp_w2_k=cmp_w2_k, cmp_w1_v=cmp_w1_v, cmp_w2_v=cmp_w2_v,
               attn_w_qg=attn_w_qg, attn_w_o=attn_w_o)
    assert x_sample.shape[1] == 1, "the decode group handles one new token per sequence"
    assert cache_kv.shape[1] == PAGE_ROWS
    w = _prepare_weights(prm)
    past_len = page_table.shape[1] * cache_kv.shape[1]
    y_s, re_s, im_s, kv_s, win_s = _run_group(x_sample, p_sample, past_len, (state_ssm_re, state_ssm_im),
                                              (cache_kv, cache_win, page_table), prm, w)
    y_p, re_p, im_p, kv_p, win_p = _run_group(x_prompt, p_prompt, 0, None, None, prm, w)
    return (y_p, y_s, re_p, im_p, re_s, im_s, kv_p, kv_s, win_p, win_s)
```

```python
import functools
import math

import jax
import jax.numpy as jnp
from jax import lax
from jax.experimental import pallas as pl
from jax.experimental.pallas import tpu as pltpu

F32 = jnp.float32
BF16 = jnp.bfloat16
HIGHEST = lax.Precision.HIGHEST

N_A_LAYERS = 2
NORM_EPS = 1e-6
SSM_GROUP = 16
SSM_STATE = 64
HEAD_DIM = 128
N_KV = 2
HEADS_PER_KV = 8
N_HEADS = N_KV * HEADS_PER_KV
N_BRANCH = 3
N_KV_SLOTS = 6
CMP_LEN = 32
CMP_STRIDE = 16
SEL_LEN = 64
SEL_TOP = 16
WINDOW = 512
ROPE_THETA = 10000.0
FORCE_BONUS = 1e3
NEG = -1e30
LOWEST = -3.0e38
ATT_SCALE = HEAD_DIM ** -0.5

LANES = 128
VMEM_LIMIT_BYTES = 56 * 2 ** 20
FFN_TILE_F = 512
S5_CHUNK = 16
S5_GROUPS_PER_STEP = LANES // SSM_GROUP
S5_STATE_LANES = S5_GROUPS_PER_STEP * SSM_STATE


def _cparams(*sem):
    return pltpu.CompilerParams(dimension_semantics=sem, vmem_limit_bytes=VMEM_LIMIT_BYTES)


def _rms(x, g):
    y = x * lax.rsqrt(jnp.mean(x * x, axis=-1, keepdims=True) + NORM_EPS)
    return y * g


def _dot(a, b, **kw):
    return jnp.dot(a, b, preferred_element_type=F32, **kw)


def _dot_nt(a, b):
    return lax.dot_general(a, b, (((1,), (1,)), ((), ())), preferred_element_type=F32)


def _rope(x, cos, sin):
    return x * cos + pltpu.roll(x, HEAD_DIM // 2, 1) * sin


def _iota(shape, axis):
    return lax.broadcasted_iota(jnp.int32, shape, axis)


def _ffn_step(k, nk, h_ref, gpre_ref, gpost_ref, gnext_ref, wg_ref, wu_ref, wd_ref, o_ref, xn_out_ref,
              xn_ref, acc_ref, side_work=None):
    @pl.when(k == 0)
    def _():
        xn_ref[...] = _rms(h_ref[...], gpre_ref[...]).astype(BF16)
        acc_ref[...] = jnp.zeros_like(acc_ref)

    xn = xn_ref[...]
    gate = _dot(xn, wg_ref[...])
    up = _dot(xn, wu_ref[...])
    mid = (jax.nn.silu(gate) * up).astype(BF16)
    acc_ref[...] += _dot(mid, wd_ref[...])
    if side_work is not None:
        side_work()

    @pl.when(k == nk - 1)
    def _():
        h_new = h_ref[...] + 0.5 * _rms(acc_ref[...], gpost_ref[...])
        o_ref[...] = h_new
        xn_out_ref[...] = _rms(h_new, gnext_ref[...])


def _ffn_body(h_ref, gpre_ref, gpost_ref, gnext_ref, wg_ref, wu_ref, wd_ref, o_ref, xn_out_ref,
              xn_ref, acc_ref, *, nk):
    _ffn_step(pl.program_id(1), nk, h_ref, gpre_ref, gpost_ref, gnext_ref,
              wg_ref, wu_ref, wd_ref, o_ref, xn_out_ref, xn_ref, acc_ref)


def _ffn(h, g_pre, g_post, g_next, wg, wu, wd, tm):
    n, d = h.shape
    nk = wg.shape[-1] // FFN_TILE_F
    row = lambda i, k: (i, 0)
    vec = lambda i, k: (0, 0)
    return pl.pallas_call(
        functools.partial(_ffn_body, nk=nk),
        grid=(n // tm, nk),
        in_specs=[pl.BlockSpec((tm, d), row), pl.BlockSpec((1, d), vec), pl.BlockSpec((1, d), vec),
                  pl.BlockSpec((1, d), vec),
                  pl.BlockSpec((d, FFN_TILE_F), lambda i, k: (0, k)),
                  pl.BlockSpec((d, FFN_TILE_F), lambda i, k: (0, k)),
                  pl.BlockSpec((FFN_TILE_F, d), lambda i, k: (k, 0))],
        out_specs=[pl.BlockSpec((tm, d), row), pl.BlockSpec((tm, d), row)],
        out_shape=[jax.ShapeDtypeStruct((n, d), F32)] * 2,
        scratch_shapes=[pltpu.VMEM((tm, d), BF16), pltpu.VMEM((tm, d), F32)],
        compiler_params=_cparams("parallel", "arbitrary"),
        name="ffn",
    )(h, g_pre, g_post, g_next, wg, wu, wd)


FFN_CONVERT_ROWS = 16


def _ffn_convert_body(h_ref, gpre_ref, gpost_ref, gnext_ref, wg_ref, wu_ref, wd_ref,
                      ng_ref, nu_ref, nd_ref, o_ref, xn_out_ref, ng16_ref, nu16_ref, nd16_ref,
                      xn_ref, acc_ref, *, nk, d_ff):
    i, k = pl.program_id(0), pl.program_id(1)

    def convert_next():
        pad = ng16_ref.shape[1] - d_ff
        for src, dst in ((ng_ref, ng16_ref), (nu_ref, nu16_ref)):
            dst[:, :d_ff] = src[...].astype(BF16)
            dst[:, d_ff:] = jnp.zeros((dst.shape[0], pad), BF16)
        rows = nd_ref.shape[0]
        live = d_ff - (i * nk + k) * rows
        nd16_ref[...] = jnp.where(_iota(nd_ref.shape, 0) < live, nd_ref[...], 0.0).astype(BF16)

    _ffn_step(k, nk, h_ref, gpre_ref, gpost_ref, gnext_ref,
              wg_ref, wu_ref, wd_ref, o_ref, xn_out_ref, xn_ref, acc_ref, convert_next)


def _ffn_convert(h, g_pre, g_post, g_next, wg, wu, wd, tm, wg32, wu32, wd32, layer, which):
    n, d = h.shape
    fp = wg.shape[-1]
    d_ff = wg32.shape[-1]
    nk = fp // FFN_TILE_F
    steps = (n // tm) * nk
    rg = FFN_CONVERT_ROWS
    rd = fp // steps
    assert fp % steps == 0 and d_ff % rd == 0 and d % rg == 0 and d // rg <= steps
    row = lambda i, k: (i, 0)
    vec = lambda i, k: (0, 0)
    up_slab = lambda i, k: jnp.minimum(i * nk + k, d // rg - 1)
    down_slab = lambda i, k: jnp.minimum(i * nk + k, d_ff // rd - 1)
    return pl.pallas_call(
        functools.partial(_ffn_convert_body, nk=nk, d_ff=d_ff),
        grid=(n // tm, nk),
        in_specs=[pl.BlockSpec((tm, d), row), pl.BlockSpec((1, d), vec), pl.BlockSpec((1, d), vec),
                  pl.BlockSpec((1, d), vec),
                  pl.BlockSpec((d, FFN_TILE_F), lambda i, k: (0, k)),
                  pl.BlockSpec((d, FFN_TILE_F), lambda i, k: (0, k)),
                  pl.BlockSpec((FFN_TILE_F, d), lambda i, k: (k, 0)),
                  pl.BlockSpec((None, None, rg, d_ff), lambda i, k: (layer, which, up_slab(i, k), 0)),
                  pl.BlockSpec((None, None, rg, d_ff), lambda i, k: (layer, which, up_slab(i, k), 0)),
                  pl.BlockSpec((None, None, rd, d), lambda i, k: (layer, which, down_slab(i, k), 0))],
        out_specs=[pl.BlockSpec((tm, d), row), pl.BlockSpec((tm, d), row),
                   pl.BlockSpec((rg, fp), lambda i, k: (up_slab(i, k), 0)),
                   pl.BlockSpec((rg, fp), lambda i, k: (up_slab(i, k), 0)),
                   pl.BlockSpec((rd, d), lambda i, k: (i * nk + k, 0))],
        out_shape=[jax.ShapeDtypeStruct((n, d), F32)] * 2
        + [jax.ShapeDtypeStruct((d, fp), BF16)] * 2 + [jax.ShapeDtypeStruct((fp, d), BF16)],
        scratch_shapes=[pltpu.VMEM((tm, d), BF16), pltpu.VMEM((tm, d), F32)],
        compiler_params=_cparams("arbitrary", "arbitrary"),
        name="ffn_convert",
    )(h, g_pre, g_post, g_next, wg, wu, wd, wg32, wu32, wd32)


def _ffn_cast_body(h_ref, gpre_ref, gpost_ref, gnext_ref, wg_ref, wu_ref, wd_ref,
                   o_ref, xn_out_ref, wg16_ref, wu16_ref, wd16_ref, xn_ref, acc_ref, *, nk, d_ff):
    k = pl.program_id(1)
    live = d_ff - k * FFN_TILE_F
    wshape, dshape = wg_ref.shape, wd_ref.shape
    wg16_ref[...] = jnp.where(_iota(wshape, 1) < live, wg_ref[...], 0.0).astype(BF16)
    wu16_ref[...] = jnp.where(_iota(wshape, 1) < live, wu_ref[...], 0.0).astype(BF16)
    wd16_ref[...] = jnp.where(_iota(dshape, 0) < live, wd_ref[...], 0.0).astype(BF16)
    _ffn_step(k, nk, h_ref, gpre_ref, gpost_ref, gnext_ref, wg16_ref, wu16_ref, wd16_ref,
              o_ref, xn_out_ref, xn_ref, acc_ref)


def _ffn_cast(h, g_pre, g_post, g_next, wg, wu, wd, layer, which):
    n, d = h.shape
    d_ff = wg.shape[-1]
    nk = -(-d_ff // FFN_TILE_F)
    fp = nk * FFN_TILE_F
    row = lambda i, k: (0, 0)
    return pl.pallas_call(
        functools.partial(_ffn_cast_body, nk=nk, d_ff=d_ff),
        grid=(1, nk),
        in_specs=[pl.BlockSpec((n, d), row), pl.BlockSpec((1, d), row), pl.BlockSpec((1, d), row),
                  pl.BlockSpec((1, d), row),
                  pl.BlockSpec((None, None, d, FFN_TILE_F), lambda i, k: (layer, which, 0, k)),
                  pl.BlockSpec((None, None, d, FFN_TILE_F), lambda i, k: (layer, which, 0, k)),
                  pl.BlockSpec((None, None, FFN_TILE_F, d), lambda i, k: (layer, which, k, 0))],
        out_specs=[pl.BlockSpec((n, d), row), pl.BlockSpec((n, d), row),
                   pl.BlockSpec((d, FFN_TILE_F), lambda i, k: (0, k)),
                   pl.BlockSpec((d, FFN_TILE_F), lambda i, k: (0, k)),
                   pl.BlockSpec((FFN_TILE_F, d), lambda i, k: (k, 0))],
        out_shape=[jax.ShapeDtypeStruct((n, d), F32)] * 2
        + [jax.ShapeDtypeStruct((d, fp), BF16)] * 2 + [jax.ShapeDtypeStruct((fp, d), BF16)],
        scratch_shapes=[pltpu.VMEM((n, d), BF16), pltpu.VMEM((n, d), F32)],
        compiler_params=_cparams("arbitrary", "arbitrary"),
        name="ffn_cast",
    )(h, g_pre, g_post, g_next, wg, wu, wd)


def _ple_body(h_ref, p_ref, wgate_ref, wproj_ref, g_ref, o_ref):
    h = h_ref[...]
    gate = jax.nn.sigmoid(_dot(h.astype(BF16), wgate_ref[...]))
    proj = _dot(p_ref[...].astype(BF16), wproj_ref[...])
    o_ref[...] = h + _rms(gate * proj, g_ref[...])


def _ple(h, p, wgate, wproj, g, layer, tm):
    n, d = h.shape
    pd = p.shape[-1]
    row = lambda i: (i, 0)
    return pl.pallas_call(
        _ple_body,
        grid=(n // tm,),
        in_specs=[pl.BlockSpec((tm, d), row), pl.BlockSpec((tm, pd), row),
                  pl.BlockSpec((None, d, d), lambda i: (layer, 0, 0)),
                  pl.BlockSpec((None, pd, d), lambda i: (layer, 0, 0)),
                  pl.BlockSpec((1, d), lambda i: (0, 0))],
        out_specs=pl.BlockSpec((tm, d), row),
        out_shape=jax.ShapeDtypeStruct((n, d), F32),
        compiler_params=_cparams("parallel"),
        name="ple",
    )(h, p, wgate, wproj, g)


ROW_SLOTS = 4
WIN_SLOTS = N_KV_SLOTS - ROW_SLOTS
ROW_LINES = ROW_SLOTS * N_KV
WIN_LINES = WIN_SLOTS * N_KV


def _kv_body(h_ref, g_ref, w_ref, cos_ref, sin_ref, rows_ref, win_ref, kv16_ref, *, tm):
    xn = _rms(h_ref[...], g_ref[...]).astype(BF16)
    kv = _dot(xn, w_ref[...])
    cos, sin = cos_ref[...], sin_ref[...]
    for slot in range(N_KV_SLOTS):
        for g in range(N_KV):
            line = slot * N_KV + g
            lo = line * HEAD_DIM
            x = kv[:, lo:lo + HEAD_DIM]
            if slot in (2, 4):
                x = _rope(x, cos, sin)
            kv16_ref[:, lo:lo + HEAD_DIM] = x.astype(BF16)
            if slot < ROW_SLOTS:
                rows_ref[pl.ds(line, tm, stride=ROW_LINES), :] = x
            else:
                win_ref[pl.ds(line - ROW_LINES, tm, stride=WIN_LINES), :] = x


def _kv_proj(h, g, w, cos, sin, tm):
    n, d = h.shape
    nw = w.shape[-1]
    nt = cos.shape[0] // tm
    row = lambda i: (i, 0)
    tab = lambda i: (i % nt, 0)
    return pl.pallas_call(
        functools.partial(_kv_body, tm=tm),
        grid=(n // tm,),
        in_specs=[pl.BlockSpec((tm, d), row), pl.BlockSpec((1, d), lambda i: (0, 0)),
                  pl.BlockSpec((d, nw), lambda i: (0, 0)),
                  pl.BlockSpec((tm, HEAD_DIM), tab), pl.BlockSpec((tm, HEAD_DIM), tab)],
        out_specs=[pl.BlockSpec((tm * ROW_LINES, HEAD_DIM), row),
                   pl.BlockSpec((tm * WIN_LINES, HEAD_DIM), row),
                   pl.BlockSpec((tm, nw), row)],
        out_shape=[jax.ShapeDtypeStruct((n * ROW_LINES, HEAD_DIM), F32),
                   jax.ShapeDtypeStruct((n * WIN_LINES, HEAD_DIM), F32),
                   jax.ShapeDtypeStruct((n, nw), BF16)],
        compiler_params=_cparams("parallel"),
        name="kv_proj",
    )(h, g, w, cos, sin)


def _qg_body(xn_ref, wq_ref, wgt_ref, cos_ref, sin_ref, qc_ref, qr_ref, gate_ref):
    xn = xn_ref[...].astype(BF16)
    q = _dot(xn, wq_ref[...])
    qc_ref[...] = q.astype(BF16)
    cos, sin = cos_ref[...], sin_ref[...]
    for hd in range(N_HEADS):
        sl = slice(hd * HEAD_DIM, (hd + 1) * HEAD_DIM)
        qr_ref[:, sl] = _rope(q[:, sl], cos, sin).astype(BF16)
    gate_ref[...] = jax.nn.sigmoid(_dot(xn, wgt_ref[...]))


def _qg_proj(xn, wq, wgt, cos, sin, layer, tm):
    n, d = xn.shape
    nq, ng = wq.shape[-1], wgt.shape[-1]
    nt = cos.shape[0] // tm
    row = lambda i: (i, 0)
    tab = lambda i: (i % nt, 0)
    return pl.pallas_call(
        _qg_body,
        grid=(n // tm,),
        in_specs=[pl.BlockSpec((tm, d), row),
                  pl.BlockSpec((None, d, nq), lambda i: (layer, 0, 0)),
                  pl.BlockSpec((None, d, ng), lambda i: (layer, 0, 0)),
                  pl.BlockSpec((tm, HEAD_DIM), tab), pl.BlockSpec((tm, HEAD_DIM), tab)],
        out_specs=[pl.BlockSpec((tm, nq), row), pl.BlockSpec((tm, nq), row), pl.BlockSpec((tm, ng), row)],
        out_shape=[jax.ShapeDtypeStruct((n, nq), BF16), jax.ShapeDtypeStruct((n, nq), BF16),
                   jax.ShapeDtypeStruct((n, ng), F32)],
        compiler_params=_cparams("parallel"),
        name="qg_proj",
    )(xn, wq, wgt, cos, sin)


def _oproj_body(h_ref, o_ref, w_ref, g_ref, out_ref):
    m = _dot(o_ref[...].astype(BF16), w_ref[...])
    out_ref[...] = h_ref[...] + _rms(m, g_ref[...])


def _o_proj(h, o, w, g, layer, tm):
    n, d = h.shape
    row = lambda i: (i, 0)
    return pl.pallas_call(
        _oproj_body,
        grid=(n // tm,),
        in_specs=[pl.BlockSpec((tm, d), row), pl.BlockSpec((tm, d), row),
                  pl.BlockSpec((None, d, d), lambda i: (layer, 0, 0)),
                  pl.BlockSpec((1, d), lambda i: (0, 0))],
        out_specs=pl.BlockSpec((tm, d), row),
        out_shape=jax.ShapeDtypeStruct((n, d), F32),
        compiler_params=_cparams("parallel"),
        name="o_proj",
    )(h, o, w, g)


def _s5gate_body(h_ref, xn_ref, y_ref, d_ref, w_ref, b_ref, g_ref, out_ref):
    y = jax.nn.gelu(y_ref[...] + d_ref[...] * xn_ref[...])
    z = jax.nn.sigmoid(_dot(y.astype(BF16), w_ref[...]) + b_ref[...])
    out_ref[...] = h_ref[...] + _rms(y * z, g_ref[...])


def _s5_gate(h, xn, y, dskip, w, b, g, layer, tm):
    n, d = h.shape
    row = lambda i: (i, 0)
    vec = lambda i: (0, 0)
    return pl.pallas_call(
        _s5gate_body,
        grid=(n // tm,),
        in_specs=[pl.BlockSpec((tm, d), row), pl.BlockSpec((tm, d), row), pl.BlockSpec((tm, d), row),
                  pl.BlockSpec((1, d), vec),
                  pl.BlockSpec((None, d, d), lambda i: (layer, 0, 0)),
                  pl.BlockSpec((1, d), vec), pl.BlockSpec((1, d), vec)],
        out_specs=pl.BlockSpec((tm, d), row),
        out_shape=jax.ShapeDtypeStruct((n, d), F32),
        compiler_params=_cparams("parallel"),
        name="s5_gate",
    )(h, xn, y, dskip, w, b, g)


def _s5_disc(lam_re, lam_im, log_dt):
    dt = jnp.exp(log_dt)
    mag = jnp.exp(lam_re * dt)
    a_re, a_im = mag * jnp.cos(lam_im * dt), mag * jnp.sin(lam_im * dt)
    den = lam_re * lam_re + lam_im * lam_im
    nr, ni = a_re - 1.0, a_im
    r_re = (nr * lam_re + ni * lam_im) / den
    r_im = (ni * lam_re - nr * lam_im) / den
    return a_re, a_im, r_re, r_im


def _cmul(ar, ai, br, bi):
    return ar * br - ai * bi, ar * bi + ai * br


def _bd_rows(m):
    shape = (LANES, S5_STATE_LANES)
    same = _iota(shape, 0) // SSM_GROUP == _iota(shape, 1) // SSM_STATE
    return jnp.where(same, jnp.concatenate([m] * S5_GROUPS_PER_STEP, axis=1), 0.0)


def _bd_cols(m):
    shape = (S5_STATE_LANES, LANES)
    same = _iota(shape, 0) // SSM_STATE == _iota(shape, 1) // SSM_GROUP
    return jnp.where(same, jnp.concatenate([m] * S5_GROUPS_PER_STEP, axis=0), 0.0)


def _s5_input_weights(lam_re, lam_im, log_dt, b_re, b_im):
    a_re, a_im, r_re, r_im = _s5_disc(lam_re, lam_im, log_dt)
    bb_re, bb_im = _cmul(r_re, r_im, b_re, b_im)
    return a_re, a_im, bb_re, bb_im


def _s5_prompt_body(u_ref, lr_re_ref, lr_im_ref, lr_dt_ref, b_re_ref, b_im_ref,
                    lc_re_ref, lc_im_ref, lc_dt_ref, c_re_ref, c_im_ref,
                    lv_re_ref, lv_im_ref, lv_dt_ref,
                    y_ref, st_ref,
                    sin_ref, toep_ref, fout_ref, v_ref, xp_ref, *, n_batch, n_chunk):
    L = S5_CHUNK
    rows = n_batch * n_chunk
    half = S5_STATE_LANES

    a_re, a_im, w_re, w_im = _s5_input_weights(lr_re_ref[...], lr_im_ref[...], lr_dt_ref[...],
                                               b_re_ref[...], b_im_ref[...])
    ws_re, ws_im = [w_re], [w_im]
    for _ in range(L - 1):
        w_re, w_im = _cmul(a_re, a_im, w_re, w_im)
        ws_re.append(w_re)
        ws_im.append(w_im)
    for s in range(L):
        k = L - 1 - s
        sin_ref[s * LANES:(s + 1) * LANES, :] = jnp.concatenate(
            [_bd_rows(ws_re[k]), _bd_rows(ws_im[k])], axis=1).astype(BF16)
    stack_re = jnp.concatenate([ws_re[L - 1 - s] for s in range(L)], axis=0)
    stack_im = jnp.concatenate([ws_im[L - 1 - s] for s in range(L)], axis=0)
    taps = (_dot(stack_re, c_re_ref[...], precision=HIGHEST)
            - _dot(stack_im, c_im_ref[...], precision=HIGHEST))
    tshape = (L * LANES, LANES)
    same = (_iota(tshape, 0) % LANES) // SSM_GROUP == _iota(tshape, 1) // SSM_GROUP
    taps = jnp.where(same, taps, 0.0).astype(BF16)
    toep_ref[:, LANES:] = taps
    toep_ref[:(L - 1) * LANES, :LANES] = taps[LANES:]
    toep_ref[(L - 1) * LANES:, :LANES] = jnp.zeros((LANES, LANES), BF16)
    ac_re, ac_im, _, _ = _s5_disc(lc_re_ref[...], lc_im_ref[...], lc_dt_ref[...])
    ca_re, ca_im = c_re_ref[...], c_im_ref[...]
    for t in range(L):
        ca_re, ca_im = _cmul(ac_re, ac_im, ca_re, ca_im)
        fout_ref[:, t * LANES:(t + 1) * LANES] = jnp.concatenate(
            [_bd_cols(ca_re), -_bd_cols(ca_im)], axis=0).astype(BF16)
    al_re, al_im, _, _ = _s5_disc(lv_re_ref[...], lv_im_ref[...], lv_dt_ref[...])
    for _ in range(int(math.log2(L))):
        al_re, al_im = _cmul(al_re, al_im, al_re, al_im)

    xb = jnp.concatenate([u_ref[pl.ds(s, rows, stride=L), :] for s in range(L)], axis=1).astype(BF16)
    v_ref[...] = _dot(xb, sin_ref[...])

    def step(c, carry):
        new = []
        for b in range(n_batch):
            r = b * n_chunk + c
            x_re, x_im = carry[b]
            xp_ref[pl.ds(r, 1), :] = jnp.concatenate([x_re, x_im], axis=1)
            inc = v_ref[pl.ds(r, 1), :]
            new.append((al_re * x_re - al_im * x_im + inc[:, :half],
                        al_re * x_im + al_im * x_re + inc[:, half:]))
        return tuple(new)

    zero = jnp.zeros((1, half), F32)
    final = lax.fori_loop(0, n_chunk, step, tuple((zero, zero) for _ in range(n_batch)))
    for b in range(n_batch):
        st_ref[b:b + 1, :] = jnp.concatenate(final[b], axis=1)

    xpb = xp_ref[...].astype(BF16)
    for t in range(0, L, 2):
        pair = (_dot(xb[:, :(t + 2) * LANES], toep_ref[(L - 2 - t) * LANES:, :])
                + _dot(xpb, fout_ref[:, t * LANES:(t + 2) * LANES]))
        y_ref[pl.ds(t, rows, stride=L), :] = pair[:, :LANES]
        y_ref[pl.ds(t + 1, rows, stride=L), :] = pair[:, LANES:]


def _s5_layouts(lam_re, lam_im, log_dt, b_re, b_im, c_re, c_im):
    n_g = lam_re.shape[0]
    ldt = jnp.broadcast_to(log_dt[:, None], lam_re.shape)
    rep = lambda a: jnp.repeat(a, SSM_GROUP, axis=0)
    rows = (rep(lam_re), rep(lam_im), rep(ldt),
            b_re.transpose(0, 2, 1).reshape(n_g * SSM_GROUP, SSM_STATE),
            b_im.transpose(0, 2, 1).reshape(n_g * SSM_GROUP, SSM_STATE))
    cols = (rep(lam_re).T, rep(lam_im).T, rep(ldt).T,
            c_re.reshape(n_g * SSM_GROUP, SSM_STATE).T, c_im.reshape(n_g * SSM_GROUP, SSM_STATE).T)
    n_j = n_g // S5_GROUPS_PER_STEP
    lanes = tuple(a.reshape(n_j, 1, S5_STATE_LANES) for a in (lam_re, lam_im, ldt))
    return rows, cols, lanes


def _s5_specs():
    rspec = pl.BlockSpec((LANES, SSM_STATE), lambda j: (j, 0))
    cspec = pl.BlockSpec((SSM_STATE, LANES), lambda j: (0, j))
    vspec = pl.BlockSpec((None, 1, S5_STATE_LANES), lambda j: (j, 0, 0))
    return [rspec] * 5 + [cspec] * 5 + [vspec] * 3


def _s5_prompt(u, n_batch, layouts):
    n, d = u.shape
    n_j = d // LANES
    n_chunk = n // n_batch // S5_CHUNK
    rows = n // S5_CHUNK
    rows_p, cols_p, lanes_p = layouts
    col = lambda j: (0, j)
    return pl.pallas_call(
        functools.partial(_s5_prompt_body, n_batch=n_batch, n_chunk=n_chunk),
        grid=(n_j,),
        in_specs=[pl.BlockSpec((n, LANES), col)] + _s5_specs(),
        out_specs=[pl.BlockSpec((n, LANES), col),
                   pl.BlockSpec((None, n_batch, 2 * S5_STATE_LANES), lambda j: (j, 0, 0))],
        out_shape=[jax.ShapeDtypeStruct((n, d), F32),
                   jax.ShapeDtypeStruct((n_j, n_batch, 2 * S5_STATE_LANES), F32)],
        scratch_shapes=[pltpu.VMEM((S5_CHUNK * LANES, 2 * S5_STATE_LANES), BF16),
                        pltpu.VMEM((S5_CHUNK * LANES, 2 * LANES), BF16),
                        pltpu.VMEM((2 * S5_STATE_LANES, S5_CHUNK * LANES), BF16),
                        pltpu.VMEM((rows, 2 * S5_STATE_LANES), F32),
                        pltpu.VMEM((rows, 2 * S5_STATE_LANES), F32)],
        compiler_params=_cparams("parallel"),
        name="s5_prompt",
    )(u, *rows_p, *cols_p, *lanes_p)


def _s5_step_body(u_ref, hre_ref, him_ref, lr_re_ref, lr_im_ref, lr_dt_ref, b_re_ref, b_im_ref,
                  lc_re_ref, lc_im_ref, lc_dt_ref, c_re_ref, c_im_ref,
                  lv_re_ref, lv_im_ref, lv_dt_ref, y_ref, ore_ref, oim_ref):
    _, _, bb_re, bb_im = _s5_input_weights(lr_re_ref[...], lr_im_ref[...], lr_dt_ref[...],
                                           b_re_ref[...], b_im_ref[...])
    a_re, a_im, _, _ = _s5_disc(lv_re_ref[...], lv_im_ref[...], lv_dt_ref[...])
    u = u_ref[...]
    h_re, h_im = hre_ref[...], him_ref[...]
    x_re = _dot(u, _bd_rows(bb_re), precision=HIGHEST) + (a_re * h_re - a_im * h_im)
    x_im = _dot(u, _bd_rows(bb_im), precision=HIGHEST) + (a_re * h_im + a_im * h_re)
    ore_ref[...] = x_re
    oim_ref[...] = x_im
    y_ref[...] = (_dot(x_re, _bd_cols(c_re_ref[...]), precision=HIGHEST)
                  - _dot(x_im, _bd_cols(c_im_ref[...]), precision=HIGHEST))


def _s5_step(u, h_re, h_im, layouts):
    nb, d = u.shape
    rows_p, cols_p, lanes_p = layouts
    col = lambda j: (0, j)
    sspec = pl.BlockSpec((nb, S5_STATE_LANES), col)
    return pl.pallas_call(
        _s5_step_body,
        grid=(d // LANES,),
        in_specs=[pl.BlockSpec((nb, LANES), col), sspec, sspec] + _s5_specs(),
        out_specs=[pl.BlockSpec((nb, LANES), col), sspec, sspec],
        out_shape=[jax.ShapeDtypeStruct((nb, d), F32),
                   jax.ShapeDtypeStruct(h_re.shape, F32), jax.ShapeDtypeStruct(h_im.shape, F32)],
        compiler_params=_cparams("parallel"),
        name="s5_step",
    )(u, h_re, h_im, *rows_p, *cols_p, *lanes_p)


PAGE_ROWS = 128
PAGE_LINES = PAGE_ROWS * ROW_LINES
CHUNKS_PER_PAGE = PAGE_ROWS // CMP_STRIDE
CMP_PAGES_PER_STEP = 8
CMP_LINES = 2 * N_KV


def _compress_body(pt_ref, *refs, n_pages):
    del pt_ref
    k_pages = CMP_PAGES_PER_STEP
    srcs = refs[:k_pages]
    (pek_ref, pev_ref, w1k_ref, w1v_ref, w2k_ref, w2v_ref, kc_ref, vc_ref, buf_ref, shift_ref) = refs[k_pages:]
    p = pl.program_id(1)
    for k in range(k_pages):
        r0 = pl.multiple_of((p * k_pages + k) * PAGE_ROWS, PAGE_ROWS)
        for sg in range(CMP_LINES):
            buf_ref[sg, pl.ds(r0, PAGE_ROWS), :] = srcs[k][pl.ds(sg, PAGE_ROWS, stride=ROW_LINES), :]

    @pl.when(p == n_pages // k_pages - 1)
    def _():
        n_chunk = n_pages * CHUNKS_PER_PAGE
        half = CMP_STRIDE
        for slot, (pe_ref, w1_ref, w2_ref, out_ref) in enumerate(
                ((pek_ref, w1k_ref, w2k_ref, kc_ref), (pev_ref, w1v_ref, w2v_ref, vc_ref))):
            for g in range(N_KV):
                sg = slot * N_KV + g
                xs = [buf_ref[sg, pl.ds(l, n_chunk, stride=CMP_STRIDE), :] for l in range(half)]
                as_first = jnp.concatenate(
                    [(x + pe_ref[l:l + 1, :]).astype(BF16) for l, x in enumerate(xs)], axis=1)
                as_second = jnp.concatenate(
                    [(x + pe_ref[half + l:half + l + 1, :]).astype(BF16) for l, x in enumerate(xs)], axis=1)
                first = _dot(as_first, w1_ref[:half * HEAD_DIM, :])
                shift_ref[0:n_chunk, :] = _dot(as_second, w1_ref[half * HEAD_DIM:, :])
                shift_ref[n_chunk:n_chunk + 8, :] = jnp.zeros((8, HEAD_DIM), F32)
                hid = first + shift_ref[1:n_chunk + 1, :]
                out = _dot(jax.nn.silu(hid).astype(BF16), w2_ref[...])
                live = _iota(out.shape, 0) < n_chunk - 1
                out_ref[g] = jnp.where(live, out, 0.0).astype(BF16)


def _compress(src, page_ids, pe_k, pe_v, w1k, w1v, w2k, w2v):
    nb, n_pages = page_ids.shape
    k_pages = CMP_PAGES_PER_STEP
    n_chunk = n_pages * CHUNKS_PER_PAGE
    const = lambda b, p, pt: (0, 0)
    page = lambda k: pl.BlockSpec((PAGE_LINES, HEAD_DIM),
                                  lambda b, p, pt: (pt[b * n_pages + p * k_pages + k], 0))
    out_spec = pl.BlockSpec((None, N_KV, n_chunk, HEAD_DIM), lambda b, p, pt: (b, 0, 0, 0))
    grid_spec = pltpu.PrefetchScalarGridSpec(
        num_scalar_prefetch=1,
        grid=(nb, n_pages // k_pages),
        in_specs=[page(k) for k in range(k_pages)]
        + [pl.BlockSpec(pe_k.shape, const), pl.BlockSpec(pe_v.shape, const),
           pl.BlockSpec(w1k.shape, const), pl.BlockSpec(w1v.shape, const),
           pl.BlockSpec(w2k.shape, const), pl.BlockSpec(w2v.shape, const)],
        out_specs=[out_spec, out_spec],
        scratch_shapes=[pltpu.VMEM((CMP_LINES, n_pages * PAGE_ROWS, HEAD_DIM), F32),
                        pltpu.VMEM((n_chunk + 8, HEAD_DIM), F32)])
    return pl.pallas_call(
        functools.partial(_compress_body, n_pages=n_pages),
        grid_spec=grid_spec,
        out_shape=[jax.ShapeDtypeStruct((nb, N_KV, n_chunk, HEAD_DIM), BF16)] * 2,
        compiler_params=_cparams("parallel", "arbitrary"),
        name="compress",
    )(page_ids.reshape(-1), *([src] * k_pages), pe_k, pe_v, w1k, w1v, w2k, w2v)


def _masked_softmax(s, mask):
    s = jnp.where(mask, s, NEG)
    m = jnp.max(s, axis=-1, keepdims=True)
    e = jnp.where(mask, jnp.exp(s - m), 0.0)
    return e / jnp.maximum(jnp.sum(e, axis=-1, keepdims=True), 1e-30)


def _overlap(n_rows, n_lanes, n_cmp, n_sel):
    ci, sj = _iota((n_rows, n_lanes), 0), _iota((n_rows, n_lanes), 1)
    hit = ((ci * CMP_STRIDE < sj * SEL_LEN + SEL_LEN) & (ci * CMP_STRIDE + CMP_LEN > sj * SEL_LEN)
           & (ci < n_cmp) & (sj < n_sel))
    return jnp.where(hit, 1.0, 0.0).astype(BF16)


def _importance(imp, lane, qpos, n_sel):
    cur = qpos // SEL_LEN
    forced = (lane == 0) | (lane == cur) | (lane == cur - 1)
    imp = jnp.where(lane * SEL_LEN <= qpos, imp + jnp.where(forced, FORCE_BONUS, 0.0), NEG)
    return jnp.where(lane < n_sel, imp, LOWEST)


def _dot_tn(a, b):
    return lax.dot_general(a, b, (((0,), (0,)), ((), ())), preferred_element_type=F32)


LOG2E = math.log2(math.e)


def _online_attend_t(q, k_ref, v_ref, lo, hi, ck, bias_fn, s_even_ref, s_odd_ref):
    cols = q.shape[0]

    def chunk_start(c):
        return pl.multiple_of(jnp.minimum(c, hi - 1) * ck, ck)

    def scores(c, s_ref):
        k0 = chunk_start(c)
        bias = jnp.concatenate([bias_fn(k0)] * (cols // Q_BLOCK), axis=1) + jnp.where(c < hi, 0.0, NEG)
        s_ref[:ck, :] = _dot_nt(k_ref[pl.ds(k0, ck), :], q) * (ATT_SCALE * LOG2E) + bias

    def update(c, s_ref, carry):
        m, l, acc = carry
        s = s_ref[:ck, :]
        m_new = jnp.maximum(m, jnp.max(s, axis=0, keepdims=True))
        alpha = jnp.exp2(m - m_new)
        e = jnp.exp2(s - m_new)
        l = alpha * l + jnp.sum(e, axis=0, keepdims=True)
        acc = alpha * acc + _dot_tn(v_ref[pl.ds(chunk_start(c), ck), :], e.astype(BF16))
        return m_new, l, acc

    def body(i, carry):
        c = lo + 2 * i
        scores(c + 1, s_odd_ref)
        carry = update(c, s_even_ref, carry)
        scores(c + 2, s_even_ref)
        return update(c + 1, s_odd_ref, carry)

    scores(lo, s_even_ref)
    init = (jnp.full((1, cols), NEG, F32), jnp.zeros((1, cols), F32), jnp.zeros((HEAD_DIM, cols), F32))
    m, l, acc = lax.fori_loop(0, (hi - lo + 1) // 2, body, init)
    return jnp.where(m > NEG, acc / jnp.maximum(l, 1e-30), 0.0)


Q_BLOCK = 128
SEL_KEY_CHUNK = 256
WIN_KEY_CHUNK = 128


def _nsa_prompt_body(qc_ref, qr_ref, gate_ref, ksel_ref, vsel_ref, kwin_ref, vwin_ref,
                     kcmp_ref, vcmp_ref, o_ref, sel_ref, s_even_ref, s_odd_ref, *, n_cmp, n_sel):
    qi = pl.program_id(2)
    q0 = qi * Q_BLOCK
    nh = HEADS_PER_KV
    stack = lambda ref: jnp.concatenate(
        [ref[:, hd * HEAD_DIM:(hd + 1) * HEAD_DIM] for hd in range(nh)], axis=0)
    per_head = lambda x: jnp.concatenate([x] * nh, axis=1)
    qc, qr = stack(qc_ref), stack(qr_ref)
    n_cmp_rows = kcmp_ref.shape[0]
    n_sel_rows = sel_ref.shape[0]

    cshape = (n_cmp_rows, Q_BLOCK)
    cblk, cq = _iota(cshape, 0), q0 + _iota(cshape, 1)
    cmask = per_head(jnp.where((cblk < n_cmp) & (cblk * CMP_STRIDE + CMP_LEN - 1 <= cq), 1.0, 0.0)) > 0.5
    s = jnp.where(cmask, _dot_nt(kcmp_ref[...], qc) * ATT_SCALE, NEG)
    e = jnp.where(cmask, jnp.exp(s - jnp.max(s, axis=0, keepdims=True)), 0.0)
    p_cmp = (e / jnp.maximum(jnp.sum(e, axis=0, keepdims=True), 1e-30)).astype(BF16)
    o_cmp = _dot_tn(vcmp_ref[...], p_cmp)

    oshape = (n_sel_rows, n_cmp_rows)
    sj, ci = _iota(oshape, 0), _iota(oshape, 1)
    hit = ((ci * CMP_STRIDE < sj * SEL_LEN + SEL_LEN) & (ci * CMP_STRIDE + CMP_LEN > sj * SEL_LEN)
           & (ci < n_cmp) & (sj < n_sel))
    imp_h = _dot(jnp.where(hit, 1.0, 0.0).astype(BF16), p_cmp)
    imp = imp_h[:, 0:Q_BLOCK]
    for hd in range(1, nh):
        imp = imp + imp_h[:, hd * Q_BLOCK:(hd + 1) * Q_BLOCK]
    ishape = (n_sel_rows, Q_BLOCK)
    blk = _iota(ishape, 0)
    imp = _importance(imp, blk, q0 + _iota(ishape, 1), n_sel)
    rank = jnp.zeros(ishape, F32)
    for k in range(n_sel):
        row = imp[k:k + 1, :]
        rank += jnp.where((row > imp) | ((row == imp) & (blk > k)), 1.0, 0.0)
    sel_ref[...] = jnp.where((rank < min(SEL_TOP, n_sel)) & (blk < n_sel), 1.0, 0.0)

    def sel_mask(k0):
        shape = (SEL_KEY_CHUNK, Q_BLOCK)
        j0 = k0 // SEL_LEN
        chosen = jnp.concatenate(
            [jnp.broadcast_to(sel_ref[pl.ds(j0 + i, 1), :], (SEL_LEN, Q_BLOCK))
             for i in range(SEL_KEY_CHUNK // SEL_LEN)], axis=0)
        live = (k0 + _iota(shape, 0) <= q0 + _iota(shape, 1)) & (chosen > 0.5)
        return jnp.where(live, 0.0, NEG)

    def win_mask(k0):
        shape = (WIN_KEY_CHUNK, Q_BLOCK)
        kpos = k0 + _iota(shape, 0)
        qp = q0 + _iota(shape, 1)
        return jnp.where((kpos <= qp) & (kpos > qp - WINDOW), 0.0, NEG)

    sel_chunks = (q0 + Q_BLOCK + SEL_KEY_CHUNK - 1) // SEL_KEY_CHUNK
    o_sel = _online_attend_t(qr, ksel_ref, vsel_ref, 0, sel_chunks, SEL_KEY_CHUNK, sel_mask,
                             s_even_ref, s_odd_ref)
    win_lo = jnp.maximum(qi - WINDOW // WIN_KEY_CHUNK, 0)
    o_win = _online_attend_t(qr, kwin_ref, vwin_ref, win_lo, qi + 1, WIN_KEY_CHUNK, win_mask,
                             s_even_ref, s_odd_ref)

    gates = jnp.transpose(gate_ref[...])
    for hd in range(nh):
        cols = slice(hd * Q_BLOCK, (hd + 1) * Q_BLOCK)
        c0 = hd * N_BRANCH
        o = (o_cmp[:, cols] * gates[c0:c0 + 1, :] + o_sel[:, cols] * gates[c0 + 1:c0 + 2, :]
             + o_win[:, cols] * gates[c0 + 2:c0 + 3, :])
        o_ref[:, hd * HEAD_DIM:(hd + 1) * HEAD_DIM] = jnp.transpose(o).astype(BF16)


def _nsa_prompt(qc, qr, gates, kv, kcmp, vcmp):
    bsz, t, dq = qc.shape
    n_cmp = (t - CMP_LEN) // CMP_STRIDE + 1
    n_sel = -(-t // SEL_LEN)
    gw = HEADS_PER_KV * HEAD_DIM
    qspec = pl.BlockSpec((None, Q_BLOCK, gw), lambda b, g, i: (b, i, g))
    kvspec = lambda slot: pl.BlockSpec((None, t, HEAD_DIM), lambda b, g, i: (b, 0, slot * N_KV + g))
    cspec = pl.BlockSpec((None, None, kcmp.shape[2], HEAD_DIM), lambda b, g, i: (b, g, 0, 0))
    return pl.pallas_call(
        functools.partial(_nsa_prompt_body, n_cmp=n_cmp, n_sel=n_sel),
        grid=(bsz, N_KV, t // Q_BLOCK),
        in_specs=[qspec, qspec, pl.BlockSpec((None, Q_BLOCK, LANES), lambda b, g, i: (b, i, g)),
                  kvspec(2), kvspec(3), kvspec(4), kvspec(5), cspec, cspec],
        out_specs=qspec,
        out_shape=jax.ShapeDtypeStruct((bsz, t, dq), BF16),
        scratch_shapes=[pltpu.VMEM((-(-n_sel // 8) * 8, Q_BLOCK), F32)]
        + [pltpu.VMEM((max(SEL_KEY_CHUNK, WIN_KEY_CHUNK), gw), F32)] * 2,
        compiler_params=_cparams("parallel", "parallel", "arbitrary"),
        name="nsa_prompt",
    )(qc, qr, gates, kv, kv, kv, kv, kcmp, vcmp)


SEL_LANES = 384


def _select_body(qc_ref, kcmp_ref, vcmp_ref, ocmp_ref, idx_ref, *, qpos, n_cmp, n_sel):
    q = qc_ref[...]
    nc = kcmp_ref.shape[0]
    ci = _iota((HEADS_PER_KV, nc), 1)
    cmask = (ci < n_cmp) & (ci * CMP_STRIDE + CMP_LEN - 1 <= qpos)
    p_cmp = _masked_softmax(_dot_nt(q, kcmp_ref[...]) * ATT_SCALE, cmask).astype(BF16)
    ocmp_ref[...] = _dot(p_cmp, vcmp_ref[...])

    imp = jnp.sum(_dot(p_cmp, _overlap(nc, SEL_LANES, n_cmp, n_sel)), axis=0, keepdims=True)
    lane = _iota((1, SEL_LANES), 1)
    imp = _importance(imp, lane, qpos, n_sel)
    sq = (SEL_LANES, SEL_LANES)
    ri, cj = _iota(sq, 0), _iota(sq, 1)
    impc = jnp.sum(jnp.where(ri == cj, imp, 0.0), axis=1, keepdims=True)
    n_top = min(SEL_TOP, n_sel)
    before_col = jnp.where((impc > imp) | ((impc == imp) & (ri < cj)), 1.0, 0.0)
    before_row = jnp.where((imp > impc) | ((imp == impc) & (cj < ri)), 1.0, 0.0)
    sel_lane = (jnp.sum(before_col, axis=0, keepdims=True) < n_top) & (lane < n_sel)
    sel_row = (jnp.sum(before_row, axis=1, keepdims=True) < n_top) & (_iota((SEL_LANES, 1), 0) < n_sel)
    slot_of = jnp.sum(jnp.where((ri < cj) & sel_row, 1.0, 0.0), axis=0, keepdims=True)
    lshape = (SEL_TOP, SEL_LANES)
    onehot = jnp.where(sel_lane & (slot_of == _iota(lshape, 0).astype(F32)), 1.0, 0.0)
    idx = jnp.sum(onehot * _iota(lshape, 1).astype(F32), axis=1, keepdims=True)
    idx_ref[...] = jnp.broadcast_to(idx, (SEL_TOP, LANES)).astype(jnp.int32)


def _nsa_select(qc, kcmp, vcmp, qpos, n_cmp, n_sel):
    nb = qc.shape[0]
    nc = kcmp.shape[2]
    hspec = pl.BlockSpec((None, None, HEADS_PER_KV, HEAD_DIM), lambda b, g: (b, g, 0, 0))
    cspec = pl.BlockSpec((None, None, nc, HEAD_DIM), lambda b, g: (b, g, 0, 0))
    return pl.pallas_call(
        functools.partial(_select_body, qpos=qpos, n_cmp=n_cmp, n_sel=n_sel),
        grid=(nb, N_KV),
        in_specs=[hspec, cspec, cspec],
        out_specs=[hspec, pl.BlockSpec((None, None, SEL_TOP, LANES), lambda b, g: (b, g, 0, 0))],
        out_shape=[jax.ShapeDtypeStruct((nb, N_KV, HEADS_PER_KV, HEAD_DIM), F32),
                   jax.ShapeDtypeStruct((nb, N_KV, SEL_TOP, LANES), jnp.int32)],
        compiler_params=_cparams("parallel", "parallel"),
        name="nsa_select",
    )(qc, kcmp, vcmp)


def _attend_plus_new(q, k, v, valid, k_new, v_new, new_ok):
    qf = q.astype(F32)
    s = _dot_nt(q, k.astype(BF16)) * ATT_SCALE
    s_new = jnp.sum(qf * k_new.astype(BF16).astype(F32), axis=1, keepdims=True) * ATT_SCALE
    s = jnp.where(valid, s, NEG)
    s_new = jnp.where(new_ok, s_new, NEG)
    m = jnp.maximum(jnp.max(s, axis=-1, keepdims=True), s_new)
    e = jnp.where(valid, jnp.exp(s - m), 0.0)
    e_new = jnp.where(new_ok, jnp.exp(s_new - m), 0.0)
    den = jnp.maximum(jnp.sum(e, axis=-1, keepdims=True) + e_new, 1e-30)
    p = (e / den).astype(BF16)
    p_new = (e_new / den).astype(BF16).astype(F32)
    return _dot(p, v.astype(BF16)) + p_new * v_new.astype(BF16).astype(F32)


SEL_BLOCK_LINES = SEL_LEN * ROW_LINES


def _attend_body(idx_ref, pt_ref, qr_ref, gate_ref, ocmp_ref, *refs, qpos, past_len, n_win):
    del pt_ref
    blocks = refs[:SEL_TOP]
    win_ref, rows_new_ref, win_new_ref, o_ref = refs[SEL_TOP:]
    b, g = pl.program_id(0), pl.program_id(1)
    q = qr_ref[...]
    n_keys = SEL_TOP * SEL_LEN
    base = (b * N_KV + g) * SEL_TOP
    lane = _iota((1, n_keys), 1)
    kpos = jnp.zeros((1, n_keys), jnp.int32)
    n_new = jnp.int32(0)
    for t in range(SEL_TOP):
        j = idx_ref[base + t]
        kpos = jnp.where(lane // SEL_LEN == t, j * SEL_LEN + lane % SEL_LEN, kpos)
        n_new = n_new + jnp.where(j == qpos // SEL_LEN, 1, 0)
    valid = (kpos <= qpos) & (kpos < past_len)
    new_ok = (jnp.zeros((HEADS_PER_KV, 1), jnp.int32) + n_new) > 0
    line = lambda ref, slot, n, per_token: ref[pl.ds(slot * N_KV + g, n, stride=per_token), :]
    k_sel = jnp.concatenate([line(blk, 2, SEL_LEN, ROW_LINES) for blk in blocks], axis=0)
    v_sel = jnp.concatenate([line(blk, 3, SEL_LEN, ROW_LINES) for blk in blocks], axis=0)
    o_sel = _attend_plus_new(q, k_sel, v_sel, valid, line(rows_new_ref, 2, 1, ROW_LINES),
                             line(rows_new_ref, 3, 1, ROW_LINES), new_ok)

    wpos = (past_len - n_win) + _iota((1, n_win), 1)
    wvalid = (wpos <= qpos) & (wpos > qpos - WINDOW) & (wpos >= 0)
    always = jnp.zeros((HEADS_PER_KV, 1), jnp.int32) == 0
    o_win = _attend_plus_new(q, line(win_ref, 0, n_win, WIN_LINES), line(win_ref, 1, n_win, WIN_LINES),
                             wvalid, line(win_new_ref, 0, 1, WIN_LINES), line(win_new_ref, 1, 1, WIN_LINES),
                             always)

    gates = gate_ref[...]
    o_ref[...] = ocmp_ref[...] * gates[:, 0:1] + o_sel * gates[:, 1:2] + o_win * gates[:, 2:3]


def _nsa_attend(idx, page_table, qr, gates, ocmp, cache, cache_win, rows_new, win_new, past_len):
    nb, n_pages = page_table.shape
    n_win = cache_win.shape[0] // (nb * WIN_LINES)
    halves = PAGE_ROWS // SEL_LEN
    n_past_blocks = n_pages * halves

    def sel_block(t):
        def index(b, g, idx_ref, pt_ref):
            j = jnp.minimum(idx_ref[(b * N_KV + g) * SEL_TOP + t], n_past_blocks - 1)
            return (pt_ref[b * n_pages + j // halves] * halves + j % halves, 0)
        return pl.BlockSpec((SEL_BLOCK_LINES, HEAD_DIM), index)

    hspec = pl.BlockSpec((None, None, HEADS_PER_KV, HEAD_DIM), lambda b, g, i, p: (b, g, 0, 0))
    grid_spec = pltpu.PrefetchScalarGridSpec(
        num_scalar_prefetch=2,
        grid=(nb, N_KV),
        in_specs=[hspec, hspec, hspec] + [sel_block(t) for t in range(SEL_TOP)]
        + [pl.BlockSpec((n_win * WIN_LINES, HEAD_DIM), lambda b, g, i, p: (b, 0)),
           pl.BlockSpec((None, ROW_LINES, HEAD_DIM), lambda b, g, i, p: (b, 0, 0)),
           pl.BlockSpec((None, WIN_LINES, HEAD_DIM), lambda b, g, i, p: (b, 0, 0))],
        out_specs=hspec)
    return pl.pallas_call(
        functools.partial(_attend_body, qpos=past_len, past_len=past_len, n_win=n_win),
        grid_spec=grid_spec,
        out_shape=jax.ShapeDtypeStruct((nb, N_KV, HEADS_PER_KV, HEAD_DIM), F32),
        compiler_params=_cparams("parallel", "arbitrary"),
        name="nsa_attend",
    )(idx, page_table.reshape(-1), qr, gates, ocmp, *([cache] * SEL_TOP), cache_win, rows_new, win_new)


PROMPT_FFN_ROWS = 512
PROMPT_ROWS = 512


def _rope_tables(pos):
    half = HEAD_DIM // 2
    inv = jnp.exp(-math.log(ROPE_THETA) * jnp.arange(half, dtype=F32) / half)
    ang = pos.astype(F32)[:, None] * inv[None, :]
    cos, sin = jnp.cos(ang), jnp.sin(ang)
    return jnp.concatenate([cos, cos], axis=1), jnp.concatenate([-sin, sin], axis=1)


def _prepare_weights(prm):
    nq = N_HEADS * HEAD_DIM
    w_qg = prm["attn_w_qg"]
    nb, d = w_qg.shape[:2]
    per_kv = HEADS_PER_KV * N_BRANCH
    w_gate = w_qg[:, :, nq:].reshape(nb, d, N_KV, per_kv)
    w_gate = jnp.pad(w_gate, ((0, 0), (0, 0), (0, 0), (0, LANES - per_kv))).reshape(nb, d, N_KV * LANES)
    return dict(
        ffn16={},
        ple_gate=prm["ple_w_gate"].astype(BF16), ple_proj=prm["ple_w_proj"].astype(BF16),
        glu=prm["glu_w"].astype(BF16), kv=prm["w_kv"].astype(BF16),
        q=w_qg[:, :, :nq].astype(BF16), gate=w_gate.astype(BF16), o=prm["attn_w_o"].astype(BF16),
        w1k=prm["cmp_w1_k"].astype(BF16), w1v=prm["cmp_w1_v"].astype(BF16),
        w2k=prm["cmp_w2_k"].astype(BF16), w2v=prm["cmp_w2_v"].astype(BF16),
        s5=[_s5_layouts(prm["ssm_lam_re"][i], prm["ssm_lam_im"][i], prm["ssm_log_dt"][i],
                        prm["ssm_b_re"][i], prm["ssm_b_im"][i], prm["ssm_c_re"][i], prm["ssm_c_im"][i])
            for i in range(N_A_LAYERS)])


def _finish(gen):
    try:
        while True:
            next(gen)
    except StopIteration as done:
        return done.value


def _run_group(x, p, start, state, past, prm, w):
    bsz, t, d = x.shape
    n = bsz * t
    prompt = past is None
    tm_ffn = PROMPT_FFN_ROWS if prompt else n
    tm = PROMPT_ROWS if prompt else n
    depth = prm["norm_g"].shape[0]
    cos, sin = _rope_tables(start + jnp.arange(t))
    if not prompt:
        cos, sin = jnp.broadcast_to(cos, (n, HEAD_DIM)), jnp.broadcast_to(sin, (n, HEAD_DIM))
    h = x.reshape(n, d)
    ssm_re, ssm_im = [], []
    rows = win = kv16 = kcmp = vcmp = None

    w32 = (prm["ffn_w_gate"], prm["ffn_w_up"], prm["ffn_w_down"])

    def ffn(h, g_pre, g_post, g_next, layer, which):
        if (layer, which) not in w["ffn16"]:
            h_new, xn, *w["ffn16"][layer, which] = _ffn_cast(h, g_pre, g_post, g_next, *w32, layer, which)
            return h_new, xn
        nxt = (layer, which + 1) if which == 0 else (layer + 1, 0)
        if not prompt or nxt[0] == depth:
            return _ffn(h, g_pre, g_post, g_next, *w["ffn16"][layer, which], tm_ffn)
        h_new, xn, *w["ffn16"][nxt] = _ffn_convert(h, g_pre, g_post, g_next, *w["ffn16"][layer, which],
                                                   tm_ffn, *w32, *nxt)
        return h_new, xn

    if not prompt:
        cache_kv, cache_win, page_table = past
        past_len = page_table.shape[1] * cache_kv.shape[1]
        cache_lines = cache_kv.reshape(-1, HEAD_DIM)
        win_lines = cache_win.reshape(-1, HEAD_DIM)
    for i in range(depth):
        g = prm["norm_g"][i]
        h, xn = ffn(h, g[0:1], g[1:2], g[2:3], i, 0)
        if i == 0:
            yield
        if i < N_A_LAYERS:
            if prompt:
                y, st = _s5_prompt(xn, bsz, w["s5"][i])
                st = st.reshape(d // LANES, bsz, 2, S5_GROUPS_PER_STEP, SSM_STATE).transpose(2, 1, 0, 3, 4)
                st = st.reshape(2, bsz, d // SSM_GROUP, SSM_STATE)
                ssm_re.append(st[0])
                ssm_im.append(st[1])
            else:
                y, s_re, s_im = _s5_step(xn, state[0][i].reshape(bsz, -1), state[1][i].reshape(bsz, -1),
                                         w["s5"][i])
                ssm_re.append(s_re.reshape(state[0][i].shape))
                ssm_im.append(s_im.reshape(state[1][i].shape))
            h = _s5_gate(h, xn, y, prm["ssm_d"][i][None], w["glu"], prm["glu_b"][i][None], g[3:4], i, tm)
        else:
            j = i - N_A_LAYERS
            qc, qr, gates = _qg_proj(xn, w["q"], w["gate"], cos, sin, j, tm)
            if prompt:
                o = _nsa_prompt(qc.reshape(bsz, t, -1), qr.reshape(bsz, t, -1), gates.reshape(bsz, t, -1),
                                kv16.reshape(bsz, t, -1), kcmp, vcmp).reshape(n, -1)
            else:
                heads = lambda a: a.reshape(bsz, N_KV, HEADS_PER_KV, HEAD_DIM)
                n_cmp = (past_len + t - CMP_LEN) // CMP_STRIDE + 1
                n_sel = -(-(past_len + t) // SEL_LEN)
                ocmp, idx = _nsa_select(heads(qc), kcmp, vcmp, past_len, n_cmp, n_sel)
                gsm = gates.reshape(bsz, N_KV, LANES)[:, :, :HEADS_PER_KV * N_BRANCH]
                gsm = gsm.reshape(bsz, N_KV, HEADS_PER_KV, N_BRANCH)
                gsm = jnp.pad(gsm, ((0, 0), (0, 0), (0, 0), (0, LANES - N_BRANCH)))
                o = _nsa_attend(idx[..., 0].reshape(-1), page_table, heads(qr), gsm, ocmp,
                                cache_lines, win_lines, rows_l.reshape(bsz, ROW_LINES, HEAD_DIM),
                                win_l.reshape(bsz, WIN_LINES, HEAD_DIM), past_len).reshape(n, -1)
            h = _o_proj(h, o, w["o"], g[3:4], j, tm)
        h, _ = ffn(h, g[4:5], g[5:6], g[6:7], i, 1)
        h = _ple(h, p[i].reshape(n, -1), w["ple_gate"], w["ple_proj"], g[6:7], i, tm)
        if i == N_A_LAYERS - 1:
            rows_l, win_l, kv16 = _kv_proj(h, prm["kv_norm_g"][None], w["kv"], cos, sin, tm)
            rows = rows_l.reshape(bsz, t, ROW_SLOTS, N_KV, HEAD_DIM)
            win = win_l.reshape(bsz, t, WIN_SLOTS, N_KV, HEAD_DIM)
            cmp_w = (prm["cmp_pe_k"], prm["cmp_pe_v"], w["w1k"], w["w1v"], w["w2k"], w["w2v"])
            if prompt:
                pages = jnp.arange(n // PAGE_ROWS, dtype=jnp.int32).reshape(bsz, t // PAGE_ROWS)
                kcmp, vcmp = _compress(rows_l, pages, *cmp_w)
                win = win[:, t - min(WINDOW, t):]
            else:
                kcmp, vcmp = _compress(cache_lines, page_table, *cmp_w)
                win = jnp.concatenate([cache_win, win], axis=1)[:, t:]
    return h.reshape(bsz, t, d), jnp.stack(ssm_re), jnp.stack(ssm_im), rows, win


def kernel(x_prompt, x_sample, state_ssm_re, state_ssm_im, cache_kv, cache_win, page_table,
           p_prompt, p_sample, norm_g, ffn_w_gate, ffn_w_up, ffn_w_down, ple_w_proj, ple_w_gate,
           ssm_lam_re, ssm_lam_im, ssm_log_dt, ssm_b_re, ssm_b_im, ssm_c_re, ssm_c_im, ssm_d,
           glu_w, glu_b, kv_norm_g, w_kv, cmp_pe_k, cmp_pe_v, cmp_w1_k, cmp_w2_k, cmp_w1_v,
           cmp_w2_v, attn_w_qg, attn_w_o):
    prm = dict(norm_g=norm_g, ffn_w_gate=ffn_w_gate, ffn_w_up=ffn_w_up, ffn_w_down=ffn_w_down,
               ple_w_proj=ple_w_proj, ple_w_gate=ple_w_gate, ssm_lam_re=ssm_lam_re,
               ssm_lam_im=ssm_lam_im, ssm_log_dt=ssm_log_dt, ssm_b_re=ssm_b_re, ssm_b_im=ssm_b_im,
               ssm_c_re=ssm_c_re, ssm_c_im=ssm_c_im, ssm_d=ssm_d, glu_w=glu_w, glu_b=glu_b,
               kv_norm_g=kv_norm_g, w_kv=w_kv, cmp_pe_k=cmp_pe_k, cmp_pe_v=cmp_pe_v,
               cmp_w1_k=cmp_w1_k, cmp_w2_k=cmp_w2_k, cmp_w1_v=cmp_w1_v, cmp_w2_v=cmp_w2_v,
               attn_w_qg=attn_w_qg, attn_w_o=attn_w_o)
    assert x_sample.shape[1] == 1, "the decode group handles one new token per sequence"
    assert cache_kv.shape[1] == PAGE_ROWS
    w = _prepare_weights(prm)
    past_len = page_table.shape[1] * cache_kv.shape[1]
    decode = _run_group(x_sample, p_sample, past_len, (state_ssm_re, state_ssm_im),
                        (cache_kv, cache_win, page_table), prm, w)
    next(decode)
    y_p, re_p, im_p, kv_p, win_p = _finish(_run_group(x_prompt, p_prompt, 0, None, None, prm, w))
    y_s, re_s, im_s, kv_s, win_s = _finish(decode)
    return (y_p, y_s, re_p, im_p, re_s, im_s, kv_p, kv_s, win_p, win_s)
```

```python
import functools
import math

import jax
import jax.numpy as jnp
from jax import lax
from jax.experimental import pallas as pl
from jax.experimental.pallas import tpu as pltpu

F32 = jnp.float32
BF16 = jnp.bfloat16
HIGHEST = lax.Precision.HIGHEST

N_A_LAYERS = 2
NORM_EPS = 1e-6
SSM_GROUP = 16
SSM_STATE = 64
HEAD_DIM = 128
N_KV = 2
HEADS_PER_KV = 8
N_HEADS = N_KV * HEADS_PER_KV
N_BRANCH = 3
N_KV_SLOTS = 6
CMP_LEN = 32
CMP_STRIDE = 16
SEL_LEN = 64
SEL_TOP = 16
WINDOW = 512
ROPE_THETA = 10000.0
FORCE_BONUS = 1e3
NEG = -1e30
LOWEST = -3.0e38
ATT_SCALE = HEAD_DIM ** -0.5

LANES = 128
V7X_VMEM_BYTES = 64 * 2 ** 20
VMEM_LIMIT_BYTES = V7X_VMEM_BYTES * 7 // 8
FFN_TILE_F = 512
S5_CHUNK = 16
S5_GROUPS_PER_STEP = LANES // SSM_GROUP
S5_STATE_LANES = S5_GROUPS_PER_STEP * SSM_STATE


def _cparams(*sem):
    return pltpu.CompilerParams(dimension_semantics=sem, vmem_limit_bytes=VMEM_LIMIT_BYTES)


def _rms(x, g):
    y = x * lax.rsqrt(jnp.mean(x * x, axis=-1, keepdims=True) + NORM_EPS)
    return y * g


def _dot(a, b, **kw):
    return jnp.dot(a, b, preferred_element_type=F32, **kw)


def _dot_nt(a, b):
    return lax.dot_general(a, b, (((1,), (1,)), ((), ())), preferred_element_type=F32)


def _rope(x, cos, sin):
    return x * cos + pltpu.roll(x, HEAD_DIM // 2, 1) * sin


def _iota(shape, axis):
    return lax.broadcasted_iota(jnp.int32, shape, axis)


def _ffn_step(k, nk, h_ref, gpre_ref, gpost_ref, gnext_ref, wg_ref, wu_ref, wd_ref, o_ref, xn_out_ref,
              xn_ref, acc_ref, side_work=None):
    @pl.when(k == 0)
    def _():
        xn_ref[...] = _rms(h_ref[...], gpre_ref[...]).astype(BF16)
        acc_ref[...] = jnp.zeros_like(acc_ref)

    xn = xn_ref[...]
    gate = _dot(xn, wg_ref[...])
    up = _dot(xn, wu_ref[...])
    mid = (jax.nn.silu(gate) * up).astype(BF16)
    acc_ref[...] += _dot(mid, wd_ref[...])
    if side_work is not None:
        side_work()

    @pl.when(k == nk - 1)
    def _():
        h_new = h_ref[...] + 0.5 * _rms(acc_ref[...], gpost_ref[...])
        o_ref[...] = h_new
        xn_out_ref[...] = _rms(h_new, gnext_ref[...])


def _ffn_body(h_ref, gpre_ref, gpost_ref, gnext_ref, wg_ref, wu_ref, wd_ref, o_ref, xn_out_ref,
              xn_ref, acc_ref, *, nk):
    _ffn_step(pl.program_id(1), nk, h_ref, gpre_ref, gpost_ref, gnext_ref,
              wg_ref, wu_ref, wd_ref, o_ref, xn_out_ref, xn_ref, acc_ref)


def _ffn(h, g_pre, g_post, g_next, wg, wu, wd, tm):
    n, d = h.shape
    nk = wg.shape[-1] // FFN_TILE_F
    row = lambda i, k: (i, 0)
    vec = lambda i, k: (0, 0)
    return pl.pallas_call(
        functools.partial(_ffn_body, nk=nk),
        grid=(n // tm, nk),
        in_specs=[pl.BlockSpec((tm, d), row), pl.BlockSpec((1, d), vec), pl.BlockSpec((1, d), vec),
                  pl.BlockSpec((1, d), vec),
                  pl.BlockSpec((d, FFN_TILE_F), lambda i, k: (0, k)),
                  pl.BlockSpec((d, FFN_TILE_F), lambda i, k: (0, k)),
                  pl.BlockSpec((FFN_TILE_F, d), lambda i, k: (k, 0))],
        out_specs=[pl.BlockSpec((tm, d), row), pl.BlockSpec((tm, d), row)],
        out_shape=[jax.ShapeDtypeStruct((n, d), F32)] * 2,
        scratch_shapes=[pltpu.VMEM((tm, d), BF16), pltpu.VMEM((tm, d), F32)],
        compiler_params=_cparams("parallel", "arbitrary"),
        name="ffn",
    )(h, g_pre, g_post, g_next, wg, wu, wd)


FFN_CONVERT_ROWS = 16


def _ffn_convert_body(h_ref, gpre_ref, gpost_ref, gnext_ref, wg_ref, wu_ref, wd_ref,
                      ng_ref, nu_ref, nd_ref, o_ref, xn_out_ref, ng16_ref, nu16_ref, nd16_ref,
                      xn_ref, acc_ref, *, nk, d_ff):
    i, k = pl.program_id(0), pl.program_id(1)

    def convert_next():
        pad = ng16_ref.shape[1] - d_ff
        for src, dst in ((ng_ref, ng16_ref), (nu_ref, nu16_ref)):
            dst[:, :d_ff] = src[...].astype(BF16)
            dst[:, d_ff:] = jnp.zeros((dst.shape[0], pad), BF16)
        rows = nd_ref.shape[0]
        live = d_ff - (i * nk + k) * rows
        nd16_ref[...] = jnp.where(_iota(nd_ref.shape, 0) < live, nd_ref[...], 0.0).astype(BF16)

    _ffn_step(k, nk, h_ref, gpre_ref, gpost_ref, gnext_ref,
              wg_ref, wu_ref, wd_ref, o_ref, xn_out_ref, xn_ref, acc_ref, convert_next)


def _ffn_convert(h, g_pre, g_post, g_next, wg, wu, wd, tm, wg32, wu32, wd32, layer, which):
    n, d = h.shape
    fp = wg.shape[-1]
    d_ff = wg32.shape[-1]
    nk = fp // FFN_TILE_F
    steps = (n // tm) * nk
    rg = FFN_CONVERT_ROWS
    rd = fp // steps
    assert fp % steps == 0 and d_ff % rd == 0 and d % rg == 0 and d // rg <= steps
    row = lambda i, k: (i, 0)
    vec = lambda i, k: (0, 0)
    up_slab = lambda i, k: jnp.minimum(i * nk + k, d // rg - 1)
    down_slab = lambda i, k: jnp.minimum(i * nk + k, d_ff // rd - 1)
    return pl.pallas_call(
        functools.partial(_ffn_convert_body, nk=nk, d_ff=d_ff),
        grid=(n // tm, nk),
        in_specs=[pl.BlockSpec((tm, d), row), pl.BlockSpec((1, d), vec), pl.BlockSpec((1, d), vec),
                  pl.BlockSpec((1, d), vec),
                  pl.BlockSpec((d, FFN_TILE_F), lambda i, k: (0, k)),
                  pl.BlockSpec((d, FFN_TILE_F), lambda i, k: (0, k)),
                  pl.BlockSpec((FFN_TILE_F, d), lambda i, k: (k, 0)),
                  pl.BlockSpec((None, None, rg, d_ff), lambda i, k: (layer, which, up_slab(i, k), 0)),
                  pl.BlockSpec((None, None, rg, d_ff), lambda i, k: (layer, which, up_slab(i, k), 0)),
                  pl.BlockSpec((None, None, rd, d), lambda i, k: (layer, which, down_slab(i, k), 0))],
        out_specs=[pl.BlockSpec((tm, d), row), pl.BlockSpec((tm, d), row),
                   pl.BlockSpec((rg, fp), lambda i, k: (up_slab(i, k), 0)),
                   pl.BlockSpec((rg, fp), lambda i, k: (up_slab(i, k), 0)),
                   pl.BlockSpec((rd, d), lambda i, k: (i * nk + k, 0))],
        out_shape=[jax.ShapeDtypeStruct((n, d), F32)] * 2
        + [jax.ShapeDtypeStruct((d, fp), BF16)] * 2 + [jax.ShapeDtypeStruct((fp, d), BF16)],
        scratch_shapes=[pltpu.VMEM((tm, d), BF16), pltpu.VMEM((tm, d), F32)],
        compiler_params=_cparams("arbitrary", "arbitrary"),
        name="ffn_convert",
    )(h, g_pre, g_post, g_next, wg, wu, wd, wg32, wu32, wd32)


def _ffn_cast_body(h_ref, gpre_ref, gpost_ref, gnext_ref, wg_ref, wu_ref, wd_ref,
                   o_ref, xn_out_ref, wg16_ref, wu16_ref, wd16_ref, xn_ref, acc_ref, *, nk, d_ff):
    k = pl.program_id(1)
    live = d_ff - k * FFN_TILE_F
    wshape, dshape = wg_ref.shape, wd_ref.shape
    wg16_ref[...] = jnp.where(_iota(wshape, 1) < live, wg_ref[...], 0.0).astype(BF16)
    wu16_ref[...] = jnp.where(_iota(wshape, 1) < live, wu_ref[...], 0.0).astype(BF16)
    wd16_ref[...] = jnp.where(_iota(dshape, 0) < live, wd_ref[...], 0.0).astype(BF16)
    _ffn_step(k, nk, h_ref, gpre_ref, gpost_ref, gnext_ref, wg16_ref, wu16_ref, wd16_ref,
              o_ref, xn_out_ref, xn_ref, acc_ref)


def _ffn_cast(h, g_pre, g_post, g_next, wg, wu, wd, layer, which):
    n, d = h.shape
    d_ff = wg.shape[-1]
    nk = -(-d_ff // FFN_TILE_F)
    fp = nk * FFN_TILE_F
    row = lambda i, k: (0, 0)
    return pl.pallas_call(
        functools.partial(_ffn_cast_body, nk=nk, d_ff=d_ff),
        grid=(1, nk),
        in_specs=[pl.BlockSpec((n, d), row), pl.BlockSpec((1, d), row), pl.BlockSpec((1, d), row),
                  pl.BlockSpec((1, d), row),
                  pl.BlockSpec((None, None, d, FFN_TILE_F), lambda i, k: (layer, which, 0, k)),
                  pl.BlockSpec((None, None, d, FFN_TILE_F), lambda i, k: (layer, which, 0, k)),
                  pl.BlockSpec((None, None, FFN_TILE_F, d), lambda i, k: (layer, which, k, 0))],
        out_specs=[pl.BlockSpec((n, d), row), pl.BlockSpec((n, d), row),
                   pl.BlockSpec((d, FFN_TILE_F), lambda i, k: (0, k)),
                   pl.BlockSpec((d, FFN_TILE_F), lambda i, k: (0, k)),
                   pl.BlockSpec((FFN_TILE_F, d), lambda i, k: (k, 0))],
        out_shape=[jax.ShapeDtypeStruct((n, d), F32)] * 2
        + [jax.ShapeDtypeStruct((d, fp), BF16)] * 2 + [jax.ShapeDtypeStruct((fp, d), BF16)],
        scratch_shapes=[pltpu.VMEM((n, d), BF16), pltpu.VMEM((n, d), F32)],
        compiler_params=_cparams("arbitrary", "arbitrary"),
        name="ffn_cast",
    )(h, g_pre, g_post, g_next, wg, wu, wd)


def _ple_body(h_ref, p_ref, wgate_ref, wproj_ref, g_ref, o_ref):
    h = h_ref[...]
    gate = jax.nn.sigmoid(_dot(h.astype(BF16), wgate_ref[...]))
    proj = _dot(p_ref[...].astype(BF16), wproj_ref[...])
    o_ref[...] = h + _rms(gate * proj, g_ref[...])


def _ple(h, p, wgate, wproj, g, layer, tm):
    n, d = h.shape
    pd = p.shape[-1]
    row = lambda i: (i, 0)
    return pl.pallas_call(
        _ple_body,
        grid=(n // tm,),
        in_specs=[pl.BlockSpec((tm, d), row), pl.BlockSpec((tm, pd), row),
                  pl.BlockSpec((None, d, d), lambda i: (layer, 0, 0)),
                  pl.BlockSpec((None, pd, d), lambda i: (layer, 0, 0)),
                  pl.BlockSpec((1, d), lambda i: (0, 0))],
        out_specs=pl.BlockSpec((tm, d), row),
        out_shape=jax.ShapeDtypeStruct((n, d), F32),
        compiler_params=_cparams("parallel"),
        name="ple",
    )(h, p, wgate, wproj, g)


ROW_SLOTS = 4
WIN_SLOTS = N_KV_SLOTS - ROW_SLOTS
ROW_LINES = ROW_SLOTS * N_KV
WIN_LINES = WIN_SLOTS * N_KV


def _kv_body(h_ref, g_ref, w_ref, cos_ref, sin_ref, rows_ref, win_ref, kv16_ref, *, tm):
    xn = _rms(h_ref[...], g_ref[...]).astype(BF16)
    kv = _dot(xn, w_ref[...])
    cos, sin = cos_ref[...], sin_ref[...]
    for slot in range(N_KV_SLOTS):
        for g in range(N_KV):
            line = slot * N_KV + g
            lo = line * HEAD_DIM
            x = kv[:, lo:lo + HEAD_DIM]
            if slot in (2, 4):
                x = _rope(x, cos, sin)
            kv16_ref[:, lo:lo + HEAD_DIM] = x.astype(BF16)
            if slot < ROW_SLOTS:
                rows_ref[pl.ds(line, tm, stride=ROW_LINES), :] = x
            else:
                win_ref[pl.ds(line - ROW_LINES, tm, stride=WIN_LINES), :] = x


def _kv_proj(h, g, w, cos, sin, tm):
    n, d = h.shape
    nw = w.shape[-1]
    nt = cos.shape[0] // tm
    row = lambda i: (i, 0)
    tab = lambda i: (i % nt, 0)
    return pl.pallas_call(
        functools.partial(_kv_body, tm=tm),
        grid=(n // tm,),
        in_specs=[pl.BlockSpec((tm, d), row), pl.BlockSpec((1, d), lambda i: (0, 0)),
                  pl.BlockSpec((d, nw), lambda i: (0, 0)),
                  pl.BlockSpec((tm, HEAD_DIM), tab), pl.BlockSpec((tm, HEAD_DIM), tab)],
        out_specs=[pl.BlockSpec((tm * ROW_LINES, HEAD_DIM), row),
                   pl.BlockSpec((tm * WIN_LINES, HEAD_DIM), row),
                   pl.BlockSpec((tm, nw), row)],
        out_shape=[jax.ShapeDtypeStruct((n * ROW_LINES, HEAD_DIM), F32),
                   jax.ShapeDtypeStruct((n * WIN_LINES, HEAD_DIM), F32),
                   jax.ShapeDtypeStruct((n, nw), BF16)],
        compiler_params=_cparams("parallel"),
        name="kv_proj",
    )(h, g, w, cos, sin)


def _qg_body(xn_ref, wq_ref, wgt_ref, cos_ref, sin_ref, qc_ref, qr_ref, gate_ref):
    xn = xn_ref[...].astype(BF16)
    q = _dot(xn, wq_ref[...])
    qc_ref[...] = q.astype(BF16)
    cos, sin = cos_ref[...], sin_ref[...]
    for hd in range(N_HEADS):
        sl = slice(hd * HEAD_DIM, (hd + 1) * HEAD_DIM)
        qr_ref[:, sl] = _rope(q[:, sl], cos, sin).astype(BF16)
    gate_ref[...] = jax.nn.sigmoid(_dot(xn, wgt_ref[...]))


def _qg_proj(xn, wq, wgt, cos, sin, layer, tm):
    n, d = xn.shape
    nq, ng = wq.shape[-1], wgt.shape[-1]
    nt = cos.shape[0] // tm
    row = lambda i: (i, 0)
    tab = lambda i: (i % nt, 0)
    return pl.pallas_call(
        _qg_body,
        grid=(n // tm,),
        in_specs=[pl.BlockSpec((tm, d), row),
                  pl.BlockSpec((None, d, nq), lambda i: (layer, 0, 0)),
                  pl.BlockSpec((None, d, ng), lambda i: (layer, 0, 0)),
                  pl.BlockSpec((tm, HEAD_DIM), tab), pl.BlockSpec((tm, HEAD_DIM), tab)],
        out_specs=[pl.BlockSpec((tm, nq), row), pl.BlockSpec((tm, nq), row), pl.BlockSpec((tm, ng), row)],
        out_shape=[jax.ShapeDtypeStruct((n, nq), BF16), jax.ShapeDtypeStruct((n, nq), BF16),
                   jax.ShapeDtypeStruct((n, ng), F32)],
        compiler_params=_cparams("parallel"),
        name="qg_proj",
    )(xn, wq, wgt, cos, sin)


def _oproj_body(h_ref, o_ref, w_ref, g_ref, out_ref):
    m = _dot(o_ref[...].astype(BF16), w_ref[...])
    out_ref[...] = h_ref[...] + _rms(m, g_ref[...])


def _o_proj(h, o, w, g, layer, tm):
    n, d = h.shape
    row = lambda i: (i, 0)
    return pl.pallas_call(
        _oproj_body,
        grid=(n // tm,),
        in_specs=[pl.BlockSpec((tm, d), row), pl.BlockSpec((tm, d), row),
                  pl.BlockSpec((None, d, d), lambda i: (layer, 0, 0)),
                  pl.BlockSpec((1, d), lambda i: (0, 0))],
        out_specs=pl.BlockSpec((tm, d), row),
        out_shape=jax.ShapeDtypeStruct((n, d), F32),
        compiler_params=_cparams("parallel"),
        name="o_proj",
    )(h, o, w, g)


def _s5gate_body(h_ref, xn_ref, y_ref, d_ref, w_ref, b_ref, g_ref, out_ref):
    y = jax.nn.gelu(y_ref[...] + d_ref[...] * xn_ref[...])
    z = jax.nn.sigmoid(_dot(y.astype(BF16), w_ref[...]) + b_ref[...])
    out_ref[...] = h_ref[...] + _rms(y * z, g_ref[...])


def _s5_gate(h, xn, y, dskip, w, b, g, layer, tm):
    n, d = h.shape
    row = lambda i: (i, 0)
    vec = lambda i: (0, 0)
    return pl.pallas_call(
        _s5gate_body,
        grid=(n // tm,),
        in_specs=[pl.BlockSpec((tm, d), row), pl.BlockSpec((tm, d), row), pl.BlockSpec((tm, d), row),
                  pl.BlockSpec((1, d), vec),
                  pl.BlockSpec((None, d, d), lambda i: (layer, 0, 0)),
                  pl.BlockSpec((1, d), vec), pl.BlockSpec((1, d), vec)],
        out_specs=pl.BlockSpec((tm, d), row),
        out_shape=jax.ShapeDtypeStruct((n, d), F32),
        compiler_params=_cparams("parallel"),
        name="s5_gate",
    )(h, xn, y, dskip, w, b, g)


def _s5_disc(lam_re, lam_im, log_dt):
    dt = jnp.exp(log_dt)
    mag = jnp.exp(lam_re * dt)
    a_re, a_im = mag * jnp.cos(lam_im * dt), mag * jnp.sin(lam_im * dt)
    den = lam_re * lam_re + lam_im * lam_im
    nr, ni = a_re - 1.0, a_im
    r_re = (nr * lam_re + ni * lam_im) / den
    r_im = (ni * lam_re - nr * lam_im) / den
    return a_re, a_im, r_re, r_im


def _cmul(ar, ai, br, bi):
    return ar * br - ai * bi, ar * bi + ai * br


def _bd_rows(m):
    shape = (LANES, S5_STATE_LANES)
    same = _iota(shape, 0) // SSM_GROUP == _iota(shape, 1) // SSM_STATE
    return jnp.where(same, jnp.concatenate([m] * S5_GROUPS_PER_STEP, axis=1), 0.0)


def _bd_cols(m):
    shape = (S5_STATE_LANES, LANES)
    same = _iota(shape, 0) // SSM_STATE == _iota(shape, 1) // SSM_GROUP
    return jnp.where(same, jnp.concatenate([m] * S5_GROUPS_PER_STEP, axis=0), 0.0)


def _s5_input_weights(lam_re, lam_im, log_dt, b_re, b_im):
    a_re, a_im, r_re, r_im = _s5_disc(lam_re, lam_im, log_dt)
    bb_re, bb_im = _cmul(r_re, r_im, b_re, b_im)
    return a_re, a_im, bb_re, bb_im


def _s5_prompt_body(u_ref, lr_re_ref, lr_im_ref, lr_dt_ref, b_re_ref, b_im_ref,
                    lc_re_ref, lc_im_ref, lc_dt_ref, c_re_ref, c_im_ref,
                    lv_re_ref, lv_im_ref, lv_dt_ref,
                    y_ref, st_ref,
                    sin_ref, toep_ref, fout_ref, v_ref, xp_ref, *, n_batch, n_chunk):
    L = S5_CHUNK
    rows = n_batch * n_chunk
    half = S5_STATE_LANES

    a_re, a_im, w_re, w_im = _s5_input_weights(lr_re_ref[...], lr_im_ref[...], lr_dt_ref[...],
                                               b_re_ref[...], b_im_ref[...])
    ws_re, ws_im = [w_re], [w_im]
    for _ in range(L - 1):
        w_re, w_im = _cmul(a_re, a_im, w_re, w_im)
        ws_re.append(w_re)
        ws_im.append(w_im)
    for s in range(L):
        k = L - 1 - s
        sin_ref[s * LANES:(s + 1) * LANES, :] = jnp.concatenate(
            [_bd_rows(ws_re[k]), _bd_rows(ws_im[k])], axis=1).astype(BF16)
    stack_re = jnp.concatenate([ws_re[L - 1 - s] for s in range(L)], axis=0)
    stack_im = jnp.concatenate([ws_im[L - 1 - s] for s in range(L)], axis=0)
    taps = (_dot(stack_re, c_re_ref[...], precision=HIGHEST)
            - _dot(stack_im, c_im_ref[...], precision=HIGHEST))
    tshape = (L * LANES, LANES)
    same = (_iota(tshape, 0) % LANES) // SSM_GROUP == _iota(tshape, 1) // SSM_GROUP
    taps = jnp.where(same, taps, 0.0).astype(BF16)
    toep_ref[:, LANES:] = taps
    toep_ref[:(L - 1) * LANES, :LANES] = taps[LANES:]
    toep_ref[(L - 1) * LANES:, :LANES] = jnp.zeros((LANES, LANES), BF16)
    ac_re, ac_im, _, _ = _s5_disc(lc_re_ref[...], lc_im_ref[...], lc_dt_ref[...])
    ca_re, ca_im = c_re_ref[...], c_im_ref[...]
    for t in range(L):
        ca_re, ca_im = _cmul(ac_re, ac_im, ca_re, ca_im)
        fout_ref[:, t * LANES:(t + 1) * LANES] = jnp.concatenate(
            [_bd_cols(ca_re), -_bd_cols(ca_im)], axis=0).astype(BF16)
    al_re, al_im, _, _ = _s5_disc(lv_re_ref[...], lv_im_ref[...], lv_dt_ref[...])
    for _ in range(int(math.log2(L))):
        al_re, al_im = _cmul(al_re, al_im, al_re, al_im)

    xb = jnp.concatenate([u_ref[pl.ds(s, rows, stride=L), :] for s in range(L)], axis=1).astype(BF16)
    v_ref[...] = _dot(xb, sin_ref[...])

    def step(c, carry):
        new = []
        for b in range(n_batch):
            r = b * n_chunk + c
            x_re, x_im = carry[b]
            xp_ref[pl.ds(r, 1), :] = jnp.concatenate([x_re, x_im], axis=1)
            inc = v_ref[pl.ds(r, 1), :]
            new.append((al_re * x_re - al_im * x_im + inc[:, :half],
                        al_re * x_im + al_im * x_re + inc[:, half:]))
        return tuple(new)

    zero = jnp.zeros((1, half), F32)
    final = lax.fori_loop(0, n_chunk, step, tuple((zero, zero) for _ in range(n_batch)), unroll=4)
    for b in range(n_batch):
        st_ref[b:b + 1, :] = jnp.concatenate(final[b], axis=1)

    xpb = xp_ref[...].astype(BF16)
    for t in range(0, L, 2):
        pair = (_dot(xb[:, :(t + 2) * LANES], toep_ref[(L - 2 - t) * LANES:, :])
                + _dot(xpb, fout_ref[:, t * LANES:(t + 2) * LANES]))
        y_ref[pl.ds(t, rows, stride=L), :] = pair[:, :LANES]
        y_ref[pl.ds(t + 1, rows, stride=L), :] = pair[:, LANES:]


def _s5_layouts(lam_re, lam_im, log_dt, b_re, b_im, c_re, c_im):
    n_g = lam_re.shape[0]
    ldt = jnp.broadcast_to(log_dt[:, None], lam_re.shape)
    rep = lambda a: jnp.repeat(a, SSM_GROUP, axis=0)
    rows = (rep(lam_re), rep(lam_im), rep(ldt),
            b_re.transpose(0, 2, 1).reshape(n_g * SSM_GROUP, SSM_STATE),
            b_im.transpose(0, 2, 1).reshape(n_g * SSM_GROUP, SSM_STATE))
    cols = (rep(lam_re).T, rep(lam_im).T, rep(ldt).T,
            c_re.reshape(n_g * SSM_GROUP, SSM_STATE).T, c_im.reshape(n_g * SSM_GROUP, SSM_STATE).T)
    n_j = n_g // S5_GROUPS_PER_STEP
    lanes = tuple(a.reshape(n_j, 1, S5_STATE_LANES) for a in (lam_re, lam_im, ldt))
    return rows, cols, lanes


def _s5_specs():
    rspec = pl.BlockSpec((LANES, SSM_STATE), lambda j: (j, 0))
    cspec = pl.BlockSpec((SSM_STATE, LANES), lambda j: (0, j))
    vspec = pl.BlockSpec((None, 1, S5_STATE_LANES), lambda j: (j, 0, 0))
    return [rspec] * 5 + [cspec] * 5 + [vspec] * 3


def _s5_prompt(u, n_batch, layouts):
    n, d = u.shape
    n_j = d // LANES
    n_chunk = n // n_batch // S5_CHUNK
    rows = n // S5_CHUNK
    rows_p, cols_p, lanes_p = layouts
    col = lambda j: (0, j)
    return pl.pallas_call(
        functools.partial(_s5_prompt_body, n_batch=n_batch, n_chunk=n_chunk),
        grid=(n_j,),
        in_specs=[pl.BlockSpec((n, LANES), col)] + _s5_specs(),
        out_specs=[pl.BlockSpec((n, LANES), col),
                   pl.BlockSpec((None, n_batch, 2 * S5_STATE_LANES), lambda j: (j, 0, 0))],
        out_shape=[jax.ShapeDtypeStruct((n, d), F32),
                   jax.ShapeDtypeStruct((n_j, n_batch, 2 * S5_STATE_LANES), F32)],
        scratch_shapes=[pltpu.VMEM((S5_CHUNK * LANES, 2 * S5_STATE_LANES), BF16),
                        pltpu.VMEM((S5_CHUNK * LANES, 2 * LANES), BF16),
                        pltpu.VMEM((2 * S5_STATE_LANES, S5_CHUNK * LANES), BF16),
                        pltpu.VMEM((rows, 2 * S5_STATE_LANES), F32),
                        pltpu.VMEM((rows, 2 * S5_STATE_LANES), F32)],
        compiler_params=_cparams("parallel"),
        name="s5_prompt",
    )(u, *rows_p, *cols_p, *lanes_p)


def _s5_step_body(u_ref, hre_ref, him_ref, lr_re_ref, lr_im_ref, lr_dt_ref, b_re_ref, b_im_ref,
                  lc_re_ref, lc_im_ref, lc_dt_ref, c_re_ref, c_im_ref,
                  lv_re_ref, lv_im_ref, lv_dt_ref, y_ref, ore_ref, oim_ref):
    _, _, bb_re, bb_im = _s5_input_weights(lr_re_ref[...], lr_im_ref[...], lr_dt_ref[...],
                                           b_re_ref[...], b_im_ref[...])
    a_re, a_im, _, _ = _s5_disc(lv_re_ref[...], lv_im_ref[...], lv_dt_ref[...])
    u = u_ref[...]
    h_re, h_im = hre_ref[...], him_ref[...]
    x_re = _dot(u, _bd_rows(bb_re), precision=HIGHEST) + (a_re * h_re - a_im * h_im)
    x_im = _dot(u, _bd_rows(bb_im), precision=HIGHEST) + (a_re * h_im + a_im * h_re)
    ore_ref[...] = x_re
    oim_ref[...] = x_im
    y_ref[...] = (_dot(x_re, _bd_cols(c_re_ref[...]), precision=HIGHEST)
                  - _dot(x_im, _bd_cols(c_im_ref[...]), precision=HIGHEST))


def _s5_step(u, h_re, h_im, layouts):
    nb, d = u.shape
    rows_p, cols_p, lanes_p = layouts
    col = lambda j: (0, j)
    sspec = pl.BlockSpec((nb, S5_STATE_LANES), col)
    return pl.pallas_call(
        _s5_step_body,
        grid=(d // LANES,),
        in_specs=[pl.BlockSpec((nb, LANES), col), sspec, sspec] + _s5_specs(),
        out_specs=[pl.BlockSpec((nb, LANES), col), sspec, sspec],
        out_shape=[jax.ShapeDtypeStruct((nb, d), F32),
                   jax.ShapeDtypeStruct(h_re.shape, F32), jax.ShapeDtypeStruct(h_im.shape, F32)],
        compiler_params=_cparams("parallel"),
        name="s5_step",
    )(u, h_re, h_im, *rows_p, *cols_p, *lanes_p)


PAGE_ROWS = 128
PAGE_LINES = PAGE_ROWS * ROW_LINES
CHUNKS_PER_PAGE = PAGE_ROWS // CMP_STRIDE
CMP_PAGES_PER_STEP = 8
CMP_LINES = 2 * N_KV


def _compress_body(pt_ref, *refs, n_pages):
    del pt_ref
    k_pages = CMP_PAGES_PER_STEP
    srcs = refs[:k_pages]
    (pek_ref, pev_ref, w1k_ref, w1v_ref, w2k_ref, w2v_ref, kc_ref, vc_ref, buf_ref, shift_ref) = refs[k_pages:]
    p = pl.program_id(1)
    for k in range(k_pages):
        r0 = pl.multiple_of((p * k_pages + k) * PAGE_ROWS, PAGE_ROWS)
        for sg in range(CMP_LINES):
            buf_ref[sg, pl.ds(r0, PAGE_ROWS), :] = srcs[k][pl.ds(sg, PAGE_ROWS, stride=ROW_LINES), :]

    @pl.when(p == n_pages // k_pages - 1)
    def _():
        n_chunk = n_pages * CHUNKS_PER_PAGE
        half = CMP_STRIDE
        for slot, (pe_ref, w1_ref, w2_ref, out_ref) in enumerate(
                ((pek_ref, w1k_ref, w2k_ref, kc_ref), (pev_ref, w1v_ref, w2v_ref, vc_ref))):
            for g in range(N_KV):
                sg = slot * N_KV + g
                xs = [buf_ref[sg, pl.ds(l, n_chunk, stride=CMP_STRIDE), :] for l in range(half)]
                as_first = jnp.concatenate(
                    [(x + pe_ref[l:l + 1, :]).astype(BF16) for l, x in enumerate(xs)], axis=1)
                as_second = jnp.concatenate(
                    [(x + pe_ref[half + l:half + l + 1, :]).astype(BF16) for l, x in enumerate(xs)], axis=1)
                first = _dot(as_first, w1_ref[:half * HEAD_DIM, :])
                shift_ref[0:n_chunk, :] = _dot(as_second, w1_ref[half * HEAD_DIM:, :])
                shift_ref[n_chunk:n_chunk + 8, :] = jnp.zeros((8, HEAD_DIM), F32)
                hid = first + shift_ref[1:n_chunk + 1, :]
                out = _dot(jax.nn.silu(hid).astype(BF16), w2_ref[...])
                live = _iota(out.shape, 0) < n_chunk - 1
                out_ref[g] = jnp.where(live, out, 0.0).astype(BF16)


def _compress(src, page_ids, pe_k, pe_v, w1k, w1v, w2k, w2v):
    nb, n_pages = page_ids.shape
    k_pages = CMP_PAGES_PER_STEP
    n_chunk = n_pages * CHUNKS_PER_PAGE
    const = lambda b, p, pt: (0, 0)
    page = lambda k: pl.BlockSpec((PAGE_LINES, HEAD_DIM),
                                  lambda b, p, pt: (pt[b * n_pages + p * k_pages + k], 0))
    out_spec = pl.BlockSpec((None, N_KV, n_chunk, HEAD_DIM), lambda b, p, pt: (b, 0, 0, 0))
    grid_spec = pltpu.PrefetchScalarGridSpec(
        num_scalar_prefetch=1,
        grid=(nb, n_pages // k_pages),
        in_specs=[page(k) for k in range(k_pages)]
        + [pl.BlockSpec(pe_k.shape, const), pl.BlockSpec(pe_v.shape, const),
           pl.BlockSpec(w1k.shape, const), pl.BlockSpec(w1v.shape, const),
           pl.BlockSpec(w2k.shape, const), pl.BlockSpec(w2v.shape, const)],
        out_specs=[out_spec, out_spec],
        scratch_shapes=[pltpu.VMEM((CMP_LINES, n_pages * PAGE_ROWS, HEAD_DIM), F32),
                        pltpu.VMEM((n_chunk + 8, HEAD_DIM), F32)])
    return pl.pallas_call(
        functools.partial(_compress_body, n_pages=n_pages),
        grid_spec=grid_spec,
        out_shape=[jax.ShapeDtypeStruct((nb, N_KV, n_chunk, HEAD_DIM), BF16)] * 2,
        compiler_params=_cparams("parallel", "arbitrary"),
        name="compress",
    )(page_ids.reshape(-1), *([src] * k_pages), pe_k, pe_v, w1k, w1v, w2k, w2v)


def _masked_softmax(s, mask):
    s = jnp.where(mask, s, NEG)
    m = jnp.max(s, axis=-1, keepdims=True)
    e = jnp.where(mask, jnp.exp(s - m), 0.0)
    return e / jnp.maximum(jnp.sum(e, axis=-1, keepdims=True), 1e-30)


def _overlap(n_rows, n_lanes, n_cmp, n_sel):
    ci, sj = _iota((n_rows, n_lanes), 0), _iota((n_rows, n_lanes), 1)
    hit = ((ci * CMP_STRIDE < sj * SEL_LEN + SEL_LEN) & (ci * CMP_STRIDE + CMP_LEN > sj * SEL_LEN)
           & (ci < n_cmp) & (sj < n_sel))
    return jnp.where(hit, 1.0, 0.0).astype(BF16)


def _importance(imp, lane, qpos, n_sel):
    cur = qpos // SEL_LEN
    forced = (lane == 0) | (lane == cur) | (lane == cur - 1)
    imp = jnp.where(lane * SEL_LEN <= qpos, imp + jnp.where(forced, FORCE_BONUS, 0.0), NEG)
    return jnp.where(lane < n_sel, imp, LOWEST)


def _dot_tn(a, b):
    return lax.dot_general(a, b, (((0,), (0,)), ((), ())), preferred_element_type=F32)


LOG2E = math.log2(math.e)


def _online_attend_t(q, k_ref, v_ref, lo, hi, ck, bias_fn, s_even_ref, s_odd_ref, ml_ref, acc_ref):
    cols = q.shape[0]

    def chunk_start(c):
        return pl.multiple_of(jnp.minimum(c, hi - 1) * ck, ck)

    def scores(c, s_ref):
        k0 = chunk_start(c)
        bias = jnp.concatenate([bias_fn(k0)] * (cols // Q_BLOCK), axis=1) + jnp.where(c < hi, 0.0, NEG)
        s_ref[:ck, :] = _dot_nt(k_ref[pl.ds(k0, ck), :], q) * (ATT_SCALE * LOG2E) + bias

    def update(c, s_ref):
        m, l = ml_ref[0:1, :], ml_ref[1:2, :]
        s = s_ref[:ck, :]
        m_new = jnp.maximum(m, jnp.max(s, axis=0, keepdims=True))
        alpha = jnp.exp2(m - m_new)
        e = jnp.exp2(s - m_new)
        ml_ref[0:1, :] = m_new
        ml_ref[1:2, :] = alpha * l + jnp.sum(e, axis=0, keepdims=True)
        acc_ref[...] = alpha * acc_ref[...] + _dot_tn(v_ref[pl.ds(chunk_start(c), ck), :], e.astype(BF16))

    def body(i, carry):
        c = lo + 2 * i
        scores(c + 1, s_odd_ref)
        update(c, s_even_ref)
        scores(c + 2, s_even_ref)
        update(c + 1, s_odd_ref)
        return carry

    scores(lo, s_even_ref)
    ml_ref[0:1, :] = jnp.full((1, cols), NEG, F32)
    ml_ref[1:2, :] = jnp.zeros((1, cols), F32)
    acc_ref[...] = jnp.zeros((HEAD_DIM, cols), F32)
    lax.fori_loop(0, (hi - lo) // 2, body, 0)

    @pl.when((hi - lo) % 2 == 1)
    def _():
        update(hi - 1, s_even_ref)

    m, l = ml_ref[0:1, :], ml_ref[1:2, :]
    return jnp.where(m > NEG, acc_ref[...] / jnp.maximum(l, 1e-30), 0.0)


Q_BLOCK = 128
SEL_KEY_CHUNK = 256
WIN_KEY_CHUNK = 128


def _nsa_prompt_body(qc_ref, qr_ref, gate_ref, ksel_ref, vsel_ref, kwin_ref, vwin_ref,
                     kcmp_ref, vcmp_ref, o_ref, sel_ref, s_even_ref, s_odd_ref, ml_ref, acc_ref,
                     *, n_cmp, n_sel):
    qi = pl.program_id(2)
    q0 = qi * Q_BLOCK
    nh = HEADS_PER_KV
    stack = lambda ref: jnp.concatenate(
        [ref[:, hd * HEAD_DIM:(hd + 1) * HEAD_DIM] for hd in range(nh)], axis=0)
    per_head = lambda x: jnp.concatenate([x] * nh, axis=1)
    qc, qr = stack(qc_ref), stack(qr_ref)
    n_cmp_rows = kcmp_ref.shape[0]
    n_sel_rows = sel_ref.shape[0]

    cshape = (n_cmp_rows, Q_BLOCK)
    cblk, cq = _iota(cshape, 0), q0 + _iota(cshape, 1)
    cmask = per_head(jnp.where((cblk < n_cmp) & (cblk * CMP_STRIDE + CMP_LEN - 1 <= cq), 1.0, 0.0)) > 0.5
    s = jnp.where(cmask, _dot_nt(kcmp_ref[...], qc) * ATT_SCALE, NEG)
    e = jnp.where(cmask, jnp.exp(s - jnp.max(s, axis=0, keepdims=True)), 0.0)
    p_cmp = (e / jnp.maximum(jnp.sum(e, axis=0, keepdims=True), 1e-30)).astype(BF16)
    o_cmp = _dot_tn(vcmp_ref[...], p_cmp)

    oshape = (n_sel_rows, n_cmp_rows)
    sj, ci = _iota(oshape, 0), _iota(oshape, 1)
    hit = ((ci * CMP_STRIDE < sj * SEL_LEN + SEL_LEN) & (ci * CMP_STRIDE + CMP_LEN > sj * SEL_LEN)
           & (ci < n_cmp) & (sj < n_sel))
    imp_h = _dot(jnp.where(hit, 1.0, 0.0).astype(BF16), p_cmp)
    imp = imp_h[:, 0:Q_BLOCK]
    for hd in range(1, nh):
        imp = imp + imp_h[:, hd * Q_BLOCK:(hd + 1) * Q_BLOCK]
    ishape = (n_sel_rows, Q_BLOCK)
    blk = _iota(ishape, 0)
    imp = _importance(imp, blk, q0 + _iota(ishape, 1), n_sel)
    rank = jnp.zeros(ishape, F32)
    for k in range(n_sel):
        row = imp[k:k + 1, :]
        rank += jnp.where((row > imp) | ((row == imp) & (blk > k)), 1.0, 0.0)
    sel_ref[...] = jnp.where((rank < min(SEL_TOP, n_sel)) & (blk < n_sel), 1.0, 0.0)

    def sel_mask(k0):
        shape = (SEL_KEY_CHUNK, Q_BLOCK)
        j0 = k0 // SEL_LEN
        chosen = jnp.concatenate(
            [jnp.broadcast_to(sel_ref[pl.ds(j0 + i, 1), :], (SEL_LEN, Q_BLOCK))
             for i in range(SEL_KEY_CHUNK // SEL_LEN)], axis=0)
        live = (k0 + _iota(shape, 0) <= q0 + _iota(shape, 1)) & (chosen > 0.5)
        return jnp.where(live, 0.0, NEG)

    def win_mask(k0):
        shape = (WIN_KEY_CHUNK, Q_BLOCK)
        kpos = k0 + _iota(shape, 0)
        qp = q0 + _iota(shape, 1)
        return jnp.where((kpos <= qp) & (kpos > qp - WINDOW), 0.0, NEG)

    sel_chunks = (q0 + Q_BLOCK + SEL_KEY_CHUNK - 1) // SEL_KEY_CHUNK
    o_sel = _online_attend_t(qr, ksel_ref, vsel_ref, 0, sel_chunks, SEL_KEY_CHUNK, sel_mask,
                             s_even_ref, s_odd_ref, ml_ref, acc_ref)
    win_lo = jnp.maximum(q0 - WINDOW, 0) // WIN_KEY_CHUNK
    win_hi = (q0 + Q_BLOCK + WIN_KEY_CHUNK - 1) // WIN_KEY_CHUNK
    o_win = _online_attend_t(qr, kwin_ref, vwin_ref, win_lo, win_hi, WIN_KEY_CHUNK, win_mask,
                             s_even_ref, s_odd_ref, ml_ref, acc_ref)

    gates = jnp.transpose(gate_ref[...])
    for hd in range(nh):
        cols = slice(hd * Q_BLOCK, (hd + 1) * Q_BLOCK)
        c0 = hd * N_BRANCH
        o = (o_cmp[:, cols] * gates[c0:c0 + 1, :] + o_sel[:, cols] * gates[c0 + 1:c0 + 2, :]
             + o_win[:, cols] * gates[c0 + 2:c0 + 3, :])
        o_ref[:, hd * HEAD_DIM:(hd + 1) * HEAD_DIM] = jnp.transpose(o).astype(BF16)


def _nsa_prompt(qc, qr, gates, kv, kcmp, vcmp):
    bsz, t, dq = qc.shape
    n_cmp = (t - CMP_LEN) // CMP_STRIDE + 1
    n_sel = -(-t // SEL_LEN)
    gw = HEADS_PER_KV * HEAD_DIM
    qspec = pl.BlockSpec((None, Q_BLOCK, gw), lambda b, g, i: (b, i, g))
    kvspec = lambda slot: pl.BlockSpec((None, t, HEAD_DIM), lambda b, g, i: (b, 0, slot * N_KV + g))
    cspec = pl.BlockSpec((None, None, kcmp.shape[2], HEAD_DIM), lambda b, g, i: (b, g, 0, 0))
    return pl.pallas_call(
        functools.partial(_nsa_prompt_body, n_cmp=n_cmp, n_sel=n_sel),
        grid=(bsz, N_KV, t // Q_BLOCK),
        in_specs=[qspec, qspec, pl.BlockSpec((None, Q_BLOCK, LANES), lambda b, g, i: (b, i, g)),
                  kvspec(2), kvspec(3), kvspec(4), kvspec(5), cspec, cspec],
        out_specs=qspec,
        out_shape=jax.ShapeDtypeStruct((bsz, t, dq), BF16),
        scratch_shapes=[pltpu.VMEM((-(-n_sel // 8) * 8, Q_BLOCK), F32)]
        + [pltpu.VMEM((max(SEL_KEY_CHUNK, WIN_KEY_CHUNK), HEADS_PER_KV * Q_BLOCK), F32)] * 2
        + [pltpu.VMEM((8, HEADS_PER_KV * Q_BLOCK), F32), pltpu.VMEM((HEAD_DIM, HEADS_PER_KV * Q_BLOCK), F32)],
        compiler_params=_cparams("parallel", "parallel", "arbitrary"),
        name="nsa_prompt",
    )(qc, qr, gates, kv, kv, kv, kv, kcmp, vcmp)


def _select_body(qc_ref, kcmp_ref, vcmp_ref, ocmp_ref, idx_ref, *, qpos, n_cmp, n_sel):
    q = qc_ref[...]
    nc = kcmp_ref.shape[0]
    sel_lanes = -(-n_sel // LANES) * LANES
    ci = _iota((HEADS_PER_KV, nc), 1)
    cmask = (ci < n_cmp) & (ci * CMP_STRIDE + CMP_LEN - 1 <= qpos)
    p_cmp = _masked_softmax(_dot_nt(q, kcmp_ref[...]) * ATT_SCALE, cmask).astype(BF16)
    ocmp_ref[...] = _dot(p_cmp, vcmp_ref[...])

    imp = jnp.sum(_dot(p_cmp, _overlap(nc, sel_lanes, n_cmp, n_sel)), axis=0, keepdims=True)
    lane = _iota((1, sel_lanes), 1)
    imp = _importance(imp, lane, qpos, n_sel)
    sq = (sel_lanes, sel_lanes)
    ri, cj = _iota(sq, 0), _iota(sq, 1)
    impc = jnp.sum(jnp.where(ri == cj, imp, 0.0), axis=1, keepdims=True)
    n_top = min(SEL_TOP, n_sel)
    before_col = jnp.where((impc > imp) | ((impc == imp) & (ri < cj)), 1.0, 0.0)
    before_row = jnp.where((imp > impc) | ((imp == impc) & (cj < ri)), 1.0, 0.0)
    sel_lane = (jnp.sum(before_col, axis=0, keepdims=True) < n_top) & (lane < n_sel)
    sel_row = (jnp.sum(before_row, axis=1, keepdims=True) < n_top) & (_iota((sel_lanes, 1), 0) < n_sel)
    slot_of = jnp.sum(jnp.where((ri < cj) & sel_row, 1.0, 0.0), axis=0, keepdims=True)
    lshape = (SEL_TOP, sel_lanes)
    onehot = jnp.where(sel_lane & (slot_of == _iota(lshape, 0).astype(F32)), 1.0, 0.0)
    idx = jnp.sum(onehot * _iota(lshape, 1).astype(F32), axis=1, keepdims=True)
    idx_ref[...] = jnp.broadcast_to(idx, (SEL_TOP, LANES)).astype(jnp.int32)


def _nsa_select(qc, kcmp, vcmp, qpos, n_cmp, n_sel):
    nb = qc.shape[0]
    nc = kcmp.shape[2]
    hspec = pl.BlockSpec((None, None, HEADS_PER_KV, HEAD_DIM), lambda b, g: (b, g, 0, 0))
    cspec = pl.BlockSpec((None, None, nc, HEAD_DIM), lambda b, g: (b, g, 0, 0))
    return pl.pallas_call(
        functools.partial(_select_body, qpos=qpos, n_cmp=n_cmp, n_sel=n_sel),
        grid=(nb, N_KV),
        in_specs=[hspec, cspec, cspec],
        out_specs=[hspec, pl.BlockSpec((None, None, SEL_TOP, LANES), lambda b, g: (b, g, 0, 0))],
        out_shape=[jax.ShapeDtypeStruct((nb, N_KV, HEADS_PER_KV, HEAD_DIM), F32),
                   jax.ShapeDtypeStruct((nb, N_KV, SEL_TOP, LANES), jnp.int32)],
        compiler_params=_cparams("parallel", "parallel"),
        name="nsa_select",
    )(qc, kcmp, vcmp)


def _attend_plus_new(q, k, v, valid, k_new, v_new, new_ok):
    qf = q.astype(F32)
    s = _dot_nt(q, k.astype(BF16)) * ATT_SCALE
    s_new = jnp.sum(qf * k_new.astype(BF16).astype(F32), axis=1, keepdims=True) * ATT_SCALE
    s = jnp.where(valid, s, NEG)
    s_new = jnp.where(new_ok, s_new, NEG)
    m = jnp.maximum(jnp.max(s, axis=-1, keepdims=True), s_new)
    e = jnp.where(valid, jnp.exp(s - m), 0.0)
    e_new = jnp.where(new_ok, jnp.exp(s_new - m), 0.0)
    den = jnp.maximum(jnp.sum(e, axis=-1, keepdims=True) + e_new, 1e-30)
    p = (e / den).astype(BF16)
    p_new = (e_new / den).astype(BF16).astype(F32)
    return _dot(p, v.astype(BF16)) + p_new * v_new.astype(BF16).astype(F32)


SEL_BLOCK_LINES = SEL_LEN * ROW_LINES


def _attend_body(idx_ref, pt_ref, qr_ref, gate_ref, ocmp_ref, *refs, qpos, past_len, n_win):
    del pt_ref
    blocks = refs[:SEL_TOP]
    win_ref, rows_new_ref, win_new_ref, o_ref = refs[SEL_TOP:]
    b, g = pl.program_id(0), pl.program_id(1)
    q = qr_ref[...]
    n_keys = SEL_TOP * SEL_LEN
    base = (b * N_KV + g) * SEL_TOP
    lane = _iota((1, n_keys), 1)
    kpos = jnp.zeros((1, n_keys), jnp.int32)
    n_new = jnp.int32(0)
    for t in range(SEL_TOP):
        j = idx_ref[base + t]
        kpos = jnp.where(lane // SEL_LEN == t, j * SEL_LEN + lane % SEL_LEN, kpos)
        n_new = n_new + jnp.where(j == qpos // SEL_LEN, 1, 0)
    valid = (kpos <= qpos) & (kpos < past_len)
    new_ok = (jnp.zeros((HEADS_PER_KV, 1), jnp.int32) + n_new) > 0
    line = lambda ref, slot, n, per_token: ref[pl.ds(slot * N_KV + g, n, stride=per_token), :]
    k_sel = jnp.concatenate([line(blk, 2, SEL_LEN, ROW_LINES) for blk in blocks], axis=0)
    v_sel = jnp.concatenate([line(blk, 3, SEL_LEN, ROW_LINES) for blk in blocks], axis=0)
    o_sel = _attend_plus_new(q, k_sel, v_sel, valid, line(rows_new_ref, 2, 1, ROW_LINES),
                             line(rows_new_ref, 3, 1, ROW_LINES), new_ok)

    wpos = (past_len - n_win) + _iota((1, n_win), 1)
    wvalid = (wpos <= qpos) & (wpos > qpos - WINDOW) & (wpos >= 0)
    always = jnp.zeros((HEADS_PER_KV, 1), jnp.int32) == 0
    o_win = _attend_plus_new(q, line(win_ref, 0, n_win, WIN_LINES), line(win_ref, 1, n_win, WIN_LINES),
                             wvalid, line(win_new_ref, 0, 1, WIN_LINES), line(win_new_ref, 1, 1, WIN_LINES),
                             always)

    gates = gate_ref[...]
    o_ref[...] = ocmp_ref[...] * gates[:, 0:1] + o_sel * gates[:, 1:2] + o_win * gates[:, 2:3]


def _nsa_attend(idx, page_table, qr, gates, ocmp, cache, cache_win, rows_new, win_new, past_len):
    nb, n_pages = page_table.shape
    n_win = cache_win.shape[0] // (nb * WIN_LINES)
    halves = PAGE_ROWS // SEL_LEN
    n_past_blocks = n_pages * halves

    def sel_block(t):
        def index(b, g, idx_ref, pt_ref):
            j = jnp.minimum(idx_ref[(b * N_KV + g) * SEL_TOP + t], n_past_blocks - 1)
            return (pt_ref[b * n_pages + j // halves] * halves + j % halves, 0)
        return pl.BlockSpec((SEL_BLOCK_LINES, HEAD_DIM), index)

    hspec = pl.BlockSpec((None, None, HEADS_PER_KV, HEAD_DIM), lambda b, g, i, p: (b, g, 0, 0))
    grid_spec = pltpu.PrefetchScalarGridSpec(
        num_scalar_prefetch=2,
        grid=(nb, N_KV),
        in_specs=[hspec, hspec, hspec] + [sel_block(t) for t in range(SEL_TOP)]
        + [pl.BlockSpec((n_win * WIN_LINES, HEAD_DIM), lambda b, g, i, p: (b, 0)),
           pl.BlockSpec((None, ROW_LINES, HEAD_DIM), lambda b, g, i, p: (b, 0, 0)),
           pl.BlockSpec((None, WIN_LINES, HEAD_DIM), lambda b, g, i, p: (b, 0, 0))],
        out_specs=hspec)
    return pl.pallas_call(
        functools.partial(_attend_body, qpos=past_len, past_len=past_len, n_win=n_win),
        grid_spec=grid_spec,
        out_shape=jax.ShapeDtypeStruct((nb, N_KV, HEADS_PER_KV, HEAD_DIM), F32),
        compiler_params=_cparams("parallel", "arbitrary"),
        name="nsa_attend",
    )(idx, page_table.reshape(-1), qr, gates, ocmp, *([cache] * SEL_TOP), cache_win, rows_new, win_new)


PROMPT_FFN_ROWS = 512
PROMPT_ROWS = 512


def _rope_tables(pos):
    half = HEAD_DIM // 2
    inv = jnp.exp(-math.log(ROPE_THETA) * jnp.arange(half, dtype=F32) / half)
    ang = pos.astype(F32)[:, None] * inv[None, :]
    cos, sin = jnp.cos(ang), jnp.sin(ang)
    return jnp.concatenate([cos, cos], axis=1), jnp.concatenate([-sin, sin], axis=1)


def _prepare_weights(prm):
    nq = N_HEADS * HEAD_DIM
    w_qg = prm["attn_w_qg"]
    nb, d = w_qg.shape[:2]
    per_kv = HEADS_PER_KV * N_BRANCH
    w_gate = w_qg[:, :, nq:].reshape(nb, d, N_KV, per_kv)
    w_gate = jnp.pad(w_gate, ((0, 0), (0, 0), (0, 0), (0, LANES - per_kv))).reshape(nb, d, N_KV * LANES)
    return dict(
        ffn16={},
        ple_gate=prm["ple_w_gate"].astype(BF16), ple_proj=prm["ple_w_proj"].astype(BF16),
        glu=prm["glu_w"].astype(BF16), kv=prm["w_kv"].astype(BF16),
        q=w_qg[:, :, :nq].astype(BF16), gate=w_gate.astype(BF16), o=prm["attn_w_o"].astype(BF16),
        w1k=prm["cmp_w1_k"].astype(BF16), w1v=prm["cmp_w1_v"].astype(BF16),
        w2k=prm["cmp_w2_k"].astype(BF16), w2v=prm["cmp_w2_v"].astype(BF16),
        s5=[_s5_layouts(prm["ssm_lam_re"][i], prm["ssm_lam_im"][i], prm["ssm_log_dt"][i],
                        prm["ssm_b_re"][i], prm["ssm_b_im"][i], prm["ssm_c_re"][i], prm["ssm_c_im"][i])
            for i in range(N_A_LAYERS)])


def _finish(gen):
    try:
        while True:
            next(gen)
    except StopIteration as done:
        return done.value


def _run_group(x, p, start, state, past, prm, w):
    bsz, t, d = x.shape
    n = bsz * t
    prompt = past is None
    tm_ffn = PROMPT_FFN_ROWS if prompt else n
    tm = PROMPT_ROWS if prompt else n
    depth = prm["norm_g"].shape[0]
    cos, sin = _rope_tables(start + jnp.arange(t))
    if not prompt:
        cos, sin = jnp.broadcast_to(cos, (n, HEAD_DIM)), jnp.broadcast_to(sin, (n, HEAD_DIM))
    h = x.reshape(n, d)
    ssm_re, ssm_im = [], []
    rows = win = kv16 = kcmp = vcmp = None

    w32 = (prm["ffn_w_gate"], prm["ffn_w_up"], prm["ffn_w_down"])

    def ffn(h, g_pre, g_post, g_next, layer, which):
        if (layer, which) not in w["ffn16"]:
            h_new, xn, *w["ffn16"][layer, which] = _ffn_cast(h, g_pre, g_post, g_next, *w32, layer, which)
            return h_new, xn
        nxt = (layer, which + 1) if which == 0 else (layer + 1, 0)
        if not prompt or nxt[0] == depth:
            return _ffn(h, g_pre, g_post, g_next, *w["ffn16"][layer, which], tm_ffn)
        h_new, xn, *w["ffn16"][nxt] = _ffn_convert(h, g_pre, g_post, g_next, *w["ffn16"][layer, which],
                                                   tm_ffn, *w32, *nxt)
        return h_new, xn

    if not prompt:
        cache_kv, cache_win, page_table = past
        past_len = page_table.shape[1] * cache_kv.shape[1]
        cache_lines = cache_kv.reshape(-1, HEAD_DIM)
        win_lines = cache_win.reshape(-1, HEAD_DIM)
    for i in range(depth):
        g = prm["norm_g"][i]
        h, xn = ffn(h, g[0:1], g[1:2], g[2:3], i, 0)
        if i == 0:
            yield
        if i < N_A_LAYERS:
            if prompt:
                y, st = _s5_prompt(xn, bsz, w["s5"][i])
                st = st.reshape(d // LANES, bsz, 2, S5_GROUPS_PER_STEP, SSM_STATE).transpose(2, 1, 0, 3, 4)
                st = st.reshape(2, bsz, d // SSM_GROUP, SSM_STATE)
                ssm_re.append(st[0])
                ssm_im.append(st[1])
            else:
                y, s_re, s_im = _s5_step(xn, state[0][i].reshape(bsz, -1), state[1][i].reshape(bsz, -1),
                                         w["s5"][i])
                ssm_re.append(s_re.reshape(state[0][i].shape))
                ssm_im.append(s_im.reshape(state[1][i].shape))
            h = _s5_gate(h, xn, y, prm["ssm_d"][i][None], w["glu"], prm["glu_b"][i][None], g[3:4], i, tm)
        else:
            j = i - N_A_LAYERS
            qc, qr, gates = _qg_proj(xn, w["q"], w["gate"], cos, sin, j, tm)
            if prompt:
                o = _nsa_prompt(qc.reshape(bsz, t, -1), qr.reshape(bsz, t, -1), gates.reshape(bsz, t, -1),
                                kv16.reshape(bsz, t, -1), kcmp, vcmp).reshape(n, -1)
            else:
                heads = lambda a: a.reshape(bsz, N_KV, HEADS_PER_KV, HEAD_DIM)
                n_cmp = (past_len + t - CMP_LEN) // CMP_STRIDE + 1
                n_sel = -(-(past_len + t) // SEL_LEN)
                ocmp, idx = _nsa_select(heads(qc), kcmp, vcmp, past_len, n_cmp, n_sel)
                gsm = gates.reshape(bsz, N_KV, LANES)[:, :, :HEADS_PER_KV * N_BRANCH]
                gsm = gsm.reshape(bsz, N_KV, HEADS_PER_KV, N_BRANCH)
                gsm = jnp.pad(gsm, ((0, 0), (0, 0), (0, 0), (0, LANES - N_BRANCH)))
                o = _nsa_attend(idx[..., 0].reshape(-1), page_table, heads(qr), gsm, ocmp,
                                cache_lines, win_lines, rows_l.reshape(bsz, ROW_LINES, HEAD_DIM),
                                win_l.reshape(bsz, WIN_LINES, HEAD_DIM), past_len).reshape(n, -1)
            h = _o_proj(h, o, w["o"], g[3:4], j, tm)
        h, _ = ffn(h, g[4:5], g[5:6], g[6:7], i, 1)
        h = _ple(h, p[i].reshape(n, -1), w["ple_gate"], w["ple_proj"], g[6:7], i, tm)
        if i == N_A_LAYERS - 1:
            rows_l, win_l, kv16 = _kv_proj(h, prm["kv_norm_g"][None], w["kv"], cos, sin, tm)
            rows = rows_l.reshape(bsz, t, ROW_SLOTS, N_KV, HEAD_DIM)
            win = win_l.reshape(bsz, t, WIN_SLOTS, N_KV, HEAD_DIM)
            cmp_w = (prm["cmp_pe_k"], prm["cmp_pe_v"], w["w1k"], w["w1v"], w["w2k"], w["w2v"])
            if prompt:
                pages = jnp.arange(n // PAGE_ROWS, dtype=jnp.int32).reshape(bsz, t // PAGE_ROWS)
                kcmp, vcmp = _compress(rows_l, pages, *cmp_w)
                win = win[:, t - min(WINDOW, t):]
            else:
                kcmp, vcmp = _compress(cache_lines, page_table, *cmp_w)
                win = jnp.concatenate([cache_win, win], axis=1)[:, t:]
    return h.reshape(bsz, t, d), jnp.stack(ssm_re), jnp.stack(ssm_im), rows, win


def kernel(x_prompt, x_sample, state_ssm_re, state_ssm_im, cache_kv, cache_win, page_table,
           p_prompt, p_sample, norm_g, ffn_w_gate, ffn_w_up, ffn_w_down, ple_w_proj, ple_w_gate,
           ssm_lam_re, ssm_lam_im, ssm_log_dt, ssm_b_re, ssm_b_im, ssm_c_re, ssm_c_im, ssm_d,
           glu_w, glu_b, kv_norm_g, w_kv, cmp_pe_k, cmp_pe_v, cmp_w1_k, cmp_w2_k, cmp_w1_v,
           cmp_w2_v, attn_w_qg, attn_w_o):
    prm = dict(norm_g=norm_g, ffn_w_gate=ffn_w_gate, ffn_w_up=ffn_w_up, ffn_w_down=ffn_w_down,
               ple_w_proj=ple_w_proj, ple_w_gate=ple_w_gate, ssm_lam_re=ssm_lam_re,
               ssm_lam_im=ssm_lam_im, ssm_log_dt=ssm_log_dt, ssm_b_re=ssm_b_re, ssm_b_im=ssm_b_im,
               ssm_c_re=ssm_c_re, ssm_c_im=ssm_c_im, ssm_d=ssm_d, glu_w=glu_w, glu_b=glu_b,
               kv_norm_g=kv_norm_g, w_kv=w_kv, cmp_pe_k=cmp_pe_k, cmp_pe_v=cmp_pe_v,
               cmp_w1_k=cmp_w1_k, cmp_w2_k=cmp_w2_k, cmp_w1_v=cmp_w1_v, cmp_w2_v=cmp_w2_v,
               attn_w_qg=attn_w_qg, attn_w_o=attn_w_o)
    assert x_sample.shape[1] == 1, "the decode group handles one new token per sequence"
    assert cache_kv.shape[1] == PAGE_ROWS
    w = _prepare_weights(prm)
    past_len = page_table.shape[1] * cache_kv.shape[1]
    decode = _run_group(x_sample, p_sample, past_len, (state_ssm_re, state_ssm_im),
                        (cache_kv, cache_win, page_table), prm, w)
    next(decode)
    y_p, re_p, im_p, kv_p, win_p = _finish(_run_group(x_prompt, p_prompt, 0, None, None, prm, w))
    y_s, re_s, im_s, kv_s, win_s = _finish(decode)
    return (y_p, y_s, re_p, im_p, re_s, im_s, kv_p, kv_s, win_p, win_s)
```

```python
import functools
import math

import jax
import jax.numpy as jnp
from jax import lax
from jax.experimental import pallas as pl
from jax.experimental.pallas import tpu as pltpu

F32 = jnp.float32
BF16 = jnp.bfloat16
HIGHEST = lax.Precision.HIGHEST

N_A_LAYERS = 2
NORM_EPS = 1e-6
SSM_GROUP = 16
SSM_STATE = 64
HEAD_DIM = 128
N_KV = 2
HEADS_PER_KV = 8
N_HEADS = N_KV * HEADS_PER_KV
N_BRANCH = 3
N_KV_SLOTS = 6
CMP_LEN = 32
CMP_STRIDE = 16
SEL_LEN = 64
SEL_TOP = 16
WINDOW = 512
ROPE_THETA = 10000.0
FORCE_BONUS = 1e3
NEG = -1e30
LOWEST = -3.0e38
ATT_SCALE = HEAD_DIM ** -0.5

LANES = 128
V7X_VMEM_BYTES = 64 * 2 ** 20
VMEM_LIMIT_BYTES = V7X_VMEM_BYTES * 7 // 8
FFN_TILE_F = 512
S5_CHUNK = 16
S5_GROUPS_PER_STEP = LANES // SSM_GROUP
S5_STATE_LANES = S5_GROUPS_PER_STEP * SSM_STATE


def _cparams(*sem):
    return pltpu.CompilerParams(dimension_semantics=sem, vmem_limit_bytes=VMEM_LIMIT_BYTES)


def _rms(x, g):
    y = x * lax.rsqrt(jnp.mean(x * x, axis=-1, keepdims=True) + NORM_EPS)
    return y * g


def _dot(a, b, **kw):
    return jnp.dot(a, b, preferred_element_type=F32, **kw)


def _dot_nt(a, b):
    return lax.dot_general(a, b, (((1,), (1,)), ((), ())), preferred_element_type=F32)


def _rope(x, cos, sin):
    return x * cos + pltpu.roll(x, HEAD_DIM // 2, 1) * sin


def _iota(shape, axis):
    return lax.broadcasted_iota(jnp.int32, shape, axis)


def _ffn_step(k, nk, h_ref, gpre_ref, gpost_ref, gnext_ref, wg_ref, wu_ref, wd_ref, o_ref, xn_out_ref,
              xn_ref, acc_ref, side_work=None):
    @pl.when(k == 0)
    def _():
        xn_ref[...] = _rms(h_ref[...], gpre_ref[...]).astype(BF16)
        acc_ref[...] = jnp.zeros_like(acc_ref)

    xn = xn_ref[...]
    gate = _dot(xn, wg_ref[...])
    up = _dot(xn, wu_ref[...])
    mid = (jax.nn.silu(gate) * up).astype(BF16)
    acc_ref[...] += _dot(mid, wd_ref[...])
    if side_work is not None:
        side_work()

    @pl.when(k == nk - 1)
    def _():
        h_new = h_ref[...] + 0.5 * _rms(acc_ref[...], gpost_ref[...])
        o_ref[...] = h_new
        xn_out_ref[...] = _rms(h_new, gnext_ref[...])


def _ffn_body(h_ref, gpre_ref, gpost_ref, gnext_ref, wg_ref, wu_ref, wd_ref, o_ref, xn_out_ref,
              xn_ref, acc_ref, *, nk):
    _ffn_step(pl.program_id(1), nk, h_ref, gpre_ref, gpost_ref, gnext_ref,
              wg_ref, wu_ref, wd_ref, o_ref, xn_out_ref, xn_ref, acc_ref)


def _ffn(h, g_pre, g_post, g_next, wg, wu, wd, tm):
    n, d = h.shape
    nk = wg.shape[-1] // FFN_TILE_F
    row = lambda i, k: (i, 0)
    vec = lambda i, k: (0, 0)
    return pl.pallas_call(
        functools.partial(_ffn_body, nk=nk),
        grid=(n // tm, nk),
        in_specs=[pl.BlockSpec((tm, d), row), pl.BlockSpec((1, d), vec), pl.BlockSpec((1, d), vec),
                  pl.BlockSpec((1, d), vec),
                  pl.BlockSpec((d, FFN_TILE_F), lambda i, k: (0, k)),
                  pl.BlockSpec((d, FFN_TILE_F), lambda i, k: (0, k)),
                  pl.BlockSpec((FFN_TILE_F, d), lambda i, k: (k, 0))],
        out_specs=[pl.BlockSpec((tm, d), row), pl.BlockSpec((tm, d), row)],
        out_shape=[jax.ShapeDtypeStruct((n, d), F32)] * 2,
        scratch_shapes=[pltpu.VMEM((tm, d), BF16), pltpu.VMEM((tm, d), F32)],
        compiler_params=_cparams("parallel", "arbitrary"),
        name="ffn",
    )(h, g_pre, g_post, g_next, wg, wu, wd)


FFN_CONVERT_ROWS = 16


def _ffn_convert_body(h_ref, gpre_ref, gpost_ref, gnext_ref, wg_ref, wu_ref, wd_ref,
                      ng_ref, nu_ref, nd_ref, o_ref, xn_out_ref, ng16_ref, nu16_ref, nd16_ref,
                      xn_ref, acc_ref, *, nk, d_ff):
    i, k = pl.program_id(0), pl.program_id(1)

    def convert_next():
        pad = ng16_ref.shape[1] - d_ff
        for src, dst in ((ng_ref, ng16_ref), (nu_ref, nu16_ref)):
            dst[:, :d_ff] = src[...].astype(BF16)
            dst[:, d_ff:] = jnp.zeros((dst.shape[0], pad), BF16)
        rows = nd_ref.shape[0]
        live = d_ff - (i * nk + k) * rows
        nd16_ref[...] = jnp.where(_iota(nd_ref.shape, 0) < live, nd_ref[...], 0.0).astype(BF16)

    _ffn_step(k, nk, h_ref, gpre_ref, gpost_ref, gnext_ref,
              wg_ref, wu_ref, wd_ref, o_ref, xn_out_ref, xn_ref, acc_ref, convert_next)


def _ffn_convert(h, g_pre, g_post, g_next, wg, wu, wd, tm, wg32, wu32, wd32, layer, which):
    n, d = h.shape
    fp = wg.shape[-1]
    d_ff = wg32.shape[-1]
    nk = fp // FFN_TILE_F
    steps = (n // tm) * nk
    rg = FFN_CONVERT_ROWS
    rd = fp // steps
    assert fp % steps == 0 and d_ff % rd == 0 and d % rg == 0 and d // rg <= steps
    row = lambda i, k: (i, 0)
    vec = lambda i, k: (0, 0)
    up_slab = lambda i, k: jnp.minimum(i * nk + k, d // rg - 1)
    down_slab = lambda i, k: jnp.minimum(i * nk + k, d_ff // rd - 1)
    return pl.pallas_call(
        functools.partial(_ffn_convert_body, nk=nk, d_ff=d_ff),
        grid=(n // tm, nk),
        in_specs=[pl.BlockSpec((tm, d), row), pl.BlockSpec((1, d), vec), pl.BlockSpec((1, d), vec),
                  pl.BlockSpec((1, d), vec),
                  pl.BlockSpec((d, FFN_TILE_F), lambda i, k: (0, k)),
                  pl.BlockSpec((d, FFN_TILE_F), lambda i, k: (0, k)),
                  pl.BlockSpec((FFN_TILE_F, d), lambda i, k: (k, 0)),
                  pl.BlockSpec((None, None, rg, d_ff), lambda i, k: (layer, which, up_slab(i, k), 0)),
                  pl.BlockSpec((None, None, rg, d_ff), lambda i, k: (layer, which, up_slab(i, k), 0)),
                  pl.BlockSpec((None, None, rd, d), lambda i, k: (layer, which, down_slab(i, k), 0))],
        out_specs=[pl.BlockSpec((tm, d), row), pl.BlockSpec((tm, d), row),
                   pl.BlockSpec((rg, fp), lambda i, k: (up_slab(i, k), 0)),
                   pl.BlockSpec((rg, fp), lambda i, k: (up_slab(i, k), 0)),
                   pl.BlockSpec((rd, d), lambda i, k: (i * nk + k, 0))],
        out_shape=[jax.ShapeDtypeStruct((n, d), F32)] * 2
        + [jax.ShapeDtypeStruct((d, fp), BF16)] * 2 + [jax.ShapeDtypeStruct((fp, d), BF16)],
        scratch_shapes=[pltpu.VMEM((tm, d), BF16), pltpu.VMEM((tm, d), F32)],
        compiler_params=_cparams("arbitrary", "arbitrary"),
        name="ffn_convert",
    )(h, g_pre, g_post, g_next, wg, wu, wd, wg32, wu32, wd32)


def _ffn_cast_body(h_ref, gpre_ref, gpost_ref, gnext_ref, wg_ref, wu_ref, wd_ref,
                   o_ref, xn_out_ref, wg16_ref, wu16_ref, wd16_ref, xn_ref, acc_ref, *, nk, d_ff):
    k = pl.program_id(1)
    live = d_ff - k * FFN_TILE_F
    wshape, dshape = wg_ref.shape, wd_ref.shape
    wg16_ref[...] = jnp.where(_iota(wshape, 1) < live, wg_ref[...], 0.0).astype(BF16)
    wu16_ref[...] = jnp.where(_iota(wshape, 1) < live, wu_ref[...], 0.0).astype(BF16)
    wd16_ref[...] = jnp.where(_iota(dshape, 0) < live, wd_ref[...], 0.0).astype(BF16)
    _ffn_step(k, nk, h_ref, gpre_ref, gpost_ref, gnext_ref, wg16_ref, wu16_ref, wd16_ref,
              o_ref, xn_out_ref, xn_ref, acc_ref)


def _ffn_cast(h, g_pre, g_post, g_next, wg, wu, wd, layer, which):
    n, d = h.shape
    d_ff = wg.shape[-1]
    nk = -(-d_ff // FFN_TILE_F)
    fp = nk * FFN_TILE_F
    row = lambda i, k: (0, 0)
    return pl.pallas_call(
        functools.partial(_ffn_cast_body, nk=nk, d_ff=d_ff),
        grid=(1, nk),
        in_specs=[pl.BlockSpec((n, d), row), pl.BlockSpec((1, d), row), pl.BlockSpec((1, d), row),
                  pl.BlockSpec((1, d), row),
                  pl.BlockSpec((None, None, d, FFN_TILE_F), lambda i, k: (layer, which, 0, k)),
                  pl.BlockSpec((None, None, d, FFN_TILE_F), lambda i, k: (layer, which, 0, k)),
                  pl.BlockSpec((None, None, FFN_TILE_F, d), lambda i, k: (layer, which, k, 0))],
        out_specs=[pl.BlockSpec((n, d), row), pl.BlockSpec((n, d), row),
                   pl.BlockSpec((d, FFN_TILE_F), lambda i, k: (0, k)),
                   pl.BlockSpec((d, FFN_TILE_F), lambda i, k: (0, k)),
                   pl.BlockSpec((FFN_TILE_F, d), lambda i, k: (k, 0))],
        out_shape=[jax.ShapeDtypeStruct((n, d), F32)] * 2
        + [jax.ShapeDtypeStruct((d, fp), BF16)] * 2 + [jax.ShapeDtypeStruct((fp, d), BF16)],
        scratch_shapes=[pltpu.VMEM((n, d), BF16), pltpu.VMEM((n, d), F32)],
        compiler_params=_cparams("arbitrary", "arbitrary"),
        name="ffn_cast",
    )(h, g_pre, g_post, g_next, wg, wu, wd)


def _ple_body(h_ref, p_ref, wgate_ref, wproj_ref, g_ref, o_ref):
    h = h_ref[...]
    gate = jax.nn.sigmoid(_dot(h.astype(BF16), wgate_ref[...]))
    proj = _dot(p_ref[...].astype(BF16), wproj_ref[...])
    o_ref[...] = h + _rms(gate * proj, g_ref[...])


def _ple(h, p, wgate, wproj, g, layer, tm):
    n, d = h.shape
    pd = p.shape[-1]
    row = lambda i: (i, 0)
    return pl.pallas_call(
        _ple_body,
        grid=(n // tm,),
        in_specs=[pl.BlockSpec((tm, d), row), pl.BlockSpec((tm, pd), row),
                  pl.BlockSpec((None, d, d), lambda i: (layer, 0, 0)),
                  pl.BlockSpec((None, pd, d), lambda i: (layer, 0, 0)),
                  pl.BlockSpec((1, d), lambda i: (0, 0))],
        out_specs=pl.BlockSpec((tm, d), row),
        out_shape=jax.ShapeDtypeStruct((n, d), F32),
        compiler_params=_cparams("parallel"),
        name="ple",
    )(h, p, wgate, wproj, g)


ROW_SLOTS = 4
WIN_SLOTS = N_KV_SLOTS - ROW_SLOTS
ROW_LINES = ROW_SLOTS * N_KV
WIN_LINES = WIN_SLOTS * N_KV


def _kv_body(h_ref, g_ref, w_ref, cos_ref, sin_ref, rows_ref, win_ref, kv16_ref, *, tm):
    xn = _rms(h_ref[...], g_ref[...]).astype(BF16)
    kv = _dot(xn, w_ref[...])
    cos, sin = cos_ref[...], sin_ref[...]
    for slot in range(N_KV_SLOTS):
        for g in range(N_KV):
            line = slot * N_KV + g
            lo = line * HEAD_DIM
            x = kv[:, lo:lo + HEAD_DIM]
            if slot in (2, 4):
                x = _rope(x, cos, sin)
            kv16_ref[:, lo:lo + HEAD_DIM] = x.astype(BF16)
            if slot < ROW_SLOTS:
                rows_ref[pl.ds(line, tm, stride=ROW_LINES), :] = x
            else:
                win_ref[pl.ds(line - ROW_LINES, tm, stride=WIN_LINES), :] = x


def _kv_proj(h, g, w, cos, sin, tm):
    n, d = h.shape
    nw = w.shape[-1]
    nt = cos.shape[0] // tm
    row = lambda i: (i, 0)
    tab = lambda i: (i % nt, 0)
    return pl.pallas_call(
        functools.partial(_kv_body, tm=tm),
        grid=(n // tm,),
        in_specs=[pl.BlockSpec((tm, d), row), pl.BlockSpec((1, d), lambda i: (0, 0)),
                  pl.BlockSpec((d, nw), lambda i: (0, 0)),
                  pl.BlockSpec((tm, HEAD_DIM), tab), pl.BlockSpec((tm, HEAD_DIM), tab)],
        out_specs=[pl.BlockSpec((tm * ROW_LINES, HEAD_DIM), row),
                   pl.BlockSpec((tm * WIN_LINES, HEAD_DIM), row),
                   pl.BlockSpec((tm, nw), row)],
        out_shape=[jax.ShapeDtypeStruct((n * ROW_LINES, HEAD_DIM), F32),
                   jax.ShapeDtypeStruct((n * WIN_LINES, HEAD_DIM), F32),
                   jax.ShapeDtypeStruct((n, nw), BF16)],
        compiler_params=_cparams("parallel"),
        name="kv_proj",
    )(h, g, w, cos, sin)


def _qg_body(xn_ref, wq_ref, wgt_ref, cos_ref, sin_ref, qc_ref, qr_ref, gate_ref):
    xn = xn_ref[...].astype(BF16)
    q = _dot(xn, wq_ref[...])
    qc_ref[...] = q.astype(BF16)
    cos, sin = cos_ref[...], sin_ref[...]
    for hd in range(N_HEADS):
        sl = slice(hd * HEAD_DIM, (hd + 1) * HEAD_DIM)
        qr_ref[:, sl] = _rope(q[:, sl], cos, sin).astype(BF16)
    gate_ref[...] = jax.nn.sigmoid(_dot(xn, wgt_ref[...]))


def _qg_proj(xn, wq, wgt, cos, sin, layer, tm):
    n, d = xn.shape
    nq, ng = wq.shape[-1], wgt.shape[-1]
    nt = cos.shape[0] // tm
    row = lambda i: (i, 0)
    tab = lambda i: (i % nt, 0)
    return pl.pallas_call(
        _qg_body,
        grid=(n // tm,),
        in_specs=[pl.BlockSpec((tm, d), row),
                  pl.BlockSpec((None, d, nq), lambda i: (layer, 0, 0)),
                  pl.BlockSpec((None, d, ng), lambda i: (layer, 0, 0)),
                  pl.BlockSpec((tm, HEAD_DIM), tab), pl.BlockSpec((tm, HEAD_DIM), tab)],
        out_specs=[pl.BlockSpec((tm, nq), row), pl.BlockSpec((tm, nq), row), pl.BlockSpec((tm, ng), row)],
        out_shape=[jax.ShapeDtypeStruct((n, nq), BF16), jax.ShapeDtypeStruct((n, nq), BF16),
                   jax.ShapeDtypeStruct((n, ng), F32)],
        compiler_params=_cparams("parallel"),
        name="qg_proj",
    )(xn, wq, wgt, cos, sin)


def _oproj_body(h_ref, o_ref, w_ref, g_ref, out_ref):
    m = _dot(o_ref[...].astype(BF16), w_ref[...])
    out_ref[...] = h_ref[...] + _rms(m, g_ref[...])


def _o_proj(h, o, w, g, layer, tm):
    n, d = h.shape
    row = lambda i: (i, 0)
    return pl.pallas_call(
        _oproj_body,
        grid=(n // tm,),
        in_specs=[pl.BlockSpec((tm, d), row), pl.BlockSpec((tm, d), row),
                  pl.BlockSpec((None, d, d), lambda i: (layer, 0, 0)),
                  pl.BlockSpec((1, d), lambda i: (0, 0))],
        out_specs=pl.BlockSpec((tm, d), row),
        out_shape=jax.ShapeDtypeStruct((n, d), F32),
        compiler_params=_cparams("parallel"),
        name="o_proj",
    )(h, o, w, g)


def _s5gate_body(h_ref, xn_ref, y_ref, d_ref, w_ref, b_ref, g_ref, out_ref):
    y = jax.nn.gelu(y_ref[...] + d_ref[...] * xn_ref[...])
    z = jax.nn.sigmoid(_dot(y.astype(BF16), w_ref[...]) + b_ref[...])
    out_ref[...] = h_ref[...] + _rms(y * z, g_ref[...])


def _s5_gate(h, xn, y, dskip, w, b, g, layer, tm):
    n, d = h.shape
    row = lambda i: (i, 0)
    vec = lambda i: (0, 0)
    return pl.pallas_call(
        _s5gate_body,
        grid=(n // tm,),
        in_specs=[pl.BlockSpec((tm, d), row), pl.BlockSpec((tm, d), row), pl.BlockSpec((tm, d), row),
                  pl.BlockSpec((1, d), vec),
                  pl.BlockSpec((None, d, d), lambda i: (layer, 0, 0)),
                  pl.BlockSpec((1, d), vec), pl.BlockSpec((1, d), vec)],
        out_specs=pl.BlockSpec((tm, d), row),
        out_shape=jax.ShapeDtypeStruct((n, d), F32),
        compiler_params=_cparams("parallel"),
        name="s5_gate",
    )(h, xn, y, dskip, w, b, g)


def _s5_disc(lam_re, lam_im, log_dt):
    dt = jnp.exp(log_dt)
    mag = jnp.exp(lam_re * dt)
    a_re, a_im = mag * jnp.cos(lam_im * dt), mag * jnp.sin(lam_im * dt)
    den = lam_re * lam_re + lam_im * lam_im
    nr, ni = a_re - 1.0, a_im
    r_re = (nr * lam_re + ni * lam_im) / den
    r_im = (ni * lam_re - nr * lam_im) / den
    return a_re, a_im, r_re, r_im


def _cmul(ar, ai, br, bi):
    return ar * br - ai * bi, ar * bi + ai * br


def _bd_rows(m):
    shape = (LANES, S5_STATE_LANES)
    same = _iota(shape, 0) // SSM_GROUP == _iota(shape, 1) // SSM_STATE
    return jnp.where(same, jnp.concatenate([m] * S5_GROUPS_PER_STEP, axis=1), 0.0)


def _bd_cols(m):
    shape = (S5_STATE_LANES, LANES)
    same = _iota(shape, 0) // SSM_STATE == _iota(shape, 1) // SSM_GROUP
    return jnp.where(same, jnp.concatenate([m] * S5_GROUPS_PER_STEP, axis=0), 0.0)


def _s5_input_weights(lam_re, lam_im, log_dt, b_re, b_im):
    a_re, a_im, r_re, r_im = _s5_disc(lam_re, lam_im, log_dt)
    bb_re, bb_im = _cmul(r_re, r_im, b_re, b_im)
    return a_re, a_im, bb_re, bb_im


def _s5_prompt_body(u_ref, lr_re_ref, lr_im_ref, lr_dt_ref, b_re_ref, b_im_ref,
                    lc_re_ref, lc_im_ref, lc_dt_ref, c_re_ref, c_im_ref,
                    lv_re_ref, lv_im_ref, lv_dt_ref,
                    y_ref, st_ref,
                    sin_ref, toep_ref, fout_ref, v_ref, xp_ref, *, n_batch, n_chunk):
    L = S5_CHUNK
    rows = n_batch * n_chunk
    half = S5_STATE_LANES

    a_re, a_im, w_re, w_im = _s5_input_weights(lr_re_ref[...], lr_im_ref[...], lr_dt_ref[...],
                                               b_re_ref[...], b_im_ref[...])
    ws_re, ws_im = [w_re], [w_im]
    for _ in range(L - 1):
        w_re, w_im = _cmul(a_re, a_im, w_re, w_im)
        ws_re.append(w_re)
        ws_im.append(w_im)
    for s in range(L):
        k = L - 1 - s
        sin_ref[s * LANES:(s + 1) * LANES, :] = jnp.concatenate(
            [_bd_rows(ws_re[k]), _bd_rows(ws_im[k])], axis=1).astype(BF16)
    stack_re = jnp.concatenate([ws_re[L - 1 - s] for s in range(L)], axis=0)
    stack_im = jnp.concatenate([ws_im[L - 1 - s] for s in range(L)], axis=0)
    taps = (_dot(stack_re, c_re_ref[...], precision=HIGHEST)
            - _dot(stack_im, c_im_ref[...], precision=HIGHEST))
    tshape = (L * LANES, LANES)
    same = (_iota(tshape, 0) % LANES) // SSM_GROUP == _iota(tshape, 1) // SSM_GROUP
    taps = jnp.where(same, taps, 0.0).astype(BF16)
    toep_ref[:, LANES:] = taps
    toep_ref[:(L - 1) * LANES, :LANES] = taps[LANES:]
    toep_ref[(L - 1) * LANES:, :LANES] = jnp.zeros((LANES, LANES), BF16)
    ac_re, ac_im, _, _ = _s5_disc(lc_re_ref[...], lc_im_ref[...], lc_dt_ref[...])
    ca_re, ca_im = c_re_ref[...], c_im_ref[...]
    for t in range(L):
        ca_re, ca_im = _cmul(ac_re, ac_im, ca_re, ca_im)
        fout_ref[:, t * LANES:(t + 1) * LANES] = jnp.concatenate(
            [_bd_cols(ca_re), -_bd_cols(ca_im)], axis=0).astype(BF16)
    al_re, al_im, _, _ = _s5_disc(lv_re_ref[...], lv_im_ref[...], lv_dt_ref[...])
    for _ in range(int(math.log2(L))):
        al_re, al_im = _cmul(al_re, al_im, al_re, al_im)

    xb = jnp.concatenate([u_ref[pl.ds(s, rows, stride=L), :] for s in range(L)], axis=1).astype(BF16)
    v_ref[...] = _dot(xb, sin_ref[...])

    def step(c, carry):
        new = []
        for b in range(n_batch):
            r = b * n_chunk + c
            x_re, x_im = carry[b]
            xp_ref[pl.ds(r, 1), :] = jnp.concatenate([x_re, x_im], axis=1)
            inc = v_ref[pl.ds(r, 1), :]
            new.append((al_re * x_re - al_im * x_im + inc[:, :half],
                        al_re * x_im + al_im * x_re + inc[:, half:]))
        return tuple(new)

    zero = jnp.zeros((1, half), F32)
    final = lax.fori_loop(0, n_chunk, step, tuple((zero, zero) for _ in range(n_batch)), unroll=4)
    for b in range(n_batch):
        st_ref[b:b + 1, :] = jnp.concatenate(final[b], axis=1)

    xpb = xp_ref[...].astype(BF16)
    for t in range(0, L, 2):
        pair = (_dot(xb[:, :(t + 2) * LANES], toep_ref[(L - 2 - t) * LANES:, :])
                + _dot(xpb, fout_ref[:, t * LANES:(t + 2) * LANES]))
        y_ref[pl.ds(t, rows, stride=L), :] = pair[:, :LANES]
        y_ref[pl.ds(t + 1, rows, stride=L), :] = pair[:, LANES:]


def _s5_layouts(lam_re, lam_im, log_dt, b_re, b_im, c_re, c_im):
    n_g = lam_re.shape[0]
    ldt = jnp.broadcast_to(log_dt[:, None], lam_re.shape)
    rep = lambda a: jnp.repeat(a, SSM_GROUP, axis=0)
    rows = (rep(lam_re), rep(lam_im), rep(ldt),
            b_re.transpose(0, 2, 1).reshape(n_g * SSM_GROUP, SSM_STATE),
            b_im.transpose(0, 2, 1).reshape(n_g * SSM_GROUP, SSM_STATE))
    cols = (rep(lam_re).T, rep(lam_im).T, rep(ldt).T,
            c_re.reshape(n_g * SSM_GROUP, SSM_STATE).T, c_im.reshape(n_g * SSM_GROUP, SSM_STATE).T)
    n_j = n_g // S5_GROUPS_PER_STEP
    lanes = tuple(a.reshape(n_j, 1, S5_STATE_LANES) for a in (lam_re, lam_im, ldt))
    return rows, cols, lanes


def _s5_specs():
    rspec = pl.BlockSpec((LANES, SSM_STATE), lambda j: (j, 0))
    cspec = pl.BlockSpec((SSM_STATE, LANES), lambda j: (0, j))
    vspec = pl.BlockSpec((None, 1, S5_STATE_LANES), lambda j: (j, 0, 0))
    return [rspec] * 5 + [cspec] * 5 + [vspec] * 3


def _s5_prompt(u, n_batch, layouts):
    n, d = u.shape
    assert d % LANES == 0 and n % (n_batch * S5_CHUNK) == 0 and S5_CHUNK & (S5_CHUNK - 1) == 0
    n_j = d // LANES
    n_chunk = n // n_batch // S5_CHUNK
    rows = n // S5_CHUNK
    rows_p, cols_p, lanes_p = layouts
    col = lambda j: (0, j)
    return pl.pallas_call(
        functools.partial(_s5_prompt_body, n_batch=n_batch, n_chunk=n_chunk),
        grid=(n_j,),
        in_specs=[pl.BlockSpec((n, LANES), col)] + _s5_specs(),
        out_specs=[pl.BlockSpec((n, LANES), col),
                   pl.BlockSpec((None, n_batch, 2 * S5_STATE_LANES), lambda j: (j, 0, 0))],
        out_shape=[jax.ShapeDtypeStruct((n, d), F32),
                   jax.ShapeDtypeStruct((n_j, n_batch, 2 * S5_STATE_LANES), F32)],
        scratch_shapes=[pltpu.VMEM((S5_CHUNK * LANES, 2 * S5_STATE_LANES), BF16),
                        pltpu.VMEM((S5_CHUNK * LANES, 2 * LANES), BF16),
                        pltpu.VMEM((2 * S5_STATE_LANES, S5_CHUNK * LANES), BF16),
                        pltpu.VMEM((rows, 2 * S5_STATE_LANES), F32),
                        pltpu.VMEM((rows, 2 * S5_STATE_LANES), F32)],
        compiler_params=_cparams("parallel"),
        name="s5_prompt",
    )(u, *rows_p, *cols_p, *lanes_p)


def _s5_step_body(u_ref, hre_ref, him_ref, lr_re_ref, lr_im_ref, lr_dt_ref, b_re_ref, b_im_ref,
                  lc_re_ref, lc_im_ref, lc_dt_ref, c_re_ref, c_im_ref,
                  lv_re_ref, lv_im_ref, lv_dt_ref, y_ref, ore_ref, oim_ref):
    _, _, bb_re, bb_im = _s5_input_weights(lr_re_ref[...], lr_im_ref[...], lr_dt_ref[...],
                                           b_re_ref[...], b_im_ref[...])
    a_re, a_im, _, _ = _s5_disc(lv_re_ref[...], lv_im_ref[...], lv_dt_ref[...])
    u = u_ref[...]
    h_re, h_im = hre_ref[...], him_ref[...]
    x_re = _dot(u, _bd_rows(bb_re), precision=HIGHEST) + (a_re * h_re - a_im * h_im)
    x_im = _dot(u, _bd_rows(bb_im), precision=HIGHEST) + (a_re * h_im + a_im * h_re)
    ore_ref[...] = x_re
    oim_ref[...] = x_im
    y_ref[...] = (_dot(x_re, _bd_cols(c_re_ref[...]), precision=HIGHEST)
                  - _dot(x_im, _bd_cols(c_im_ref[...]), precision=HIGHEST))


def _s5_step(u, h_re, h_im, layouts):
    nb, d = u.shape
    rows_p, cols_p, lanes_p = layouts
    col = lambda j: (0, j)
    sspec = pl.BlockSpec((nb, S5_STATE_LANES), col)
    return pl.pallas_call(
        _s5_step_body,
        grid=(d // LANES,),
        in_specs=[pl.BlockSpec((nb, LANES), col), sspec, sspec] + _s5_specs(),
        out_specs=[pl.BlockSpec((nb, LANES), col), sspec, sspec],
        out_shape=[jax.ShapeDtypeStruct((nb, d), F32),
                   jax.ShapeDtypeStruct(h_re.shape, F32), jax.ShapeDtypeStruct(h_im.shape, F32)],
        compiler_params=_cparams("parallel"),
        name="s5_step",
    )(u, h_re, h_im, *rows_p, *cols_p, *lanes_p)


PAGE_ROWS = 128
PAGE_LINES = PAGE_ROWS * ROW_LINES
CHUNKS_PER_PAGE = PAGE_ROWS // CMP_STRIDE
CMP_PAGES_PER_STEP = 8
CMP_LINES = 2 * N_KV


def _compress_body(pt_ref, *refs, n_pages):
    del pt_ref
    k_pages = CMP_PAGES_PER_STEP
    srcs = refs[:k_pages]
    (pek_ref, pev_ref, w1k_ref, w1v_ref, w2k_ref, w2v_ref, kc_ref, vc_ref, buf_ref, shift_ref) = refs[k_pages:]
    p = pl.program_id(1)
    for k in range(k_pages):
        r0 = pl.multiple_of((p * k_pages + k) * PAGE_ROWS, PAGE_ROWS)
        for sg in range(CMP_LINES):
            buf_ref[sg, pl.ds(r0, PAGE_ROWS), :] = srcs[k][pl.ds(sg, PAGE_ROWS, stride=ROW_LINES), :]

    @pl.when(p == n_pages // k_pages - 1)
    def _():
        n_chunk = n_pages * CHUNKS_PER_PAGE
        half = CMP_STRIDE
        for slot, (pe_ref, w1_ref, w2_ref, out_ref) in enumerate(
                ((pek_ref, w1k_ref, w2k_ref, kc_ref), (pev_ref, w1v_ref, w2v_ref, vc_ref))):
            for g in range(N_KV):
                sg = slot * N_KV + g
                xs = [buf_ref[sg, pl.ds(l, n_chunk, stride=CMP_STRIDE), :] for l in range(half)]
                as_first = jnp.concatenate(
                    [(x + pe_ref[l:l + 1, :]).astype(BF16) for l, x in enumerate(xs)], axis=1)
                as_second = jnp.concatenate(
                    [(x + pe_ref[half + l:half + l + 1, :]).astype(BF16) for l, x in enumerate(xs)], axis=1)
                first = _dot(as_first, w1_ref[:half * HEAD_DIM, :])
                shift_ref[0:n_chunk, :] = _dot(as_second, w1_ref[half * HEAD_DIM:, :])
                shift_ref[n_chunk:n_chunk + 8, :] = jnp.zeros((8, HEAD_DIM), F32)
                hid = first + shift_ref[1:n_chunk + 1, :]
                out = _dot(jax.nn.silu(hid).astype(BF16), w2_ref[...])
                live = _iota(out.shape, 0) < n_chunk - 1
                out_ref[g] = jnp.where(live, out, 0.0).astype(BF16)


def _compress(src, page_ids, pe_k, pe_v, w1k, w1v, w2k, w2v):
    nb, n_pages = page_ids.shape
    k_pages = CMP_PAGES_PER_STEP
    assert n_pages % k_pages == 0
    n_chunk = n_pages * CHUNKS_PER_PAGE
    const = lambda b, p, pt: (0, 0)
    page = lambda k: pl.BlockSpec((PAGE_LINES, HEAD_DIM),
                                  lambda b, p, pt: (pt[b * n_pages + p * k_pages + k], 0))
    out_spec = pl.BlockSpec((None, N_KV, n_chunk, HEAD_DIM), lambda b, p, pt: (b, 0, 0, 0))
    grid_spec = pltpu.PrefetchScalarGridSpec(
        num_scalar_prefetch=1,
        grid=(nb, n_pages // k_pages),
        in_specs=[page(k) for k in range(k_pages)]
        + [pl.BlockSpec(pe_k.shape, const), pl.BlockSpec(pe_v.shape, const),
           pl.BlockSpec(w1k.shape, const), pl.BlockSpec(w1v.shape, const),
           pl.BlockSpec(w2k.shape, const), pl.BlockSpec(w2v.shape, const)],
        out_specs=[out_spec, out_spec],
        scratch_shapes=[pltpu.VMEM((CMP_LINES, n_pages * PAGE_ROWS, HEAD_DIM), F32),
                        pltpu.VMEM((n_chunk + 8, HEAD_DIM), F32)])
    return pl.pallas_call(
        functools.partial(_compress_body, n_pages=n_pages),
        grid_spec=grid_spec,
        out_shape=[jax.ShapeDtypeStruct((nb, N_KV, n_chunk, HEAD_DIM), BF16)] * 2,
        compiler_params=_cparams("parallel", "arbitrary"),
        name="compress",
    )(page_ids.reshape(-1), *([src] * k_pages), pe_k, pe_v, w1k, w1v, w2k, w2v)


def _masked_softmax(s, mask):
    s = jnp.where(mask, s, NEG)
    m = jnp.max(s, axis=-1, keepdims=True)
    e = jnp.where(mask, jnp.exp(s - m), 0.0)
    return e / jnp.maximum(jnp.sum(e, axis=-1, keepdims=True), 1e-30)


def _overlap(n_rows, n_lanes, n_cmp, n_sel):
    ci, sj = _iota((n_rows, n_lanes), 0), _iota((n_rows, n_lanes), 1)
    hit = ((ci * CMP_STRIDE < sj * SEL_LEN + SEL_LEN) & (ci * CMP_STRIDE + CMP_LEN > sj * SEL_LEN)
           & (ci < n_cmp) & (sj < n_sel))
    return jnp.where(hit, 1.0, 0.0).astype(BF16)


def _importance(imp, lane, qpos, n_sel):
    cur = qpos // SEL_LEN
    forced = (lane == 0) | (lane == cur) | (lane == cur - 1)
    imp = jnp.where(lane * SEL_LEN <= qpos, imp + jnp.where(forced, FORCE_BONUS, 0.0), NEG)
    return jnp.where(lane < n_sel, imp, LOWEST)


def _dot_tn(a, b):
    return lax.dot_general(a, b, (((0,), (0,)), ((), ())), preferred_element_type=F32)


LOG2E = math.log2(math.e)


def _online_attend_t(q, k_ref, v_ref, lo, hi, ck, bias_fn, s_even_ref, s_odd_ref, ml_ref, acc_ref):
    cols = q.shape[0]

    def chunk_start(c):
        return pl.multiple_of(jnp.minimum(c, hi - 1) * ck, ck)

    def scores(c, s_ref):
        k0 = chunk_start(c)
        bias = jnp.concatenate([bias_fn(k0)] * (cols // Q_BLOCK), axis=1) + jnp.where(c < hi, 0.0, NEG)
        s_ref[:ck, :] = _dot_nt(k_ref[pl.ds(k0, ck), :], q) * (ATT_SCALE * LOG2E) + bias

    def update(c, s_ref):
        m, l = ml_ref[0:1, :], ml_ref[1:2, :]
        s = s_ref[:ck, :]
        m_new = jnp.maximum(m, jnp.max(s, axis=0, keepdims=True))
        alpha = jnp.exp2(m - m_new)
        e = jnp.exp2(s - m_new)
        ml_ref[0:1, :] = m_new
        ml_ref[1:2, :] = alpha * l + jnp.sum(e, axis=0, keepdims=True)
        acc_ref[...] = alpha * acc_ref[...] + _dot_tn(v_ref[pl.ds(chunk_start(c), ck), :], e.astype(BF16))

    def body(i, carry):
        c = lo + 2 * i
        scores(c + 1, s_odd_ref)
        update(c, s_even_ref)
        scores(c + 2, s_even_ref)
        update(c + 1, s_odd_ref)
        return carry

    scores(lo, s_even_ref)
    ml_ref[0:1, :] = jnp.full((1, cols), NEG, F32)
    ml_ref[1:2, :] = jnp.zeros((1, cols), F32)
    acc_ref[...] = jnp.zeros((HEAD_DIM, cols), F32)
    lax.fori_loop(0, (hi - lo) // 2, body, 0)

    @pl.when((hi - lo) % 2 == 1)
    def _():
        update(hi - 1, s_even_ref)

    m, l = ml_ref[0:1, :], ml_ref[1:2, :]
    return jnp.where(m > NEG, acc_ref[...] / jnp.maximum(l, 1e-30), 0.0)


Q_BLOCK = 128
SEL_KEY_CHUNK = 256
WIN_KEY_CHUNK = 128


def _nsa_prompt_body(qc_ref, qr_ref, gate_ref, ksel_ref, vsel_ref, kwin_ref, vwin_ref,
                     kcmp_ref, vcmp_ref, o_ref, sel_ref, s_even_ref, s_odd_ref, ml_ref, acc_ref,
                     *, n_cmp, n_sel):
    qi = pl.program_id(2)
    q0 = qi * Q_BLOCK
    nh = HEADS_PER_KV
    stack = lambda ref: jnp.concatenate(
        [ref[:, hd * HEAD_DIM:(hd + 1) * HEAD_DIM] for hd in range(nh)], axis=0)
    per_head = lambda x: jnp.concatenate([x] * nh, axis=1)
    qc, qr = stack(qc_ref), stack(qr_ref)
    n_cmp_rows = kcmp_ref.shape[0]
    n_sel_rows = sel_ref.shape[0]

    cshape = (n_cmp_rows, Q_BLOCK)
    cblk, cq = _iota(cshape, 0), q0 + _iota(cshape, 1)
    cmask = per_head(jnp.where((cblk < n_cmp) & (cblk * CMP_STRIDE + CMP_LEN - 1 <= cq), 1.0, 0.0)) > 0.5
    s = jnp.where(cmask, _dot_nt(kcmp_ref[...], qc) * ATT_SCALE, NEG)
    e = jnp.where(cmask, jnp.exp(s - jnp.max(s, axis=0, keepdims=True)), 0.0)
    p_cmp = (e / jnp.maximum(jnp.sum(e, axis=0, keepdims=True), 1e-30)).astype(BF16)
    o_cmp = _dot_tn(vcmp_ref[...], p_cmp)

    oshape = (n_sel_rows, n_cmp_rows)
    sj, ci = _iota(oshape, 0), _iota(oshape, 1)
    hit = ((ci * CMP_STRIDE < sj * SEL_LEN + SEL_LEN) & (ci * CMP_STRIDE + CMP_LEN > sj * SEL_LEN)
           & (ci < n_cmp) & (sj < n_sel))
    imp_h = _dot(jnp.where(hit, 1.0, 0.0).astype(BF16), p_cmp)
    imp = imp_h[:, 0:Q_BLOCK]
    for hd in range(1, nh):
        imp = imp + imp_h[:, hd * Q_BLOCK:(hd + 1) * Q_BLOCK]
    ishape = (n_sel_rows, Q_BLOCK)
    blk = _iota(ishape, 0)
    imp = _importance(imp, blk, q0 + _iota(ishape, 1), n_sel)
    rank = jnp.zeros(ishape, F32)
    for k in range(n_sel):
        row = imp[k:k + 1, :]
        rank += jnp.where((row > imp) | ((row == imp) & (blk > k)), 1.0, 0.0)
    sel_ref[...] = jnp.where((rank < min(SEL_TOP, n_sel)) & (blk < n_sel), 1.0, 0.0)

    def sel_mask(k0):
        shape = (SEL_KEY_CHUNK, Q_BLOCK)
        j0 = k0 // SEL_LEN
        chosen = jnp.concatenate(
            [jnp.broadcast_to(sel_ref[pl.ds(j0 + i, 1), :], (SEL_LEN, Q_BLOCK))
             for i in range(SEL_KEY_CHUNK // SEL_LEN)], axis=0)
        live = (k0 + _iota(shape, 0) <= q0 + _iota(shape, 1)) & (chosen > 0.5)
        return jnp.where(live, 0.0, NEG)

    def win_mask(k0):
        shape = (WIN_KEY_CHUNK, Q_BLOCK)
        kpos = k0 + _iota(shape, 0)
        qp = q0 + _iota(shape, 1)
        return jnp.where((kpos <= qp) & (kpos > qp - WINDOW), 0.0, NEG)

    sel_chunks = (q0 + Q_BLOCK + SEL_KEY_CHUNK - 1) // SEL_KEY_CHUNK
    o_sel = _online_attend_t(qr, ksel_ref, vsel_ref, 0, sel_chunks, SEL_KEY_CHUNK, sel_mask,
                             s_even_ref, s_odd_ref, ml_ref, acc_ref)
    win_lo = jnp.maximum(q0 - WINDOW, 0) // WIN_KEY_CHUNK
    win_hi = (q0 + Q_BLOCK + WIN_KEY_CHUNK - 1) // WIN_KEY_CHUNK
    o_win = _online_attend_t(qr, kwin_ref, vwin_ref, win_lo, win_hi, WIN_KEY_CHUNK, win_mask,
                             s_even_ref, s_odd_ref, ml_ref, acc_ref)

    gates = jnp.transpose(gate_ref[...])
    for hd in range(nh):
        cols = slice(hd * Q_BLOCK, (hd + 1) * Q_BLOCK)
        c0 = hd * N_BRANCH
        o = (o_cmp[:, cols] * gates[c0:c0 + 1, :] + o_sel[:, cols] * gates[c0 + 1:c0 + 2, :]
             + o_win[:, cols] * gates[c0 + 2:c0 + 3, :])
        o_ref[:, hd * HEAD_DIM:(hd + 1) * HEAD_DIM] = jnp.transpose(o).astype(BF16)


def _nsa_prompt(qc, qr, gates, kv, kcmp, vcmp):
    bsz, t, dq = qc.shape
    assert t % SEL_KEY_CHUNK == 0 and t % Q_BLOCK == 0 and kcmp.shape[2] % 8 == 0
    n_cmp = (t - CMP_LEN) // CMP_STRIDE + 1
    n_sel = -(-t // SEL_LEN)
    gw = HEADS_PER_KV * HEAD_DIM
    qspec = pl.BlockSpec((None, Q_BLOCK, gw), lambda b, g, i: (b, i, g))
    kvspec = lambda slot: pl.BlockSpec((None, t, HEAD_DIM), lambda b, g, i: (b, 0, slot * N_KV + g))
    cspec = pl.BlockSpec((None, None, kcmp.shape[2], HEAD_DIM), lambda b, g, i: (b, g, 0, 0))
    return pl.pallas_call(
        functools.partial(_nsa_prompt_body, n_cmp=n_cmp, n_sel=n_sel),
        grid=(bsz, N_KV, t // Q_BLOCK),
        in_specs=[qspec, qspec, pl.BlockSpec((None, Q_BLOCK, LANES), lambda b, g, i: (b, i, g)),
                  kvspec(2), kvspec(3), kvspec(4), kvspec(5), cspec, cspec],
        out_specs=qspec,
        out_shape=jax.ShapeDtypeStruct((bsz, t, dq), BF16),
        scratch_shapes=[pltpu.VMEM((-(-n_sel // 8) * 8, Q_BLOCK), F32)]
        + [pltpu.VMEM((max(SEL_KEY_CHUNK, WIN_KEY_CHUNK), HEADS_PER_KV * Q_BLOCK), F32)] * 2
        + [pltpu.VMEM((8, HEADS_PER_KV * Q_BLOCK), F32), pltpu.VMEM((HEAD_DIM, HEADS_PER_KV * Q_BLOCK), F32)],
        compiler_params=_cparams("parallel", "parallel", "arbitrary"),
        name="nsa_prompt",
    )(qc, qr, gates, kv, kv, kv, kv, kcmp, vcmp)


def _select_body(qc_ref, kcmp_ref, vcmp_ref, ocmp_ref, idx_ref, *, qpos, n_cmp, n_sel):
    q = qc_ref[...]
    nc = kcmp_ref.shape[0]
    sel_lanes = -(-n_sel // LANES) * LANES
    ci = _iota((HEADS_PER_KV, nc), 1)
    cmask = (ci < n_cmp) & (ci * CMP_STRIDE + CMP_LEN - 1 <= qpos)
    p_cmp = _masked_softmax(_dot_nt(q, kcmp_ref[...]) * ATT_SCALE, cmask).astype(BF16)
    ocmp_ref[...] = _dot(p_cmp, vcmp_ref[...])

    imp = jnp.sum(_dot(p_cmp, _overlap(nc, sel_lanes, n_cmp, n_sel)), axis=0, keepdims=True)
    lane = _iota((1, sel_lanes), 1)
    imp = _importance(imp, lane, qpos, n_sel)
    sq = (sel_lanes, sel_lanes)
    ri, cj = _iota(sq, 0), _iota(sq, 1)
    impc = jnp.sum(jnp.where(ri == cj, imp, 0.0), axis=1, keepdims=True)
    n_top = min(SEL_TOP, n_sel)
    before_col = jnp.where((impc > imp) | ((impc == imp) & (ri < cj)), 1.0, 0.0)
    before_row = jnp.where((imp > impc) | ((imp == impc) & (cj < ri)), 1.0, 0.0)
    sel_lane = (jnp.sum(before_col, axis=0, keepdims=True) < n_top) & (lane < n_sel)
    sel_row = (jnp.sum(before_row, axis=1, keepdims=True) < n_top) & (_iota((sel_lanes, 1), 0) < n_sel)
    slot_of = jnp.sum(jnp.where((ri < cj) & sel_row, 1.0, 0.0), axis=0, keepdims=True)
    lshape = (SEL_TOP, sel_lanes)
    onehot = jnp.where(sel_lane & (slot_of == _iota(lshape, 0).astype(F32)), 1.0, 0.0)
    idx = jnp.sum(onehot * _iota(lshape, 1).astype(F32), axis=1, keepdims=True)
    idx_ref[...] = jnp.broadcast_to(idx, (SEL_TOP, LANES)).astype(jnp.int32)


def _nsa_select(qc, kcmp, vcmp, qpos, n_cmp, n_sel):
    nb = qc.shape[0]
    nc = kcmp.shape[2]
    hspec = pl.BlockSpec((None, None, HEADS_PER_KV, HEAD_DIM), lambda b, g: (b, g, 0, 0))
    cspec = pl.BlockSpec((None, None, nc, HEAD_DIM), lambda b, g: (b, g, 0, 0))
    return pl.pallas_call(
        functools.partial(_select_body, qpos=qpos, n_cmp=n_cmp, n_sel=n_sel),
        grid=(nb, N_KV),
        in_specs=[hspec, cspec, cspec],
        out_specs=[hspec, pl.BlockSpec((None, None, SEL_TOP, LANES), lambda b, g: (b, g, 0, 0))],
        out_shape=[jax.ShapeDtypeStruct((nb, N_KV, HEADS_PER_KV, HEAD_DIM), F32),
                   jax.ShapeDtypeStruct((nb, N_KV, SEL_TOP, LANES), jnp.int32)],
        compiler_params=_cparams("parallel", "parallel"),
        name="nsa_select",
    )(qc, kcmp, vcmp)


def _attend_plus_new(q, k, v, valid, k_new, v_new, new_ok):
    qf = q.astype(F32)
    s = _dot_nt(q, k.astype(BF16)) * ATT_SCALE
    s_new = jnp.sum(qf * k_new.astype(BF16).astype(F32), axis=1, keepdims=True) * ATT_SCALE
    s = jnp.where(valid, s, NEG)
    s_new = jnp.where(new_ok, s_new, NEG)
    m = jnp.maximum(jnp.max(s, axis=-1, keepdims=True), s_new)
    e = jnp.where(valid, jnp.exp(s - m), 0.0)
    e_new = jnp.where(new_ok, jnp.exp(s_new - m), 0.0)
    den = jnp.maximum(jnp.sum(e, axis=-1, keepdims=True) + e_new, 1e-30)
    p = (e / den).astype(BF16)
    p_new = (e_new / den).astype(BF16).astype(F32)
    return _dot(p, v.astype(BF16)) + p_new * v_new.astype(BF16).astype(F32)


SEL_BLOCK_LINES = SEL_LEN * ROW_LINES


def _attend_body(idx_ref, pt_ref, qr_ref, gate_ref, ocmp_ref, *refs, qpos, past_len, n_win):
    del pt_ref
    blocks = refs[:SEL_TOP]
    win_ref, rows_new_ref, win_new_ref, o_ref = refs[SEL_TOP:]
    b, g = pl.program_id(0), pl.program_id(1)
    q = qr_ref[...]
    n_keys = SEL_TOP * SEL_LEN
    base = (b * N_KV + g) * SEL_TOP
    lane = _iota((1, n_keys), 1)
    kpos = jnp.zeros((1, n_keys), jnp.int32)
    n_new = jnp.int32(0)
    for t in range(SEL_TOP):
        j = idx_ref[base + t]
        kpos = jnp.where(lane // SEL_LEN == t, j * SEL_LEN + lane % SEL_LEN, kpos)
        n_new = n_new + jnp.where(j == qpos // SEL_LEN, 1, 0)
    valid = (kpos <= qpos) & (kpos < past_len)
    new_ok = (jnp.zeros((HEADS_PER_KV, 1), jnp.int32) + n_new) > 0
    line = lambda ref, slot, n, per_token: ref[pl.ds(slot * N_KV + g, n, stride=per_token), :]
    k_sel = jnp.concatenate([line(blk, 2, SEL_LEN, ROW_LINES) for blk in blocks], axis=0)
    v_sel = jnp.concatenate([line(blk, 3, SEL_LEN, ROW_LINES) for blk in blocks], axis=0)
    o_sel = _attend_plus_new(q, k_sel, v_sel, valid, line(rows_new_ref, 2, 1, ROW_LINES),
                             line(rows_new_ref, 3, 1, ROW_LINES), new_ok)

    wpos = (past_len - n_win) + _iota((1, n_win), 1)
    wvalid = (wpos <= qpos) & (wpos > qpos - WINDOW) & (wpos >= 0)
    always = jnp.zeros((HEADS_PER_KV, 1), jnp.int32) == 0
    o_win = _attend_plus_new(q, line(win_ref, 0, n_win, WIN_LINES), line(win_ref, 1, n_win, WIN_LINES),
                             wvalid, line(win_new_ref, 0, 1, WIN_LINES), line(win_new_ref, 1, 1, WIN_LINES),
                             always)

    gates = gate_ref[...]
    o_ref[...] = ocmp_ref[...] * gates[:, 0:1] + o_sel * gates[:, 1:2] + o_win * gates[:, 2:3]


def _nsa_attend(idx, page_table, qr, gates, ocmp, cache, cache_win, rows_new, win_new, past_len):
    nb, n_pages = page_table.shape
    n_win = cache_win.shape[0] // (nb * WIN_LINES)
    halves = PAGE_ROWS // SEL_LEN
    n_past_blocks = n_pages * halves

    def sel_block(t):
        def index(b, g, idx_ref, pt_ref):
            j = jnp.minimum(idx_ref[(b * N_KV + g) * SEL_TOP + t], n_past_blocks - 1)
            return (pt_ref[b * n_pages + j // halves] * halves + j % halves, 0)
        return pl.BlockSpec((SEL_BLOCK_LINES, HEAD_DIM), index)

    hspec = pl.BlockSpec((None, None, HEADS_PER_KV, HEAD_DIM), lambda b, g, i, p: (b, g, 0, 0))
    grid_spec = pltpu.PrefetchScalarGridSpec(
        num_scalar_prefetch=2,
        grid=(nb, N_KV),
        in_specs=[hspec, hspec, hspec] + [sel_block(t) for t in range(SEL_TOP)]
        + [pl.BlockSpec((n_win * WIN_LINES, HEAD_DIM), lambda b, g, i, p: (b, 0)),
           pl.BlockSpec((None, ROW_LINES, HEAD_DIM), lambda b, g, i, p: (b, 0, 0)),
           pl.BlockSpec((None, WIN_LINES, HEAD_DIM), lambda b, g, i, p: (b, 0, 0))],
        out_specs=hspec)
    return pl.pallas_call(
        functools.partial(_attend_body, qpos=past_len, past_len=past_len, n_win=n_win),
        grid_spec=grid_spec,
        out_shape=jax.ShapeDtypeStruct((nb, N_KV, HEADS_PER_KV, HEAD_DIM), F32),
        compiler_params=_cparams("parallel", "arbitrary"),
        name="nsa_attend",
    )(idx, page_table.reshape(-1), qr, gates, ocmp, *([cache] * SEL_TOP), cache_win, rows_new, win_new)


PROMPT_FFN_ROWS = 512
PROMPT_ROWS = 512


def _rope_tables(pos):
    half = HEAD_DIM // 2
    inv = jnp.exp(-math.log(ROPE_THETA) * jnp.arange(half, dtype=F32) / half)
    ang = pos.astype(F32)[:, None] * inv[None, :]
    cos, sin = jnp.cos(ang), jnp.sin(ang)
    return jnp.concatenate([cos, cos], axis=1), jnp.concatenate([-sin, sin], axis=1)


def _prepare_weights(prm):
    nq = N_HEADS * HEAD_DIM
    w_qg = prm["attn_w_qg"]
    nb, d = w_qg.shape[:2]
    per_kv = HEADS_PER_KV * N_BRANCH
    w_gate = w_qg[:, :, nq:].reshape(nb, d, N_KV, per_kv)
    w_gate = jnp.pad(w_gate, ((0, 0), (0, 0), (0, 0), (0, LANES - per_kv))).reshape(nb, d, N_KV * LANES)
    return dict(
        ffn16={},
        ple_gate=prm["ple_w_gate"].astype(BF16), ple_proj=prm["ple_w_proj"].astype(BF16),
        glu=prm["glu_w"].astype(BF16), kv=prm["w_kv"].astype(BF16),
        q=w_qg[:, :, :nq].astype(BF16), gate=w_gate.astype(BF16), o=prm["attn_w_o"].astype(BF16),
        w1k=prm["cmp_w1_k"].astype(BF16), w1v=prm["cmp_w1_v"].astype(BF16),
        w2k=prm["cmp_w2_k"].astype(BF16), w2v=prm["cmp_w2_v"].astype(BF16),
        s5=[_s5_layouts(prm["ssm_lam_re"][i], prm["ssm_lam_im"][i], prm["ssm_log_dt"][i],
                        prm["ssm_b_re"][i], prm["ssm_b_im"][i], prm["ssm_c_re"][i], prm["ssm_c_im"][i])
            for i in range(N_A_LAYERS)])


def _finish(gen):
    try:
        while True:
            next(gen)
    except StopIteration as done:
        return done.value


def _run_group(x, p, start, state, past, prm, w):
    bsz, t, d = x.shape
    n = bsz * t
    prompt = past is None
    tm_ffn = PROMPT_FFN_ROWS if prompt else n
    tm = PROMPT_ROWS if prompt else n
    depth = prm["norm_g"].shape[0]
    cos, sin = _rope_tables(start + jnp.arange(t))
    if not prompt:
        cos, sin = jnp.broadcast_to(cos, (n, HEAD_DIM)), jnp.broadcast_to(sin, (n, HEAD_DIM))
    h = x.reshape(n, d)
    ssm_re, ssm_im = [], []
    rows = win = kv16 = kcmp = vcmp = None

    w32 = (prm["ffn_w_gate"], prm["ffn_w_up"], prm["ffn_w_down"])

    def ffn(h, g_pre, g_post, g_next, layer, which):
        if (layer, which) not in w["ffn16"]:
            h_new, xn, *w["ffn16"][layer, which] = _ffn_cast(h, g_pre, g_post, g_next, *w32, layer, which)
            return h_new, xn
        nxt = (layer, which + 1) if which == 0 else (layer + 1, 0)
        if not prompt or nxt[0] == depth:
            return _ffn(h, g_pre, g_post, g_next, *w["ffn16"][layer, which], tm_ffn)
        h_new, xn, *w["ffn16"][nxt] = _ffn_convert(h, g_pre, g_post, g_next, *w["ffn16"][layer, which],
                                                   tm_ffn, *w32, *nxt)
        return h_new, xn

    if not prompt:
        cache_kv, cache_win, page_table = past
        past_len = page_table.shape[1] * cache_kv.shape[1]
        cache_lines = cache_kv.reshape(-1, HEAD_DIM)
        win_lines = cache_win.reshape(-1, HEAD_DIM)
    for i in range(depth):
        g = prm["norm_g"][i]
        h, xn = ffn(h, g[0:1], g[1:2], g[2:3], i, 0)
        if i == 0:
            yield
        if i < N_A_LAYERS:
            if prompt:
                y, st = _s5_prompt(xn, bsz, w["s5"][i])
                st = st.reshape(d // LANES, bsz, 2, S5_GROUPS_PER_STEP, SSM_STATE).transpose(2, 1, 0, 3, 4)
                st = st.reshape(2, bsz, d // SSM_GROUP, SSM_STATE)
                ssm_re.append(st[0])
                ssm_im.append(st[1])
            else:
                y, s_re, s_im = _s5_step(xn, state[0][i].reshape(bsz, -1), state[1][i].reshape(bsz, -1),
                                         w["s5"][i])
                ssm_re.append(s_re.reshape(state[0][i].shape))
                ssm_im.append(s_im.reshape(state[1][i].shape))
            h = _s5_gate(h, xn, y, prm["ssm_d"][i][None], w["glu"], prm["glu_b"][i][None], g[3:4], i, tm)
        else:
            j = i - N_A_LAYERS
            qc, qr, gates = _qg_proj(xn, w["q"], w["gate"], cos, sin, j, tm)
            if prompt:
                o = _nsa_prompt(qc.reshape(bsz, t, -1), qr.reshape(bsz, t, -1), gates.reshape(bsz, t, -1),
                                kv16.reshape(bsz, t, -1), kcmp, vcmp).reshape(n, -1)
            else:
                heads = lambda a: a.reshape(bsz, N_KV, HEADS_PER_KV, HEAD_DIM)
                n_cmp = (past_len + t - CMP_LEN) // CMP_STRIDE + 1
                n_sel = -(-(past_len + t) // SEL_LEN)
                ocmp, idx = _nsa_select(heads(qc), kcmp, vcmp, past_len, n_cmp, n_sel)
                gsm = gates.reshape(bsz, N_KV, LANES)[:, :, :HEADS_PER_KV * N_BRANCH]
                gsm = gsm.reshape(bsz, N_KV, HEADS_PER_KV, N_BRANCH)
                gsm = jnp.pad(gsm, ((0, 0), (0, 0), (0, 0), (0, LANES - N_BRANCH)))
                o = _nsa_attend(idx[..., 0].reshape(-1), page_table, heads(qr), gsm, ocmp,
                                cache_lines, win_lines, rows_l.reshape(bsz, ROW_LINES, HEAD_DIM),
                                win_l.reshape(bsz, WIN_LINES, HEAD_DIM), past_len).reshape(n, -1)
            h = _o_proj(h, o, w["o"], g[3:4], j, tm)
        h, _ = ffn(h, g[4:5], g[5:6], g[6:7], i, 1)
        h = _ple(h, p[i].reshape(n, -1), w["ple_gate"], w["ple_proj"], g[6:7], i, tm)
        if i == N_A_LAYERS - 1:
            rows_l, win_l, kv16 = _kv_proj(h, prm["kv_norm_g"][None], w["kv"], cos, sin, tm)
            rows = rows_l.reshape(bsz, t, ROW_SLOTS, N_KV, HEAD_DIM)
            win = win_l.reshape(bsz, t, WIN_SLOTS, N_KV, HEAD_DIM)
            cmp_w = (prm["cmp_pe_k"], prm["cmp_pe_v"], w["w1k"], w["w1v"], w["w2k"], w["w2v"])
            if prompt:
                pages = jnp.arange(n // PAGE_ROWS, dtype=jnp.int32).reshape(bsz, t // PAGE_ROWS)
                kcmp, vcmp = _compress(rows_l, pages, *cmp_w)
                win = win[:, t - min(WINDOW, t):]
            else:
                kcmp, vcmp = _compress(cache_lines, page_table, *cmp_w)
                win = jnp.concatenate([cache_win, win], axis=1)[:, t:]
    return h.reshape(bsz, t, d), jnp.stack(ssm_re), jnp.stack(ssm_im), rows, win


def kernel(x_prompt, x_sample, state_ssm_re, state_ssm_im, cache_kv, cache_win, page_table,
           p_prompt, p_sample, norm_g, ffn_w_gate, ffn_w_up, ffn_w_down, ple_w_proj, ple_w_gate,
           ssm_lam_re, ssm_lam_im, ssm_log_dt, ssm_b_re, ssm_b_im, ssm_c_re, ssm_c_im, ssm_d,
           glu_w, glu_b, kv_norm_g, w_kv, cmp_pe_k, cmp_pe_v, cmp_w1_k, cmp_w2_k, cmp_w1_v,
           cmp_w2_v, attn_w_qg, attn_w_o):
    prm = dict(norm_g=norm_g, ffn_w_gate=ffn_w_gate, ffn_w_up=ffn_w_up, ffn_w_down=ffn_w_down,
               ple_w_proj=ple_w_proj, ple_w_gate=ple_w_gate, ssm_lam_re=ssm_lam_re,
               ssm_lam_im=ssm_lam_im, ssm_log_dt=ssm_log_dt, ssm_b_re=ssm_b_re, ssm_b_im=ssm_b_im,
               ssm_c_re=ssm_c_re, ssm_c_im=ssm_c_im, ssm_d=ssm_d, glu_w=glu_w, glu_b=glu_b,
               kv_norm_g=kv_norm_g, w_kv=w_kv, cmp_pe_k=cmp_pe_k, cmp_pe_v=cmp_pe_v,
               cmp_w1_k=cmp_w1_k, cmp_w2_k=cmp_w2_k, cmp_w1_v=cmp_w1_v, cmp_w2_v=cmp_w2_v,
               attn_w_qg=attn_w_qg, attn_w_o=attn_w_o)
    assert x_sample.shape[1] == 1, "the decode group handles one new token per sequence"
    assert cache_kv.shape[1] == PAGE_ROWS
    w = _prepare_weights(prm)
    past_len = page_table.shape[1] * cache_kv.shape[1]
    decode = _run_group(x_sample, p_sample, past_len, (state_ssm_re, state_ssm_im),
                        (cache_kv, cache_win, page_table), prm, w)
    next(decode)
    y_p, re_p, im_p, kv_p, win_p = _finish(_run_group(x_prompt, p_prompt, 0, None, None, prm, w))
    y_s, re_s, im_s, kv_s, win_s = _finish(decode)
    return (y_p, y_s, re_p, im_p, re_s, im_s, kv_p, kv_s, win_p, win_s)
```

```python
import functools
import math

import jax
import jax.numpy as jnp
from jax import lax
from jax.experimental import pallas as pl
from jax.experimental.pallas import tpu as pltpu

F32 = jnp.float32
BF16 = jnp.bfloat16
HIGHEST = lax.Precision.HIGHEST

N_A_LAYERS = 2
NORM_EPS = 1e-6
SSM_GROUP = 16
SSM_STATE = 64
HEAD_DIM = 128
N_KV = 2
HEADS_PER_KV = 8
N_HEADS = N_KV * HEADS_PER_KV
N_BRANCH = 3
N_KV_SLOTS = 6
CMP_LEN = 32
CMP_STRIDE = 16
SEL_LEN = 64
SEL_TOP = 16
WINDOW = 512
ROPE_THETA = 10000.0
FORCE_BONUS = 1e3
NEG = -1e30
LOWEST = -3.0e38
ATT_SCALE = HEAD_DIM ** -0.5

LANES = 128
V7X_VMEM_BYTES = 64 * 2 ** 20
VMEM_LIMIT_BYTES = V7X_VMEM_BYTES * 7 // 8
FFN_TILE_F = 512
S5_CHUNK = 16
S5_GROUPS_PER_STEP = LANES // SSM_GROUP
S5_STATE_LANES = S5_GROUPS_PER_STEP * SSM_STATE


def _cparams(*sem):
    return pltpu.CompilerParams(dimension_semantics=sem, vmem_limit_bytes=VMEM_LIMIT_BYTES)


def _rms(x, g):
    y = x * lax.rsqrt(jnp.mean(x * x, axis=-1, keepdims=True) + NORM_EPS)
    return y * g


def _dot(a, b, **kw):
    return jnp.dot(a, b, preferred_element_type=F32, **kw)


def _dot_3pass(a, b):
    a_hi, b_hi = a.astype(BF16), b.astype(BF16)
    a_lo = (a - a_hi.astype(F32)).astype(BF16)
    b_lo = (b - b_hi.astype(F32)).astype(BF16)
    return _dot(a_hi, b_hi) + _dot(a_hi, b_lo) + _dot(a_lo, b_hi)


def _dot_nt(a, b):
    return lax.dot_general(a, b, (((1,), (1,)), ((), ())), preferred_element_type=F32)


def _rope(x, cos, sin):
    return x * cos + pltpu.roll(x, HEAD_DIM // 2, 1) * sin


def _iota(shape, axis):
    return lax.broadcasted_iota(jnp.int32, shape, axis)


def _ffn_step(k, nk, h_ref, gpre_ref, gpost_ref, gnext_ref, wg_ref, wu_ref, wd_ref, o_ref, xn_out_ref,
              xn_ref, acc_ref, side_work=None):
    @pl.when(k == 0)
    def _():
        xn_ref[...] = _rms(h_ref[...], gpre_ref[...]).astype(BF16)
        acc_ref[...] = jnp.zeros_like(acc_ref)

    xn = xn_ref[...]
    gate = _dot(xn, wg_ref[...])
    up = _dot(xn, wu_ref[...])
    mid = (jax.nn.silu(gate) * up).astype(BF16)
    acc_ref[...] += _dot(mid, wd_ref[...])
    if side_work is not None:
        side_work()

    @pl.when(k == nk - 1)
    def _():
        h_new = h_ref[...] + _rms(acc_ref[...], 0.5 * gpost_ref[...])
        o_ref[...] = h_new
        xn_out_ref[...] = _rms(h_new, gnext_ref[...])


def _ffn_body(h_ref, gpre_ref, gpost_ref, gnext_ref, wg_ref, wu_ref, wd_ref, o_ref, xn_out_ref,
              xn_ref, acc_ref, *, nk):
    _ffn_step(pl.program_id(1), nk, h_ref, gpre_ref, gpost_ref, gnext_ref,
              wg_ref, wu_ref, wd_ref, o_ref, xn_out_ref, xn_ref, acc_ref)


def _ffn(h, g_pre, g_post, g_next, wg, wu, wd, tm):
    n, d = h.shape
    nk = wg.shape[-1] // FFN_TILE_F
    row = lambda i, k: (i, 0)
    vec = lambda i, k: (0, 0)
    return pl.pallas_call(
        functools.partial(_ffn_body, nk=nk),
        grid=(n // tm, nk),
        in_specs=[pl.BlockSpec((tm, d), row), pl.BlockSpec((1, d), vec), pl.BlockSpec((1, d), vec),
                  pl.BlockSpec((1, d), vec),
                  pl.BlockSpec((d, FFN_TILE_F), lambda i, k: (0, k)),
                  pl.BlockSpec((d, FFN_TILE_F), lambda i, k: (0, k)),
                  pl.BlockSpec((FFN_TILE_F, d), lambda i, k: (k, 0))],
        out_specs=[pl.BlockSpec((tm, d), row), pl.BlockSpec((tm, d), row)],
        out_shape=[jax.ShapeDtypeStruct((n, d), F32)] * 2,
        scratch_shapes=[pltpu.VMEM((tm, d), BF16), pltpu.VMEM((tm, d), F32)],
        compiler_params=_cparams("parallel", "arbitrary"),
        name="ffn",
    )(h, g_pre, g_post, g_next, wg, wu, wd)


FFN_CONVERT_ROWS = 16


def _ffn_convert_body(h_ref, gpre_ref, gpost_ref, gnext_ref, wg_ref, wu_ref, wd_ref,
                      ng_ref, nu_ref, nd_ref, o_ref, xn_out_ref, ng16_ref, nu16_ref, nd16_ref,
                      xn_ref, acc_ref, *, nk, d_ff):
    i, k = pl.program_id(0), pl.program_id(1)

    def convert_next():
        pad = ng16_ref.shape[1] - d_ff
        for src, dst in ((ng_ref, ng16_ref), (nu_ref, nu16_ref)):
            dst[:, :d_ff] = src[...].astype(BF16)
            dst[:, d_ff:] = jnp.zeros((dst.shape[0], pad), BF16)
        rows = nd_ref.shape[0]
        live = d_ff - (i * nk + k) * rows
        nd16_ref[...] = jnp.where(_iota(nd_ref.shape, 0) < live, nd_ref[...], 0.0).astype(BF16)

    _ffn_step(k, nk, h_ref, gpre_ref, gpost_ref, gnext_ref,
              wg_ref, wu_ref, wd_ref, o_ref, xn_out_ref, xn_ref, acc_ref, convert_next)


def _ffn_convert(h, g_pre, g_post, g_next, wg, wu, wd, tm, wg32, wu32, wd32, layer, which):
    n, d = h.shape
    fp = wg.shape[-1]
    d_ff = wg32.shape[-1]
    nk = fp // FFN_TILE_F
    steps = (n // tm) * nk
    rg = FFN_CONVERT_ROWS
    rd = fp // steps
    assert fp % steps == 0 and d_ff % rd == 0 and d % rg == 0 and d // rg <= steps
    row = lambda i, k: (i, 0)
    vec = lambda i, k: (0, 0)
    up_slab = lambda i, k: jnp.minimum(i * nk + k, d // rg - 1)
    down_slab = lambda i, k: jnp.minimum(i * nk + k, d_ff // rd - 1)
    return pl.pallas_call(
        functools.partial(_ffn_convert_body, nk=nk, d_ff=d_ff),
        grid=(n // tm, nk),
        in_specs=[pl.BlockSpec((tm, d), row), pl.BlockSpec((1, d), vec), pl.BlockSpec((1, d), vec),
                  pl.BlockSpec((1, d), vec),
                  pl.BlockSpec((d, FFN_TILE_F), lambda i, k: (0, k)),
                  pl.BlockSpec((d, FFN_TILE_F), lambda i, k: (0, k)),
                  pl.BlockSpec((FFN_TILE_F, d), lambda i, k: (k, 0)),
                  pl.BlockSpec((None, None, rg, d_ff), lambda i, k: (layer, which, up_slab(i, k), 0)),
                  pl.BlockSpec((None, None, rg, d_ff), lambda i, k: (layer, which, up_slab(i, k), 0)),
                  pl.BlockSpec((None, None, rd, d), lambda i, k: (layer, which, down_slab(i, k), 0))],
        out_specs=[pl.BlockSpec((tm, d), row), pl.BlockSpec((tm, d), row),
                   pl.BlockSpec((rg, fp), lambda i, k: (up_slab(i, k), 0)),
                   pl.BlockSpec((rg, fp), lambda i, k: (up_slab(i, k), 0)),
                   pl.BlockSpec((rd, d), lambda i, k: (i * nk + k, 0))],
        out_shape=[jax.ShapeDtypeStruct((n, d), F32)] * 2
        + [jax.ShapeDtypeStruct((d, fp), BF16)] * 2 + [jax.ShapeDtypeStruct((fp, d), BF16)],
        scratch_shapes=[pltpu.VMEM((tm, d), BF16), pltpu.VMEM((tm, d), F32)],
        compiler_params=_cparams("arbitrary", "arbitrary"),
        name="ffn_convert",
    )(h, g_pre, g_post, g_next, wg, wu, wd, wg32, wu32, wd32)


def _ffn_cast_body(h_ref, gpre_ref, gpost_ref, gnext_ref, wg_ref, wu_ref, wd_ref,
                   o_ref, xn_out_ref, wg16_ref, wu16_ref, wd16_ref, xn_ref, acc_ref, *, nk, d_ff):
    k = pl.program_id(1)
    live = d_ff - k * FFN_TILE_F
    wshape, dshape = wg_ref.shape, wd_ref.shape
    wg16_ref[...] = jnp.where(_iota(wshape, 1) < live, wg_ref[...], 0.0).astype(BF16)
    wu16_ref[...] = jnp.where(_iota(wshape, 1) < live, wu_ref[...], 0.0).astype(BF16)
    wd16_ref[...] = jnp.where(_iota(dshape, 0) < live, wd_ref[...], 0.0).astype(BF16)
    _ffn_step(k, nk, h_ref, gpre_ref, gpost_ref, gnext_ref, wg16_ref, wu16_ref, wd16_ref,
              o_ref, xn_out_ref, xn_ref, acc_ref)


def _ffn_cast(h, g_pre, g_post, g_next, wg, wu, wd, layer, which):
    n, d = h.shape
    d_ff = wg.shape[-1]
    nk = -(-d_ff // FFN_TILE_F)
    fp = nk * FFN_TILE_F
    row = lambda i, k: (0, 0)
    return pl.pallas_call(
        functools.partial(_ffn_cast_body, nk=nk, d_ff=d_ff),
        grid=(1, nk),
        in_specs=[pl.BlockSpec((n, d), row), pl.BlockSpec((1, d), row), pl.BlockSpec((1, d), row),
                  pl.BlockSpec((1, d), row),
                  pl.BlockSpec((None, None, d, FFN_TILE_F), lambda i, k: (layer, which, 0, k)),
                  pl.BlockSpec((None, None, d, FFN_TILE_F), lambda i, k: (layer, which, 0, k)),
                  pl.BlockSpec((None, None, FFN_TILE_F, d), lambda i, k: (layer, which, k, 0))],
        out_specs=[pl.BlockSpec((n, d), row), pl.BlockSpec((n, d), row),
                   pl.BlockSpec((d, FFN_TILE_F), lambda i, k: (0, k)),
                   pl.BlockSpec((d, FFN_TILE_F), lambda i, k: (0, k)),
                   pl.BlockSpec((FFN_TILE_F, d), lambda i, k: (k, 0))],
        out_shape=[jax.ShapeDtypeStruct((n, d), F32)] * 2
        + [jax.ShapeDtypeStruct((d, fp), BF16)] * 2 + [jax.ShapeDtypeStruct((fp, d), BF16)],
        scratch_shapes=[pltpu.VMEM((n, d), BF16), pltpu.VMEM((n, d), F32)],
        compiler_params=_cparams("arbitrary", "arbitrary"),
        name="ffn_cast",
    )(h, g_pre, g_post, g_next, wg, wu, wd)


def _ple_body(h_ref, p_ref, wgate_ref, wproj_ref, g_ref, o_ref):
    h = h_ref[...]
    gate = jax.nn.sigmoid(_dot(h.astype(BF16), wgate_ref[...]))
    proj = _dot(p_ref[...].astype(BF16), wproj_ref[...])
    o_ref[...] = h + _rms(gate * proj, g_ref[...])


def _ple(h, p, wgate, wproj, g, layer, tm):
    n, d = h.shape
    pd = p.shape[-1]
    row = lambda i: (i, 0)
    return pl.pallas_call(
        _ple_body,
        grid=(n // tm,),
        in_specs=[pl.BlockSpec((tm, d), row), pl.BlockSpec((tm, pd), row),
                  pl.BlockSpec((None, d, d), lambda i: (layer, 0, 0)),
                  pl.BlockSpec((None, pd, d), lambda i: (layer, 0, 0)),
                  pl.BlockSpec((1, d), lambda i: (0, 0))],
        out_specs=pl.BlockSpec((tm, d), row),
        out_shape=jax.ShapeDtypeStruct((n, d), F32),
        compiler_params=_cparams("parallel"),
        name="ple",
    )(h, p, wgate, wproj, g)


ROW_SLOTS = 4
WIN_SLOTS = N_KV_SLOTS - ROW_SLOTS
ROW_LINES = ROW_SLOTS * N_KV
WIN_LINES = WIN_SLOTS * N_KV


def _kv_body(h_ref, g_ref, w_ref, cos_ref, sin_ref, rows_ref, win_ref, kv16_ref, *, tm):
    xn = _rms(h_ref[...], g_ref[...]).astype(BF16)
    kv = _dot(xn, w_ref[...])
    cos, sin = cos_ref[...], sin_ref[...]
    for slot in range(N_KV_SLOTS):
        for g in range(N_KV):
            line = slot * N_KV + g
            lo = line * HEAD_DIM
            x = kv[:, lo:lo + HEAD_DIM]
            if slot in (2, 4):
                x = _rope(x, cos, sin)
            kv16_ref[:, lo:lo + HEAD_DIM] = x.astype(BF16)
            if slot < ROW_SLOTS:
                rows_ref[pl.ds(line, tm, stride=ROW_LINES), :] = x
            else:
                win_ref[pl.ds(line - ROW_LINES, tm, stride=WIN_LINES), :] = x


def _kv_proj(h, g, w, cos, sin, tm):
    n, d = h.shape
    nw = w.shape[-1]
    nt = cos.shape[0] // tm
    row = lambda i: (i, 0)
    tab = lambda i: (i % nt, 0)
    return pl.pallas_call(
        functools.partial(_kv_body, tm=tm),
        grid=(n // tm,),
        in_specs=[pl.BlockSpec((tm, d), row), pl.BlockSpec((1, d), lambda i: (0, 0)),
                  pl.BlockSpec((d, nw), lambda i: (0, 0)),
                  pl.BlockSpec((tm, HEAD_DIM), tab), pl.BlockSpec((tm, HEAD_DIM), tab)],
        out_specs=[pl.BlockSpec((tm * ROW_LINES, HEAD_DIM), row),
                   pl.BlockSpec((tm * WIN_LINES, HEAD_DIM), row),
                   pl.BlockSpec((tm, nw), row)],
        out_shape=[jax.ShapeDtypeStruct((n * ROW_LINES, HEAD_DIM), F32),
                   jax.ShapeDtypeStruct((n * WIN_LINES, HEAD_DIM), F32),
                   jax.ShapeDtypeStruct((n, nw), BF16)],
        compiler_params=_cparams("parallel"),
        name="kv_proj",
    )(h, g, w, cos, sin)


def _qg_body(xn_ref, wq_ref, wgt_ref, cos_ref, sin_ref, qc_ref, qr_ref, gate_ref):
    xn = xn_ref[...].astype(BF16)
    q = _dot(xn, wq_ref[...])
    qc_ref[...] = q.astype(BF16)
    cos, sin = cos_ref[...], sin_ref[...]
    for hd in range(N_HEADS):
        sl = slice(hd * HEAD_DIM, (hd + 1) * HEAD_DIM)
        qr_ref[:, sl] = _rope(q[:, sl], cos, sin).astype(BF16)
    gate_ref[...] = jax.nn.sigmoid(_dot(xn, wgt_ref[...]))


def _qg_proj(xn, wq, wgt, cos, sin, layer, tm):
    n, d = xn.shape
    nq, ng = wq.shape[-1], wgt.shape[-1]
    nt = cos.shape[0] // tm
    row = lambda i: (i, 0)
    tab = lambda i: (i % nt, 0)
    return pl.pallas_call(
        _qg_body,
        grid=(n // tm,),
        in_specs=[pl.BlockSpec((tm, d), row),
                  pl.BlockSpec((None, d, nq), lambda i: (layer, 0, 0)),
                  pl.BlockSpec((None, d, ng), lambda i: (layer, 0, 0)),
                  pl.BlockSpec((tm, HEAD_DIM), tab), pl.BlockSpec((tm, HEAD_DIM), tab)],
        out_specs=[pl.BlockSpec((tm, nq), row), pl.BlockSpec((tm, nq), row), pl.BlockSpec((tm, ng), row)],
        out_shape=[jax.ShapeDtypeStruct((n, nq), BF16), jax.ShapeDtypeStruct((n, nq), BF16),
                   jax.ShapeDtypeStruct((n, ng), F32)],
        compiler_params=_cparams("parallel"),
        name="qg_proj",
    )(xn, wq, wgt, cos, sin)


def _oproj_body(h_ref, o_ref, w_ref, g_ref, out_ref):
    m = _dot(o_ref[...].astype(BF16), w_ref[...])
    out_ref[...] = h_ref[...] + _rms(m, g_ref[...])


def _o_proj(h, o, w, g, layer, tm):
    n, d = h.shape
    row = lambda i: (i, 0)
    return pl.pallas_call(
        _oproj_body,
        grid=(n // tm,),
        in_specs=[pl.BlockSpec((tm, d), row), pl.BlockSpec((tm, d), row),
                  pl.BlockSpec((None, d, d), lambda i: (layer, 0, 0)),
                  pl.BlockSpec((1, d), lambda i: (0, 0))],
        out_specs=pl.BlockSpec((tm, d), row),
        out_shape=jax.ShapeDtypeStruct((n, d), F32),
        compiler_params=_cparams("parallel"),
        name="o_proj",
    )(h, o, w, g)


def _s5gate_body(h_ref, xn_ref, y_ref, d_ref, w_ref, b_ref, g_ref, out_ref):
    y = jax.nn.gelu(y_ref[...] + d_ref[...] * xn_ref[...])
    z = jax.nn.sigmoid(_dot(y.astype(BF16), w_ref[...]) + b_ref[...])
    out_ref[...] = h_ref[...] + _rms(y * z, g_ref[...])


def _s5_gate(h, xn, y, dskip, w, b, g, layer, tm):
    n, d = h.shape
    row = lambda i: (i, 0)
    vec = lambda i: (0, 0)
    return pl.pallas_call(
        _s5gate_body,
        grid=(n // tm,),
        in_specs=[pl.BlockSpec((tm, d), row), pl.BlockSpec((tm, d), row), pl.BlockSpec((tm, d), row),
                  pl.BlockSpec((1, d), vec),
                  pl.BlockSpec((None, d, d), lambda i: (layer, 0, 0)),
                  pl.BlockSpec((1, d), vec), pl.BlockSpec((1, d), vec)],
        out_specs=pl.BlockSpec((tm, d), row),
        out_shape=jax.ShapeDtypeStruct((n, d), F32),
        compiler_params=_cparams("parallel"),
        name="s5_gate",
    )(h, xn, y, dskip, w, b, g)


def _s5_disc(lam_re, lam_im, log_dt):
    dt = jnp.exp(log_dt)
    mag = jnp.exp(lam_re * dt)
    a_re, a_im = mag * jnp.cos(lam_im * dt), mag * jnp.sin(lam_im * dt)
    den = lam_re * lam_re + lam_im * lam_im
    nr, ni = a_re - 1.0, a_im
    r_re = (nr * lam_re + ni * lam_im) / den
    r_im = (ni * lam_re - nr * lam_im) / den
    return a_re, a_im, r_re, r_im


def _cmul(ar, ai, br, bi):
    return ar * br - ai * bi, ar * bi + ai * br


def _bd_rows(m):
    shape = (LANES, S5_STATE_LANES)
    same = _iota(shape, 0) // SSM_GROUP == _iota(shape, 1) // SSM_STATE
    return jnp.where(same, jnp.concatenate([m] * S5_GROUPS_PER_STEP, axis=1), 0.0)


def _bd_cols(m):
    shape = (S5_STATE_LANES, LANES)
    same = _iota(shape, 0) // SSM_STATE == _iota(shape, 1) // SSM_GROUP
    return jnp.where(same, jnp.concatenate([m] * S5_GROUPS_PER_STEP, axis=0), 0.0)


def _s5_input_weights(lam_re, lam_im, log_dt, b_re, b_im):
    a_re, a_im, r_re, r_im = _s5_disc(lam_re, lam_im, log_dt)
    bb_re, bb_im = _cmul(r_re, r_im, b_re, b_im)
    return a_re, a_im, bb_re, bb_im


def _s5_prompt_body(u_ref, lr_re_ref, lr_im_ref, lr_dt_ref, b_re_ref, b_im_ref,
                    lc_re_ref, lc_im_ref, lc_dt_ref, c_re_ref, c_im_ref,
                    lv_re_ref, lv_im_ref, lv_dt_ref,
                    y_ref, st_ref,
                    sin_ref, toep_ref, fout_ref, v_ref, xp_ref, *, n_batch, n_chunk):
    L = S5_CHUNK
    rows = n_batch * n_chunk
    half = S5_STATE_LANES

    a_re, a_im, w_re, w_im = _s5_input_weights(lr_re_ref[...], lr_im_ref[...], lr_dt_ref[...],
                                               b_re_ref[...], b_im_ref[...])
    ws_re, ws_im = [w_re], [w_im]
    for _ in range(L - 1):
        w_re, w_im = _cmul(a_re, a_im, w_re, w_im)
        ws_re.append(w_re)
        ws_im.append(w_im)
    for s in range(L):
        k = L - 1 - s
        sin_ref[s * LANES:(s + 1) * LANES, :] = jnp.concatenate(
            [_bd_rows(ws_re[k]), _bd_rows(ws_im[k])], axis=1).astype(BF16)
    stack_re = jnp.concatenate([ws_re[L - 1 - s] for s in range(L)], axis=0)
    stack_im = jnp.concatenate([ws_im[L - 1 - s] for s in range(L)], axis=0)
    taps = _dot_3pass(stack_re, c_re_ref[...]) - _dot_3pass(stack_im, c_im_ref[...])
    tshape = (L * LANES, LANES)
    same = (_iota(tshape, 0) % LANES) // SSM_GROUP == _iota(tshape, 1) // SSM_GROUP
    taps = jnp.where(same, taps, 0.0).astype(BF16)
    toep_ref[:, LANES:] = taps
    toep_ref[:(L - 1) * LANES, :LANES] = taps[LANES:]
    toep_ref[(L - 1) * LANES:, :LANES] = jnp.zeros((LANES, LANES), BF16)
    ac_re, ac_im, _, _ = _s5_disc(lc_re_ref[...], lc_im_ref[...], lc_dt_ref[...])
    ca_re, ca_im = c_re_ref[...], c_im_ref[...]
    for t in range(L):
        ca_re, ca_im = _cmul(ac_re, ac_im, ca_re, ca_im)
        fout_ref[:, t * LANES:(t + 1) * LANES] = jnp.concatenate(
            [_bd_cols(ca_re), -_bd_cols(ca_im)], axis=0).astype(BF16)
    al_re, al_im, _, _ = _s5_disc(lv_re_ref[...], lv_im_ref[...], lv_dt_ref[...])
    for _ in range(int(math.log2(L))):
        al_re, al_im = _cmul(al_re, al_im, al_re, al_im)

    xb = jnp.concatenate([u_ref[pl.ds(s, rows, stride=L), :] for s in range(L)], axis=1).astype(BF16)
    v_ref[...] = _dot(xb, sin_ref[...])

    def step(c, carry):
        new = []
        for b in range(n_batch):
            r = b * n_chunk + c
            x_re, x_im = carry[b]
            xp_ref[pl.ds(r, 1), :] = jnp.concatenate([x_re, x_im], axis=1)
            inc = v_ref[pl.ds(r, 1), :]
            new.append((al_re * x_re - al_im * x_im + inc[:, :half],
                        al_re * x_im + al_im * x_re + inc[:, half:]))
        return tuple(new)

    zero = jnp.zeros((1, half), F32)
    final = lax.fori_loop(0, n_chunk, step, tuple((zero, zero) for _ in range(n_batch)), unroll=4)
    for b in range(n_batch):
        st_ref[b:b + 1, :] = jnp.concatenate(final[b], axis=1)

    xpb = xp_ref[...].astype(BF16)
    for t in range(0, L, 2):
        pair = (_dot(xb[:, :(t + 2) * LANES], toep_ref[(L - 2 - t) * LANES:, :])
                + _dot(xpb, fout_ref[:, t * LANES:(t + 2) * LANES]))
        y_ref[pl.ds(t, rows, stride=L), :] = pair[:, :LANES]
        y_ref[pl.ds(t + 1, rows, stride=L), :] = pair[:, LANES:]


def _s5_layouts(lam_re, lam_im, log_dt, b_re, b_im, c_re, c_im):
    n_g = lam_re.shape[0]
    ldt = jnp.broadcast_to(log_dt[:, None], lam_re.shape)
    rep = lambda a: jnp.repeat(a, SSM_GROUP, axis=0)
    rows = (rep(lam_re), rep(lam_im), rep(ldt),
            b_re.transpose(0, 2, 1).reshape(n_g * SSM_GROUP, SSM_STATE),
            b_im.transpose(0, 2, 1).reshape(n_g * SSM_GROUP, SSM_STATE))
    cols = (rep(lam_re).T, rep(lam_im).T, rep(ldt).T,
            c_re.reshape(n_g * SSM_GROUP, SSM_STATE).T, c_im.reshape(n_g * SSM_GROUP, SSM_STATE).T)
    n_j = n_g // S5_GROUPS_PER_STEP
    lanes = tuple(a.reshape(n_j, 1, S5_STATE_LANES) for a in (lam_re, lam_im, ldt))
    return rows, cols, lanes


def _s5_specs():
    rspec = pl.BlockSpec((LANES, SSM_STATE), lambda j: (j, 0))
    cspec = pl.BlockSpec((SSM_STATE, LANES), lambda j: (0, j))
    vspec = pl.BlockSpec((None, 1, S5_STATE_LANES), lambda j: (j, 0, 0))
    return [rspec] * 5 + [cspec] * 5 + [vspec] * 3


def _s5_prompt(u, n_batch, layouts):
    n, d = u.shape
    assert d % LANES == 0 and n % (n_batch * S5_CHUNK) == 0 and S5_CHUNK & (S5_CHUNK - 1) == 0
    n_j = d // LANES
    n_chunk = n // n_batch // S5_CHUNK
    rows = n // S5_CHUNK
    rows_p, cols_p, lanes_p = layouts
    col = lambda j: (0, j)
    return pl.pallas_call(
        functools.partial(_s5_prompt_body, n_batch=n_batch, n_chunk=n_chunk),
        grid=(n_j,),
        in_specs=[pl.BlockSpec((n, LANES), col)] + _s5_specs(),
        out_specs=[pl.BlockSpec((n, LANES), col),
                   pl.BlockSpec((None, n_batch, 2 * S5_STATE_LANES), lambda j: (j, 0, 0))],
        out_shape=[jax.ShapeDtypeStruct((n, d), F32),
                   jax.ShapeDtypeStruct((n_j, n_batch, 2 * S5_STATE_LANES), F32)],
        scratch_shapes=[pltpu.VMEM((S5_CHUNK * LANES, 2 * S5_STATE_LANES), BF16),
                        pltpu.VMEM((S5_CHUNK * LANES, 2 * LANES), BF16),
                        pltpu.VMEM((2 * S5_STATE_LANES, S5_CHUNK * LANES), BF16),
                        pltpu.VMEM((rows, 2 * S5_STATE_LANES), F32),
                        pltpu.VMEM((rows, 2 * S5_STATE_LANES), F32)],
        compiler_params=_cparams("parallel"),
        name="s5_prompt",
    )(u, *rows_p, *cols_p, *lanes_p)


def _s5_step_body(u_ref, hre_ref, him_ref, lr_re_ref, lr_im_ref, lr_dt_ref, b_re_ref, b_im_ref,
                  lc_re_ref, lc_im_ref, lc_dt_ref, c_re_ref, c_im_ref,
                  lv_re_ref, lv_im_ref, lv_dt_ref, y_ref, ore_ref, oim_ref):
    _, _, bb_re, bb_im = _s5_input_weights(lr_re_ref[...], lr_im_ref[...], lr_dt_ref[...],
                                           b_re_ref[...], b_im_ref[...])
    a_re, a_im, _, _ = _s5_disc(lv_re_ref[...], lv_im_ref[...], lv_dt_ref[...])
    u = u_ref[...]
    h_re, h_im = hre_ref[...], him_ref[...]
    x_re = _dot(u, _bd_rows(bb_re), precision=HIGHEST) + (a_re * h_re - a_im * h_im)
    x_im = _dot(u, _bd_rows(bb_im), precision=HIGHEST) + (a_re * h_im + a_im * h_re)
    ore_ref[...] = x_re
    oim_ref[...] = x_im
    y_ref[...] = (_dot(x_re, _bd_cols(c_re_ref[...]), precision=HIGHEST)
                  - _dot(x_im, _bd_cols(c_im_ref[...]), precision=HIGHEST))


def _s5_step(u, h_re, h_im, layouts):
    nb, d = u.shape
    rows_p, cols_p, lanes_p = layouts
    col = lambda j: (0, j)
    sspec = pl.BlockSpec((nb, S5_STATE_LANES), col)
    return pl.pallas_call(
        _s5_step_body,
        grid=(d // LANES,),
        in_specs=[pl.BlockSpec((nb, LANES), col), sspec, sspec] + _s5_specs(),
        out_specs=[pl.BlockSpec((nb, LANES), col), sspec, sspec],
        out_shape=[jax.ShapeDtypeStruct((nb, d), F32),
                   jax.ShapeDtypeStruct(h_re.shape, F32), jax.ShapeDtypeStruct(h_im.shape, F32)],
        compiler_params=_cparams("parallel"),
        name="s5_step",
    )(u, h_re, h_im, *rows_p, *cols_p, *lanes_p)


PAGE_ROWS = 128
PAGE_LINES = PAGE_ROWS * ROW_LINES
CHUNKS_PER_PAGE = PAGE_ROWS // CMP_STRIDE
CMP_PAGES_PER_STEP = 8
CMP_LINES = 2 * N_KV


def _compress_body(pt_ref, *refs, n_pages):
    del pt_ref
    k_pages = CMP_PAGES_PER_STEP
    srcs = refs[:k_pages]
    (pek_ref, pev_ref, w1k_ref, w1v_ref, w2k_ref, w2v_ref, kc_ref, vc_ref, buf_ref, shift_ref) = refs[k_pages:]
    p = pl.program_id(1)
    for k in range(k_pages):
        r0 = pl.multiple_of((p * k_pages + k) * PAGE_ROWS, PAGE_ROWS)
        for sg in range(CMP_LINES):
            buf_ref[sg, pl.ds(r0, PAGE_ROWS), :] = srcs[k][pl.ds(sg, PAGE_ROWS, stride=ROW_LINES), :]

    @pl.when(p == n_pages // k_pages - 1)
    def _():
        n_chunk = n_pages * CHUNKS_PER_PAGE
        half = CMP_STRIDE
        for slot, (pe_ref, w1_ref, w2_ref, out_ref) in enumerate(
                ((pek_ref, w1k_ref, w2k_ref, kc_ref), (pev_ref, w1v_ref, w2v_ref, vc_ref))):
            for g in range(N_KV):
                sg = slot * N_KV + g
                xs = [buf_ref[sg, pl.ds(l, n_chunk, stride=CMP_STRIDE), :] for l in range(half)]
                as_first = jnp.concatenate(
                    [(x + pe_ref[l:l + 1, :]).astype(BF16) for l, x in enumerate(xs)], axis=1)
                as_second = jnp.concatenate(
                    [(x + pe_ref[half + l:half + l + 1, :]).astype(BF16) for l, x in enumerate(xs)], axis=1)
                first = _dot(as_first, w1_ref[:half * HEAD_DIM, :])
                shift_ref[0:n_chunk, :] = _dot(as_second, w1_ref[half * HEAD_DIM:, :])
                shift_ref[n_chunk:n_chunk + 8, :] = jnp.zeros((8, HEAD_DIM), F32)
                hid = first + shift_ref[1:n_chunk + 1, :]
                out = _dot(jax.nn.silu(hid).astype(BF16), w2_ref[...])
                live = _iota(out.shape, 0) < n_chunk - 1
                out_ref[g] = jnp.where(live, out, 0.0).astype(BF16)


def _compress(src, page_ids, pe_k, pe_v, w1k, w1v, w2k, w2v):
    nb, n_pages = page_ids.shape
    k_pages = CMP_PAGES_PER_STEP
    assert n_pages % k_pages == 0
    n_chunk = n_pages * CHUNKS_PER_PAGE
    const = lambda b, p, pt: (0, 0)
    page = lambda k: pl.BlockSpec((PAGE_LINES, HEAD_DIM),
                                  lambda b, p, pt: (pt[b * n_pages + p * k_pages + k], 0))
    out_spec = pl.BlockSpec((None, N_KV, n_chunk, HEAD_DIM), lambda b, p, pt: (b, 0, 0, 0))
    grid_spec = pltpu.PrefetchScalarGridSpec(
        num_scalar_prefetch=1,
        grid=(nb, n_pages // k_pages),
        in_specs=[page(k) for k in range(k_pages)]
        + [pl.BlockSpec(pe_k.shape, const), pl.BlockSpec(pe_v.shape, const),
           pl.BlockSpec(w1k.shape, const), pl.BlockSpec(w1v.shape, const),
           pl.BlockSpec(w2k.shape, const), pl.BlockSpec(w2v.shape, const)],
        out_specs=[out_spec, out_spec],
        scratch_shapes=[pltpu.VMEM((CMP_LINES, n_pages * PAGE_ROWS, HEAD_DIM), F32),
                        pltpu.VMEM((n_chunk + 8, HEAD_DIM), F32)])
    return pl.pallas_call(
        functools.partial(_compress_body, n_pages=n_pages),
        grid_spec=grid_spec,
        out_shape=[jax.ShapeDtypeStruct((nb, N_KV, n_chunk, HEAD_DIM), BF16)] * 2,
        compiler_params=_cparams("parallel", "arbitrary"),
        name="compress",
    )(page_ids.reshape(-1), *([src] * k_pages), pe_k, pe_v, w1k, w1v, w2k, w2v)


def _masked_softmax(s, mask):
    s = jnp.where(mask, s, NEG)
    m = jnp.max(s, axis=-1, keepdims=True)
    e = jnp.where(mask, jnp.exp(s - m), 0.0)
    return e / jnp.maximum(jnp.sum(e, axis=-1, keepdims=True), 1e-30)


def _overlap(n_rows, n_lanes, n_cmp, n_sel):
    ci, sj = _iota((n_rows, n_lanes), 0), _iota((n_rows, n_lanes), 1)
    hit = ((ci * CMP_STRIDE < sj * SEL_LEN + SEL_LEN) & (ci * CMP_STRIDE + CMP_LEN > sj * SEL_LEN)
           & (ci < n_cmp) & (sj < n_sel))
    return jnp.where(hit, 1.0, 0.0).astype(BF16)


def _importance(imp, lane, qpos, n_sel):
    cur = qpos // SEL_LEN
    forced = (lane == 0) | (lane == cur) | (lane == cur - 1)
    imp = jnp.where(lane * SEL_LEN <= qpos, imp + jnp.where(forced, FORCE_BONUS, 0.0), NEG)
    return jnp.where(lane < n_sel, imp, LOWEST)


def _dot_tn(a, b):
    return lax.dot_general(a, b, (((0,), (0,)), ((), ())), preferred_element_type=F32)


LOG2E = math.log2(math.e)


def _online_attend_t(q, k_ref, v_ref, lo, hi, ck, bias_fn, s_even_ref, s_odd_ref, ml_ref, acc_ref):
    cols = q.shape[0]

    def chunk_start(c):
        return pl.multiple_of(jnp.minimum(c, hi - 1) * ck, ck)

    def scores(c, s_ref):
        k0 = chunk_start(c)
        bias = jnp.concatenate([bias_fn(k0)] * (cols // Q_BLOCK), axis=1) + jnp.where(c < hi, 0.0, NEG)
        s_ref[:ck, :] = _dot_nt(k_ref[pl.ds(k0, ck), :], q) * (ATT_SCALE * LOG2E) + bias

    def update(c, s_ref):
        m, l = ml_ref[0:1, :], ml_ref[1:2, :]
        s = s_ref[:ck, :]
        m_new = jnp.maximum(m, jnp.max(s, axis=0, keepdims=True))
        alpha = jnp.exp2(m - m_new)
        e = jnp.exp2(s - m_new)
        ml_ref[0:1, :] = m_new
        ml_ref[1:2, :] = alpha * l + jnp.sum(e, axis=0, keepdims=True)
        acc_ref[...] = alpha * acc_ref[...] + _dot_tn(v_ref[pl.ds(chunk_start(c), ck), :], e.astype(BF16))

    def body(i, carry):
        c = lo + 2 * i
        scores(c + 1, s_odd_ref)
        update(c, s_even_ref)
        scores(c + 2, s_even_ref)
        update(c + 1, s_odd_ref)
        return carry

    scores(lo, s_even_ref)
    ml_ref[0:1, :] = jnp.full((1, cols), NEG, F32)
    ml_ref[1:2, :] = jnp.zeros((1, cols), F32)
    acc_ref[...] = jnp.zeros((HEAD_DIM, cols), F32)
    lax.fori_loop(0, (hi - lo) // 2, body, 0)

    @pl.when((hi - lo) % 2 == 1)
    def _():
        update(hi - 1, s_even_ref)

    m, l = ml_ref[0:1, :], ml_ref[1:2, :]
    return jnp.where(m > NEG, acc_ref[...] / jnp.maximum(l, 1e-30), 0.0)


Q_BLOCK = 128
SEL_KEY_CHUNK = 256
WIN_KEY_CHUNK = 128


def _nsa_prompt_body(qc_ref, qr_ref, gate_ref, ksel_ref, vsel_ref, kwin_ref, vwin_ref,
                     kcmp_ref, vcmp_ref, o_ref, sel_ref, s_even_ref, s_odd_ref, ml_ref, acc_ref,
                     *, n_cmp, n_sel):
    qi = pl.program_id(2)
    q0 = qi * Q_BLOCK
    nh = HEADS_PER_KV
    stack = lambda ref: jnp.concatenate(
        [ref[:, hd * HEAD_DIM:(hd + 1) * HEAD_DIM] for hd in range(nh)], axis=0)
    per_head = lambda x: jnp.concatenate([x] * nh, axis=1)
    qc, qr = stack(qc_ref), stack(qr_ref)
    n_cmp_rows = kcmp_ref.shape[0]
    n_sel_rows = sel_ref.shape[0]

    cshape = (n_cmp_rows, Q_BLOCK)
    cblk, cq = _iota(cshape, 0), q0 + _iota(cshape, 1)
    cmask = per_head(jnp.where((cblk < n_cmp) & (cblk * CMP_STRIDE + CMP_LEN - 1 <= cq), 1.0, 0.0)) > 0.5
    s = jnp.where(cmask, _dot_nt(kcmp_ref[...], qc) * ATT_SCALE, NEG)
    e = jnp.where(cmask, jnp.exp(s - jnp.max(s, axis=0, keepdims=True)), 0.0)
    p_cmp = (e / jnp.maximum(jnp.sum(e, axis=0, keepdims=True), 1e-30)).astype(BF16)
    o_cmp = _dot_tn(vcmp_ref[...], p_cmp)

    oshape = (n_sel_rows, n_cmp_rows)
    sj, ci = _iota(oshape, 0), _iota(oshape, 1)
    hit = ((ci * CMP_STRIDE < sj * SEL_LEN + SEL_LEN) & (ci * CMP_STRIDE + CMP_LEN > sj * SEL_LEN)
           & (ci < n_cmp) & (sj < n_sel))
    imp_h = _dot(jnp.where(hit, 1.0, 0.0).astype(BF16), p_cmp)
    imp = imp_h[:, 0:Q_BLOCK]
    for hd in range(1, nh):
        imp = imp + imp_h[:, hd * Q_BLOCK:(hd + 1) * Q_BLOCK]
    ishape = (n_sel_rows, Q_BLOCK)
    blk = _iota(ishape, 0)
    imp = _importance(imp, blk, q0 + _iota(ishape, 1), n_sel)
    rank = jnp.zeros(ishape, F32)
    for k in range(n_sel):
        row = imp[k:k + 1, :]
        rank += jnp.where((row > imp) | ((row == imp) & (blk > k)), 1.0, 0.0)
    sel_ref[...] = jnp.where((rank < min(SEL_TOP, n_sel)) & (blk < n_sel), 1.0, 0.0)

    def sel_mask(k0):
        shape = (SEL_KEY_CHUNK, Q_BLOCK)
        j0 = k0 // SEL_LEN
        chosen = jnp.concatenate(
            [jnp.broadcast_to(sel_ref[pl.ds(j0 + i, 1), :], (SEL_LEN, Q_BLOCK))
             for i in range(SEL_KEY_CHUNK // SEL_LEN)], axis=0)
        live = (k0 + _iota(shape, 0) <= q0 + _iota(shape, 1)) & (chosen > 0.5)
        return jnp.where(live, 0.0, NEG)

    def win_mask(k0):
        shape = (WIN_KEY_CHUNK, Q_BLOCK)
        kpos = k0 + _iota(shape, 0)
        qp = q0 + _iota(shape, 1)
        return jnp.where((kpos <= qp) & (kpos > qp - WINDOW), 0.0, NEG)

    sel_chunks = (q0 + Q_BLOCK + SEL_KEY_CHUNK - 1) // SEL_KEY_CHUNK
    o_sel = _online_attend_t(qr, ksel_ref, vsel_ref, 0, sel_chunks, SEL_KEY_CHUNK, sel_mask,
                             s_even_ref, s_odd_ref, ml_ref, acc_ref)
    win_lo = jnp.maximum(q0 - WINDOW, 0) // WIN_KEY_CHUNK
    win_hi = (q0 + Q_BLOCK + WIN_KEY_CHUNK - 1) // WIN_KEY_CHUNK
    o_win = _online_attend_t(qr, kwin_ref, vwin_ref, win_lo, win_hi, WIN_KEY_CHUNK, win_mask,
                             s_even_ref, s_odd_ref, ml_ref, acc_ref)

    gates = jnp.transpose(gate_ref[...])
    for hd in range(nh):
        cols = slice(hd * Q_BLOCK, (hd + 1) * Q_BLOCK)
        c0 = hd * N_BRANCH
        o = (o_cmp[:, cols] * gates[c0:c0 + 1, :] + o_sel[:, cols] * gates[c0 + 1:c0 + 2, :]
             + o_win[:, cols] * gates[c0 + 2:c0 + 3, :])
        o_ref[:, hd * HEAD_DIM:(hd + 1) * HEAD_DIM] = jnp.transpose(o).astype(BF16)


def _nsa_prompt(qc, qr, gates, kv, kcmp, vcmp):
    bsz, t, dq = qc.shape
    assert t % SEL_KEY_CHUNK == 0 and t % Q_BLOCK == 0 and kcmp.shape[2] % 8 == 0
    n_cmp = (t - CMP_LEN) // CMP_STRIDE + 1
    n_sel = -(-t // SEL_LEN)
    gw = HEADS_PER_KV * HEAD_DIM
    qspec = pl.BlockSpec((None, Q_BLOCK, gw), lambda b, g, i: (b, i, g))
    kvspec = lambda slot: pl.BlockSpec((None, t, HEAD_DIM), lambda b, g, i: (b, 0, slot * N_KV + g))
    cspec = pl.BlockSpec((None, None, kcmp.shape[2], HEAD_DIM), lambda b, g, i: (b, g, 0, 0))
    return pl.pallas_call(
        functools.partial(_nsa_prompt_body, n_cmp=n_cmp, n_sel=n_sel),
        grid=(bsz, N_KV, t // Q_BLOCK),
        in_specs=[qspec, qspec, pl.BlockSpec((None, Q_BLOCK, LANES), lambda b, g, i: (b, i, g)),
                  kvspec(2), kvspec(3), kvspec(4), kvspec(5), cspec, cspec],
        out_specs=qspec,
        out_shape=jax.ShapeDtypeStruct((bsz, t, dq), BF16),
        scratch_shapes=[pltpu.VMEM((-(-n_sel // 8) * 8, Q_BLOCK), F32)]
        + [pltpu.VMEM((max(SEL_KEY_CHUNK, WIN_KEY_CHUNK), HEADS_PER_KV * Q_BLOCK), F32)] * 2
        + [pltpu.VMEM((8, HEADS_PER_KV * Q_BLOCK), F32), pltpu.VMEM((HEAD_DIM, HEADS_PER_KV * Q_BLOCK), F32)],
        compiler_params=_cparams("parallel", "parallel", "arbitrary"),
        name="nsa_prompt",
    )(qc, qr, gates, kv, kv, kv, kv, kcmp, vcmp)


def _select_body(qc_ref, kcmp_ref, vcmp_ref, ocmp_ref, idx_ref, *, qpos, n_cmp, n_sel):
    q = qc_ref[...]
    nc = kcmp_ref.shape[0]
    sel_lanes = -(-n_sel // LANES) * LANES
    ci = _iota((HEADS_PER_KV, nc), 1)
    cmask = (ci < n_cmp) & (ci * CMP_STRIDE + CMP_LEN - 1 <= qpos)
    p_cmp = _masked_softmax(_dot_nt(q, kcmp_ref[...]) * ATT_SCALE, cmask).astype(BF16)
    ocmp_ref[...] = _dot(p_cmp, vcmp_ref[...])

    imp = jnp.sum(_dot(p_cmp, _overlap(nc, sel_lanes, n_cmp, n_sel)), axis=0, keepdims=True)
    lane = _iota((1, sel_lanes), 1)
    imp = _importance(imp, lane, qpos, n_sel)
    sq = (sel_lanes, sel_lanes)
    ri, cj = _iota(sq, 0), _iota(sq, 1)
    impc = jnp.sum(jnp.where(ri == cj, imp, 0.0), axis=1, keepdims=True)
    n_top = min(SEL_TOP, n_sel)
    before_col = jnp.where((impc > imp) | ((impc == imp) & (ri < cj)), 1.0, 0.0)
    before_row = jnp.where((imp > impc) | ((imp == impc) & (cj < ri)), 1.0, 0.0)
    sel_lane = (jnp.sum(before_col, axis=0, keepdims=True) < n_top) & (lane < n_sel)
    sel_row = (jnp.sum(before_row, axis=1, keepdims=True) < n_top) & (_iota((sel_lanes, 1), 0) < n_sel)
    slot_of = jnp.sum(jnp.where((ri < cj) & sel_row, 1.0, 0.0), axis=0, keepdims=True)
    lshape = (SEL_TOP, sel_lanes)
    onehot = jnp.where(sel_lane & (slot_of == _iota(lshape, 0).astype(F32)), 1.0, 0.0)
    idx = jnp.sum(onehot * _iota(lshape, 1).astype(F32), axis=1, keepdims=True)
    idx_ref[...] = jnp.broadcast_to(idx, (SEL_TOP, LANES)).astype(jnp.int32)


def _nsa_select(qc, kcmp, vcmp, qpos, n_cmp, n_sel):
    nb = qc.shape[0]
    nc = kcmp.shape[2]
    hspec = pl.BlockSpec((None, None, HEADS_PER_KV, HEAD_DIM), lambda b, g: (b, g, 0, 0))
    cspec = pl.BlockSpec((None, None, nc, HEAD_DIM), lambda b, g: (b, g, 0, 0))
    return pl.pallas_call(
        functools.partial(_select_body, qpos=qpos, n_cmp=n_cmp, n_sel=n_sel),
        grid=(nb, N_KV),
        in_specs=[hspec, cspec, cspec],
        out_specs=[hspec, pl.BlockSpec((None, None, SEL_TOP, LANES), lambda b, g: (b, g, 0, 0))],
        out_shape=[jax.ShapeDtypeStruct((nb, N_KV, HEADS_PER_KV, HEAD_DIM), F32),
                   jax.ShapeDtypeStruct((nb, N_KV, SEL_TOP, LANES), jnp.int32)],
        compiler_params=_cparams("parallel", "parallel"),
        name="nsa_select",
    )(qc, kcmp, vcmp)


def _attend_plus_new(q, k, v, valid, k_new, v_new, new_ok):
    qf = q.astype(F32)
    s = _dot_nt(q, k.astype(BF16)) * ATT_SCALE
    s_new = jnp.sum(qf * k_new.astype(BF16).astype(F32), axis=1, keepdims=True) * ATT_SCALE
    s = jnp.where(valid, s, NEG)
    s_new = jnp.where(new_ok, s_new, NEG)
    m = jnp.maximum(jnp.max(s, axis=-1, keepdims=True), s_new)
    e = jnp.where(valid, jnp.exp(s - m), 0.0)
    e_new = jnp.where(new_ok, jnp.exp(s_new - m), 0.0)
    den = jnp.maximum(jnp.sum(e, axis=-1, keepdims=True) + e_new, 1e-30)
    p = (e / den).astype(BF16)
    p_new = (e_new / den).astype(BF16).astype(F32)
    return _dot(p, v.astype(BF16)) + p_new * v_new.astype(BF16).astype(F32)


SEL_BLOCK_LINES = SEL_LEN * ROW_LINES


def _attend_body(idx_ref, pt_ref, qr_ref, gate_ref, ocmp_ref, *refs, qpos, past_len, n_win):
    del pt_ref
    blocks = refs[:SEL_TOP]
    win_ref, rows_new_ref, win_new_ref, o_ref = refs[SEL_TOP:]
    b, g = pl.program_id(0), pl.program_id(1)
    q = qr_ref[...]
    n_keys = SEL_TOP * SEL_LEN
    base = (b * N_KV + g) * SEL_TOP
    lane = _iota((1, n_keys), 1)
    kpos = jnp.zeros((1, n_keys), jnp.int32)
    n_new = jnp.int32(0)
    for t in range(SEL_TOP):
        j = idx_ref[base + t]
        kpos = jnp.where(lane // SEL_LEN == t, j * SEL_LEN + lane % SEL_LEN, kpos)
        n_new = n_new + jnp.where(j == qpos // SEL_LEN, 1, 0)
    valid = (kpos <= qpos) & (kpos < past_len)
    new_ok = (jnp.zeros((HEADS_PER_KV, 1), jnp.int32) + n_new) > 0
    line = lambda ref, slot, n, per_token: ref[pl.ds(slot * N_KV + g, n, stride=per_token), :]
    k_sel = jnp.concatenate([line(blk, 2, SEL_LEN, ROW_LINES) for blk in blocks], axis=0)
    v_sel = jnp.concatenate([line(blk, 3, SEL_LEN, ROW_LINES) for blk in blocks], axis=0)
    o_sel = _attend_plus_new(q, k_sel, v_sel, valid, line(rows_new_ref, 2, 1, ROW_LINES),
                             line(rows_new_ref, 3, 1, ROW_LINES), new_ok)

    wpos = (past_len - n_win) + _iota((1, n_win), 1)
    wvalid = (wpos <= qpos) & (wpos > qpos - WINDOW) & (wpos >= 0)
    always = jnp.zeros((HEADS_PER_KV, 1), jnp.int32) == 0
    o_win = _attend_plus_new(q, line(win_ref, 0, n_win, WIN_LINES), line(win_ref, 1, n_win, WIN_LINES),
                             wvalid, line(win_new_ref, 0, 1, WIN_LINES), line(win_new_ref, 1, 1, WIN_LINES),
                             always)

    gates = gate_ref[...]
    o_ref[...] = ocmp_ref[...] * gates[:, 0:1] + o_sel * gates[:, 1:2] + o_win * gates[:, 2:3]


def _nsa_attend(idx, page_table, qr, gates, ocmp, cache, cache_win, rows_new, win_new, past_len):
    nb, n_pages = page_table.shape
    n_win = cache_win.shape[0] // (nb * WIN_LINES)
    halves = PAGE_ROWS // SEL_LEN
    n_past_blocks = n_pages * halves

    def sel_block(t):
        def index(b, g, idx_ref, pt_ref):
            j = jnp.minimum(idx_ref[(b * N_KV + g) * SEL_TOP + t], n_past_blocks - 1)
            return (pt_ref[b * n_pages + j // halves] * halves + j % halves, 0)
        return pl.BlockSpec((SEL_BLOCK_LINES, HEAD_DIM), index)

    hspec = pl.BlockSpec((None, None, HEADS_PER_KV, HEAD_DIM), lambda b, g, i, p: (b, g, 0, 0))
    grid_spec = pltpu.PrefetchScalarGridSpec(
        num_scalar_prefetch=2,
        grid=(nb, N_KV),
        in_specs=[hspec, hspec, hspec] + [sel_block(t) for t in range(SEL_TOP)]
        + [pl.BlockSpec((n_win * WIN_LINES, HEAD_DIM), lambda b, g, i, p: (b, 0)),
           pl.BlockSpec((None, ROW_LINES, HEAD_DIM), lambda b, g, i, p: (b, 0, 0)),
           pl.BlockSpec((None, WIN_LINES, HEAD_DIM), lambda b, g, i, p: (b, 0, 0))],
        out_specs=hspec)
    return pl.pallas_call(
        functools.partial(_attend_body, qpos=past_len, past_len=past_len, n_win=n_win),
        grid_spec=grid_spec,
        out_shape=jax.ShapeDtypeStruct((nb, N_KV, HEADS_PER_KV, HEAD_DIM), F32),
        compiler_params=_cparams("parallel", "arbitrary"),
        name="nsa_attend",
    )(idx, page_table.reshape(-1), qr, gates, ocmp, *([cache] * SEL_TOP), cache_win, rows_new, win_new)


PROMPT_FFN_ROWS = 512
PROMPT_ROWS = 512


def _rope_tables(pos):
    half = HEAD_DIM // 2
    inv = jnp.exp(-math.log(ROPE_THETA) * jnp.arange(half, dtype=F32) / half)
    ang = pos.astype(F32)[:, None] * inv[None, :]
    cos, sin = jnp.cos(ang), jnp.sin(ang)
    return jnp.concatenate([cos, cos], axis=1), jnp.concatenate([-sin, sin], axis=1)


def _prepare_weights(prm):
    nq = N_HEADS * HEAD_DIM
    w_qg = prm["attn_w_qg"]
    nb, d = w_qg.shape[:2]
    per_kv = HEADS_PER_KV * N_BRANCH
    w_gate = w_qg[:, :, nq:].reshape(nb, d, N_KV, per_kv)
    w_gate = jnp.pad(w_gate, ((0, 0), (0, 0), (0, 0), (0, LANES - per_kv))).reshape(nb, d, N_KV * LANES)
    return dict(
        ffn16={},
        ple_gate=prm["ple_w_gate"].astype(BF16), ple_proj=prm["ple_w_proj"].astype(BF16),
        glu=prm["glu_w"].astype(BF16), kv=prm["w_kv"].astype(BF16),
        q=w_qg[:, :, :nq].astype(BF16), gate=w_gate.astype(BF16), o=prm["attn_w_o"].astype(BF16),
        w1k=prm["cmp_w1_k"].astype(BF16), w1v=prm["cmp_w1_v"].astype(BF16),
        w2k=prm["cmp_w2_k"].astype(BF16), w2v=prm["cmp_w2_v"].astype(BF16),
        s5=[_s5_layouts(prm["ssm_lam_re"][i], prm["ssm_lam_im"][i], prm["ssm_log_dt"][i],
                        prm["ssm_b_re"][i], prm["ssm_b_im"][i], prm["ssm_c_re"][i], prm["ssm_c_im"][i])
            for i in range(N_A_LAYERS)])


def _finish(gen):
    try:
        while True:
            next(gen)
    except StopIteration as done:
        return done.value


def _run_group(x, p, start, state, past, prm, w):
    bsz, t, d = x.shape
    n = bsz * t
    prompt = past is None
    tm_ffn = PROMPT_FFN_ROWS if prompt else n
    tm = PROMPT_ROWS if prompt else n
    depth = prm["norm_g"].shape[0]
    cos, sin = _rope_tables(start + jnp.arange(t))
    if not prompt:
        cos, sin = jnp.broadcast_to(cos, (n, HEAD_DIM)), jnp.broadcast_to(sin, (n, HEAD_DIM))
    h = x.reshape(n, d)
    ssm_re, ssm_im = [], []
    rows = win = kv16 = kcmp = vcmp = None

    w32 = (prm["ffn_w_gate"], prm["ffn_w_up"], prm["ffn_w_down"])

    def ffn(h, g_pre, g_post, g_next, layer, which):
        if (layer, which) not in w["ffn16"]:
            h_new, xn, *w["ffn16"][layer, which] = _ffn_cast(h, g_pre, g_post, g_next, *w32, layer, which)
            return h_new, xn
        nxt = (layer, which + 1) if which == 0 else (layer + 1, 0)
        if not prompt or nxt[0] == depth:
            return _ffn(h, g_pre, g_post, g_next, *w["ffn16"][layer, which], tm_ffn)
        h_new, xn, *w["ffn16"][nxt] = _ffn_convert(h, g_pre, g_post, g_next, *w["ffn16"][layer, which],
                                                   tm_ffn, *w32, *nxt)
        return h_new, xn

    if not prompt:
        cache_kv, cache_win, page_table = past
        past_len = page_table.shape[1] * cache_kv.shape[1]
        cache_lines = cache_kv.reshape(-1, HEAD_DIM)
        win_lines = cache_win.reshape(-1, HEAD_DIM)
    for i in range(depth):
        g = prm["norm_g"][i]
        h, xn = ffn(h, g[0:1], g[1:2], g[2:3], i, 0)
        if i == 0:
            yield
        if i < N_A_LAYERS:
            if prompt:
                y, st = _s5_prompt(xn, bsz, w["s5"][i])
                st = st.reshape(d // LANES, bsz, 2, S5_GROUPS_PER_STEP, SSM_STATE).transpose(2, 1, 0, 3, 4)
                st = st.reshape(2, bsz, d // SSM_GROUP, SSM_STATE)
                ssm_re.append(st[0])
                ssm_im.append(st[1])
            else:
                y, s_re, s_im = _s5_step(xn, state[0][i].reshape(bsz, -1), state[1][i].reshape(bsz, -1),
                                         w["s5"][i])
                ssm_re.append(s_re.reshape(state[0][i].shape))
                ssm_im.append(s_im.reshape(state[1][i].shape))
            h = _s5_gate(h, xn, y, prm["ssm_d"][i][None], w["glu"], prm["glu_b"][i][None], g[3:4], i, tm)
        else:
            j = i - N_A_LAYERS
            qc, qr, gates = _qg_proj(xn, w["q"], w["gate"], cos, sin, j, tm)
            if prompt:
                o = _nsa_prompt(qc.reshape(bsz, t, -1), qr.reshape(bsz, t, -1), gates.reshape(bsz, t, -1),
                                kv16.reshape(bsz, t, -1), kcmp, vcmp).reshape(n, -1)
            else:
                heads = lambda a: a.reshape(bsz, N_KV, HEADS_PER_KV, HEAD_DIM)
                n_cmp = (past_len + t - CMP_LEN) // CMP_STRIDE + 1
                n_sel = -(-(past_len + t) // SEL_LEN)
                ocmp, idx = _nsa_select(heads(qc), kcmp, vcmp, past_len, n_cmp, n_sel)
                gsm = gates.reshape(bsz, N_KV, LANES)[:, :, :HEADS_PER_KV * N_BRANCH]
                gsm = gsm.reshape(bsz, N_KV, HEADS_PER_KV, N_BRANCH)
                gsm = jnp.pad(gsm, ((0, 0), (0, 0), (0, 0), (0, LANES - N_BRANCH)))
                o = _nsa_attend(idx[..., 0].reshape(-1), page_table, heads(qr), gsm, ocmp,
                                cache_lines, win_lines, rows_l.reshape(bsz, ROW_LINES, HEAD_DIM),
                                win_l.reshape(bsz, WIN_LINES, HEAD_DIM), past_len).reshape(n, -1)
            h = _o_proj(h, o, w["o"], g[3:4], j, tm)
        h, _ = ffn(h, g[4:5], g[5:6], g[6:7], i, 1)
        h = _ple(h, p[i].reshape(n, -1), w["ple_gate"], w["ple_proj"], g[6:7], i, tm)
        if i == N_A_LAYERS - 1:
            rows_l, win_l, kv16 = _kv_proj(h, prm["kv_norm_g"][None], w["kv"], cos, sin, tm)
            rows = rows_l.reshape(bsz, t, ROW_SLOTS, N_KV, HEAD_DIM)
            win = win_l.reshape(bsz, t, WIN_SLOTS, N_KV, HEAD_DIM)
            cmp_w = (prm["cmp_pe_k"], prm["cmp_pe_v"], w["w1k"], w["w1v"], w["w2k"], w["w2v"])
            if prompt:
                pages = jnp.arange(n // PAGE_ROWS, dtype=jnp.int32).reshape(bsz, t // PAGE_ROWS)
                kcmp, vcmp = _compress(rows_l, pages, *cmp_w)
                win = win[:, t - min(WINDOW, t):]
            else:
                kcmp, vcmp = _compress(cache_lines, page_table, *cmp_w)
                win = jnp.concatenate([cache_win, win], axis=1)[:, t:]
    return h.reshape(bsz, t, d), jnp.stack(ssm_re), jnp.stack(ssm_im), rows, win


def kernel(x_prompt, x_sample, state_ssm_re, state_ssm_im, cache_kv, cache_win, page_table,
           p_prompt, p_sample, norm_g, ffn_w_gate, ffn_w_up, ffn_w_down, ple_w_proj, ple_w_gate,
           ssm_lam_re, ssm_lam_im, ssm_log_dt, ssm_b_re, ssm_b_im, ssm_c_re, ssm_c_im, ssm_d,
           glu_w, glu_b, kv_norm_g, w_kv, cmp_pe_k, cmp_pe_v, cmp_w1_k, cmp_w2_k, cmp_w1_v,
           cmp_w2_v, attn_w_qg, attn_w_o):
    prm = dict(norm_g=norm_g, ffn_w_gate=ffn_w_gate, ffn_w_up=ffn_w_up, ffn_w_down=ffn_w_down,
               ple_w_proj=ple_w_proj, ple_w_gate=ple_w_gate, ssm_lam_re=ssm_lam_re,
               ssm_lam_im=ssm_lam_im, ssm_log_dt=ssm_log_dt, ssm_b_re=ssm_b_re, ssm_b_im=ssm_b_im,
               ssm_c_re=ssm_c_re, ssm_c_im=ssm_c_im, ssm_d=ssm_d, glu_w=glu_w, glu_b=glu_b,
               kv_norm_g=kv_norm_g, w_kv=w_kv, cmp_pe_k=cmp_pe_k, cmp_pe_v=cmp_pe_v,
               cmp_w1_k=cmp_w1_k, cmp_w2_k=cmp_w2_k, cmp_w1_v=cmp_w1_v, cmp_w2_v=cmp_w2_v,
               attn_w_qg=attn_w_qg, attn_w_o=attn_w_o)
    assert x_sample.shape[1] == 1, "the decode group handles one new token per sequence"
    assert cache_kv.shape[1] == PAGE_ROWS
    w = _prepare_weights(prm)
    past_len = page_table.shape[1] * cache_kv.shape[1]
    decode = _run_group(x_sample, p_sample, past_len, (state_ssm_re, state_ssm_im),
                        (cache_kv, cache_win, page_table), prm, w)
    next(decode)
    y_p, re_p, im_p, kv_p, win_p = _finish(_run_group(x_prompt, p_prompt, 0, None, None, prm, w))
    y_s, re_s, im_s, kv_s, win_s = _finish(decode)
    return (y_p, y_s, re_p, im_p, re_s, im_s, kv_p, kv_s, win_p, win_s)
```

```python
import functools
import math

import jax
import jax.numpy as jnp
from jax import lax
from jax.experimental import pallas as pl
from jax.experimental.pallas import tpu as pltpu

F32 = jnp.float32
BF16 = jnp.bfloat16
HIGHEST = lax.Precision.HIGHEST

N_A_LAYERS = 2
NORM_EPS = 1e-6
SSM_GROUP = 16
SSM_STATE = 64
HEAD_DIM = 128
N_KV = 2
HEADS_PER_KV = 8
N_HEADS = N_KV * HEADS_PER_KV
N_BRANCH = 3
N_KV_SLOTS = 6
CMP_LEN = 32
CMP_STRIDE = 16
SEL_LEN = 64
SEL_TOP = 16
WINDOW = 512
ROPE_THETA = 10000.0
FORCE_BONUS = 1e3
NEG = -1e30
LOWEST = -3.0e38
ATT_SCALE = HEAD_DIM ** -0.5

LANES = 128
V7X_VMEM_BYTES = 64 * 2 ** 20
VMEM_LIMIT_BYTES = V7X_VMEM_BYTES * 7 // 8
FFN_TILE_F = 512
S5_CHUNK = 16
S5_GROUPS_PER_STEP = LANES // SSM_GROUP
S5_STATE_LANES = S5_GROUPS_PER_STEP * SSM_STATE


def _cparams(*sem):
    return pltpu.CompilerParams(dimension_semantics=sem, vmem_limit_bytes=VMEM_LIMIT_BYTES)


def _rms(x, g):
    y = x * lax.rsqrt(jnp.mean(x * x, axis=-1, keepdims=True) + NORM_EPS)
    return y * g


def _dot(a, b, **kw):
    return jnp.dot(a, b, preferred_element_type=F32, **kw)


def _dot_3pass(a, b):
    a_hi, b_hi = a.astype(BF16), b.astype(BF16)
    a_lo = (a - a_hi.astype(F32)).astype(BF16)
    b_lo = (b - b_hi.astype(F32)).astype(BF16)
    return _dot(a_hi, b_hi) + _dot(a_hi, b_lo) + _dot(a_lo, b_hi)


def _dot_nt(a, b):
    return lax.dot_general(a, b, (((1,), (1,)), ((), ())), preferred_element_type=F32)


def _rope(x, cos, sin):
    return x * cos + pltpu.roll(x, HEAD_DIM // 2, 1) * sin


def _iota(shape, axis):
    return lax.broadcasted_iota(jnp.int32, shape, axis)


def _ffn_step(k, nk, h_ref, gpre_ref, gpost_ref, gnext_ref, wg_ref, wu_ref, wd_ref, o_ref, xn_out_ref,
              xn_ref, acc_ref, side_work=None):
    @pl.when(k == 0)
    def _():
        xn_ref[...] = _rms(h_ref[...], gpre_ref[...]).astype(BF16)
        acc_ref[...] = jnp.zeros_like(acc_ref)

    xn = xn_ref[...]
    gate = _dot(xn, wg_ref[...])
    up = _dot(xn, wu_ref[...])
    mid = (jax.nn.silu(gate) * up).astype(BF16)
    acc_ref[...] += _dot(mid, wd_ref[...])
    if side_work is not None:
        side_work()

    @pl.when(k == nk - 1)
    def _():
        h_new = h_ref[...] + _rms(acc_ref[...], 0.5 * gpost_ref[...])
        o_ref[...] = h_new
        if xn_out_ref is not None:
            xn_out_ref[...] = _rms(h_new, gnext_ref[...])


def _ffn_body(h_ref, gpre_ref, gpost_ref, gnext_ref, wg_ref, wu_ref, wd_ref, o_ref, *rest, nk):
    *xn_out_ref, xn_ref, acc_ref = rest
    _ffn_step(pl.program_id(1), nk, h_ref, gpre_ref, gpost_ref, gnext_ref,
              wg_ref, wu_ref, wd_ref, o_ref, xn_out_ref[0] if xn_out_ref else None, xn_ref, acc_ref)


def _ffn(h, g_pre, g_post, g_next, wg, wu, wd, tm):
    n, d = h.shape
    nk = wg.shape[-1] // FFN_TILE_F
    row = lambda i, k: (i, 0)
    vec = lambda i, k: (0, 0)
    n_out = 1 if g_next is None else 2
    out = pl.pallas_call(
        functools.partial(_ffn_body, nk=nk),
        grid=(n // tm, nk),
        in_specs=[pl.BlockSpec((tm, d), row), pl.BlockSpec((1, d), vec), pl.BlockSpec((1, d), vec),
                  pl.BlockSpec((1, d), vec),
                  pl.BlockSpec((d, FFN_TILE_F), lambda i, k: (0, k)),
                  pl.BlockSpec((d, FFN_TILE_F), lambda i, k: (0, k)),
                  pl.BlockSpec((FFN_TILE_F, d), lambda i, k: (k, 0))],
        out_specs=[pl.BlockSpec((tm, d), row)] * n_out,
        out_shape=[jax.ShapeDtypeStruct((n, d), F32)] * n_out,
        scratch_shapes=[pltpu.VMEM((tm, d), BF16), pltpu.VMEM((tm, d), F32)],
        compiler_params=_cparams("parallel", "arbitrary"),
        name="ffn",
    )(h, g_pre, g_post, g_post if g_next is None else g_next, wg, wu, wd)
    return (out[0], None) if g_next is None else tuple(out)


FFN_CONVERT_ROWS = 16


def _ffn_convert_body(h_ref, gpre_ref, gpost_ref, gnext_ref, wg_ref, wu_ref, wd_ref,
                      ng_ref, nu_ref, nd_ref, o_ref, *rest, nk, d_ff):
    *xn_out_ref, ng16_ref, nu16_ref, nd16_ref, xn_ref, acc_ref = rest
    xn_out_ref = xn_out_ref[0] if xn_out_ref else None
    i, k = pl.program_id(0), pl.program_id(1)

    def convert_next():
        pad = ng16_ref.shape[1] - d_ff
        for src, dst in ((ng_ref, ng16_ref), (nu_ref, nu16_ref)):
            dst[:, :d_ff] = src[...].astype(BF16)
            dst[:, d_ff:] = jnp.zeros((dst.shape[0], pad), BF16)
        rows = nd_ref.shape[0]
        live = d_ff - (i * nk + k) * rows
        nd16_ref[...] = jnp.where(_iota(nd_ref.shape, 0) < live, nd_ref[...], 0.0).astype(BF16)

    _ffn_step(k, nk, h_ref, gpre_ref, gpost_ref, gnext_ref,
              wg_ref, wu_ref, wd_ref, o_ref, xn_out_ref, xn_ref, acc_ref, convert_next)


def _ffn_convert(h, g_pre, g_post, g_next, wg, wu, wd, tm, wg32, wu32, wd32, layer, which):
    n, d = h.shape
    fp = wg.shape[-1]
    d_ff = wg32.shape[-1]
    nk = fp // FFN_TILE_F
    steps = (n // tm) * nk
    rg = FFN_CONVERT_ROWS
    rd = fp // steps
    assert fp % steps == 0 and d_ff % rd == 0 and d % rg == 0 and d // rg <= steps
    row = lambda i, k: (i, 0)
    vec = lambda i, k: (0, 0)
    up_slab = lambda i, k: jnp.minimum(i * nk + k, d // rg - 1)
    down_slab = lambda i, k: jnp.minimum(i * nk + k, d_ff // rd - 1)
    n_out = 1 if g_next is None else 2
    out = pl.pallas_call(
        functools.partial(_ffn_convert_body, nk=nk, d_ff=d_ff),
        grid=(n // tm, nk),
        in_specs=[pl.BlockSpec((tm, d), row), pl.BlockSpec((1, d), vec), pl.BlockSpec((1, d), vec),
                  pl.BlockSpec((1, d), vec),
                  pl.BlockSpec((d, FFN_TILE_F), lambda i, k: (0, k)),
                  pl.BlockSpec((d, FFN_TILE_F), lambda i, k: (0, k)),
                  pl.BlockSpec((FFN_TILE_F, d), lambda i, k: (k, 0)),
                  pl.BlockSpec((None, None, rg, d_ff), lambda i, k: (layer, which, up_slab(i, k), 0)),
                  pl.BlockSpec((None, None, rg, d_ff), lambda i, k: (layer, which, up_slab(i, k), 0)),
                  pl.BlockSpec((None, None, rd, d), lambda i, k: (layer, which, down_slab(i, k), 0))],
        out_specs=[pl.BlockSpec((tm, d), row)] * n_out
        + [pl.BlockSpec((rg, fp), lambda i, k: (up_slab(i, k), 0)),
           pl.BlockSpec((rg, fp), lambda i, k: (up_slab(i, k), 0)),
           pl.BlockSpec((rd, d), lambda i, k: (i * nk + k, 0))],
        out_shape=[jax.ShapeDtypeStruct((n, d), F32)] * n_out
        + [jax.ShapeDtypeStruct((d, fp), BF16)] * 2 + [jax.ShapeDtypeStruct((fp, d), BF16)],
        scratch_shapes=[pltpu.VMEM((tm, d), BF16), pltpu.VMEM((tm, d), F32)],
        compiler_params=_cparams("arbitrary", "arbitrary"),
        name="ffn_convert",
    )(h, g_pre, g_post, g_post if g_next is None else g_next, wg, wu, wd, wg32, wu32, wd32)
    return (out[0], None, *out[1:]) if g_next is None else tuple(out)


def _ffn_cast_body(h_ref, gpre_ref, gpost_ref, gnext_ref, wg_ref, wu_ref, wd_ref,
                   o_ref, xn_out_ref, wg16_ref, wu16_ref, wd16_ref, xn_ref, acc_ref, *, nk, d_ff):
    k = pl.program_id(1)
    live = d_ff - k * FFN_TILE_F
    wshape, dshape = wg_ref.shape, wd_ref.shape
    wg16_ref[...] = jnp.where(_iota(wshape, 1) < live, wg_ref[...], 0.0).astype(BF16)
    wu16_ref[...] = jnp.where(_iota(wshape, 1) < live, wu_ref[...], 0.0).astype(BF16)
    wd16_ref[...] = jnp.where(_iota(dshape, 0) < live, wd_ref[...], 0.0).astype(BF16)
    _ffn_step(k, nk, h_ref, gpre_ref, gpost_ref, gnext_ref, wg16_ref, wu16_ref, wd16_ref,
              o_ref, xn_out_ref, xn_ref, acc_ref)


def _ffn_cast(h, g_pre, g_post, g_next, wg, wu, wd, layer, which):
    n, d = h.shape
    d_ff = wg.shape[-1]
    nk = -(-d_ff // FFN_TILE_F)
    fp = nk * FFN_TILE_F
    row = lambda i, k: (0, 0)
    return pl.pallas_call(
        functools.partial(_ffn_cast_body, nk=nk, d_ff=d_ff),
        grid=(1, nk),
        in_specs=[pl.BlockSpec((n, d), row), pl.BlockSpec((1, d), row), pl.BlockSpec((1, d), row),
                  pl.BlockSpec((1, d), row),
                  pl.BlockSpec((None, None, d, FFN_TILE_F), lambda i, k: (layer, which, 0, k)),
                  pl.BlockSpec((None, None, d, FFN_TILE_F), lambda i, k: (layer, which, 0, k)),
                  pl.BlockSpec((None, None, FFN_TILE_F, d), lambda i, k: (layer, which, k, 0))],
        out_specs=[pl.BlockSpec((n, d), row), pl.BlockSpec((n, d), row),
                   pl.BlockSpec((d, FFN_TILE_F), lambda i, k: (0, k)),
                   pl.BlockSpec((d, FFN_TILE_F), lambda i, k: (0, k)),
                   pl.BlockSpec((FFN_TILE_F, d), lambda i, k: (k, 0))],
        out_shape=[jax.ShapeDtypeStruct((n, d), F32)] * 2
        + [jax.ShapeDtypeStruct((d, fp), BF16)] * 2 + [jax.ShapeDtypeStruct((fp, d), BF16)],
        scratch_shapes=[pltpu.VMEM((n, d), BF16), pltpu.VMEM((n, d), F32)],
        compiler_params=_cparams("arbitrary", "arbitrary"),
        name="ffn_cast",
    )(h, g_pre, g_post, g_next, wg, wu, wd)


def _ple_body(h_ref, p_ref, wgate_ref, wproj_ref, g_ref, o_ref):
    h = h_ref[...]
    gate = jax.nn.sigmoid(_dot(h.astype(BF16), wgate_ref[...]))
    proj = _dot(p_ref[...].astype(BF16), wproj_ref[...])
    o_ref[...] = h + _rms(gate * proj, g_ref[...])


def _ple(h, p, wgate, wproj, g, layer, tm):
    n, d = h.shape
    pd = p.shape[-1]
    row = lambda i: (i, 0)
    return pl.pallas_call(
        _ple_body,
        grid=(n // tm,),
        in_specs=[pl.BlockSpec((tm, d), row), pl.BlockSpec((tm, pd), row),
                  pl.BlockSpec((None, d, d), lambda i: (layer, 0, 0)),
                  pl.BlockSpec((None, pd, d), lambda i: (layer, 0, 0)),
                  pl.BlockSpec((1, d), lambda i: (0, 0))],
        out_specs=pl.BlockSpec((tm, d), row),
        out_shape=jax.ShapeDtypeStruct((n, d), F32),
        compiler_params=_cparams("parallel"),
        name="ple",
    )(h, p, wgate, wproj, g)


ROW_SLOTS = 4
WIN_SLOTS = N_KV_SLOTS - ROW_SLOTS
ROW_LINES = ROW_SLOTS * N_KV
WIN_LINES = WIN_SLOTS * N_KV


def _kv_body(h_ref, g_ref, w_ref, cos_ref, sin_ref, rows_ref, win_ref, kv16_ref, *, tm):
    xn = _rms(h_ref[...], g_ref[...]).astype(BF16)
    kv = _dot(xn, w_ref[...])
    cos, sin = cos_ref[...], sin_ref[...]
    for slot in range(N_KV_SLOTS):
        for g in range(N_KV):
            line = slot * N_KV + g
            lo = line * HEAD_DIM
            x = kv[:, lo:lo + HEAD_DIM]
            if slot in (2, 4):
                x = _rope(x, cos, sin)
            kv16_ref[:, lo:lo + HEAD_DIM] = x.astype(BF16)
            if slot < ROW_SLOTS:
                rows_ref[pl.ds(line, tm, stride=ROW_LINES), :] = x
            else:
                win_ref[pl.ds(line - ROW_LINES, tm, stride=WIN_LINES), :] = x


def _kv_proj(h, g, w, cos, sin, tm):
    n, d = h.shape
    nw = w.shape[-1]
    nt = cos.shape[0] // tm
    row = lambda i: (i, 0)
    tab = lambda i: (i % nt, 0)
    return pl.pallas_call(
        functools.partial(_kv_body, tm=tm),
        grid=(n // tm,),
        in_specs=[pl.BlockSpec((tm, d), row), pl.BlockSpec((1, d), lambda i: (0, 0)),
                  pl.BlockSpec((d, nw), lambda i: (0, 0)),
                  pl.BlockSpec((tm, HEAD_DIM), tab), pl.BlockSpec((tm, HEAD_DIM), tab)],
        out_specs=[pl.BlockSpec((tm * ROW_LINES, HEAD_DIM), row),
                   pl.BlockSpec((tm * WIN_LINES, HEAD_DIM), row),
                   pl.BlockSpec((tm, nw), row)],
        out_shape=[jax.ShapeDtypeStruct((n * ROW_LINES, HEAD_DIM), F32),
                   jax.ShapeDtypeStruct((n * WIN_LINES, HEAD_DIM), F32),
                   jax.ShapeDtypeStruct((n, nw), BF16)],
        compiler_params=_cparams("parallel"),
        name="kv_proj",
    )(h, g, w, cos, sin)


def _qg_body(xn_ref, wq_ref, wgt_ref, cos_ref, sin_ref, qc_ref, qr_ref, gate_ref):
    xn = xn_ref[...].astype(BF16)
    q = _dot(xn, wq_ref[...])
    qc_ref[...] = q.astype(BF16)
    cos, sin = cos_ref[...], sin_ref[...]
    for hd in range(N_HEADS):
        sl = slice(hd * HEAD_DIM, (hd + 1) * HEAD_DIM)
        qr_ref[:, sl] = _rope(q[:, sl], cos, sin).astype(BF16)
    gate_ref[...] = jax.nn.sigmoid(_dot(xn, wgt_ref[...]))


def _qg_proj(xn, wq, wgt, cos, sin, layer, tm):
    n, d = xn.shape
    nq, ng = wq.shape[-1], wgt.shape[-1]
    nt = cos.shape[0] // tm
    row = lambda i: (i, 0)
    tab = lambda i: (i % nt, 0)
    return pl.pallas_call(
        _qg_body,
        grid=(n // tm,),
        in_specs=[pl.BlockSpec((tm, d), row),
                  pl.BlockSpec((None, d, nq), lambda i: (layer, 0, 0)),
                  pl.BlockSpec((None, d, ng), lambda i: (layer, 0, 0)),
                  pl.BlockSpec((tm, HEAD_DIM), tab), pl.BlockSpec((tm, HEAD_DIM), tab)],
        out_specs=[pl.BlockSpec((tm, nq), row), pl.BlockSpec((tm, nq), row), pl.BlockSpec((tm, ng), row)],
        out_shape=[jax.ShapeDtypeStruct((n, nq), BF16), jax.ShapeDtypeStruct((n, nq), BF16),
                   jax.ShapeDtypeStruct((n, ng), F32)],
        compiler_params=_cparams("parallel"),
        name="qg_proj",
    )(xn, wq, wgt, cos, sin)


def _oproj_body(h_ref, o_ref, w_ref, g_ref, out_ref):
    m = _dot(o_ref[...].astype(BF16), w_ref[...])
    out_ref[...] = h_ref[...] + _rms(m, g_ref[...])


def _o_proj(h, o, w, g, layer, tm):
    n, d = h.shape
    row = lambda i: (i, 0)
    return pl.pallas_call(
        _oproj_body,
        grid=(n // tm,),
        in_specs=[pl.BlockSpec((tm, d), row), pl.BlockSpec((tm, d), row),
                  pl.BlockSpec((None, d, d), lambda i: (layer, 0, 0)),
                  pl.BlockSpec((1, d), lambda i: (0, 0))],
        out_specs=pl.BlockSpec((tm, d), row),
        out_shape=jax.ShapeDtypeStruct((n, d), F32),
        compiler_params=_cparams("parallel"),
        name="o_proj",
    )(h, o, w, g)


def _s5gate_body(h_ref, xn_ref, y_ref, d_ref, w_ref, b_ref, g_ref, out_ref):
    y = jax.nn.gelu(y_ref[...] + d_ref[...] * xn_ref[...])
    z = jax.nn.sigmoid(_dot(y.astype(BF16), w_ref[...]) + b_ref[...])
    out_ref[...] = h_ref[...] + _rms(y * z, g_ref[...])


def _s5_gate(h, xn, y, dskip, w, b, g, layer, tm):
    n, d = h.shape
    row = lambda i: (i, 0)
    vec = lambda i: (0, 0)
    return pl.pallas_call(
        _s5gate_body,
        grid=(n // tm,),
        in_specs=[pl.BlockSpec((tm, d), row), pl.BlockSpec((tm, d), row), pl.BlockSpec((tm, d), row),
                  pl.BlockSpec((1, d), vec),
                  pl.BlockSpec((None, d, d), lambda i: (layer, 0, 0)),
                  pl.BlockSpec((1, d), vec), pl.BlockSpec((1, d), vec)],
        out_specs=pl.BlockSpec((tm, d), row),
        out_shape=jax.ShapeDtypeStruct((n, d), F32),
        compiler_params=_cparams("parallel"),
        name="s5_gate",
    )(h, xn, y, dskip, w, b, g)


def _s5_disc(lam_re, lam_im, log_dt):
    dt = jnp.exp(log_dt)
    mag = jnp.exp(lam_re * dt)
    a_re, a_im = mag * jnp.cos(lam_im * dt), mag * jnp.sin(lam_im * dt)
    den = lam_re * lam_re + lam_im * lam_im
    nr, ni = a_re - 1.0, a_im
    r_re = (nr * lam_re + ni * lam_im) / den
    r_im = (ni * lam_re - nr * lam_im) / den
    return a_re, a_im, r_re, r_im


def _cmul(ar, ai, br, bi):
    return ar * br - ai * bi, ar * bi + ai * br


def _bd_rows(m):
    shape = (LANES, S5_STATE_LANES)
    same = _iota(shape, 0) // SSM_GROUP == _iota(shape, 1) // SSM_STATE
    return jnp.where(same, jnp.concatenate([m] * S5_GROUPS_PER_STEP, axis=1), 0.0)


def _bd_cols(m):
    shape = (S5_STATE_LANES, LANES)
    same = _iota(shape, 0) // SSM_STATE == _iota(shape, 1) // SSM_GROUP
    return jnp.where(same, jnp.concatenate([m] * S5_GROUPS_PER_STEP, axis=0), 0.0)


def _s5_input_weights(lam_re, lam_im, log_dt, b_re, b_im):
    a_re, a_im, r_re, r_im = _s5_disc(lam_re, lam_im, log_dt)
    bb_re, bb_im = _cmul(r_re, r_im, b_re, b_im)
    return a_re, a_im, bb_re, bb_im


def _s5_prompt_body(u_ref, lr_re_ref, lr_im_ref, lr_dt_ref, b_re_ref, b_im_ref,
                    lc_re_ref, lc_im_ref, lc_dt_ref, c_re_ref, c_im_ref,
                    lv_re_ref, lv_im_ref, lv_dt_ref,
                    y_ref, st_ref,
                    sin_ref, toep_ref, fout_ref, v_ref, xp_ref, *, n_batch, n_chunk):
    L = S5_CHUNK
    rows = n_batch * n_chunk
    half = S5_STATE_LANES

    a_re, a_im, w_re, w_im = _s5_input_weights(lr_re_ref[...], lr_im_ref[...], lr_dt_ref[...],
                                               b_re_ref[...], b_im_ref[...])
    ws_re, ws_im = [w_re], [w_im]
    for _ in range(L - 1):
        w_re, w_im = _cmul(a_re, a_im, w_re, w_im)
        ws_re.append(w_re)
        ws_im.append(w_im)
    for s in range(L):
        k = L - 1 - s
        sin_ref[s * LANES:(s + 1) * LANES, :] = jnp.concatenate(
            [_bd_rows(ws_re[k]), _bd_rows(ws_im[k])], axis=1).astype(BF16)
    stack_re = jnp.concatenate([ws_re[L - 1 - s] for s in range(L)], axis=0)
    stack_im = jnp.concatenate([ws_im[L - 1 - s] for s in range(L)], axis=0)
    taps = _dot_3pass(stack_re, c_re_ref[...]) - _dot_3pass(stack_im, c_im_ref[...])
    tshape = (L * LANES, LANES)
    same = (_iota(tshape, 0) % LANES) // SSM_GROUP == _iota(tshape, 1) // SSM_GROUP
    taps = jnp.where(same, taps, 0.0).astype(BF16)
    toep_ref[:, LANES:] = taps
    toep_ref[:(L - 1) * LANES, :LANES] = taps[LANES:]
    toep_ref[(L - 1) * LANES:, :LANES] = jnp.zeros((LANES, LANES), BF16)
    ac_re, ac_im, _, _ = _s5_disc(lc_re_ref[...], lc_im_ref[...], lc_dt_ref[...])
    ca_re, ca_im = c_re_ref[...], c_im_ref[...]
    for t in range(L):
        ca_re, ca_im = _cmul(ac_re, ac_im, ca_re, ca_im)
        fout_ref[:, t * LANES:(t + 1) * LANES] = jnp.concatenate(
            [_bd_cols(ca_re), -_bd_cols(ca_im)], axis=0).astype(BF16)
    al_re, al_im, _, _ = _s5_disc(lv_re_ref[...], lv_im_ref[...], lv_dt_ref[...])
    for _ in range(int(math.log2(L))):
        al_re, al_im = _cmul(al_re, al_im, al_re, al_im)

    xb = jnp.concatenate([u_ref[pl.ds(s, rows, stride=L), :] for s in range(L)], axis=1).astype(BF16)
    v_ref[...] = _dot(xb, sin_ref[...])

    def step(c, carry):
        new = []
        for b in range(n_batch):
            r = b * n_chunk + c
            x_re, x_im = carry[b]
            xp_ref[pl.ds(r, 1), :] = jnp.concatenate([x_re, x_im], axis=1)
            inc = v_ref[pl.ds(r, 1), :]
            new.append((al_re * x_re - al_im * x_im + inc[:, :half],
                        al_re * x_im + al_im * x_re + inc[:, half:]))
        return tuple(new)

    zero = jnp.zeros((1, half), F32)
    final = lax.fori_loop(0, n_chunk, step, tuple((zero, zero) for _ in range(n_batch)), unroll=4)
    for b in range(n_batch):
        st_ref[b:b + 1, :] = jnp.concatenate(final[b], axis=1)

    xpb = xp_ref[...].astype(BF16)
    for t in range(0, L, 2):
        pair = (_dot(xb[:, :(t + 2) * LANES], toep_ref[(L - 2 - t) * LANES:, :])
                + _dot(xpb, fout_ref[:, t * LANES:(t + 2) * LANES]))
        y_ref[pl.ds(t, rows, stride=L), :] = pair[:, :LANES]
        y_ref[pl.ds(t + 1, rows, stride=L), :] = pair[:, LANES:]


def _s5_layouts(lam_re, lam_im, log_dt, b_re, b_im, c_re, c_im):
    n_g = lam_re.shape[0]
    ldt = jnp.broadcast_to(log_dt[:, None], lam_re.shape)
    rep = lambda a: jnp.repeat(a, SSM_GROUP, axis=0)
    rows = (rep(lam_re), rep(lam_im), rep(ldt),
            b_re.transpose(0, 2, 1).reshape(n_g * SSM_GROUP, SSM_STATE),
            b_im.transpose(0, 2, 1).reshape(n_g * SSM_GROUP, SSM_STATE))
    cols = (rep(lam_re).T, rep(lam_im).T, rep(ldt).T,
            c_re.reshape(n_g * SSM_GROUP, SSM_STATE).T, c_im.reshape(n_g * SSM_GROUP, SSM_STATE).T)
    n_j = n_g // S5_GROUPS_PER_STEP
    lanes = tuple(a.reshape(n_j, 1, S5_STATE_LANES) for a in (lam_re, lam_im, ldt))
    return rows, cols, lanes


def _s5_specs():
    rspec = pl.BlockSpec((LANES, SSM_STATE), lambda j: (j, 0))
    cspec = pl.BlockSpec((SSM_STATE, LANES), lambda j: (0, j))
    vspec = pl.BlockSpec((None, 1, S5_STATE_LANES), lambda j: (j, 0, 0))
    return [rspec] * 5 + [cspec] * 5 + [vspec] * 3


def _s5_prompt(u, n_batch, layouts):
    n, d = u.shape
    assert d % LANES == 0 and n % (n_batch * S5_CHUNK) == 0 and S5_CHUNK & (S5_CHUNK - 1) == 0
    n_j = d // LANES
    n_chunk = n // n_batch // S5_CHUNK
    rows = n // S5_CHUNK
    rows_p, cols_p, lanes_p = layouts
    col = lambda j: (0, j)
    return pl.pallas_call(
        functools.partial(_s5_prompt_body, n_batch=n_batch, n_chunk=n_chunk),
        grid=(n_j,),
        in_specs=[pl.BlockSpec((n, LANES), col)] + _s5_specs(),
        out_specs=[pl.BlockSpec((n, LANES), col),
                   pl.BlockSpec((None, n_batch, 2 * S5_STATE_LANES), lambda j: (j, 0, 0))],
        out_shape=[jax.ShapeDtypeStruct((n, d), F32),
                   jax.ShapeDtypeStruct((n_j, n_batch, 2 * S5_STATE_LANES), F32)],
        scratch_shapes=[pltpu.VMEM((S5_CHUNK * LANES, 2 * S5_STATE_LANES), BF16),
                        pltpu.VMEM((S5_CHUNK * LANES, 2 * LANES), BF16),
                        pltpu.VMEM((2 * S5_STATE_LANES, S5_CHUNK * LANES), BF16),
                        pltpu.VMEM((rows, 2 * S5_STATE_LANES), F32),
                        pltpu.VMEM((rows, 2 * S5_STATE_LANES), F32)],
        compiler_params=_cparams("parallel"),
        name="s5_prompt",
    )(u, *rows_p, *cols_p, *lanes_p)


def _s5_step_body(u_ref, hre_ref, him_ref, lr_re_ref, lr_im_ref, lr_dt_ref, b_re_ref, b_im_ref,
                  lc_re_ref, lc_im_ref, lc_dt_ref, c_re_ref, c_im_ref,
                  lv_re_ref, lv_im_ref, lv_dt_ref, y_ref, ore_ref, oim_ref):
    _, _, bb_re, bb_im = _s5_input_weights(lr_re_ref[...], lr_im_ref[...], lr_dt_ref[...],
                                           b_re_ref[...], b_im_ref[...])
    a_re, a_im, _, _ = _s5_disc(lv_re_ref[...], lv_im_ref[...], lv_dt_ref[...])
    u = u_ref[...]
    h_re, h_im = hre_ref[...], him_ref[...]
    x_re = _dot(u, _bd_rows(bb_re), precision=HIGHEST) + (a_re * h_re - a_im * h_im)
    x_im = _dot(u, _bd_rows(bb_im), precision=HIGHEST) + (a_re * h_im + a_im * h_re)
    ore_ref[...] = x_re
    oim_ref[...] = x_im
    y_ref[...] = (_dot(x_re, _bd_cols(c_re_ref[...]), precision=HIGHEST)
                  - _dot(x_im, _bd_cols(c_im_ref[...]), precision=HIGHEST))


def _s5_step(u, h_re, h_im, layouts):
    nb, d = u.shape
    rows_p, cols_p, lanes_p = layouts
    col = lambda j: (0, j)
    sspec = pl.BlockSpec((nb, S5_STATE_LANES), col)
    return pl.pallas_call(
        _s5_step_body,
        grid=(d // LANES,),
        in_specs=[pl.BlockSpec((nb, LANES), col), sspec, sspec] + _s5_specs(),
        out_specs=[pl.BlockSpec((nb, LANES), col), sspec, sspec],
        out_shape=[jax.ShapeDtypeStruct((nb, d), F32),
                   jax.ShapeDtypeStruct(h_re.shape, F32), jax.ShapeDtypeStruct(h_im.shape, F32)],
        compiler_params=_cparams("parallel"),
        name="s5_step",
    )(u, h_re, h_im, *rows_p, *cols_p, *lanes_p)


PAGE_ROWS = 128
PAGE_LINES = PAGE_ROWS * ROW_LINES
CHUNKS_PER_PAGE = PAGE_ROWS // CMP_STRIDE
CMP_PAGES_PER_STEP = 16
CMP_LINES = 2 * N_KV


def _compress_body(pt_ref, *refs, n_pages):
    del pt_ref
    k_pages = CMP_PAGES_PER_STEP
    srcs = refs[:k_pages]
    (pek_ref, pev_ref, w1k_ref, w1v_ref, w2k_ref, w2v_ref, kc_ref, vc_ref, buf_ref, shift_ref) = refs[k_pages:]
    p = pl.program_id(1)
    for k in range(k_pages):
        r0 = pl.multiple_of((p * k_pages + k) * PAGE_ROWS, PAGE_ROWS)
        for sg in range(CMP_LINES):
            buf_ref[sg, pl.ds(r0, PAGE_ROWS), :] = srcs[k][pl.ds(sg, PAGE_ROWS, stride=ROW_LINES), :]

    @pl.when(p == n_pages // k_pages - 1)
    def _():
        n_chunk = n_pages * CHUNKS_PER_PAGE
        half = CMP_STRIDE
        for slot, (pe_ref, w1_ref, w2_ref, out_ref) in enumerate(
                ((pek_ref, w1k_ref, w2k_ref, kc_ref), (pev_ref, w1v_ref, w2v_ref, vc_ref))):
            for g in range(N_KV):
                sg = slot * N_KV + g
                xs = [buf_ref[sg, pl.ds(l, n_chunk, stride=CMP_STRIDE), :] for l in range(half)]
                as_first = jnp.concatenate(
                    [(x + pe_ref[l:l + 1, :]).astype(BF16) for l, x in enumerate(xs)], axis=1)
                as_second = jnp.concatenate(
                    [(x + pe_ref[half + l:half + l + 1, :]).astype(BF16) for l, x in enumerate(xs)], axis=1)
                first = _dot(as_first, w1_ref[:half * HEAD_DIM, :])
                shift_ref[0:n_chunk, :] = _dot(as_second, w1_ref[half * HEAD_DIM:, :])
                shift_ref[n_chunk:n_chunk + 8, :] = jnp.zeros((8, HEAD_DIM), F32)
                hid = first + shift_ref[1:n_chunk + 1, :]
                out = _dot(jax.nn.silu(hid).astype(BF16), w2_ref[...])
                live = _iota(out.shape, 0) < n_chunk - 1
                out_ref[g] = jnp.where(live, out, 0.0).astype(BF16)


def _compress(src, page_ids, pe_k, pe_v, w1k, w1v, w2k, w2v):
    nb, n_pages = page_ids.shape
    k_pages = CMP_PAGES_PER_STEP
    assert n_pages % k_pages == 0
    n_chunk = n_pages * CHUNKS_PER_PAGE
    const = lambda b, p, pt: (0, 0)
    page = lambda k: pl.BlockSpec((PAGE_LINES, HEAD_DIM),
                                  lambda b, p, pt: (pt[b * n_pages + p * k_pages + k], 0))
    out_spec = pl.BlockSpec((None, N_KV, n_chunk, HEAD_DIM), lambda b, p, pt: (b, 0, 0, 0))
    grid_spec = pltpu.PrefetchScalarGridSpec(
        num_scalar_prefetch=1,
        grid=(nb, n_pages // k_pages),
        in_specs=[page(k) for k in range(k_pages)]
        + [pl.BlockSpec(pe_k.shape, const), pl.BlockSpec(pe_v.shape, const),
           pl.BlockSpec(w1k.shape, const), pl.BlockSpec(w1v.shape, const),
           pl.BlockSpec(w2k.shape, const), pl.BlockSpec(w2v.shape, const)],
        out_specs=[out_spec, out_spec],
        scratch_shapes=[pltpu.VMEM((CMP_LINES, n_pages * PAGE_ROWS, HEAD_DIM), F32),
                        pltpu.VMEM((n_chunk + 8, HEAD_DIM), F32)])
    return pl.pallas_call(
        functools.partial(_compress_body, n_pages=n_pages),
        grid_spec=grid_spec,
        out_shape=[jax.ShapeDtypeStruct((nb, N_KV, n_chunk, HEAD_DIM), BF16)] * 2,
        compiler_params=_cparams("parallel", "arbitrary"),
        name="compress",
    )(page_ids.reshape(-1), *([src] * k_pages), pe_k, pe_v, w1k, w1v, w2k, w2v)


def _masked_softmax(s, mask):
    s = jnp.where(mask, s, NEG)
    m = jnp.max(s, axis=-1, keepdims=True)
    e = jnp.where(mask, jnp.exp(s - m), 0.0)
    return e / jnp.maximum(jnp.sum(e, axis=-1, keepdims=True), 1e-30)


def _overlap(n_rows, n_lanes, n_cmp, n_sel):
    ci, sj = _iota((n_rows, n_lanes), 0), _iota((n_rows, n_lanes), 1)
    hit = ((ci * CMP_STRIDE < sj * SEL_LEN + SEL_LEN) & (ci * CMP_STRIDE + CMP_LEN > sj * SEL_LEN)
           & (ci < n_cmp) & (sj < n_sel))
    return jnp.where(hit, 1.0, 0.0).astype(BF16)


def _importance(imp, lane, qpos, n_sel):
    cur = qpos // SEL_LEN
    forced = (lane == 0) | (lane == cur) | (lane == cur - 1)
    imp = jnp.where(lane * SEL_LEN <= qpos, imp + jnp.where(forced, FORCE_BONUS, 0.0), NEG)
    return jnp.where(lane < n_sel, imp, LOWEST)


def _dot_tn(a, b):
    return lax.dot_general(a, b, (((0,), (0,)), ((), ())), preferred_element_type=F32)


LOG2E = math.log2(math.e)


def _online_attend_t(q, k_ref, v_ref, lo, hi, ck, bias_fn, s_even_ref, s_odd_ref, ml_ref, acc_ref):
    cols = q.shape[0]

    def chunk_start(c):
        return pl.multiple_of(jnp.minimum(c, hi - 1) * ck, ck)

    def scores(c, s_ref):
        k0 = chunk_start(c)
        bias = jnp.concatenate([bias_fn(k0)] * (cols // Q_BLOCK), axis=1) + jnp.where(c < hi, 0.0, NEG)
        s_ref[:ck, :] = _dot_nt(k_ref[pl.ds(k0, ck), :], q) * (ATT_SCALE * LOG2E) + bias

    def update(c, s_ref):
        m, l = ml_ref[0:1, :], ml_ref[1:2, :]
        s = s_ref[:ck, :]
        m_new = jnp.maximum(m, jnp.max(s, axis=0, keepdims=True))
        alpha = jnp.exp2(m - m_new)
        e = jnp.exp2(s - m_new)
        ml_ref[0:1, :] = m_new
        ml_ref[1:2, :] = alpha * l + jnp.sum(e, axis=0, keepdims=True)
        acc_ref[...] = alpha * acc_ref[...] + _dot_tn(v_ref[pl.ds(chunk_start(c), ck), :], e.astype(BF16))

    def body(i, carry):
        c = lo + 2 * i
        scores(c + 1, s_odd_ref)
        update(c, s_even_ref)
        scores(c + 2, s_even_ref)
        update(c + 1, s_odd_ref)
        return carry

    scores(lo, s_even_ref)
    ml_ref[0:1, :] = jnp.full((1, cols), NEG, F32)
    ml_ref[1:2, :] = jnp.zeros((1, cols), F32)
    acc_ref[...] = jnp.zeros((HEAD_DIM, cols), F32)
    lax.fori_loop(0, (hi - lo) // 2, body, 0)

    @pl.when((hi - lo) % 2 == 1)
    def _():
        update(hi - 1, s_even_ref)

    m, l = ml_ref[0:1, :], ml_ref[1:2, :]
    return jnp.where(m > NEG, acc_ref[...] / jnp.maximum(l, 1e-30), 0.0)


Q_BLOCK = 128
SEL_KEY_CHUNK = 256
WIN_KEY_CHUNK = 128


def _nsa_prompt_body(qc_ref, qr_ref, gate_ref, ksel_ref, vsel_ref, kwin_ref, vwin_ref,
                     kcmp_ref, vcmp_ref, o_ref, sel_ref, s_even_ref, s_odd_ref, ml_ref, acc_ref,
                     *, n_cmp, n_sel):
    qi = pl.program_id(2)
    q0 = qi * Q_BLOCK
    nh = HEADS_PER_KV
    stack = lambda ref: jnp.concatenate(
        [ref[:, hd * HEAD_DIM:(hd + 1) * HEAD_DIM] for hd in range(nh)], axis=0)
    per_head = lambda x: jnp.concatenate([x] * nh, axis=1)
    qc, qr = stack(qc_ref), stack(qr_ref)
    n_cmp_rows = kcmp_ref.shape[0]
    n_sel_rows = sel_ref.shape[0]

    cshape = (n_cmp_rows, Q_BLOCK)
    cblk, cq = _iota(cshape, 0), q0 + _iota(cshape, 1)
    cmask = per_head(jnp.where((cblk < n_cmp) & (cblk * CMP_STRIDE + CMP_LEN - 1 <= cq), 1.0, 0.0)) > 0.5
    s = jnp.where(cmask, _dot_nt(kcmp_ref[...], qc) * ATT_SCALE, NEG)
    e = jnp.where(cmask, jnp.exp(s - jnp.max(s, axis=0, keepdims=True)), 0.0)
    p_cmp = (e / jnp.maximum(jnp.sum(e, axis=0, keepdims=True), 1e-30)).astype(BF16)
    o_cmp = _dot_tn(vcmp_ref[...], p_cmp)

    oshape = (n_sel_rows, n_cmp_rows)
    sj, ci = _iota(oshape, 0), _iota(oshape, 1)
    hit = ((ci * CMP_STRIDE < sj * SEL_LEN + SEL_LEN) & (ci * CMP_STRIDE + CMP_LEN > sj * SEL_LEN)
           & (ci < n_cmp) & (sj < n_sel))
    imp_h = _dot(jnp.where(hit, 1.0, 0.0).astype(BF16), p_cmp)
    imp = imp_h[:, 0:Q_BLOCK]
    for hd in range(1, nh):
        imp = imp + imp_h[:, hd * Q_BLOCK:(hd + 1) * Q_BLOCK]
    ishape = (n_sel_rows, Q_BLOCK)
    blk = _iota(ishape, 0)
    imp = _importance(imp, blk, q0 + _iota(ishape, 1), n_sel)
    rank = jnp.zeros(ishape, F32)
    for k in range(n_sel):
        row = imp[k:k + 1, :]
        rank += jnp.where((row > imp) | ((row == imp) & (blk > k)), 1.0, 0.0)
    sel_ref[...] = jnp.where((rank < min(SEL_TOP, n_sel)) & (blk < n_sel), 1.0, 0.0)

    def sel_mask(k0):
        shape = (SEL_KEY_CHUNK, Q_BLOCK)
        j0 = k0 // SEL_LEN
        chosen = jnp.concatenate(
            [jnp.broadcast_to(sel_ref[pl.ds(j0 + i, 1), :], (SEL_LEN, Q_BLOCK))
             for i in range(SEL_KEY_CHUNK // SEL_LEN)], axis=0)
        live = (k0 + _iota(shape, 0) <= q0 + _iota(shape, 1)) & (chosen > 0.5)
        return jnp.where(live, 0.0, NEG)

    def win_mask(k0):
        shape = (WIN_KEY_CHUNK, Q_BLOCK)
        kpos = k0 + _iota(shape, 0)
        qp = q0 + _iota(shape, 1)
        return jnp.where((kpos <= qp) & (kpos > qp - WINDOW), 0.0, NEG)

    sel_chunks = (q0 + Q_BLOCK + SEL_KEY_CHUNK - 1) // SEL_KEY_CHUNK
    o_sel = _online_attend_t(qr, ksel_ref, vsel_ref, 0, sel_chunks, SEL_KEY_CHUNK, sel_mask,
                             s_even_ref, s_odd_ref, ml_ref, acc_ref)
    win_lo = jnp.maximum(q0 - WINDOW, 0) // WIN_KEY_CHUNK
    win_hi = (q0 + Q_BLOCK + WIN_KEY_CHUNK - 1) // WIN_KEY_CHUNK
    o_win = _online_attend_t(qr, kwin_ref, vwin_ref, win_lo, win_hi, WIN_KEY_CHUNK, win_mask,
                             s_even_ref, s_odd_ref, ml_ref, acc_ref)

    gates = jnp.transpose(gate_ref[...])
    for hd in range(nh):
        cols = slice(hd * Q_BLOCK, (hd + 1) * Q_BLOCK)
        c0 = hd * N_BRANCH
        o = (o_cmp[:, cols] * gates[c0:c0 + 1, :] + o_sel[:, cols] * gates[c0 + 1:c0 + 2, :]
             + o_win[:, cols] * gates[c0 + 2:c0 + 3, :])
        o_ref[:, hd * HEAD_DIM:(hd + 1) * HEAD_DIM] = jnp.transpose(o).astype(BF16)


def _nsa_prompt(qc, qr, gates, kv, kcmp, vcmp):
    bsz, t, dq = qc.shape
    assert t % SEL_KEY_CHUNK == 0 and t % Q_BLOCK == 0 and kcmp.shape[2] % 8 == 0
    n_cmp = (t - CMP_LEN) // CMP_STRIDE + 1
    n_sel = -(-t // SEL_LEN)
    gw = HEADS_PER_KV * HEAD_DIM
    qspec = pl.BlockSpec((None, Q_BLOCK, gw), lambda b, g, i: (b, i, g))
    kvspec = lambda slot: pl.BlockSpec((None, t, HEAD_DIM), lambda b, g, i: (b, 0, slot * N_KV + g))
    cspec = pl.BlockSpec((None, None, kcmp.shape[2], HEAD_DIM), lambda b, g, i: (b, g, 0, 0))
    return pl.pallas_call(
        functools.partial(_nsa_prompt_body, n_cmp=n_cmp, n_sel=n_sel),
        grid=(bsz, N_KV, t // Q_BLOCK),
        in_specs=[qspec, qspec, pl.BlockSpec((None, Q_BLOCK, LANES), lambda b, g, i: (b, i, g)),
                  kvspec(2), kvspec(3), kvspec(4), kvspec(5), cspec, cspec],
        out_specs=qspec,
        out_shape=jax.ShapeDtypeStruct((bsz, t, dq), BF16),
        scratch_shapes=[pltpu.VMEM((-(-n_sel // 8) * 8, Q_BLOCK), F32)]
        + [pltpu.VMEM((max(SEL_KEY_CHUNK, WIN_KEY_CHUNK), HEADS_PER_KV * Q_BLOCK), F32)] * 2
        + [pltpu.VMEM((8, HEADS_PER_KV * Q_BLOCK), F32), pltpu.VMEM((HEAD_DIM, HEADS_PER_KV * Q_BLOCK), F32)],
        compiler_params=_cparams("parallel", "parallel", "arbitrary"),
        name="nsa_prompt",
    )(qc, qr, gates, kv, kv, kv, kv, kcmp, vcmp)


def _select_body(qc_ref, kcmp_ref, vcmp_ref, ocmp_ref, idx_ref, *, qpos, n_cmp, n_sel):
    q = qc_ref[...]
    nc = kcmp_ref.shape[0]
    sel_lanes = -(-n_sel // LANES) * LANES
    ci = _iota((HEADS_PER_KV, nc), 1)
    cmask = (ci < n_cmp) & (ci * CMP_STRIDE + CMP_LEN - 1 <= qpos)
    p_cmp = _masked_softmax(_dot_nt(q, kcmp_ref[...]) * ATT_SCALE, cmask).astype(BF16)
    ocmp_ref[...] = _dot(p_cmp, vcmp_ref[...])

    imp = jnp.sum(_dot(p_cmp, _overlap(nc, sel_lanes, n_cmp, n_sel)), axis=0, keepdims=True)
    lane = _iota((1, sel_lanes), 1)
    imp = _importance(imp, lane, qpos, n_sel)
    sq = (sel_lanes, sel_lanes)
    ri, cj = _iota(sq, 0), _iota(sq, 1)
    impc = jnp.sum(jnp.where(ri == cj, imp, 0.0), axis=1, keepdims=True)
    n_top = min(SEL_TOP, n_sel)
    before_col = jnp.where((impc > imp) | ((impc == imp) & (ri < cj)), 1.0, 0.0)
    before_row = jnp.where((imp > impc) | ((imp == impc) & (cj < ri)), 1.0, 0.0)
    sel_lane = (jnp.sum(before_col, axis=0, keepdims=True) < n_top) & (lane < n_sel)
    sel_row = (jnp.sum(before_row, axis=1, keepdims=True) < n_top) & (_iota((sel_lanes, 1), 0) < n_sel)
    slot_of = jnp.sum(jnp.where((ri < cj) & sel_row, 1.0, 0.0), axis=0, keepdims=True)
    lshape = (SEL_TOP, sel_lanes)
    onehot = jnp.where(sel_lane & (slot_of == _iota(lshape, 0).astype(F32)), 1.0, 0.0)
    idx = jnp.sum(onehot * _iota(lshape, 1).astype(F32), axis=1, keepdims=True)
    idx_ref[...] = jnp.broadcast_to(idx, (SEL_TOP, LANES)).astype(jnp.int32)


def _nsa_select(qc, kcmp, vcmp, qpos, n_cmp, n_sel):
    nb = qc.shape[0]
    nc = kcmp.shape[2]
    hspec = pl.BlockSpec((None, None, HEADS_PER_KV, HEAD_DIM), lambda b, g: (b, g, 0, 0))
    cspec = pl.BlockSpec((None, None, nc, HEAD_DIM), lambda b, g: (b, g, 0, 0))
    return pl.pallas_call(
        functools.partial(_select_body, qpos=qpos, n_cmp=n_cmp, n_sel=n_sel),
        grid=(nb, N_KV),
        in_specs=[hspec, cspec, cspec],
        out_specs=[hspec, pl.BlockSpec((None, None, SEL_TOP, LANES), lambda b, g: (b, g, 0, 0))],
        out_shape=[jax.ShapeDtypeStruct((nb, N_KV, HEADS_PER_KV, HEAD_DIM), F32),
                   jax.ShapeDtypeStruct((nb, N_KV, SEL_TOP, LANES), jnp.int32)],
        compiler_params=_cparams("parallel", "parallel"),
        name="nsa_select",
    )(qc, kcmp, vcmp)


def _attend_plus_new(q, k, v, valid, k_new, v_new, new_ok):
    qf = q.astype(F32)
    s = _dot_nt(q, k.astype(BF16)) * ATT_SCALE
    s_new = jnp.sum(qf * k_new.astype(BF16).astype(F32), axis=1, keepdims=True) * ATT_SCALE
    s = jnp.where(valid, s, NEG)
    s_new = jnp.where(new_ok, s_new, NEG)
    m = jnp.maximum(jnp.max(s, axis=-1, keepdims=True), s_new)
    e = jnp.where(valid, jnp.exp(s - m), 0.0)
    e_new = jnp.where(new_ok, jnp.exp(s_new - m), 0.0)
    den = jnp.maximum(jnp.sum(e, axis=-1, keepdims=True) + e_new, 1e-30)
    p = (e / den).astype(BF16)
    p_new = (e_new / den).astype(BF16).astype(F32)
    return _dot(p, v.astype(BF16)) + p_new * v_new.astype(BF16).astype(F32)


SEL_BLOCK_LINES = SEL_LEN * ROW_LINES


def _attend_body(idx_ref, pt_ref, qr_ref, gate_ref, ocmp_ref, *refs, qpos, past_len, n_win):
    del pt_ref
    blocks = refs[:SEL_TOP]
    win_ref, rows_new_ref, win_new_ref, o_ref = refs[SEL_TOP:]
    b, g = pl.program_id(0), pl.program_id(1)
    q = qr_ref[...]
    n_keys = SEL_TOP * SEL_LEN
    base = (b * N_KV + g) * SEL_TOP
    lane = _iota((1, n_keys), 1)
    kpos = jnp.zeros((1, n_keys), jnp.int32)
    n_new = jnp.int32(0)
    for t in range(SEL_TOP):
        j = idx_ref[base + t]
        kpos = jnp.where(lane // SEL_LEN == t, j * SEL_LEN + lane % SEL_LEN, kpos)
        n_new = n_new + jnp.where(j == qpos // SEL_LEN, 1, 0)
    valid = (kpos <= qpos) & (kpos < past_len)
    new_ok = (jnp.zeros((HEADS_PER_KV, 1), jnp.int32) + n_new) > 0
    line = lambda ref, slot, n, per_token: ref[pl.ds(slot * N_KV + g, n, stride=per_token), :]
    k_sel = jnp.concatenate([line(blk, 2, SEL_LEN, ROW_LINES) for blk in blocks], axis=0)
    v_sel = jnp.concatenate([line(blk, 3, SEL_LEN, ROW_LINES) for blk in blocks], axis=0)
    o_sel = _attend_plus_new(q, k_sel, v_sel, valid, line(rows_new_ref, 2, 1, ROW_LINES),
                             line(rows_new_ref, 3, 1, ROW_LINES), new_ok)

    wpos = (past_len - n_win) + _iota((1, n_win), 1)
    wvalid = (wpos <= qpos) & (wpos > qpos - WINDOW) & (wpos >= 0)
    always = jnp.zeros((HEADS_PER_KV, 1), jnp.int32) == 0
    o_win = _attend_plus_new(q, line(win_ref, 0, n_win, WIN_LINES), line(win_ref, 1, n_win, WIN_LINES),
                             wvalid, line(win_new_ref, 0, 1, WIN_LINES), line(win_new_ref, 1, 1, WIN_LINES),
                             always)

    gates = gate_ref[...]
    o_ref[...] = ocmp_ref[...] * gates[:, 0:1] + o_sel * gates[:, 1:2] + o_win * gates[:, 2:3]


def _nsa_attend(idx, page_table, qr, gates, ocmp, cache, cache_win, rows_new, win_new, past_len):
    nb, n_pages = page_table.shape
    n_win = cache_win.shape[0] // (nb * WIN_LINES)
    halves = PAGE_ROWS // SEL_LEN
    n_past_blocks = n_pages * halves

    def sel_block(t):
        def index(b, g, idx_ref, pt_ref):
            j = jnp.minimum(idx_ref[(b * N_KV + g) * SEL_TOP + t], n_past_blocks - 1)
            return (pt_ref[b * n_pages + j // halves] * halves + j % halves, 0)
        return pl.BlockSpec((SEL_BLOCK_LINES, HEAD_DIM), index)

    hspec = pl.BlockSpec((None, None, HEADS_PER_KV, HEAD_DIM), lambda b, g, i, p: (b, g, 0, 0))
    grid_spec = pltpu.PrefetchScalarGridSpec(
        num_scalar_prefetch=2,
        grid=(nb, N_KV),
        in_specs=[hspec, hspec, hspec] + [sel_block(t) for t in range(SEL_TOP)]
        + [pl.BlockSpec((n_win * WIN_LINES, HEAD_DIM), lambda b, g, i, p: (b, 0)),
           pl.BlockSpec((None, ROW_LINES, HEAD_DIM), lambda b, g, i, p: (b, 0, 0)),
           pl.BlockSpec((None, WIN_LINES, HEAD_DIM), lambda b, g, i, p: (b, 0, 0))],
        out_specs=hspec)
    return pl.pallas_call(
        functools.partial(_attend_body, qpos=past_len, past_len=past_len, n_win=n_win),
        grid_spec=grid_spec,
        out_shape=jax.ShapeDtypeStruct((nb, N_KV, HEADS_PER_KV, HEAD_DIM), F32),
        compiler_params=_cparams("parallel", "arbitrary"),
        name="nsa_attend",
    )(idx, page_table.reshape(-1), qr, gates, ocmp, *([cache] * SEL_TOP), cache_win, rows_new, win_new)


PROMPT_FFN_ROWS = 512
PROMPT_ROWS = 512


def _rope_tables(pos):
    half = HEAD_DIM // 2
    inv = jnp.exp(-math.log(ROPE_THETA) * jnp.arange(half, dtype=F32) / half)
    ang = pos.astype(F32)[:, None] * inv[None, :]
    cos, sin = jnp.cos(ang), jnp.sin(ang)
    return jnp.concatenate([cos, cos], axis=1), jnp.concatenate([-sin, sin], axis=1)


def _prepare_weights(prm):
    nq = N_HEADS * HEAD_DIM
    w_qg = prm["attn_w_qg"]
    nb, d = w_qg.shape[:2]
    per_kv = HEADS_PER_KV * N_BRANCH
    w_gate = w_qg[:, :, nq:].reshape(nb, d, N_KV, per_kv)
    w_gate = jnp.pad(w_gate, ((0, 0), (0, 0), (0, 0), (0, LANES - per_kv))).reshape(nb, d, N_KV * LANES)
    return dict(
        ffn16={},
        ple_gate=prm["ple_w_gate"].astype(BF16), ple_proj=prm["ple_w_proj"].astype(BF16),
        glu=prm["glu_w"].astype(BF16), kv=prm["w_kv"].astype(BF16),
        q=w_qg[:, :, :nq].astype(BF16), gate=w_gate.astype(BF16), o=prm["attn_w_o"].astype(BF16),
        w1k=prm["cmp_w1_k"].astype(BF16), w1v=prm["cmp_w1_v"].astype(BF16),
        w2k=prm["cmp_w2_k"].astype(BF16), w2v=prm["cmp_w2_v"].astype(BF16),
        s5=[_s5_layouts(prm["ssm_lam_re"][i], prm["ssm_lam_im"][i], prm["ssm_log_dt"][i],
                        prm["ssm_b_re"][i], prm["ssm_b_im"][i], prm["ssm_c_re"][i], prm["ssm_c_im"][i])
            for i in range(N_A_LAYERS)])


def _finish(gen):
    try:
        while True:
            next(gen)
    except StopIteration as done:
        return done.value


def _run_group(x, p, start, state, past, prm, w):
    bsz, t, d = x.shape
    n = bsz * t
    prompt = past is None
    tm_ffn = PROMPT_FFN_ROWS if prompt else n
    tm = PROMPT_ROWS if prompt else n
    depth = prm["norm_g"].shape[0]
    cos, sin = _rope_tables(start + jnp.arange(t))
    if not prompt:
        cos, sin = jnp.broadcast_to(cos, (n, HEAD_DIM)), jnp.broadcast_to(sin, (n, HEAD_DIM))
    h = x.reshape(n, d)
    ssm_re, ssm_im = [], []
    rows = win = kv16 = kcmp = vcmp = None

    w32 = (prm["ffn_w_gate"], prm["ffn_w_up"], prm["ffn_w_down"])

    def ffn(h, g_pre, g_post, g_next, layer, which):
        if (layer, which) not in w["ffn16"]:
            assert g_next is not None
            h_new, xn, *w["ffn16"][layer, which] = _ffn_cast(h, g_pre, g_post, g_next, *w32, layer, which)
            return h_new, xn
        nxt = (layer, which + 1) if which == 0 else (layer + 1, 0)
        if not prompt or nxt[0] == depth:
            return _ffn(h, g_pre, g_post, g_next, *w["ffn16"][layer, which], tm_ffn)
        h_new, xn, *w["ffn16"][nxt] = _ffn_convert(h, g_pre, g_post, g_next, *w["ffn16"][layer, which],
                                                   tm_ffn, *w32, *nxt)
        return h_new, xn

    if not prompt:
        cache_kv, cache_win, page_table = past
        past_len = page_table.shape[1] * cache_kv.shape[1]
        cache_lines = cache_kv.reshape(-1, HEAD_DIM)
        win_lines = cache_win.reshape(-1, HEAD_DIM)
    for i in range(depth):
        g = prm["norm_g"][i]
        h, xn = ffn(h, g[0:1], g[1:2], g[2:3], i, 0)
        if i == 0:
            yield
        if i < N_A_LAYERS:
            if prompt:
                y, st = _s5_prompt(xn, bsz, w["s5"][i])
                st = st.reshape(d // LANES, bsz, 2, S5_GROUPS_PER_STEP, SSM_STATE).transpose(2, 1, 0, 3, 4)
                st = st.reshape(2, bsz, d // SSM_GROUP, SSM_STATE)
                ssm_re.append(st[0])
                ssm_im.append(st[1])
            else:
                y, s_re, s_im = _s5_step(xn, state[0][i].reshape(bsz, -1), state[1][i].reshape(bsz, -1),
                                         w["s5"][i])
                ssm_re.append(s_re.reshape(state[0][i].shape))
                ssm_im.append(s_im.reshape(state[1][i].shape))
            h = _s5_gate(h, xn, y, prm["ssm_d"][i][None], w["glu"], prm["glu_b"][i][None], g[3:4], i, tm)
        else:
            j = i - N_A_LAYERS
            qc, qr, gates = _qg_proj(xn, w["q"], w["gate"], cos, sin, j, tm)
            if prompt:
                o = _nsa_prompt(qc.reshape(bsz, t, -1), qr.reshape(bsz, t, -1), gates.reshape(bsz, t, -1),
                                kv16.reshape(bsz, t, -1), kcmp, vcmp).reshape(n, -1)
            else:
                heads = lambda a: a.reshape(bsz, N_KV, HEADS_PER_KV, HEAD_DIM)
                n_cmp = (past_len + t - CMP_LEN) // CMP_STRIDE + 1
                n_sel = -(-(past_len + t) // SEL_LEN)
                ocmp, idx = _nsa_select(heads(qc), kcmp, vcmp, past_len, n_cmp, n_sel)
                gsm = gates.reshape(bsz, N_KV, LANES)[:, :, :HEADS_PER_KV * N_BRANCH]
                gsm = gsm.reshape(bsz, N_KV, HEADS_PER_KV, N_BRANCH)
                gsm = jnp.pad(gsm, ((0, 0), (0, 0), (0, 0), (0, LANES - N_BRANCH)))
                o = _nsa_attend(idx[..., 0].reshape(-1), page_table, heads(qr), gsm, ocmp,
                                cache_lines, win_lines, rows_l.reshape(bsz, ROW_LINES, HEAD_DIM),
                                win_l.reshape(bsz, WIN_LINES, HEAD_DIM), past_len).reshape(n, -1)
            h = _o_proj(h, o, w["o"], g[3:4], j, tm)
        h, _ = ffn(h, g[4:5], g[5:6], None, i, 1)
        h = _ple(h, p[i].reshape(n, -1), w["ple_gate"], w["ple_proj"], g[6:7], i, tm)
        if i == N_A_LAYERS - 1:
            rows_l, win_l, kv16 = _kv_proj(h, prm["kv_norm_g"][None], w["kv"], cos, sin, tm)
            rows = rows_l.reshape(bsz, t, ROW_SLOTS, N_KV, HEAD_DIM)
            win = win_l.reshape(bsz, t, WIN_SLOTS, N_KV, HEAD_DIM)
            cmp_w = (prm["cmp_pe_k"], prm["cmp_pe_v"], w["w1k"], w["w1v"], w["w2k"], w["w2v"])
            if prompt:
                pages = jnp.arange(n // PAGE_ROWS, dtype=jnp.int32).reshape(bsz, t // PAGE_ROWS)
                kcmp, vcmp = _compress(rows_l, pages, *cmp_w)
                win = win[:, t - min(WINDOW, t):]
            else:
                kcmp, vcmp = _compress(cache_lines, page_table, *cmp_w)
                win = jnp.concatenate([cache_win, win], axis=1)[:, t:]
    return h.reshape(bsz, t, d), jnp.stack(ssm_re), jnp.stack(ssm_im), rows, win


def kernel(x_prompt, x_sample, state_ssm_re, state_ssm_im, cache_kv, cache_win, page_table,
           p_prompt, p_sample, norm_g, ffn_w_gate, ffn_w_up, ffn_w_down, ple_w_proj, ple_w_gate,
           ssm_lam_re, ssm_lam_im, ssm_log_dt, ssm_b_re, ssm_b_im, ssm_c_re, ssm_c_im, ssm_d,
           glu_w, glu_b, kv_norm_g, w_kv, cmp_pe_k, cmp_pe_v, cmp_w1_k, cmp_w2_k, cmp_w1_v,
           cmp_w2_v, attn_w_qg, attn_w_o):
    prm = dict(norm_g=norm_g, ffn_w_gate=ffn_w_gate, ffn_w_up=ffn_w_up, ffn_w_down=ffn_w_down,
               ple_w_proj=ple_w_proj, ple_w_gate=ple_w_gate, ssm_lam_re=ssm_lam_re,
               ssm_lam_im=ssm_lam_im, ssm_log_dt=ssm_log_dt, ssm_b_re=ssm_b_re, ssm_b_im=ssm_b_im,
               ssm_c_re=ssm_c_re, ssm_c_im=ssm_c_im, ssm_d=ssm_d, glu_w=glu_w, glu_b=glu_b,
               kv_norm_g=kv_norm_g, w_kv=w_kv, cmp_pe_k=cmp_pe_k, cmp_pe_v=cmp_pe_v,
               cmp_w1_k=cmp_w1_k, cmp_w2_k=cmp_w2_k, cmp_w1_v=cmp_w1_v, cmp_w2_v=cmp_w2_v,
               attn_w_qg=attn_w_qg, attn_w_o=attn_w_o)
    assert x_sample.shape[1] == 1, "the decode group handles one new token per sequence"
    assert cache_kv.shape[1] == PAGE_ROWS
    w = _prepare_weights(prm)
    past_len = page_table.shape[1] * cache_kv.shape[1]
    decode = _run_group(x_sample, p_sample, past_len, (state_ssm_re, state_ssm_im),
                        (cache_kv, cache_win, page_table), prm, w)
    next(decode)
    y_p, re_p, im_p, kv_p, win_p = _finish(_run_group(x_prompt, p_prompt, 0, None, None, prm, w))
    y_s, re_s, im_s, kv_s, win_s = _finish(decode)
    return (y_p, y_s, re_p, im_p, re_s, im_s, kv_p, kv_s, win_p, win_s)
```

```python
import functools
import math

import jax
import jax.numpy as jnp
from jax import lax
from jax.experimental import pallas as pl
from jax.experimental.pallas import tpu as pltpu

F32 = jnp.float32
BF16 = jnp.bfloat16
HIGHEST = lax.Precision.HIGHEST

N_A_LAYERS = 2
NORM_EPS = 1e-6
SSM_GROUP = 16
SSM_STATE = 64
HEAD_DIM = 128
N_KV = 2
HEADS_PER_KV = 8
N_HEADS = N_KV * HEADS_PER_KV
N_BRANCH = 3
N_KV_SLOTS = 6
CMP_LEN = 32
CMP_STRIDE = 16
SEL_LEN = 64
SEL_TOP = 16
WINDOW = 512
ROPE_THETA = 10000.0
FORCE_BONUS = 1e3
NEG = -1e30
LOWEST = -3.0e38
ATT_SCALE = HEAD_DIM ** -0.5

LANES = 128
V7X_VMEM_BYTES = 64 * 2 ** 20
VMEM_LIMIT_BYTES = V7X_VMEM_BYTES * 7 // 8
FFN_TILE_F = 512
S5_CHUNK = 16
S5_GROUPS_PER_STEP = LANES // SSM_GROUP
S5_STATE_LANES = S5_GROUPS_PER_STEP * SSM_STATE


def _cparams(*sem):
    return pltpu.CompilerParams(dimension_semantics=sem, vmem_limit_bytes=VMEM_LIMIT_BYTES)


def _rms(x, g):
    y = x * lax.rsqrt(jnp.mean(x * x, axis=-1, keepdims=True) + NORM_EPS)
    return y * g


def _dot(a, b, **kw):
    return jnp.dot(a, b, preferred_element_type=F32, **kw)


def _dot_3pass(a, b):
    a_hi, b_hi = a.astype(BF16), b.astype(BF16)
    a_lo = (a - a_hi.astype(F32)).astype(BF16)
    b_lo = (b - b_hi.astype(F32)).astype(BF16)
    return _dot(a_hi, b_hi) + _dot(a_hi, b_lo) + _dot(a_lo, b_hi)


def _dot_nt(a, b):
    return lax.dot_general(a, b, (((1,), (1,)), ((), ())), preferred_element_type=F32)


def _rope(x, cos, sin):
    return x * cos + pltpu.roll(x, HEAD_DIM // 2, 1) * sin


def _iota(shape, axis):
    return lax.broadcasted_iota(jnp.int32, shape, axis)


def _ffn_step(k, nk, h_ref, gpre_ref, gpost_ref, gnext_ref, wg_ref, wu_ref, wd_ref, o_ref, xn_out_ref,
              xn_ref, acc_ref, side_work=None):
    @pl.when(k == 0)
    def _():
        xn_ref[...] = _rms(h_ref[...], gpre_ref[...]).astype(BF16)
        acc_ref[...] = jnp.zeros_like(acc_ref)

    xn = xn_ref[...]
    gate = _dot(xn, wg_ref[...])
    up = _dot(xn, wu_ref[...])
    mid = (jax.nn.silu(gate) * up).astype(BF16)
    acc_ref[...] += _dot(mid, wd_ref[...])
    if side_work is not None:
        side_work()

    @pl.when(k == nk - 1)
    def _():
        h_new = h_ref[...] + _rms(acc_ref[...], 0.5 * gpost_ref[...])
        o_ref[...] = h_new
        if xn_out_ref is not None:
            xn_out_ref[...] = _rms(h_new, gnext_ref[...])


def _ffn_body(h_ref, gpre_ref, gpost_ref, gnext_ref, wg_ref, wu_ref, wd_ref, o_ref, *rest, nk):
    *xn_out_ref, xn_ref, acc_ref = rest
    _ffn_step(pl.program_id(1), nk, h_ref, gpre_ref, gpost_ref, gnext_ref,
              wg_ref, wu_ref, wd_ref, o_ref, xn_out_ref[0] if xn_out_ref else None, xn_ref, acc_ref)


def _ffn(h, g_pre, g_post, g_next, wg, wu, wd, tm):
    n, d = h.shape
    nk = wg.shape[-1] // FFN_TILE_F
    row = lambda i, k: (i, 0)
    vec = lambda i, k: (0, 0)
    n_out = 1 if g_next is None else 2
    out = pl.pallas_call(
        functools.partial(_ffn_body, nk=nk),
        grid=(n // tm, nk),
        in_specs=[pl.BlockSpec((tm, d), row), pl.BlockSpec((1, d), vec), pl.BlockSpec((1, d), vec),
                  pl.BlockSpec((1, d), vec),
                  pl.BlockSpec((d, FFN_TILE_F), lambda i, k: (0, k)),
                  pl.BlockSpec((d, FFN_TILE_F), lambda i, k: (0, k)),
                  pl.BlockSpec((FFN_TILE_F, d), lambda i, k: (k, 0))],
        out_specs=[pl.BlockSpec((tm, d), row)] * n_out,
        out_shape=[jax.ShapeDtypeStruct((n, d), F32)] * n_out,
        scratch_shapes=[pltpu.VMEM((tm, d), BF16), pltpu.VMEM((tm, d), F32)],
        compiler_params=_cparams("parallel", "arbitrary"),
        name="ffn",
    )(h, g_pre, g_post, g_post if g_next is None else g_next, wg, wu, wd)
    return (out[0], None) if g_next is None else tuple(out)


FFN_CONVERT_ROWS = 16


def _ffn_convert_body(h_ref, gpre_ref, gpost_ref, gnext_ref, wg_ref, wu_ref, wd_ref,
                      ng_ref, nu_ref, nd_ref, o_ref, *rest, nk, d_ff):
    *xn_out_ref, ng16_ref, nu16_ref, nd16_ref, xn_ref, acc_ref = rest
    xn_out_ref = xn_out_ref[0] if xn_out_ref else None
    i, k = pl.program_id(0), pl.program_id(1)

    def convert_next():
        pad = ng16_ref.shape[1] - d_ff
        for src, dst in ((ng_ref, ng16_ref), (nu_ref, nu16_ref)):
            dst[:, :d_ff] = src[...].astype(BF16)
            dst[:, d_ff:] = jnp.zeros((dst.shape[0], pad), BF16)
        rows = nd_ref.shape[0]
        live = d_ff - (i * nk + k) * rows
        nd16_ref[...] = jnp.where(_iota(nd_ref.shape, 0) < live, nd_ref[...], 0.0).astype(BF16)

    _ffn_step(k, nk, h_ref, gpre_ref, gpost_ref, gnext_ref,
              wg_ref, wu_ref, wd_ref, o_ref, xn_out_ref, xn_ref, acc_ref, convert_next)


def _ffn_convert(h, g_pre, g_post, g_next, wg, wu, wd, tm, wg32, wu32, wd32, layer, which):
    n, d = h.shape
    fp = wg.shape[-1]
    d_ff = wg32.shape[-1]
    nk = fp // FFN_TILE_F
    steps = (n // tm) * nk
    rg = FFN_CONVERT_ROWS
    rd = fp // steps
    assert fp % steps == 0 and d_ff % rd == 0 and d % rg == 0 and d // rg <= steps
    row = lambda i, k: (i, 0)
    vec = lambda i, k: (0, 0)
    up_slab = lambda i, k: jnp.minimum(i * nk + k, d // rg - 1)
    down_slab = lambda i, k: jnp.minimum(i * nk + k, d_ff // rd - 1)
    n_out = 1 if g_next is None else 2
    out = pl.pallas_call(
        functools.partial(_ffn_convert_body, nk=nk, d_ff=d_ff),
        grid=(n // tm, nk),
        in_specs=[pl.BlockSpec((tm, d), row), pl.BlockSpec((1, d), vec), pl.BlockSpec((1, d), vec),
                  pl.BlockSpec((1, d), vec),
                  pl.BlockSpec((d, FFN_TILE_F), lambda i, k: (0, k)),
                  pl.BlockSpec((d, FFN_TILE_F), lambda i, k: (0, k)),
                  pl.BlockSpec((FFN_TILE_F, d), lambda i, k: (k, 0)),
                  pl.BlockSpec((None, None, rg, d_ff), lambda i, k: (layer, which, up_slab(i, k), 0)),
                  pl.BlockSpec((None, None, rg, d_ff), lambda i, k: (layer, which, up_slab(i, k), 0)),
                  pl.BlockSpec((None, None, rd, d), lambda i, k: (layer, which, down_slab(i, k), 0))],
        out_specs=[pl.BlockSpec((tm, d), row)] * n_out
        + [pl.BlockSpec((rg, fp), lambda i, k: (up_slab(i, k), 0)),
           pl.BlockSpec((rg, fp), lambda i, k: (up_slab(i, k), 0)),
           pl.BlockSpec((rd, d), lambda i, k: (i * nk + k, 0))],
        out_shape=[jax.ShapeDtypeStruct((n, d), F32)] * n_out
        + [jax.ShapeDtypeStruct((d, fp), BF16)] * 2 + [jax.ShapeDtypeStruct((fp, d), BF16)],
        scratch_shapes=[pltpu.VMEM((tm, d), BF16), pltpu.VMEM((tm, d), F32)],
        compiler_params=_cparams("arbitrary", "arbitrary"),
        name="ffn_convert",
    )(h, g_pre, g_post, g_post if g_next is None else g_next, wg, wu, wd, wg32, wu32, wd32)
    return (out[0], None, *out[1:]) if g_next is None else tuple(out)


def _ffn_cast_body(h_ref, gpre_ref, gpost_ref, gnext_ref, wg_ref, wu_ref, wd_ref,
                   o_ref, xn_out_ref, wg16_ref, wu16_ref, wd16_ref, xn_ref, acc_ref, *, nk, d_ff):
    k = pl.program_id(1)
    live = d_ff - k * FFN_TILE_F
    wshape, dshape = wg_ref.shape, wd_ref.shape
    wg16_ref[...] = jnp.where(_iota(wshape, 1) < live, wg_ref[...], 0.0).astype(BF16)
    wu16_ref[...] = jnp.where(_iota(wshape, 1) < live, wu_ref[...], 0.0).astype(BF16)
    wd16_ref[...] = jnp.where(_iota(dshape, 0) < live, wd_ref[...], 0.0).astype(BF16)
    _ffn_step(k, nk, h_ref, gpre_ref, gpost_ref, gnext_ref, wg16_ref, wu16_ref, wd16_ref,
              o_ref, xn_out_ref, xn_ref, acc_ref)


def _ffn_cast(h, g_pre, g_post, g_next, wg, wu, wd, layer, which):
    n, d = h.shape
    d_ff = wg.shape[-1]
    nk = -(-d_ff // FFN_TILE_F)
    fp = nk * FFN_TILE_F
    row = lambda i, k: (0, 0)
    return pl.pallas_call(
        functools.partial(_ffn_cast_body, nk=nk, d_ff=d_ff),
        grid=(1, nk),
        in_specs=[pl.BlockSpec((n, d), row), pl.BlockSpec((1, d), row), pl.BlockSpec((1, d), row),
                  pl.BlockSpec((1, d), row),
                  pl.BlockSpec((None, None, d, FFN_TILE_F), lambda i, k: (layer, which, 0, k)),
                  pl.BlockSpec((None, None, d, FFN_TILE_F), lambda i, k: (layer, which, 0, k)),
                  pl.BlockSpec((None, None, FFN_TILE_F, d), lambda i, k: (layer, which, k, 0))],
        out_specs=[pl.BlockSpec((n, d), row), pl.BlockSpec((n, d), row),
                   pl.BlockSpec((d, FFN_TILE_F), lambda i, k: (0, k)),
                   pl.BlockSpec((d, FFN_TILE_F), lambda i, k: (0, k)),
                   pl.BlockSpec((FFN_TILE_F, d), lambda i, k: (k, 0))],
        out_shape=[jax.ShapeDtypeStruct((n, d), F32)] * 2
        + [jax.ShapeDtypeStruct((d, fp), BF16)] * 2 + [jax.ShapeDtypeStruct((fp, d), BF16)],
        scratch_shapes=[pltpu.VMEM((n, d), BF16), pltpu.VMEM((n, d), F32)],
        compiler_params=_cparams("arbitrary", "arbitrary"),
        name="ffn_cast",
    )(h, g_pre, g_post, g_next, wg, wu, wd)


def _ple_body(h_ref, p_ref, wgate_ref, wproj_ref, g_ref, o_ref):
    h = h_ref[...]
    gate = jax.nn.sigmoid(_dot(h.astype(BF16), wgate_ref[...]))
    proj = _dot(p_ref[...].astype(BF16), wproj_ref[...])
    o_ref[...] = h + _rms(gate * proj, g_ref[...])


def _ple(h, p, wgate, wproj, g, layer, tm):
    n, d = h.shape
    pd = p.shape[-1]
    row = lambda i: (i, 0)
    return pl.pallas_call(
        _ple_body,
        grid=(n // tm,),
        in_specs=[pl.BlockSpec((tm, d), row), pl.BlockSpec((tm, pd), row),
                  pl.BlockSpec((None, d, d), lambda i: (layer, 0, 0)),
                  pl.BlockSpec((None, pd, d), lambda i: (layer, 0, 0)),
                  pl.BlockSpec((1, d), lambda i: (0, 0))],
        out_specs=pl.BlockSpec((tm, d), row),
        out_shape=jax.ShapeDtypeStruct((n, d), F32),
        compiler_params=_cparams("parallel"),
        name="ple",
    )(h, p, wgate, wproj, g)


ROW_SLOTS = 4
WIN_SLOTS = N_KV_SLOTS - ROW_SLOTS
ROW_LINES = ROW_SLOTS * N_KV
WIN_LINES = WIN_SLOTS * N_KV


def _kv_body(h_ref, g_ref, w_ref, cos_ref, sin_ref, rows_ref, win_ref, kv16_ref, *, tm):
    xn = _rms(h_ref[...], g_ref[...]).astype(BF16)
    kv = _dot(xn, w_ref[...])
    cos, sin = cos_ref[...], sin_ref[...]
    for slot in range(N_KV_SLOTS):
        for g in range(N_KV):
            line = slot * N_KV + g
            lo = line * HEAD_DIM
            x = kv[:, lo:lo + HEAD_DIM]
            if slot in (2, 4):
                x = _rope(x, cos, sin)
            kv16_ref[:, lo:lo + HEAD_DIM] = x.astype(BF16)
            if slot < ROW_SLOTS:
                rows_ref[pl.ds(line, tm, stride=ROW_LINES), :] = x
            else:
                win_ref[pl.ds(line - ROW_LINES, tm, stride=WIN_LINES), :] = x


def _kv_proj(h, g, w, cos, sin, tm):
    n, d = h.shape
    nw = w.shape[-1]
    nt = cos.shape[0] // tm
    row = lambda i: (i, 0)
    tab = lambda i: (i % nt, 0)
    return pl.pallas_call(
        functools.partial(_kv_body, tm=tm),
        grid=(n // tm,),
        in_specs=[pl.BlockSpec((tm, d), row), pl.BlockSpec((1, d), lambda i: (0, 0)),
                  pl.BlockSpec((d, nw), lambda i: (0, 0)),
                  pl.BlockSpec((tm, HEAD_DIM), tab), pl.BlockSpec((tm, HEAD_DIM), tab)],
        out_specs=[pl.BlockSpec((tm * ROW_LINES, HEAD_DIM), row),
                   pl.BlockSpec((tm * WIN_LINES, HEAD_DIM), row),
                   pl.BlockSpec((tm, nw), row)],
        out_shape=[jax.ShapeDtypeStruct((n * ROW_LINES, HEAD_DIM), F32),
                   jax.ShapeDtypeStruct((n * WIN_LINES, HEAD_DIM), F32),
                   jax.ShapeDtypeStruct((n, nw), BF16)],
        compiler_params=_cparams("parallel"),
        name="kv_proj",
    )(h, g, w, cos, sin)


def _qg_body(xn_ref, wq_ref, wgt_ref, cos_ref, sin_ref, qc_ref, qr_ref, gate_ref):
    xn = xn_ref[...].astype(BF16)
    q = _dot(xn, wq_ref[...])
    qc_ref[...] = q.astype(BF16)
    cos, sin = cos_ref[...], sin_ref[...]
    for hd in range(N_HEADS):
        sl = slice(hd * HEAD_DIM, (hd + 1) * HEAD_DIM)
        qr_ref[:, sl] = _rope(q[:, sl], cos, sin).astype(BF16)
    gate_ref[...] = jax.nn.sigmoid(_dot(xn, wgt_ref[...]))


def _qg_proj(xn, wq, wgt, cos, sin, layer, tm):
    n, d = xn.shape
    nq, ng = wq.shape[-1], wgt.shape[-1]
    nt = cos.shape[0] // tm
    row = lambda i: (i, 0)
    tab = lambda i: (i % nt, 0)
    return pl.pallas_call(
        _qg_body,
        grid=(n // tm,),
        in_specs=[pl.BlockSpec((tm, d), row),
                  pl.BlockSpec((None, d, nq), lambda i: (layer, 0, 0)),
                  pl.BlockSpec((None, d, ng), lambda i: (layer, 0, 0)),
                  pl.BlockSpec((tm, HEAD_DIM), tab), pl.BlockSpec((tm, HEAD_DIM), tab)],
        out_specs=[pl.BlockSpec((tm, nq), row), pl.BlockSpec((tm, nq), row), pl.BlockSpec((tm, ng), row)],
        out_shape=[jax.ShapeDtypeStruct((n, nq), BF16), jax.ShapeDtypeStruct((n, nq), BF16),
                   jax.ShapeDtypeStruct((n, ng), F32)],
        compiler_params=_cparams("parallel"),
        name="qg_proj",
    )(xn, wq, wgt, cos, sin)


def _oproj_body(h_ref, o_ref, w_ref, g_ref, out_ref):
    m = _dot(o_ref[...].astype(BF16), w_ref[...])
    out_ref[...] = h_ref[...] + _rms(m, g_ref[...])


def _o_proj(h, o, w, g, layer, tm):
    n, d = h.shape
    row = lambda i: (i, 0)
    return pl.pallas_call(
        _oproj_body,
        grid=(n // tm,),
        in_specs=[pl.BlockSpec((tm, d), row), pl.BlockSpec((tm, d), row),
                  pl.BlockSpec((None, d, d), lambda i: (layer, 0, 0)),
                  pl.BlockSpec((1, d), lambda i: (0, 0))],
        out_specs=pl.BlockSpec((tm, d), row),
        out_shape=jax.ShapeDtypeStruct((n, d), F32),
        compiler_params=_cparams("parallel"),
        name="o_proj",
    )(h, o, w, g)


def _s5gate_body(h_ref, xn_ref, y_ref, d_ref, w_ref, b_ref, g_ref, out_ref):
    y = jax.nn.gelu(y_ref[...] + d_ref[...] * xn_ref[...])
    z = jax.nn.sigmoid(_dot(y.astype(BF16), w_ref[...]) + b_ref[...])
    out_ref[...] = h_ref[...] + _rms(y * z, g_ref[...])


def _s5_gate(h, xn, y, dskip, w, b, g, layer, tm):
    n, d = h.shape
    row = lambda i: (i, 0)
    vec = lambda i: (0, 0)
    return pl.pallas_call(
        _s5gate_body,
        grid=(n // tm,),
        in_specs=[pl.BlockSpec((tm, d), row), pl.BlockSpec((tm, d), row), pl.BlockSpec((tm, d), row),
                  pl.BlockSpec((1, d), vec),
                  pl.BlockSpec((None, d, d), lambda i: (layer, 0, 0)),
                  pl.BlockSpec((1, d), vec), pl.BlockSpec((1, d), vec)],
        out_specs=pl.BlockSpec((tm, d), row),
        out_shape=jax.ShapeDtypeStruct((n, d), F32),
        compiler_params=_cparams("parallel"),
        name="s5_gate",
    )(h, xn, y, dskip, w, b, g)


def _s5_disc(lam_re, lam_im, log_dt):
    dt = jnp.exp(log_dt)
    mag = jnp.exp(lam_re * dt)
    a_re, a_im = mag * jnp.cos(lam_im * dt), mag * jnp.sin(lam_im * dt)
    den = lam_re * lam_re + lam_im * lam_im
    nr, ni = a_re - 1.0, a_im
    r_re = (nr * lam_re + ni * lam_im) / den
    r_im = (ni * lam_re - nr * lam_im) / den
    return a_re, a_im, r_re, r_im


def _cmul(ar, ai, br, bi):
    return ar * br - ai * bi, ar * bi + ai * br


def _bd_rows(m):
    shape = (LANES, S5_STATE_LANES)
    same = _iota(shape, 0) // SSM_GROUP == _iota(shape, 1) // SSM_STATE
    return jnp.where(same, jnp.concatenate([m] * S5_GROUPS_PER_STEP, axis=1), 0.0)


def _bd_cols(m):
    shape = (S5_STATE_LANES, LANES)
    same = _iota(shape, 0) // SSM_STATE == _iota(shape, 1) // SSM_GROUP
    return jnp.where(same, jnp.concatenate([m] * S5_GROUPS_PER_STEP, axis=0), 0.0)


def _s5_input_weights(lam_re, lam_im, log_dt, b_re, b_im):
    a_re, a_im, r_re, r_im = _s5_disc(lam_re, lam_im, log_dt)
    bb_re, bb_im = _cmul(r_re, r_im, b_re, b_im)
    return a_re, a_im, bb_re, bb_im


def _s5_prompt_body(u_ref, lr_re_ref, lr_im_ref, lr_dt_ref, b_re_ref, b_im_ref,
                    lc_re_ref, lc_im_ref, lc_dt_ref, c_re_ref, c_im_ref,
                    lv_re_ref, lv_im_ref, lv_dt_ref,
                    y_ref, st_ref,
                    sin_ref, toep_ref, fout_ref, v_ref, xp_ref, *, n_batch, n_chunk):
    L = S5_CHUNK
    rows = n_batch * n_chunk
    half = S5_STATE_LANES

    a_re, a_im, w_re, w_im = _s5_input_weights(lr_re_ref[...], lr_im_ref[...], lr_dt_ref[...],
                                               b_re_ref[...], b_im_ref[...])
    ws_re, ws_im = [w_re], [w_im]
    for _ in range(L - 1):
        w_re, w_im = _cmul(a_re, a_im, w_re, w_im)
        ws_re.append(w_re)
        ws_im.append(w_im)
    for s in range(L):
        k = L - 1 - s
        sin_ref[s * LANES:(s + 1) * LANES, :] = jnp.concatenate(
            [_bd_rows(ws_re[k]), _bd_rows(ws_im[k])], axis=1).astype(BF16)
    stack_re = jnp.concatenate([ws_re[L - 1 - s] for s in range(L)], axis=0)
    stack_im = jnp.concatenate([ws_im[L - 1 - s] for s in range(L)], axis=0)
    taps = _dot_3pass(stack_re, c_re_ref[...]) - _dot_3pass(stack_im, c_im_ref[...])
    tshape = (L * LANES, LANES)
    same = (_iota(tshape, 0) % LANES) // SSM_GROUP == _iota(tshape, 1) // SSM_GROUP
    taps = jnp.where(same, taps, 0.0).astype(BF16)
    toep_ref[:, LANES:] = taps
    toep_ref[:(L - 1) * LANES, :LANES] = taps[LANES:]
    toep_ref[(L - 1) * LANES:, :LANES] = jnp.zeros((LANES, LANES), BF16)
    ac_re, ac_im, _, _ = _s5_disc(lc_re_ref[...], lc_im_ref[...], lc_dt_ref[...])
    ca_re, ca_im = c_re_ref[...], c_im_ref[...]
    for t in range(L):
        ca_re, ca_im = _cmul(ac_re, ac_im, ca_re, ca_im)
        fout_ref[:, t * LANES:(t + 1) * LANES] = jnp.concatenate(
            [_bd_cols(ca_re), -_bd_cols(ca_im)], axis=0).astype(BF16)
    al_re, al_im, _, _ = _s5_disc(lv_re_ref[...], lv_im_ref[...], lv_dt_ref[...])
    for _ in range(int(math.log2(L))):
        al_re, al_im = _cmul(al_re, al_im, al_re, al_im)

    xb = jnp.concatenate([u_ref[pl.ds(s, rows, stride=L), :] for s in range(L)], axis=1).astype(BF16)
    v_ref[...] = _dot(xb, sin_ref[...])

    def step(c, carry):
        new = []
        for b in range(n_batch):
            r = b * n_chunk + c
            x_re, x_im = carry[b]
            xp_ref[pl.ds(r, 1), :] = jnp.concatenate([x_re, x_im], axis=1)
            inc = v_ref[pl.ds(r, 1), :]
            new.append((al_re * x_re - al_im * x_im + inc[:, :half],
                        al_re * x_im + al_im * x_re + inc[:, half:]))
        return tuple(new)

    zero = jnp.zeros((1, half), F32)
    final = lax.fori_loop(0, n_chunk, step, tuple((zero, zero) for _ in range(n_batch)), unroll=4)
    for b in range(n_batch):
        st_ref[b:b + 1, :] = jnp.concatenate(final[b], axis=1)

    xpb = xp_ref[...].astype(BF16)
    for t in range(0, L, 2):
        pair = (_dot(xb[:, :(t + 2) * LANES], toep_ref[(L - 2 - t) * LANES:, :])
                + _dot(xpb, fout_ref[:, t * LANES:(t + 2) * LANES]))
        y_ref[pl.ds(t, rows, stride=L), :] = pair[:, :LANES]
        y_ref[pl.ds(t + 1, rows, stride=L), :] = pair[:, LANES:]


def _s5_layouts(lam_re, lam_im, log_dt, b_re, b_im, c_re, c_im):
    n_g = lam_re.shape[0]
    ldt = jnp.broadcast_to(log_dt[:, None], lam_re.shape)
    rep = lambda a: jnp.repeat(a, SSM_GROUP, axis=0)
    rows = (rep(lam_re), rep(lam_im), rep(ldt),
            b_re.transpose(0, 2, 1).reshape(n_g * SSM_GROUP, SSM_STATE),
            b_im.transpose(0, 2, 1).reshape(n_g * SSM_GROUP, SSM_STATE))
    cols = (rep(lam_re).T, rep(lam_im).T, rep(ldt).T,
            c_re.reshape(n_g * SSM_GROUP, SSM_STATE).T, c_im.reshape(n_g * SSM_GROUP, SSM_STATE).T)
    n_j = n_g // S5_GROUPS_PER_STEP
    lanes = tuple(a.reshape(n_j, 1, S5_STATE_LANES) for a in (lam_re, lam_im, ldt))
    return rows, cols, lanes


def _s5_specs():
    rspec = pl.BlockSpec((LANES, SSM_STATE), lambda j: (j, 0))
    cspec = pl.BlockSpec((SSM_STATE, LANES), lambda j: (0, j))
    vspec = pl.BlockSpec((None, 1, S5_STATE_LANES), lambda j: (j, 0, 0))
    return [rspec] * 5 + [cspec] * 5 + [vspec] * 3


def _s5_prompt(u, n_batch, layouts):
    n, d = u.shape
    assert d % LANES == 0 and n % (n_batch * S5_CHUNK) == 0 and S5_CHUNK & (S5_CHUNK - 1) == 0
    n_j = d // LANES
    n_chunk = n // n_batch // S5_CHUNK
    rows = n // S5_CHUNK
    rows_p, cols_p, lanes_p = layouts
    col = lambda j: (0, j)
    return pl.pallas_call(
        functools.partial(_s5_prompt_body, n_batch=n_batch, n_chunk=n_chunk),
        grid=(n_j,),
        in_specs=[pl.BlockSpec((n, LANES), col)] + _s5_specs(),
        out_specs=[pl.BlockSpec((n, LANES), col),
                   pl.BlockSpec((None, n_batch, 2 * S5_STATE_LANES), lambda j: (j, 0, 0))],
        out_shape=[jax.ShapeDtypeStruct((n, d), F32),
                   jax.ShapeDtypeStruct((n_j, n_batch, 2 * S5_STATE_LANES), F32)],
        scratch_shapes=[pltpu.VMEM((S5_CHUNK * LANES, 2 * S5_STATE_LANES), BF16),
                        pltpu.VMEM((S5_CHUNK * LANES, 2 * LANES), BF16),
                        pltpu.VMEM((2 * S5_STATE_LANES, S5_CHUNK * LANES), BF16),
                        pltpu.VMEM((rows, 2 * S5_STATE_LANES), F32),
                        pltpu.VMEM((rows, 2 * S5_STATE_LANES), F32)],
        compiler_params=_cparams("parallel"),
        name="s5_prompt",
    )(u, *rows_p, *cols_p, *lanes_p)


def _s5_step_body(u_ref, hre_ref, him_ref, lr_re_ref, lr_im_ref, lr_dt_ref, b_re_ref, b_im_ref,
                  lc_re_ref, lc_im_ref, lc_dt_ref, c_re_ref, c_im_ref,
                  lv_re_ref, lv_im_ref, lv_dt_ref, y_ref, ore_ref, oim_ref):
    _, _, bb_re, bb_im = _s5_input_weights(lr_re_ref[...], lr_im_ref[...], lr_dt_ref[...],
                                           b_re_ref[...], b_im_ref[...])
    a_re, a_im, _, _ = _s5_disc(lv_re_ref[...], lv_im_ref[...], lv_dt_ref[...])
    u = u_ref[...]
    h_re, h_im = hre_ref[...], him_ref[...]
    x_re = _dot(u, _bd_rows(bb_re), precision=HIGHEST) + (a_re * h_re - a_im * h_im)
    x_im = _dot(u, _bd_rows(bb_im), precision=HIGHEST) + (a_re * h_im + a_im * h_re)
    ore_ref[...] = x_re
    oim_ref[...] = x_im
    y_ref[...] = (_dot(x_re, _bd_cols(c_re_ref[...]), precision=HIGHEST)
                  - _dot(x_im, _bd_cols(c_im_ref[...]), precision=HIGHEST))


def _s5_step(u, h_re, h_im, layouts):
    nb, d = u.shape
    rows_p, cols_p, lanes_p = layouts
    col = lambda j: (0, j)
    sspec = pl.BlockSpec((nb, S5_STATE_LANES), col)
    return pl.pallas_call(
        _s5_step_body,
        grid=(d // LANES,),
        in_specs=[pl.BlockSpec((nb, LANES), col), sspec, sspec] + _s5_specs(),
        out_specs=[pl.BlockSpec((nb, LANES), col), sspec, sspec],
        out_shape=[jax.ShapeDtypeStruct((nb, d), F32),
                   jax.ShapeDtypeStruct(h_re.shape, F32), jax.ShapeDtypeStruct(h_im.shape, F32)],
        compiler_params=_cparams("parallel"),
        name="s5_step",
    )(u, h_re, h_im, *rows_p, *cols_p, *lanes_p)


PAGE_ROWS = 128
PAGE_LINES = PAGE_ROWS * ROW_LINES
CHUNKS_PER_PAGE = PAGE_ROWS // CMP_STRIDE
CMP_PAGES_PER_STEP = 16
CMP_LINES = 2 * N_KV


def _compress_body(pt_ref, *refs, n_pages):
    del pt_ref
    k_pages = CMP_PAGES_PER_STEP
    srcs = refs[:k_pages]
    (pek_ref, pev_ref, w1k_ref, w1v_ref, w2k_ref, w2v_ref, kc_ref, vc_ref, buf_ref, shift_ref) = refs[k_pages:]
    p = pl.program_id(1)
    for k in range(k_pages):
        r0 = pl.multiple_of((p * k_pages + k) * PAGE_ROWS, PAGE_ROWS)
        for sg in range(CMP_LINES):
            buf_ref[sg, pl.ds(r0, PAGE_ROWS), :] = srcs[k][pl.ds(sg, PAGE_ROWS, stride=ROW_LINES), :]

    @pl.when(p == n_pages // k_pages - 1)
    def _():
        n_chunk = n_pages * CHUNKS_PER_PAGE
        half = CMP_STRIDE
        for slot, (pe_ref, w1_ref, w2_ref, out_ref) in enumerate(
                ((pek_ref, w1k_ref, w2k_ref, kc_ref), (pev_ref, w1v_ref, w2v_ref, vc_ref))):
            for g in range(N_KV):
                sg = slot * N_KV + g
                xs = [buf_ref[sg, pl.ds(l, n_chunk, stride=CMP_STRIDE), :] for l in range(half)]
                as_first = jnp.concatenate(
                    [(x + pe_ref[l:l + 1, :]).astype(BF16) for l, x in enumerate(xs)], axis=1)
                as_second = jnp.concatenate(
                    [(x + pe_ref[half + l:half + l + 1, :]).astype(BF16) for l, x in enumerate(xs)], axis=1)
                first = _dot(as_first, w1_ref[:half * HEAD_DIM, :])
                shift_ref[0:n_chunk, :] = _dot(as_second, w1_ref[half * HEAD_DIM:, :])
                shift_ref[n_chunk:n_chunk + 8, :] = jnp.zeros((8, HEAD_DIM), F32)
                hid = first + shift_ref[1:n_chunk + 1, :]
                out = _dot(jax.nn.silu(hid).astype(BF16), w2_ref[...])
                live = _iota(out.shape, 0) < n_chunk - 1
                out_ref[g] = jnp.where(live, out, 0.0).astype(BF16)


def _compress(src, page_ids, pe_k, pe_v, w1k, w1v, w2k, w2v):
    nb, n_pages = page_ids.shape
    k_pages = CMP_PAGES_PER_STEP
    assert n_pages % k_pages == 0
    n_chunk = n_pages * CHUNKS_PER_PAGE
    const = lambda b, p, pt: (0, 0)
    page = lambda k: pl.BlockSpec((PAGE_LINES, HEAD_DIM),
                                  lambda b, p, pt: (pt[b * n_pages + p * k_pages + k], 0))
    out_spec = pl.BlockSpec((None, N_KV, n_chunk, HEAD_DIM), lambda b, p, pt: (b, 0, 0, 0))
    grid_spec = pltpu.PrefetchScalarGridSpec(
        num_scalar_prefetch=1,
        grid=(nb, n_pages // k_pages),
        in_specs=[page(k) for k in range(k_pages)]
        + [pl.BlockSpec(pe_k.shape, const), pl.BlockSpec(pe_v.shape, const),
           pl.BlockSpec(w1k.shape, const), pl.BlockSpec(w1v.shape, const),
           pl.BlockSpec(w2k.shape, const), pl.BlockSpec(w2v.shape, const)],
        out_specs=[out_spec, out_spec],
        scratch_shapes=[pltpu.VMEM((CMP_LINES, n_pages * PAGE_ROWS, HEAD_DIM), F32),
                        pltpu.VMEM((n_chunk + 8, HEAD_DIM), F32)])
    return pl.pallas_call(
        functools.partial(_compress_body, n_pages=n_pages),
        grid_spec=grid_spec,
        out_shape=[jax.ShapeDtypeStruct((nb, N_KV, n_chunk, HEAD_DIM), BF16)] * 2,
        compiler_params=_cparams("parallel", "arbitrary"),
        name="compress",
    )(page_ids.reshape(-1), *([src] * k_pages), pe_k, pe_v, w1k, w1v, w2k, w2v)


def _masked_softmax(s, mask):
    s = jnp.where(mask, s, NEG)
    m = jnp.max(s, axis=-1, keepdims=True)
    e = jnp.where(mask, jnp.exp(s - m), 0.0)
    return e / jnp.maximum(jnp.sum(e, axis=-1, keepdims=True), 1e-30)


def _overlap(n_rows, n_lanes, n_cmp, n_sel):
    ci, sj = _iota((n_rows, n_lanes), 0), _iota((n_rows, n_lanes), 1)
    hit = ((ci * CMP_STRIDE < sj * SEL_LEN + SEL_LEN) & (ci * CMP_STRIDE + CMP_LEN > sj * SEL_LEN)
           & (ci < n_cmp) & (sj < n_sel))
    return jnp.where(hit, 1.0, 0.0).astype(BF16)


def _importance(imp, lane, qpos, n_sel):
    cur = qpos // SEL_LEN
    forced = (lane == 0) | (lane == cur) | (lane == cur - 1)
    imp = jnp.where(lane * SEL_LEN <= qpos, imp + jnp.where(forced, FORCE_BONUS, 0.0), NEG)
    return jnp.where(lane < n_sel, imp, LOWEST)


def _dot_tn(a, b):
    return lax.dot_general(a, b, (((0,), (0,)), ((), ())), preferred_element_type=F32)


LOG2E = math.log2(math.e)


def _online_attend_t(q, k_ref, v_ref, lo, hi, ck, bias_fn, s_even_ref, s_odd_ref, ml_ref, acc_ref):
    cols = q.shape[0]

    def chunk_start(c):
        return pl.multiple_of(jnp.minimum(c, hi - 1) * ck, ck)

    def scores(c, s_ref):
        k0 = chunk_start(c)
        bias = jnp.concatenate([bias_fn(k0)] * (cols // Q_BLOCK), axis=1) + jnp.where(c < hi, 0.0, NEG)
        s_ref[:ck, :] = _dot_nt(k_ref[pl.ds(k0, ck), :], q) * (ATT_SCALE * LOG2E) + bias

    def update(c, s_ref):
        m, l = ml_ref[0:1, :], ml_ref[1:2, :]
        s = s_ref[:ck, :]
        m_new = jnp.maximum(m, jnp.max(s, axis=0, keepdims=True))
        alpha = jnp.exp2(m - m_new)
        e = jnp.exp2(s - m_new)
        ml_ref[0:1, :] = m_new
        ml_ref[1:2, :] = alpha * l + jnp.sum(e, axis=0, keepdims=True)
        acc_ref[...] = alpha * acc_ref[...] + _dot_tn(v_ref[pl.ds(chunk_start(c), ck), :], e.astype(BF16))

    def trip(i):
        c = lo + 2 * i
        scores(c + 1, s_odd_ref)
        update(c, s_even_ref)
        scores(c + 2, s_even_ref)
        update(c + 1, s_odd_ref)

    def start():
        scores(lo, s_even_ref)
        ml_ref[0:1, :] = jnp.full((1, cols), NEG, F32)
        ml_ref[1:2, :] = jnp.zeros((1, cols), F32)
        acc_ref[...] = jnp.zeros((HEAD_DIM, cols), F32)

    def finish():
        @pl.when((hi - lo) % 2 == 1)
        def _():
            update(hi - 1, s_even_ref)

        m, l = ml_ref[0:1, :], ml_ref[1:2, :]
        return jnp.where(m > NEG, acc_ref[...] / jnp.maximum(l, 1e-30), 0.0)

    return start, trip, (hi - lo) // 2, finish


def _run_together(first, second):
    (start_a, trip_a, pairs_a, finish_a), (start_b, trip_b, pairs_b, finish_b) = first, second
    start_a()
    start_b()
    common = jnp.minimum(pairs_a, pairs_b)

    def both(i, carry):
        trip_a(i)
        trip_b(i)
        return carry

    lax.fori_loop(0, common, both, 0)
    lax.fori_loop(common, pairs_a, lambda i, c: (trip_a(i), c)[1], 0)
    lax.fori_loop(common, pairs_b, lambda i, c: (trip_b(i), c)[1], 0)
    return finish_a(), finish_b()


Q_BLOCK = 128
SEL_KEY_CHUNK = 256
WIN_KEY_CHUNK = 128


def _nsa_prompt_body(qc_ref, qr_ref, gate_ref, ksel_ref, vsel_ref, kwin_ref, vwin_ref,
                     kcmp_ref, vcmp_ref, o_ref, sel_ref, s_even_ref, s_odd_ref, ml_ref, acc_ref,
                     w_even_ref, w_odd_ref, wml_ref, wacc_ref, *, n_cmp, n_sel):
    qi = pl.program_id(2)
    q0 = qi * Q_BLOCK
    nh = HEADS_PER_KV
    stack = lambda ref: jnp.concatenate(
        [ref[:, hd * HEAD_DIM:(hd + 1) * HEAD_DIM] for hd in range(nh)], axis=0)
    per_head = lambda x: jnp.concatenate([x] * nh, axis=1)
    qc, qr = stack(qc_ref), stack(qr_ref)
    n_cmp_rows = kcmp_ref.shape[0]
    n_sel_rows = sel_ref.shape[0]

    cshape = (n_cmp_rows, Q_BLOCK)
    cblk, cq = _iota(cshape, 0), q0 + _iota(cshape, 1)
    cmask = per_head(jnp.where((cblk < n_cmp) & (cblk * CMP_STRIDE + CMP_LEN - 1 <= cq), 1.0, 0.0)) > 0.5
    s = jnp.where(cmask, _dot_nt(kcmp_ref[...], qc) * ATT_SCALE, NEG)
    e = jnp.where(cmask, jnp.exp(s - jnp.max(s, axis=0, keepdims=True)), 0.0)
    p_cmp = (e / jnp.maximum(jnp.sum(e, axis=0, keepdims=True), 1e-30)).astype(BF16)
    o_cmp = _dot_tn(vcmp_ref[...], p_cmp)

    oshape = (n_sel_rows, n_cmp_rows)
    sj, ci = _iota(oshape, 0), _iota(oshape, 1)
    hit = ((ci * CMP_STRIDE < sj * SEL_LEN + SEL_LEN) & (ci * CMP_STRIDE + CMP_LEN > sj * SEL_LEN)
           & (ci < n_cmp) & (sj < n_sel))
    imp_h = _dot(jnp.where(hit, 1.0, 0.0).astype(BF16), p_cmp)
    imp = imp_h[:, 0:Q_BLOCK]
    for hd in range(1, nh):
        imp = imp + imp_h[:, hd * Q_BLOCK:(hd + 1) * Q_BLOCK]
    ishape = (n_sel_rows, Q_BLOCK)
    blk = _iota(ishape, 0)
    imp = _importance(imp, blk, q0 + _iota(ishape, 1), n_sel)
    rank = jnp.zeros(ishape, F32)
    for k in range(n_sel):
        row = imp[k:k + 1, :]
        rank += jnp.where((row > imp) | ((row == imp) & (blk > k)), 1.0, 0.0)
    sel_ref[...] = jnp.where((rank < min(SEL_TOP, n_sel)) & (blk < n_sel), 1.0, 0.0)

    def sel_mask(k0):
        shape = (SEL_KEY_CHUNK, Q_BLOCK)
        j0 = k0 // SEL_LEN
        chosen = jnp.concatenate(
            [jnp.broadcast_to(sel_ref[pl.ds(j0 + i, 1), :], (SEL_LEN, Q_BLOCK))
             for i in range(SEL_KEY_CHUNK // SEL_LEN)], axis=0)
        live = (k0 + _iota(shape, 0) <= q0 + _iota(shape, 1)) & (chosen > 0.5)
        return jnp.where(live, 0.0, NEG)

    def win_mask(k0):
        shape = (WIN_KEY_CHUNK, Q_BLOCK)
        kpos = k0 + _iota(shape, 0)
        qp = q0 + _iota(shape, 1)
        return jnp.where((kpos <= qp) & (kpos > qp - WINDOW), 0.0, NEG)

    sel_chunks = (q0 + Q_BLOCK + SEL_KEY_CHUNK - 1) // SEL_KEY_CHUNK
    win_lo = jnp.maximum(q0 - WINDOW, 0) // WIN_KEY_CHUNK
    win_hi = (q0 + Q_BLOCK + WIN_KEY_CHUNK - 1) // WIN_KEY_CHUNK
    o_sel, o_win = _run_together(
        _online_attend_t(qr, ksel_ref, vsel_ref, 0, sel_chunks, SEL_KEY_CHUNK, sel_mask,
                         s_even_ref, s_odd_ref, ml_ref, acc_ref),
        _online_attend_t(qr, kwin_ref, vwin_ref, win_lo, win_hi, WIN_KEY_CHUNK, win_mask,
                         w_even_ref, w_odd_ref, wml_ref, wacc_ref))

    gates = jnp.transpose(gate_ref[...])
    for hd in range(nh):
        cols = slice(hd * Q_BLOCK, (hd + 1) * Q_BLOCK)
        c0 = hd * N_BRANCH
        o = (o_cmp[:, cols] * gates[c0:c0 + 1, :] + o_sel[:, cols] * gates[c0 + 1:c0 + 2, :]
             + o_win[:, cols] * gates[c0 + 2:c0 + 3, :])
        o_ref[:, hd * HEAD_DIM:(hd + 1) * HEAD_DIM] = jnp.transpose(o).astype(BF16)


def _nsa_prompt(qc, qr, gates, kv, kcmp, vcmp):
    bsz, t, dq = qc.shape
    assert t % SEL_KEY_CHUNK == 0 and t % Q_BLOCK == 0 and kcmp.shape[2] % 8 == 0
    n_cmp = (t - CMP_LEN) // CMP_STRIDE + 1
    n_sel = -(-t // SEL_LEN)
    gw = HEADS_PER_KV * HEAD_DIM
    qspec = pl.BlockSpec((None, Q_BLOCK, gw), lambda b, g, i: (b, i, g))
    kvspec = lambda slot: pl.BlockSpec((None, t, HEAD_DIM), lambda b, g, i: (b, 0, slot * N_KV + g))
    cspec = pl.BlockSpec((None, None, kcmp.shape[2], HEAD_DIM), lambda b, g, i: (b, g, 0, 0))
    return pl.pallas_call(
        functools.partial(_nsa_prompt_body, n_cmp=n_cmp, n_sel=n_sel),
        grid=(bsz, N_KV, t // Q_BLOCK),
        in_specs=[qspec, qspec, pl.BlockSpec((None, Q_BLOCK, LANES), lambda b, g, i: (b, i, g)),
                  kvspec(2), kvspec(3), kvspec(4), kvspec(5), cspec, cspec],
        out_specs=qspec,
        out_shape=jax.ShapeDtypeStruct((bsz, t, dq), BF16),
        scratch_shapes=[pltpu.VMEM((-(-n_sel // 8) * 8, Q_BLOCK), F32)]
        + [pltpu.VMEM((ck, HEADS_PER_KV * Q_BLOCK), F32)
           for ck in (SEL_KEY_CHUNK, SEL_KEY_CHUNK, 8, HEAD_DIM, WIN_KEY_CHUNK, WIN_KEY_CHUNK, 8, HEAD_DIM)],
        compiler_params=_cparams("parallel", "parallel", "arbitrary"),
        name="nsa_prompt",
    )(qc, qr, gates, kv, kv, kv, kv, kcmp, vcmp)


def _select_body(qc_ref, kcmp_ref, vcmp_ref, ocmp_ref, idx_ref, *, qpos, n_cmp, n_sel):
    q = qc_ref[...]
    nc = kcmp_ref.shape[0]
    sel_lanes = -(-n_sel // LANES) * LANES
    ci = _iota((HEADS_PER_KV, nc), 1)
    cmask = (ci < n_cmp) & (ci * CMP_STRIDE + CMP_LEN - 1 <= qpos)
    p_cmp = _masked_softmax(_dot_nt(q, kcmp_ref[...]) * ATT_SCALE, cmask).astype(BF16)
    ocmp_ref[...] = _dot(p_cmp, vcmp_ref[...])

    imp = jnp.sum(_dot(p_cmp, _overlap(nc, sel_lanes, n_cmp, n_sel)), axis=0, keepdims=True)
    lane = _iota((1, sel_lanes), 1)
    imp = _importance(imp, lane, qpos, n_sel)
    sq = (sel_lanes, sel_lanes)
    ri, cj = _iota(sq, 0), _iota(sq, 1)
    impc = jnp.sum(jnp.where(ri == cj, imp, 0.0), axis=1, keepdims=True)
    n_top = min(SEL_TOP, n_sel)
    before_col = jnp.where((impc > imp) | ((impc == imp) & (ri < cj)), 1.0, 0.0)
    before_row = jnp.where((imp > impc) | ((imp == impc) & (cj < ri)), 1.0, 0.0)
    sel_lane = (jnp.sum(before_col, axis=0, keepdims=True) < n_top) & (lane < n_sel)
    sel_row = (jnp.sum(before_row, axis=1, keepdims=True) < n_top) & (_iota((sel_lanes, 1), 0) < n_sel)
    slot_of = jnp.sum(jnp.where((ri < cj) & sel_row, 1.0, 0.0), axis=0, keepdims=True)
    lshape = (SEL_TOP, sel_lanes)
    onehot = jnp.where(sel_lane & (slot_of == _iota(lshape, 0).astype(F32)), 1.0, 0.0)
    idx = jnp.sum(onehot * _iota(lshape, 1).astype(F32), axis=1, keepdims=True)
    idx_ref[...] = jnp.broadcast_to(idx, (SEL_TOP, LANES)).astype(jnp.int32)


def _nsa_select(qc, kcmp, vcmp, qpos, n_cmp, n_sel):
    nb = qc.shape[0]
    nc = kcmp.shape[2]
    hspec = pl.BlockSpec((None, None, HEADS_PER_KV, HEAD_DIM), lambda b, g: (b, g, 0, 0))
    cspec = pl.BlockSpec((None, None, nc, HEAD_DIM), lambda b, g: (b, g, 0, 0))
    return pl.pallas_call(
        functools.partial(_select_body, qpos=qpos, n_cmp=n_cmp, n_sel=n_sel),
        grid=(nb, N_KV),
        in_specs=[hspec, cspec, cspec],
        out_specs=[hspec, pl.BlockSpec((None, None, SEL_TOP, LANES), lambda b, g: (b, g, 0, 0))],
        out_shape=[jax.ShapeDtypeStruct((nb, N_KV, HEADS_PER_KV, HEAD_DIM), F32),
                   jax.ShapeDtypeStruct((nb, N_KV, SEL_TOP, LANES), jnp.int32)],
        compiler_params=_cparams("parallel", "parallel"),
        name="nsa_select",
    )(qc, kcmp, vcmp)


def _attend_plus_new(q, k, v, valid, k_new, v_new, new_ok):
    qf = q.astype(F32)
    s = _dot_nt(q, k.astype(BF16)) * ATT_SCALE
    s_new = jnp.sum(qf * k_new.astype(BF16).astype(F32), axis=1, keepdims=True) * ATT_SCALE
    s = jnp.where(valid, s, NEG)
    s_new = jnp.where(new_ok, s_new, NEG)
    m = jnp.maximum(jnp.max(s, axis=-1, keepdims=True), s_new)
    e = jnp.where(valid, jnp.exp(s - m), 0.0)
    e_new = jnp.where(new_ok, jnp.exp(s_new - m), 0.0)
    den = jnp.maximum(jnp.sum(e, axis=-1, keepdims=True) + e_new, 1e-30)
    p = (e / den).astype(BF16)
    p_new = (e_new / den).astype(BF16).astype(F32)
    return _dot(p, v.astype(BF16)) + p_new * v_new.astype(BF16).astype(F32)


SEL_BLOCK_LINES = SEL_LEN * ROW_LINES


def _attend_body(idx_ref, pt_ref, qr_ref, gate_ref, ocmp_ref, *refs, qpos, past_len, n_win):
    del pt_ref
    blocks = refs[:SEL_TOP]
    win_ref, rows_new_ref, win_new_ref, o_ref = refs[SEL_TOP:]
    b, g = pl.program_id(0), pl.program_id(1)
    q = qr_ref[...]
    n_keys = SEL_TOP * SEL_LEN
    base = (b * N_KV + g) * SEL_TOP
    lane = _iota((1, n_keys), 1)
    kpos = jnp.zeros((1, n_keys), jnp.int32)
    n_new = jnp.int32(0)
    for t in range(SEL_TOP):
        j = idx_ref[base + t]
        kpos = jnp.where(lane // SEL_LEN == t, j * SEL_LEN + lane % SEL_LEN, kpos)
        n_new = n_new + jnp.where(j == qpos // SEL_LEN, 1, 0)
    valid = (kpos <= qpos) & (kpos < past_len)
    new_ok = (jnp.zeros((HEADS_PER_KV, 1), jnp.int32) + n_new) > 0
    line = lambda ref, slot, n, per_token: ref[pl.ds(slot * N_KV + g, n, stride=per_token), :]
    k_sel = jnp.concatenate([line(blk, 2, SEL_LEN, ROW_LINES) for blk in blocks], axis=0)
    v_sel = jnp.concatenate([line(blk, 3, SEL_LEN, ROW_LINES) for blk in blocks], axis=0)
    o_sel = _attend_plus_new(q, k_sel, v_sel, valid, line(rows_new_ref, 2, 1, ROW_LINES),
                             line(rows_new_ref, 3, 1, ROW_LINES), new_ok)

    wpos = (past_len - n_win) + _iota((1, n_win), 1)
    wvalid = (wpos <= qpos) & (wpos > qpos - WINDOW) & (wpos >= 0)
    always = jnp.zeros((HEADS_PER_KV, 1), jnp.int32) == 0
    o_win = _attend_plus_new(q, line(win_ref, 0, n_win, WIN_LINES), line(win_ref, 1, n_win, WIN_LINES),
                             wvalid, line(win_new_ref, 0, 1, WIN_LINES), line(win_new_ref, 1, 1, WIN_LINES),
                             always)

    gates = gate_ref[...]
    o_ref[...] = ocmp_ref[...] * gates[:, 0:1] + o_sel * gates[:, 1:2] + o_win * gates[:, 2:3]


def _nsa_attend(idx, page_table, qr, gates, ocmp, cache, cache_win, rows_new, win_new, past_len):
    nb, n_pages = page_table.shape
    n_win = cache_win.shape[0] // (nb * WIN_LINES)
    halves = PAGE_ROWS // SEL_LEN
    n_past_blocks = n_pages * halves

    def sel_block(t):
        def index(b, g, idx_ref, pt_ref):
            j = jnp.minimum(idx_ref[(b * N_KV + g) * SEL_TOP + t], n_past_blocks - 1)
            return (pt_ref[b * n_pages + j // halves] * halves + j % halves, 0)
        return pl.BlockSpec((SEL_BLOCK_LINES, HEAD_DIM), index)

    hspec = pl.BlockSpec((None, None, HEADS_PER_KV, HEAD_DIM), lambda b, g, i, p: (b, g, 0, 0))
    grid_spec = pltpu.PrefetchScalarGridSpec(
        num_scalar_prefetch=2,
        grid=(nb, N_KV),
        in_specs=[hspec, hspec, hspec] + [sel_block(t) for t in range(SEL_TOP)]
        + [pl.BlockSpec((n_win * WIN_LINES, HEAD_DIM), lambda b, g, i, p: (b, 0)),
           pl.BlockSpec((None, ROW_LINES, HEAD_DIM), lambda b, g, i, p: (b, 0, 0)),
           pl.BlockSpec((None, WIN_LINES, HEAD_DIM), lambda b, g, i, p: (b, 0, 0))],
        out_specs=hspec)
    return pl.pallas_call(
        functools.partial(_attend_body, qpos=past_len, past_len=past_len, n_win=n_win),
        grid_spec=grid_spec,
        out_shape=jax.ShapeDtypeStruct((nb, N_KV, HEADS_PER_KV, HEAD_DIM), F32),
        compiler_params=_cparams("parallel", "arbitrary"),
        name="nsa_attend",
    )(idx, page_table.reshape(-1), qr, gates, ocmp, *([cache] * SEL_TOP), cache_win, rows_new, win_new)


PROMPT_FFN_ROWS = 512
PROMPT_ROWS = 512


def _rope_tables(pos):
    half = HEAD_DIM // 2
    inv = jnp.exp(-math.log(ROPE_THETA) * jnp.arange(half, dtype=F32) / half)
    ang = pos.astype(F32)[:, None] * inv[None, :]
    cos, sin = jnp.cos(ang), jnp.sin(ang)
    return jnp.concatenate([cos, cos], axis=1), jnp.concatenate([-sin, sin], axis=1)


def _prepare_weights(prm):
    nq = N_HEADS * HEAD_DIM
    w_qg = prm["attn_w_qg"]
    nb, d = w_qg.shape[:2]
    per_kv = HEADS_PER_KV * N_BRANCH
    w_gate = w_qg[:, :, nq:].reshape(nb, d, N_KV, per_kv)
    w_gate = jnp.pad(w_gate, ((0, 0), (0, 0), (0, 0), (0, LANES - per_kv))).reshape(nb, d, N_KV * LANES)
    return dict(
        ffn16={},
        ple_gate=prm["ple_w_gate"].astype(BF16), ple_proj=prm["ple_w_proj"].astype(BF16),
        glu=prm["glu_w"].astype(BF16), kv=prm["w_kv"].astype(BF16),
        q=w_qg[:, :, :nq].astype(BF16), gate=w_gate.astype(BF16), o=prm["attn_w_o"].astype(BF16),
        w1k=prm["cmp_w1_k"].astype(BF16), w1v=prm["cmp_w1_v"].astype(BF16),
        w2k=prm["cmp_w2_k"].astype(BF16), w2v=prm["cmp_w2_v"].astype(BF16),
        s5=[_s5_layouts(prm["ssm_lam_re"][i], prm["ssm_lam_im"][i], prm["ssm_log_dt"][i],
                        prm["ssm_b_re"][i], prm["ssm_b_im"][i], prm["ssm_c_re"][i], prm["ssm_c_im"][i])
            for i in range(N_A_LAYERS)])


def _finish(gen):
    try:
        while True:
            next(gen)
    except StopIteration as done:
        return done.value


def _run_group(x, p, start, state, past, prm, w):
    bsz, t, d = x.shape
    n = bsz * t
    prompt = past is None
    tm_ffn = PROMPT_FFN_ROWS if prompt else n
    tm = PROMPT_ROWS if prompt else n
    depth = prm["norm_g"].shape[0]
    cos, sin = _rope_tables(start + jnp.arange(t))
    if not prompt:
        cos, sin = jnp.broadcast_to(cos, (n, HEAD_DIM)), jnp.broadcast_to(sin, (n, HEAD_DIM))
    h = x.reshape(n, d)
    ssm_re, ssm_im = [], []
    rows = win = kv16 = kcmp = vcmp = None

    w32 = (prm["ffn_w_gate"], prm["ffn_w_up"], prm["ffn_w_down"])

    def ffn(h, g_pre, g_post, g_next, layer, which):
        if (layer, which) not in w["ffn16"]:
            assert g_next is not None
            h_new, xn, *w["ffn16"][layer, which] = _ffn_cast(h, g_pre, g_post, g_next, *w32, layer, which)
            return h_new, xn
        nxt = (layer, which + 1) if which == 0 else (layer + 1, 0)
        if not prompt or nxt[0] == depth:
            return _ffn(h, g_pre, g_post, g_next, *w["ffn16"][layer, which], tm_ffn)
        h_new, xn, *w["ffn16"][nxt] = _ffn_convert(h, g_pre, g_post, g_next, *w["ffn16"][layer, which],
                                                   tm_ffn, *w32, *nxt)
        return h_new, xn

    if not prompt:
        cache_kv, cache_win, page_table = past
        past_len = page_table.shape[1] * cache_kv.shape[1]
        cache_lines = cache_kv.reshape(-1, HEAD_DIM)
        win_lines = cache_win.reshape(-1, HEAD_DIM)
    for i in range(depth):
        g = prm["norm_g"][i]
        h, xn = ffn(h, g[0:1], g[1:2], g[2:3], i, 0)
        if i == 0:
            yield
        if i < N_A_LAYERS:
            if prompt:
                y, st = _s5_prompt(xn, bsz, w["s5"][i])
                st = st.reshape(d // LANES, bsz, 2, S5_GROUPS_PER_STEP, SSM_STATE).transpose(2, 1, 0, 3, 4)
                st = st.reshape(2, bsz, d // SSM_GROUP, SSM_STATE)
                ssm_re.append(st[0])
                ssm_im.append(st[1])
            else:
                y, s_re, s_im = _s5_step(xn, state[0][i].reshape(bsz, -1), state[1][i].reshape(bsz, -1),
                                         w["s5"][i])
                ssm_re.append(s_re.reshape(state[0][i].shape))
                ssm_im.append(s_im.reshape(state[1][i].shape))
            h = _s5_gate(h, xn, y, prm["ssm_d"][i][None], w["glu"], prm["glu_b"][i][None], g[3:4], i, tm)
        else:
            j = i - N_A_LAYERS
            qc, qr, gates = _qg_proj(xn, w["q"], w["gate"], cos, sin, j, tm)
            if prompt:
                o = _nsa_prompt(qc.reshape(bsz, t, -1), qr.reshape(bsz, t, -1), gates.reshape(bsz, t, -1),
                                kv16.reshape(bsz, t, -1), kcmp, vcmp).reshape(n, -1)
            else:
                heads = lambda a: a.reshape(bsz, N_KV, HEADS_PER_KV, HEAD_DIM)
                n_cmp = (past_len + t - CMP_LEN) // CMP_STRIDE + 1
                n_sel = -(-(past_len + t) // SEL_LEN)
                ocmp, idx = _nsa_select(heads(qc), kcmp, vcmp, past_len, n_cmp, n_sel)
                gsm = gates.reshape(bsz, N_KV, LANES)[:, :, :HEADS_PER_KV * N_BRANCH]
                gsm = gsm.reshape(bsz, N_KV, HEADS_PER_KV, N_BRANCH)
                gsm = jnp.pad(gsm, ((0, 0), (0, 0), (0, 0), (0, LANES - N_BRANCH)))
                o = _nsa_attend(idx[..., 0].reshape(-1), page_table, heads(qr), gsm, ocmp,
                                cache_lines, win_lines, rows_l.reshape(bsz, ROW_LINES, HEAD_DIM),
                                win_l.reshape(bsz, WIN_LINES, HEAD_DIM), past_len).reshape(n, -1)
            h = _o_proj(h, o, w["o"], g[3:4], j, tm)
        h, _ = ffn(h, g[4:5], g[5:6], None, i, 1)
        h = _ple(h, p[i].reshape(n, -1), w["ple_gate"], w["ple_proj"], g[6:7], i, tm)
        if i == N_A_LAYERS - 1:
            rows_l, win_l, kv16 = _kv_proj(h, prm["kv_norm_g"][None], w["kv"], cos, sin, tm)
            rows = rows_l.reshape(bsz, t, ROW_SLOTS, N_KV, HEAD_DIM)
            win = win_l.reshape(bsz, t, WIN_SLOTS, N_KV, HEAD_DIM)
            cmp_w = (prm["cmp_pe_k"], prm["cmp_pe_v"], w["w1k"], w["w1v"], w["w2k"], w["w2v"])
            if prompt:
                pages = jnp.arange(n // PAGE_ROWS, dtype=jnp.int32).reshape(bsz, t // PAGE_ROWS)
                kcmp, vcmp = _compress(rows_l, pages, *cmp_w)
                win = win[:, t - min(WINDOW, t):]
            else:
                kcmp, vcmp = _compress(cache_lines, page_table, *cmp_w)
                win = jnp.concatenate([cache_win, win], axis=1)[:, t:]
    return h.reshape(bsz, t, d), jnp.stack(ssm_re), jnp.stack(ssm_im), rows, win


def kernel(x_prompt, x_sample, state_ssm_re, state_ssm_im, cache_kv, cache_win, page_table,
           p_prompt, p_sample, norm_g, ffn_w_gate, ffn_w_up, ffn_w_down, ple_w_proj, ple_w_gate,
           ssm_lam_re, ssm_lam_im, ssm_log_dt, ssm_b_re, ssm_b_im, ssm_c_re, ssm_c_im, ssm_d,
           glu_w, glu_b, kv_norm_g, w_kv, cmp_pe_k, cmp_pe_v, cmp_w1_k, cmp_w2_k, cmp_w1_v,
           cmp_w2_v, attn_w_qg, attn_w_o):
    prm = dict(norm_g=norm_g, ffn_w_gate=ffn_w_gate, ffn_w_up=ffn_w_up, ffn_w_down=ffn_w_down,
               ple_w_proj=ple_w_proj, ple_w_gate=ple_w_gate, ssm_lam_re=ssm_lam_re,
               ssm_lam_im=ssm_lam_im, ssm_log_dt=ssm_log_dt, ssm_b_re=ssm_b_re, ssm_b_im=ssm_b_im,
               ssm_c_re=ssm_c_re, ssm_c_im=ssm_c_im, ssm_d=ssm_d, glu_w=glu_w, glu_b=glu_b,
               kv_norm_g=kv_norm_g, w_kv=w_kv, cmp_pe_k=cmp_pe_k, cmp_pe_v=cmp_pe_v,
               cmp_w1_k=cmp_w1_k, cmp_w2_k=cmp_w2_k, cmp_w1_v=cmp_w1_v, cmp_w2_v=cmp_w2_v,
               attn_w_qg=attn_w_qg, attn_w_o=attn_w_o)
    assert x_sample.shape[1] == 1, "the decode group handles one new token per sequence"
    assert cache_kv.shape[1] == PAGE_ROWS
    w = _prepare_weights(prm)
    past_len = page_table.shape[1] * cache_kv.shape[1]
    decode = _run_group(x_sample, p_sample, past_len, (state_ssm_re, state_ssm_im),
                        (cache_kv, cache_win, page_table), prm, w)
    next(decode)
    y_p, re_p, im_p, kv_p, win_p = _finish(_run_group(x_prompt, p_prompt, 0, None, None, prm, w))
    y_s, re_s, im_s, kv_s, win_s = _finish(decode)
    return (y_p, y_s, re_p, im_p, re_s, im_s, kv_p, kv_s, win_p, win_s)
```
